```python
import jax, jax.numpy as jnp
from jax import lax
import numpy as np

D_MODEL = 2048
BATCH = 4
SEQ = 4096
DEPTH = 2

HEAD_DIM = 128
ROPE_THETA = 10000.0
NORM_EPS = 1e-6
NEG_INF = -1e30
BLOCK = 128

DIL_GROUPS = ((128, 1), (512, 4), (2048, 16))
DIL_HEADS_PER_GROUP = 4
DIL_HEADS = DIL_HEADS_PER_GROUP * len(DIL_GROUPS)
DIL_OUT = DIL_HEADS_PER_GROUP * HEAD_DIM

NSA_Q_HEADS = 16
NSA_KV_HEADS = 2
NSA_REP = NSA_Q_HEADS // NSA_KV_HEADS
CMP_LEN = 32
CMP_STRIDE = 16
CMP_HIDDEN = 256
SEL_LEN = 64
SEL_TOPK = 16
SEL_CHUNK = 64
WIN_LEN = 512
N_NSA_BRANCH = 3
FORCE_BONUS = 1e4

D_FF = -(-8 * D_MODEL // (3 * 256)) * 256

COLS = (DIL_HEADS * HEAD_DIM, DIL_HEADS * HEAD_DIM, DIL_HEADS * HEAD_DIM,
        NSA_Q_HEADS * HEAD_DIM,
        NSA_KV_HEADS * HEAD_DIM, NSA_KV_HEADS * HEAD_DIM,
        NSA_KV_HEADS * HEAD_DIM, NSA_KV_HEADS * HEAD_DIM,
        NSA_KV_HEADS * HEAD_DIM, NSA_KV_HEADS * HEAD_DIM,
        NSA_Q_HEADS * N_NSA_BRANCH, D_MODEL, D_MODEL)
N_IN = sum(COLS)

kernel_name = "hybrid_dilated_nsa_gated_block"


def rms_norm(x, g):
    xf = x.astype(jnp.float32)
    y = xf * lax.rsqrt(jnp.mean(xf * xf, axis=-1, keepdims=True) + NORM_EPS)
    return (y * g.astype(jnp.float32)).astype(x.dtype)


def rope(x, pos):
    half = HEAD_DIM // 2
    inv = ROPE_THETA ** (-2.0 * jnp.arange(half, dtype=jnp.float32) / HEAD_DIM)
    ang = pos.astype(jnp.float32)[..., None] * inv
    cos = jnp.cos(ang)[:, :, None, :]
    sin = jnp.sin(ang)[:, :, None, :]
    xf = x.astype(jnp.float32)
    x1, x2 = xf[..., :half], xf[..., half:]
    return jnp.concatenate([x1 * cos - x2 * sin, x2 * cos + x1 * sin], axis=-1).astype(x.dtype)


def banded_attention(q, k, v, max_dist):
    B, S, G, R, dh = q.shape
    nb = S // BLOCK
    nprev = -(-max_dist // BLOCK)
    pad = nprev * BLOCK
    kp = jnp.pad(k, ((0, 0), (pad, 0), (0, 0), (0, 0))).reshape(B, nb + nprev, BLOCK, G, dh)
    vp = jnp.pad(v, ((0, 0), (pad, 0), (0, 0), (0, 0))).reshape(B, nb + nprev, BLOCK, G, dh)
    kb = jnp.concatenate([kp[:, i:i + nb] for i in range(nprev + 1)], axis=2)
    vb = jnp.concatenate([vp[:, i:i + nb] for i in range(nprev + 1)], axis=2)
    qb = q.reshape(B, nb, BLOCK, G, R, dh)
    s = jnp.einsum('bnqgrd,bnkgd->bngrqk', qb, kb).astype(jnp.float32) * (dh ** -0.5)
    kb_len = (nprev + 1) * BLOCK
    qi = jnp.arange(BLOCK)[:, None]
    kk = jnp.arange(kb_len)[None, :]
    dist = pad + qi - kk
    kabs = jnp.arange(nb)[:, None, None] * BLOCK - pad + kk[None]
    mask = (dist >= 0)[None] & (dist <= max_dist)[None] & (kabs >= 0)
    s = jnp.where(mask[None, :, None, None], s, NEG_INF)
    m = jnp.max(s, axis=-1, keepdims=True)
    p = jnp.exp(s - m)
    l = jnp.sum(p, axis=-1, keepdims=True)
    o = jnp.einsum('bngrqk,bnkgd->bnqgrd', (p / l).astype(v.dtype), vb)
    lse = (m + jnp.log(l))[..., 0]
    lse = jnp.transpose(lse, (0, 1, 4, 2, 3)).reshape(B, S, G, R)
    return o.reshape(B, S, G, R, dh), lse


def dilated_window_attention(q, k, v, window, dilation):
    B, S, H, dh = q.shape
    unit = dilation * BLOCK
    Sp = -(-S // unit) * unit
    L = Sp // dilation

    def fold(t):
        t = jnp.pad(t, ((0, 0), (0, Sp - S), (0, 0), (0, 0)))
        return t.reshape(B, L, dilation, H, dh).transpose(0, 2, 1, 3, 4).reshape(B * dilation, L, H, dh)

    o, lse = banded_attention(fold(q)[:, :, :, None, :], fold(k), fold(v), window // dilation)
    o = o[:, :, :, 0].reshape(B, dilation, L, H, dh).transpose(0, 2, 1, 3, 4).reshape(B, Sp, H, dh)[:, :S]
    lse = lse[:, :, :, 0].reshape(B, dilation, L, H).transpose(0, 2, 1, 3).reshape(B, Sp, H)[:, :S]
    return o.astype(jnp.float32), lse


def compress_blocks(t, pos_emb, w1, w2):
    B, S, G, dh = t.shape
    n_c = (S - CMP_LEN) // CMP_STRIDE + 1
    idx = jnp.arange(n_c)[:, None] * CMP_STRIDE + jnp.arange(CMP_LEN)[None, :]
    blk = t[:, idx] + pos_emb[None, None, :, None, :]
    blk = jnp.transpose(blk, (0, 1, 3, 2, 4)).reshape(B, n_c, G, CMP_LEN * dh)
    return jax.nn.gelu(blk @ w1) @ w2


def nsa_attention(q, kc, vc, ks, vs, kw, vw, gates, positions,
                  cmp_pos_k, cmp_pos_v, cmp_w1_k, cmp_w2_k, cmp_w1_v, cmp_w2_v):
    B, S, G, R, dh = q.shape
    scale = dh ** -0.5
    t = jnp.arange(S)

    n_c = (S - CMP_LEN) // CMP_STRIDE + 1
    blk_end = jnp.arange(n_c) * CMP_STRIDE + CMP_LEN - 1
    k_cmp = rope(compress_blocks(kc, cmp_pos_k, cmp_w1_k, cmp_w2_k), positions[:, blk_end])
    v_cmp = compress_blocks(vc, cmp_pos_v, cmp_w1_v, cmp_w2_v)
    s = jnp.einsum('bsgrd,bcgd->bgrsc', q, k_cmp).astype(jnp.float32) * scale
    valid_c = blk_end[None, :] <= t[:, None]
    s = jnp.where(valid_c, s, NEG_INF)
    m = jnp.max(s, axis=-1, keepdims=True)
    p = jnp.where(valid_c, jnp.exp(s - m), 0.0)
    p = p / jnp.maximum(jnp.sum(p, axis=-1, keepdims=True), 1e-30)
    o_cmp = jnp.einsum('bgrsc,bcgd->bsgrd', p.astype(vc.dtype), v_cmp)

    n_s = S // SEL_LEN
    n_sel = min(SEL_TOPK, n_s)
    c_start = jnp.arange(n_c) * CMP_STRIDE
    s_start = jnp.arange(n_s) * SEL_LEN
    overlap = ((c_start[:, None] <= s_start[None, :] + SEL_LEN - 1) &
               (c_start[:, None] + CMP_LEN - 1 >= s_start[None, :])).astype(jnp.float32)
    p_slc = jnp.einsum('bgsc,cj->bsgj', jnp.sum(p, axis=2), overlap)
    blk_t = t // SEL_LEN
    j = jnp.arange(n_s)
    valid_s = j[None, :] <= blk_t[:, None]
    forced = (j[None, :] == 0) | (j[None, :] == blk_t[:, None]) | (j[None, :] == blk_t[:, None] - 1)
    score = jnp.where(valid_s[None, :, None, :],
                      p_slc + jnp.where(forced, FORCE_BONUS, 0.0)[None, :, None, :], -1.0)
    _, sel_idx = lax.top_k(score, n_sel)

    kblk = ks.reshape(B, n_s, SEL_LEN, G, dh).transpose(0, 3, 1, 2, 4)
    vblk = vs.reshape(B, n_s, SEL_LEN, G, dh).transpose(0, 3, 1, 2, 4)
    bi = jnp.arange(B)[:, None, None, None]
    gi = jnp.arange(G)[None, None, :, None]
    nch = S // SEL_CHUNK

    def sel_chunk(args):
        q_c, idx_c, t_c = args
        k_sel = kblk[bi, gi, idx_c]
        v_sel = vblk[bi, gi, idx_c]
        sc = jnp.einsum('bqgrd,bqgnld->bqgrnl', q_c, k_sel).astype(jnp.float32) * scale
        kpos = idx_c[..., None] * SEL_LEN + jnp.arange(SEL_LEN)
        ok = kpos <= t_c[None, :, None, None, None]
        sc = jnp.where(ok[:, :, :, None], sc, NEG_INF)
        shp = sc.shape
        pr = jax.nn.softmax(sc.reshape(shp[:4] + (shp[4] * shp[5],)), axis=-1).reshape(shp)
        return jnp.einsum('bqgrnl,bqgnld->bqgrd', pr.astype(v_sel.dtype), v_sel)

    q_chunks = q.reshape(B, nch, SEL_CHUNK, G, R, dh).swapaxes(0, 1)
    i_chunks = sel_idx.reshape(B, nch, SEL_CHUNK, G, n_sel).swapaxes(0, 1)
    t_chunks = t.reshape(nch, SEL_CHUNK)
    o_slc = lax.map(sel_chunk, (q_chunks, i_chunks, t_chunks))
    o_slc = o_slc.swapaxes(0, 1).reshape(B, S, G, R, dh)

    o_win, _ = banded_attention(q, kw, vw, WIN_LEN - 1)

    o = (gates[..., 0:1] * o_cmp.astype(jnp.float32) +
         gates[..., 1:2] * o_slc.astype(jnp.float32) +
         gates[..., 2:3] * o_win.astype(jnp.float32))
    return o.astype(q.dtype)


def hybrid_layer(x, positions, ln_mix, w_in, cmp_pos_k, cmp_pos_v, cmp_w1_k, cmp_w2_k,
                 cmp_w1_v, cmp_w2_v, w_out_a, w_out_b, w_out, ln_ffn,
                 w_ffn_gate, w_ffn_up, w_ffn_down):
    B, S, _ = x.shape
    dh = HEAD_DIM
    h = rms_norm(x, ln_mix)
    proj = h @ w_in
    offs = [int(o) for o in np.cumsum(COLS)[:-1]]
    (qa, ka, va, qb, kc, vc, ks, vs, kw, vw,
     gate_b, gate_a_merge, gate_b_merge) = jnp.split(proj, offs, axis=-1)

    qa = rope(qa.reshape(B, S, DIL_HEADS, dh), positions)
    ka = rope(ka.reshape(B, S, DIL_HEADS, dh), positions)
    va = va.reshape(B, S, DIL_HEADS, dh)
    outs, lses = [], []
    for gi_, (win, dil) in enumerate(DIL_GROUPS):
        sl = slice(gi_ * DIL_HEADS_PER_GROUP, (gi_ + 1) * DIL_HEADS_PER_GROUP)
        o_g, lse_g = dilated_window_attention(qa[:, :, sl], ka[:, :, sl], va[:, :, sl], win, dil)
        outs.append(o_g)
        lses.append(lse_g)
    wts = jax.nn.softmax(jnp.stack(lses, axis=0), axis=0)
    ya = jnp.sum(wts[..., None] * jnp.stack(outs, axis=0), axis=0).astype(x.dtype)
    ya = ya.reshape(B, S, DIL_OUT) @ w_out_a

    G, R = NSA_KV_HEADS, NSA_REP
    qb = rope(qb.reshape(B, S, NSA_Q_HEADS, dh), positions).reshape(B, S, G, R, dh)
    kv = lambda t_: t_.reshape(B, S, G, dh)
    gates = jax.nn.sigmoid(gate_b.astype(jnp.float32)).reshape(B, S, G, R, N_NSA_BRANCH)
    yb = nsa_attention(qb, kv(kc), kv(vc), rope(kv(ks), positions), kv(vs),
                       rope(kv(kw), positions), kv(vw), gates, positions,
                       cmp_pos_k, cmp_pos_v, cmp_w1_k, cmp_w2_k, cmp_w1_v, cmp_w2_v)
    yb = yb.reshape(B, S, NSA_Q_HEADS * dh) @ w_out_b

    ga = jax.nn.sigmoid(gate_a_merge.astype(jnp.float32))
    gb = jax.nn.sigmoid(gate_b_merge.astype(jnp.float32))
    y = (ga * ya.astype(jnp.float32) + gb * yb.astype(jnp.float32)).astype(x.dtype)
    x = x + y @ w_out

    h2 = rms_norm(x, ln_ffn)
    x = x + (jax.nn.silu(h2 @ w_ffn_gate) * (h2 @ w_ffn_up)) @ w_ffn_down
    return x


def setup_inputs(seed: int = 0) -> dict:
    key = jax.random.key(seed)
    k = jax.random.split(key, 20)
    f32 = jnp.float32
    nrm = lambda kk, shape, fan_in: jax.random.normal(kk, shape, f32) * (fan_in ** -0.5)
    L = DEPTH
    return {
        "x": jax.random.normal(k[0], (BATCH, SEQ, D_MODEL), f32),
        "positions": jnp.broadcast_to(jnp.arange(SEQ, dtype=jnp.int32), (BATCH, SEQ)),
        "ln_mix": 1.0 + 0.02 * jax.random.normal(k[1], (L, D_MODEL), f32),
        "w_in": nrm(k[2], (L, D_MODEL, N_IN), D_MODEL),
        "cmp_pos_k": 0.02 * jax.random.normal(k[3], (L, CMP_LEN, HEAD_DIM), f32),
        "cmp_pos_v": 0.02 * jax.random.normal(k[4], (L, CMP_LEN, HEAD_DIM), f32),
        "cmp_w1_k": nrm(k[5], (L, CMP_LEN * HEAD_DIM, CMP_HIDDEN), CMP_LEN * HEAD_DIM),
        "cmp_w2_k": nrm(k[6], (L, CMP_HIDDEN, HEAD_DIM), CMP_HIDDEN),
        "cmp_w1_v": nrm(k[7], (L, CMP_LEN * HEAD_DIM, CMP_HIDDEN), CMP_LEN * HEAD_DIM),
        "cmp_w2_v": nrm(k[8], (L, CMP_HIDDEN, HEAD_DIM), CMP_HIDDEN),
        "w_out_a": nrm(k[9], (L, DIL_OUT, D_MODEL), DIL_OUT),
        "w_out_b": nrm(k[10], (L, NSA_Q_HEADS * HEAD_DIM, D_MODEL), NSA_Q_HEADS * HEAD_DIM),
        "w_out": nrm(k[11], (L, D_MODEL, D_MODEL), D_MODEL),
        "ln_ffn": 1.0 + 0.02 * jax.random.normal(k[12], (L, D_MODEL), f32),
        "w_ffn_gate": nrm(k[13], (L, D_MODEL, D_FF), D_MODEL),
        "w_ffn_up": nrm(k[14], (L, D_MODEL, D_FF), D_MODEL),
        "w_ffn_down": nrm(k[15], (L, D_FF, D_MODEL), D_FF),
        "ln_final": 1.0 + 0.02 * jax.random.normal(k[16], (D_MODEL,), f32),
    }


def reference(x, positions, ln_mix, w_in, cmp_pos_k, cmp_pos_v, cmp_w1_k, cmp_w2_k,
              cmp_w1_v, cmp_w2_v, w_out_a, w_out_b, w_out, ln_ffn,
              w_ffn_gate, w_ffn_up, w_ffn_down, ln_final):
    for l in range(DEPTH):
        x = hybrid_layer(x, positions, ln_mix[l], w_in[l], cmp_pos_k[l], cmp_pos_v[l],
                         cmp_w1_k[l], cmp_w2_k[l], cmp_w1_v[l], cmp_w2_v[l],
                         w_out_a[l], w_out_b[l], w_out[l], ln_ffn[l],
                         w_ffn_gate[l], w_ffn_up[l], w_ffn_down[l])
    return rms_norm(x, ln_final)
```

```python
import functools

import numpy as np
import jax
import jax.numpy as jnp
from jax import lax
from jax.experimental import pallas as pl
from jax.experimental.pallas import tpu as pltpu

F32 = jnp.float32
BF16 = jnp.bfloat16

D_MODEL = 2048
DH = 128
HALF = DH // 2
ROPE_THETA = 10000.0
NORM_EPS = 1e-6
NEG_INF = -1e30
BLOCK = 128

DIL_GROUPS = ((128, 1), (512, 4), (2048, 16))
DIL_HPG = 4
DIL_HEADS = DIL_HPG * len(DIL_GROUPS)
DIL_OUT = DIL_HPG * DH

NSA_Q_HEADS = 16
NSA_G = 2
NSA_R = NSA_Q_HEADS // NSA_G
CMP_LEN = 32
CMP_STRIDE = 16
CMP_HIDDEN = 256
SEL_LEN = 64
SEL_TOPK = 16
WIN_LEN = 512
FORCE_BONUS = 1e4
D_FF = 5632

W_QA = DIL_HEADS * DH
W_QB = NSA_Q_HEADS * DH
W_KV = NSA_G * DH
N_R = 2 * W_QA + W_QB + 2 * W_KV
N_P = W_QA + 2 * W_KV
N_C = 2 * W_KV
N_G = 2 * D_MODEL + NSA_G * DH
TN_IN = 256
VMEM_LIMIT = 56 * 1024 * 1024


def _cparams(sem):
    return pltpu.CompilerParams(dimension_semantics=sem, vmem_limit_bytes=VMEM_LIMIT)


def _rope_tab_kernel(pos_ref, inv_ref, sgn_ref, cos_ref, sin_ref):
    ang = pos_ref[...] * inv_ref[...]
    cos_ref[...] = jnp.cos(ang)
    sin_ref[...] = jnp.sin(ang) * sgn_ref[...]


def _rope_tables(pos_f, tm):
    m = pos_f.shape[0]
    inv = ROPE_THETA ** (-2.0 * jnp.arange(HALF, dtype=F32) / DH)
    inv = jnp.concatenate([inv, inv])[None, :]
    sgn = jnp.concatenate([-jnp.ones((HALF,), F32), jnp.ones((HALF,), F32)])[None, :]
    return pl.pallas_call(
        _rope_tab_kernel,
        grid=(m // tm,),
        in_specs=[pl.BlockSpec((tm, 1), lambda i: (i, 0)),
                  pl.BlockSpec((1, DH), lambda i: (0, 0)),
                  pl.BlockSpec((1, DH), lambda i: (0, 0))],
        out_specs=[pl.BlockSpec((tm, DH), lambda i: (i, 0)),
                   pl.BlockSpec((tm, DH), lambda i: (i, 0))],
        out_shape=[jax.ShapeDtypeStruct((m, DH), F32)] * 2,
        compiler_params=_cparams(("arbitrary",)),
        name="rope_tables",
    )(pos_f, inv, sgn)


def _rope(a, cos, sin_signed):
    return a * cos + pltpu.roll(a, HALF, 1) * sin_signed


def _rms(x, g):
    ms = jnp.mean(x * x, axis=-1, keepdims=True)
    return x * lax.rsqrt(ms + NORM_EPS) * g


def _inproj_kernel(x_ref, g_ref, w_ref, cs_ref, cos_ref, sin_ref,
                   or_ref, op_ref, oc_ref, og_ref, h_ref, *, n_r, n_p, n_c):
    j = pl.program_id(1)

    @pl.when(j == 0)
    def _():
        h_ref[...] = _rms(x_ref[...], g_ref[...]).astype(BF16)

    acc = jnp.dot(h_ref[...], w_ref[...], preferred_element_type=F32)

    @pl.when(j < n_r)
    def _():
        cos = cos_ref[...]
        sin = sin_ref[...]
        for hh in range(TN_IN // DH):
            sl = slice(hh * DH, (hh + 1) * DH)
            r = _rope(acc[:, sl], cos, sin) * cs_ref[:, sl]
            or_ref[:, sl] = r.astype(BF16)

    @pl.when((j >= n_r) & (j < n_r + n_p))
    def _():
        op_ref[...] = acc.astype(BF16)

    @pl.when((j >= n_r + n_p) & (j < n_r + n_p + n_c))
    def _():
        oc_ref[...] = acc

    @pl.when(j >= n_r + n_p + n_c)
    def _():
        og_ref[...] = jax.nn.sigmoid(acc)


def _inproj(x2, g, w, cs, cos, sin, tm):
    m = x2.shape[0]
    n_r, n_p, n_c, n_g = N_R // TN_IN, N_P // TN_IN, N_C // TN_IN, N_G // TN_IN
    nt = n_r + n_p + n_c + n_g
    clamp = lambda v, hi: jnp.minimum(jnp.maximum(v, 0), hi)
    return pl.pallas_call(
        functools.partial(_inproj_kernel, n_r=n_r, n_p=n_p, n_c=n_c),
        grid=(m // tm, nt),
        in_specs=[pl.BlockSpec((tm, D_MODEL), lambda i, j: (i, 0)),
                  pl.BlockSpec((1, D_MODEL), lambda i, j: (0, 0)),
                  pl.BlockSpec((D_MODEL, TN_IN), lambda i, j: (0, j)),
                  pl.BlockSpec((1, TN_IN), lambda i, j: (0, clamp(j, n_r - 1))),
                  pl.BlockSpec((tm, DH), lambda i, j: (i, 0)),
                  pl.BlockSpec((tm, DH), lambda i, j: (i, 0))],
        out_specs=[pl.BlockSpec((tm, TN_IN), lambda i, j: (i, clamp(j, n_r - 1))),
                   pl.BlockSpec((tm, TN_IN), lambda i, j: (i, clamp(j - n_r, n_p - 1))),
                   pl.BlockSpec((tm, TN_IN), lambda i, j: (i, clamp(j - n_r - n_p, n_c - 1))),
                   pl.BlockSpec((tm, TN_IN), lambda i, j: (i, clamp(j - n_r - n_p - n_c, n_g - 1)))],
        out_shape=[jax.ShapeDtypeStruct((m, N_R), BF16),
                   jax.ShapeDtypeStruct((m, N_P), BF16),
                   jax.ShapeDtypeStruct((m, N_C), F32),
                   jax.ShapeDtypeStruct((m, N_G), F32)],
        scratch_shapes=[pltpu.VMEM((tm, D_MODEL), BF16)],
        compiler_params=_cparams(("arbitrary", "arbitrary")),
        name="inproj",
    )(x2, g, w, cs, cos, sin)


def _dot_nt(a, b):
    return lax.dot_general(a, b, (((1,), (1,)), ((), ())), preferred_element_type=F32)


def _dil_kernel(q_ref, kp_ref, kc_ref, vp_ref, vc_ref, o_ref, lse_ref):
    ub = pl.program_id(2)
    row = lax.broadcasted_iota(jnp.int32, (BLOCK, BLOCK), 0)
    col = lax.broadcasted_iota(jnp.int32, (BLOCK, BLOCK), 1)
    mask_c = col <= row
    mask_p = (col >= row) & (ub > 0)
    for h in range(DIL_HPG):
        sl = slice(h * DH, (h + 1) * DH)
        q = q_ref[:, sl]
        s_c = jnp.where(mask_c, _dot_nt(q, kc_ref[:, sl]), NEG_INF)
        s_p = jnp.where(mask_p, _dot_nt(q, kp_ref[:, sl]), NEG_INF)
        m = jnp.maximum(jnp.max(s_c, axis=-1, keepdims=True), jnp.max(s_p, axis=-1, keepdims=True))
        p_c = jnp.exp(s_c - m)
        p_p = jnp.exp(s_p - m)
        l = jnp.sum(p_c, axis=-1, keepdims=True) + jnp.sum(p_p, axis=-1, keepdims=True)
        o = (jnp.dot(p_c.astype(BF16), vc_ref[:, sl], preferred_element_type=F32) +
             jnp.dot(p_p.astype(BF16), vp_ref[:, sl], preferred_element_type=F32))
        o_ref[:, sl] = o / l
        lse_ref[:, sl] = jnp.broadcast_to(m + jnp.log(l), (BLOCK, DH))


def _dilated(o_r, o_p, b, s, gi, dil):
    lf = s // dil
    r_view = o_r.reshape(b, lf, dil * N_R)
    p_view = o_p.reshape(b, lf, dil * N_P)
    nqb = N_R // DIL_OUT
    npb = N_P // DIL_OUT
    kb = W_QA // DIL_OUT
    prev = lambda ub: jnp.maximum(ub - 1, 0)
    blk = (None, BLOCK, DIL_OUT)
    out = pl.pallas_call(
        _dil_kernel,
        grid=(b, dil, lf // BLOCK),
        in_specs=[pl.BlockSpec(blk, lambda bi, rho, ub: (bi, ub, rho * nqb + gi)),
                  pl.BlockSpec(blk, lambda bi, rho, ub: (bi, prev(ub), rho * nqb + kb + gi)),
                  pl.BlockSpec(blk, lambda bi, rho, ub: (bi, ub, rho * nqb + kb + gi)),
                  pl.BlockSpec(blk, lambda bi, rho, ub: (bi, prev(ub), rho * npb + gi)),
                  pl.BlockSpec(blk, lambda bi, rho, ub: (bi, ub, rho * npb + gi))],
        out_specs=[pl.BlockSpec(blk, lambda bi, rho, ub: (bi, ub, rho)),
                   pl.BlockSpec(blk, lambda bi, rho, ub: (bi, ub, rho))],
        out_shape=[jax.ShapeDtypeStruct((b, lf, dil * DIL_OUT), F32)] * 2,
        compiler_params=_cparams(("arbitrary", "arbitrary", "arbitrary")),
        name=f"dilated_{dil}",
    )(r_view, r_view, r_view, p_view, p_view)
    return out[0].reshape(b * s, DIL_OUT), out[1].reshape(b * s, DIL_OUT)


def _cmp_kernel(x_ref, posk_ref, posv_ref, w1k_ref, w1v_ref, w2k_ref, w2v_ref,
                cos_ref, sin_ref, kc_ref, vc_ref, *, ncp):
    half = CMP_LEN // 2
    for kind in range(2):
        pos_ref, w1_ref, w2_ref, o_ref = ((posk_ref, w1k_ref, w2k_ref, kc_ref) if kind == 0
                                          else (posv_ref, w1v_ref, w2v_ref, vc_ref))
        for g in range(NSA_G):
            off = kind * W_KV + g * DH
            a = jnp.zeros((ncp, CMP_HIDDEN), F32)
            bm = jnp.zeros((ncp, CMP_HIDDEN), F32)
            for l in range(half):
                x = x_ref[:, l * N_C + off:l * N_C + off + DH]
                a = a + jnp.dot((x + pos_ref[l:l + 1, :]).astype(BF16), w1_ref[l],
                                preferred_element_type=F32)
                bm = bm + jnp.dot((x + pos_ref[half + l:half + l + 1, :]).astype(BF16),
                                  w1_ref[half + l], preferred_element_type=F32)
            hid = a + pltpu.roll(bm, ncp - 1, 0)
            hid = jax.nn.gelu(hid, approximate=True).astype(BF16)
            out = jnp.dot(hid, w2_ref[...], preferred_element_type=F32)
            if kind == 0:
                out = _rope(out, cos_ref[...], sin_ref[...])
            o_ref[g] = out.astype(BF16)


def _compress(o_c, b, s, posk, posv, w1k, w1v, w2k, w2v, cosc, sinc):
    ncp = s // CMP_STRIDE
    x_view = o_c.reshape(b, ncp, CMP_STRIDE * N_C)
    full = lambda shape: pl.BlockSpec(shape, lambda bi: (0,) * len(shape))
    return pl.pallas_call(
        functools.partial(_cmp_kernel, ncp=ncp),
        grid=(b,),
        in_specs=[pl.BlockSpec((None, ncp, CMP_STRIDE * N_C), lambda bi: (bi, 0, 0)),
                  full((CMP_LEN, DH)), full((CMP_LEN, DH)),
                  full((CMP_LEN, DH, CMP_HIDDEN)), full((CMP_LEN, DH, CMP_HIDDEN)),
                  full((CMP_HIDDEN, DH)), full((CMP_HIDDEN, DH)),
                  pl.BlockSpec((None, ncp, DH), lambda bi: (bi, 0, 0)),
                  pl.BlockSpec((None, ncp, DH), lambda bi: (bi, 0, 0))],
        out_specs=[pl.BlockSpec((None, NSA_G, ncp, DH), lambda bi: (bi, 0, 0, 0)),
                   pl.BlockSpec((None, NSA_G, ncp, DH), lambda bi: (bi, 0, 0, 0))],
        out_shape=[jax.ShapeDtypeStruct((b, NSA_G, ncp, DH), BF16)] * 2,
        compiler_params=_cparams(("arbitrary",)),
        name="nsa_compress",
    )(x_view, posk, posv, w1k, w1v, w2k, w2v, cosc, sinc)


def _nsa_kernel(q_ref, ks_ref, vs_ref, kw_ref, vw_ref, kc_ref, vc_ref, gate_ref, ov_ref, e_ref,
                out_ref, qst_ref, mx_ref, m_ref, l_ref, acc_ref, *, tq, tk, s_len, ncp, n_s):
    R = NSA_R
    t0 = pl.program_id(2) * tq
    for r in range(R):
        qst_ref[r * tq:(r + 1) * tq, :] = q_ref[:, r * DH:(r + 1) * DH]
    q = qst_ref[...]
    trow = t0 + lax.broadcasted_iota(jnp.int32, (tq, 1), 0)

    def softmax_pv(s3, ok, v):
        s3 = jnp.where(ok[None], s3, NEG_INF)
        m = jnp.max(s3, axis=-1, keepdims=True)
        p = jnp.where(ok[None], jnp.exp(s3 - m), 0.0)
        l = jnp.maximum(jnp.sum(p, axis=-1, keepdims=True), 1e-30)
        p = p / l
        o = jnp.dot(p.reshape(R * tq, p.shape[-1]).astype(BF16), v, preferred_element_type=F32)
        return o, p

    cend = lax.broadcasted_iota(jnp.int32, (1, ncp), 1) * CMP_STRIDE + (CMP_LEN - 1)
    valid_c = cend <= trow
    s3 = _dot_nt(q, kc_ref[...]).reshape(R, tq, ncp)
    o_cmp, p = softmax_pv(s3, valid_c, vc_ref[...])
    psum = jnp.sum(p, axis=0)
    hi = psum.astype(BF16)
    rem = psum - hi.astype(F32)
    mid = rem.astype(BF16)
    lo = (rem - mid.astype(F32)).astype(BF16)
    ov = ov_ref[...]
    p_slc = (jnp.dot(hi, ov, preferred_element_type=F32) +
             jnp.dot(mid, ov, preferred_element_type=F32) +
             jnp.dot(lo, ov, preferred_element_type=F32))

    jj = lax.broadcasted_iota(jnp.int32, (1, n_s), 1)
    blk_t = trow // SEL_LEN
    forced = (jj == 0) | (jj == blk_t) | (jj == blk_t - 1)
    score = jnp.where(jj <= blk_t, p_slc + jnp.where(forced, FORCE_BONUS, 0.0), -1.0)
    cnt = jnp.zeros((tq, n_s), F32)
    for i in range(n_s):
        ci = score[:, i:i + 1]
        beats = (ci > score) | ((ci == score) & (jj > i))
        cnt = cnt + jnp.where(beats, 1.0, 0.0)
    sel = jnp.where(cnt < float(min(SEL_TOPK, n_s)), 1.0, 0.0).astype(BF16)
    for kt in range(s_len // tk):
        mx_ref[kt] = jnp.dot(sel, e_ref[:, kt * tk:(kt + 1) * tk],
                             preferred_element_type=F32).astype(BF16)

    m_ref[...] = jnp.full(m_ref.shape, NEG_INF, F32)
    l_ref[...] = jnp.zeros(l_ref.shape, F32)
    acc_ref[...] = jnp.zeros(acc_ref.shape, F32)

    def body(kt, carry):
        k0 = pl.multiple_of(kt * tk, tk)
        kpos = k0 + lax.broadcasted_iota(jnp.int32, (1, tk), 1)
        ok = (mx_ref[kt] > 0.5) & (kpos <= trow)
        s3 = _dot_nt(q, ks_ref[pl.ds(k0, tk), :]).reshape(R, tq, tk)
        s3 = jnp.where(ok[None], s3, NEG_INF)
        m_old = m_ref[...]
        m_new = jnp.maximum(m_old, jnp.max(s3, axis=-1, keepdims=True))
        alpha = jnp.exp(m_old - m_new)
        p = jnp.exp(s3 - m_new)
        l_ref[...] = alpha * l_ref[...] + jnp.sum(p, axis=-1, keepdims=True)
        pv = jnp.dot(p.reshape(R * tq, tk).astype(BF16), vs_ref[pl.ds(k0, tk), :],
                     preferred_element_type=F32)
        acc_ref[...] = alpha * acc_ref[...] + pv.reshape(R, tq, DH)
        m_ref[...] = m_new
        return carry

    lax.fori_loop(0, (t0 + tq - 1) // tk + 1, body, 0)
    o_slc = (acc_ref[...] / l_ref[...]).reshape(R * tq, DH)

    wlen = (-(-(WIN_LEN - 1) // BLOCK)) * BLOCK + tq
    w0 = pl.multiple_of(jnp.maximum(t0 + tq - wlen, 0), BLOCK)
    kpos = w0 + lax.broadcasted_iota(jnp.int32, (1, wlen), 1)
    ok_w = (kpos <= trow) & (trow - kpos <= WIN_LEN - 1)
    s3 = _dot_nt(q, kw_ref[pl.ds(w0, wlen), :]).reshape(R, tq, wlen)
    o_win, _ = softmax_pv(s3, ok_w, vw_ref[pl.ds(w0, wlen), :])

    gate = gate_ref[...]
    for r in range(R):
        rows = slice(r * tq, (r + 1) * tq)
        y = (gate[:, r:r + 1] * o_cmp[rows] +
             gate[:, R + r:R + r + 1] * o_slc[rows] +
             gate[:, 2 * R + r:2 * R + r + 1] * o_win[rows])
        out_ref[:, r * DH:(r + 1) * DH] = y.astype(BF16)


def _nsa(o_r, o_p, o_g, kcmp, vcmp, b, s, tq, tk):
    ncp = s // CMP_STRIDE
    n_s = s // SEL_LEN
    r_view = o_r.reshape(b, s, N_R)
    p_view = o_p.reshape(b, s, N_P)
    g_view = o_g.reshape(b, s, N_G)
    c = np.arange(ncp)[:, None]
    j = np.arange(n_s)[None, :]
    n_c = (s - CMP_LEN) // CMP_STRIDE + 1
    ov = ((c * CMP_STRIDE <= j * SEL_LEN + SEL_LEN - 1) &
          (c * CMP_STRIDE + CMP_LEN - 1 >= j * SEL_LEN) & (c < n_c))
    ov = jnp.asarray(ov.astype(np.float32), BF16)
    expand = jnp.asarray((np.arange(s)[None, :] // SEL_LEN == np.arange(n_s)[:, None])
                         .astype(np.float32), BF16)
    wq = NSA_R * DH
    qcol = (2 * W_QA) // wq
    ks_col = (2 * W_QA + W_QB) // DH
    kw_col = ks_col + NSA_G
    vs_col = W_QA // DH
    vw_col = vs_col + NSA_G
    gate_col = (2 * D_MODEL) // DH
    kv = (None, s, DH)
    out = pl.pallas_call(
        functools.partial(_nsa_kernel, tq=tq, tk=tk, s_len=s, ncp=ncp, n_s=n_s),
        grid=(b, NSA_G, s // tq),
        in_specs=[pl.BlockSpec((None, tq, wq), lambda bi, g, qi: (bi, qi, qcol + g)),
                  pl.BlockSpec(kv, lambda bi, g, qi: (bi, 0, ks_col + g)),
                  pl.BlockSpec(kv, lambda bi, g, qi: (bi, 0, vs_col + g)),
                  pl.BlockSpec(kv, lambda bi, g, qi: (bi, 0, kw_col + g)),
                  pl.BlockSpec(kv, lambda bi, g, qi: (bi, 0, vw_col + g)),
                  pl.BlockSpec((None, None, ncp, DH), lambda bi, g, qi: (bi, g, 0, 0)),
                  pl.BlockSpec((None, None, ncp, DH), lambda bi, g, qi: (bi, g, 0, 0)),
                  pl.BlockSpec((None, tq, DH), lambda bi, g, qi: (bi, qi, gate_col + g)),
                  pl.BlockSpec((ncp, n_s), lambda bi, g, qi: (0, 0)),
                  pl.BlockSpec((n_s, s), lambda bi, g, qi: (0, 0))],
        out_specs=pl.BlockSpec((None, tq, wq), lambda bi, g, qi: (bi, qi, g)),
        out_shape=jax.ShapeDtypeStruct((b, s, W_QB), BF16),
        scratch_shapes=[pltpu.VMEM((NSA_R * tq, DH), BF16),
                        pltpu.VMEM((s // tk, tq, tk), BF16),
                        pltpu.VMEM((NSA_R, tq, 1), F32),
                        pltpu.VMEM((NSA_R, tq, 1), F32),
                        pltpu.VMEM((NSA_R, tq, DH), F32)],
        compiler_params=_cparams(("arbitrary", "arbitrary", "arbitrary")),
        name="nsa_attention",
    )(r_view, r_view, p_view, r_view, p_view, kcmp, vcmp, g_view, ov, expand)
    return out.reshape(b * s, W_QB)


def _out_kernel(o0_ref, o1_ref, o2_ref, l0_ref, l1_ref, l2_ref, yb_ref, ga_ref, gb_ref, x_ref,
                woa_ref, wob_ref, wo_ref, out_ref):
    l0, l1, l2 = l0_ref[...], l1_ref[...], l2_ref[...]
    m = jnp.maximum(jnp.maximum(l0, l1), l2)
    e0, e1, e2 = jnp.exp(l0 - m), jnp.exp(l1 - m), jnp.exp(l2 - m)
    ya = (e0 * o0_ref[...] + e1 * o1_ref[...] + e2 * o2_ref[...]) / (e0 + e1 + e2)
    pa = jnp.dot(ya.astype(BF16), woa_ref[...], preferred_element_type=F32)
    pb = jnp.dot(yb_ref[...], wob_ref[...], preferred_element_type=F32)
    y = (ga_ref[...] * pa + gb_ref[...] * pb).astype(BF16)
    out_ref[...] = x_ref[...] + jnp.dot(y, wo_ref[...], preferred_element_type=F32)


def _merge_out(oa, lse, yb, o_g, x2, woa, wob, wo, tm):
    m = x2.shape[0]
    row = lambda w: pl.BlockSpec((tm, w), lambda i: (i, 0))
    const = lambda shape: pl.BlockSpec(shape, lambda i: (0, 0), pipeline_mode=pl.Buffered(1))
    return pl.pallas_call(
        _out_kernel,
        grid=(m // tm,),
        in_specs=[row(DIL_OUT)] * 6 + [row(W_QB),
                  pl.BlockSpec((tm, D_MODEL), lambda i: (i, 0)),
                  pl.BlockSpec((tm, D_MODEL), lambda i: (i, 1)),
                  row(D_MODEL),
                  const((DIL_OUT, D_MODEL)), const((W_QB, D_MODEL)), const((D_MODEL, D_MODEL))],
        out_specs=row(D_MODEL),
        out_shape=jax.ShapeDtypeStruct((m, D_MODEL), F32),
        compiler_params=_cparams(("arbitrary",)),
        name="merge_out",
    )(*oa, *lse, yb, o_g, o_g, x2, woa, wob, wo)


def _ffn_kernel(x_ref, g_ref, wg_ref, wu_ref, wd_ref, gf_ref, out_ref, h_ref, *, nf, final):
    f = pl.program_id(1)

    @pl.when(f == 0)
    def _():
        x = x_ref[...]
        h_ref[...] = _rms(x, g_ref[...]).astype(BF16)
        out_ref[...] = x

    h = h_ref[...]
    a = jnp.dot(h, wg_ref[...], preferred_element_type=F32)
    u = jnp.dot(h, wu_ref[...], preferred_element_type=F32)
    act = (a * jax.nn.sigmoid(a) * u).astype(BF16)
    out_ref[...] += jnp.dot(act, wd_ref[...], preferred_element_type=F32)

    if final:
        @pl.when(f == nf - 1)
        def _():
            out_ref[...] = _rms(out_ref[...], gf_ref[...])


def _ffn(x2, g, wg, wu, wd, gf, tm, tf, final):
    m = x2.shape[0]
    nf = D_FF // tf
    return pl.pallas_call(
        functools.partial(_ffn_kernel, nf=nf, final=final),
        grid=(m // tm, nf),
        in_specs=[pl.BlockSpec((tm, D_MODEL), lambda i, f: (i, 0)),
                  pl.BlockSpec((1, D_MODEL), lambda i, f: (0, 0)),
                  pl.BlockSpec((D_MODEL, tf), lambda i, f: (0, f)),
                  pl.BlockSpec((D_MODEL, tf), lambda i, f: (0, f)),
                  pl.BlockSpec((tf, D_MODEL), lambda i, f: (f, 0)),
                  pl.BlockSpec((1, D_MODEL), lambda i, f: (0, 0))],
        out_specs=pl.BlockSpec((tm, D_MODEL), lambda i, f: (i, 0)),
        out_shape=jax.ShapeDtypeStruct((m, D_MODEL), F32),
        scratch_shapes=[pltpu.VMEM((tm, D_MODEL), BF16)],
        compiler_params=_cparams(("arbitrary", "arbitrary")),
        name="ffn",
    )(x2, g, wg, wu, wd, gf)


def _prep_w_in(w):
    cols = (W_QA, W_QA, W_QA, W_QB, W_KV, W_KV, W_KV, W_KV, W_KV, W_KV,
            NSA_Q_HEADS * 3, D_MODEL, D_MODEL)
    offs = np.concatenate([[0], np.cumsum(cols)])
    seg = [w[:, int(offs[i]):int(offs[i + 1])] for i in range(len(cols))]
    qa, ka, va, qb, kc, vc, ks, vs, kw, vw, gate_b, gam, gbm = seg
    gate_b = gate_b.reshape(D_MODEL, NSA_G, NSA_R, 3).transpose(0, 1, 3, 2).reshape(D_MODEL, NSA_G, 3 * NSA_R)
    gate_b = jnp.pad(gate_b, ((0, 0), (0, 0), (0, DH - 3 * NSA_R))).reshape(D_MODEL, NSA_G * DH)
    return jnp.concatenate([qa, ka, qb, ks, kw, va, vs, vw, kc, vc, gam, gbm, gate_b],
                           axis=1).astype(BF16)


def kernel(x, positions, ln_mix, w_in, cmp_pos_k, cmp_pos_v, cmp_w1_k, cmp_w2_k, cmp_w1_v, cmp_w2_v,
           w_out_a, w_out_b, w_out, ln_ffn, w_ffn_gate, w_ffn_up, w_ffn_down, ln_final):
    b, s, d = x.shape
    depth = w_in.shape[0]
    assert d == D_MODEL and s % (DIL_GROUPS[-1][1] * BLOCK) == 0
    m = b * s
    tm_in = min(1024, m)
    tm_out = min(256, m)
    tm_ffn = min(512, m)
    tq, tk = 128, min(512, s)
    ncp = s // CMP_STRIDE
    n_c = (s - CMP_LEN) // CMP_STRIDE + 1

    pos_f = positions.astype(F32)
    cos, sin = _rope_tables(pos_f.reshape(m, 1), tm_in)
    blk_end = np.minimum(np.arange(ncp) * CMP_STRIDE + CMP_LEN - 1, s - 1)
    cosc, sinc = _rope_tables(pos_f[:, blk_end].reshape(b * ncp, 1), ncp)
    cosc = cosc.reshape(b, ncp, DH)
    sinc = sinc.reshape(b, ncp, DH)
    del n_c

    scale = DH ** -0.5
    cs = np.ones((1, N_R), np.float32)
    cs[:, :W_QA] = scale
    cs[:, 2 * W_QA:2 * W_QA + W_QB] = scale
    cs = jnp.asarray(cs)

    x2 = x.reshape(m, d)
    for l in range(depth):
        w_l = _prep_w_in(w_in[l])
        o_r, o_p, o_c, o_g = _inproj(x2, ln_mix[l][None, :], w_l, cs, cos, sin, tm_in)
        oa, lse = [], []
        for gi, (_, dil) in enumerate(DIL_GROUPS):
            o, ls = _dilated(o_r, o_p, b, s, gi, dil)
            oa.append(o)
            lse.append(ls)
        kcmp, vcmp = _compress(
            o_c, b, s, cmp_pos_k[l], cmp_pos_v[l],
            cmp_w1_k[l].reshape(CMP_LEN, DH, CMP_HIDDEN).astype(BF16),
            cmp_w1_v[l].reshape(CMP_LEN, DH, CMP_HIDDEN).astype(BF16),
            cmp_w2_k[l].astype(BF16), cmp_w2_v[l].astype(BF16), cosc, sinc)
        yb = _nsa(o_r, o_p, o_g, kcmp, vcmp, b, s, tq, tk)
        x2 = _merge_out(oa, lse, yb, o_g, x2, w_out_a[l].astype(BF16), w_out_b[l].astype(BF16),
                        w_out[l].astype(BF16), tm_out)
        x2 = _ffn(x2, ln_ffn[l][None, :], w_ffn_gate[l].astype(BF16), w_ffn_up[l].astype(BF16),
                  w_ffn_down[l].astype(BF16), ln_final[None, :], tm_ffn, 512, l == depth - 1)
    return x2.reshape(b, s, d)
```

```python
import functools
import math

import numpy as np
import jax
import jax.numpy as jnp
from jax import lax
from jax.experimental import pallas as pl
from jax.experimental.pallas import tpu as pltpu

F32 = jnp.float32
BF16 = jnp.bfloat16

D_MODEL = 2048
DH = 128
HALF = DH // 2
ROPE_THETA = 10000.0
NORM_EPS = 1e-6
NEG_INF = -1e30
BLOCK = 128

DIL_GROUPS = ((128, 1), (512, 4), (2048, 16))
DIL_HPG = 4
DIL_HEADS = DIL_HPG * len(DIL_GROUPS)
DIL_OUT = DIL_HPG * DH
DIL_UNIT = DIL_GROUPS[-1][1] * BLOCK

NSA_Q_HEADS = 16
NSA_G = 2
NSA_R = NSA_Q_HEADS // NSA_G
CMP_LEN = 32
CMP_STRIDE = 16
CMP_HIDDEN = 256
SEL_LEN = 64
SEL_TOPK = 16
WIN_LEN = 512
FORCE_BONUS = 1e4
D_FF = 5632

W_QA = DIL_HEADS * DH
W_QB = NSA_Q_HEADS * DH
W_KV = NSA_G * DH
N_A = 3 * W_QA
N_R = W_QB + 2 * W_KV
N_P = 2 * W_KV
N_C = 2 * W_KV
N_G = 2 * D_MODEL + NSA_G * DH
N_ALL = N_A + N_R + N_P + N_C + N_G
TN_IN = 256
VMEM_LIMIT = 56 * 1024 * 1024
Q_SCALE = DH ** -0.5 * math.log2(math.e)


def _cparams(sem):
    return pltpu.CompilerParams(dimension_semantics=sem, vmem_limit_bytes=VMEM_LIMIT)


def _dot_nt(a, b):
    return lax.dot_general(a, b, (((1,), (1,)), ((), ())), preferred_element_type=F32)


def _dot(a, b):
    return jnp.dot(a, b, preferred_element_type=F32)


def _rope_tab_kernel(pos_ref, inv_ref, sgn_ref, cos_ref, sin_ref):
    ang = pos_ref[...] * inv_ref[...]
    cos_ref[...] = jnp.cos(ang)
    sin_ref[...] = jnp.sin(ang) * sgn_ref[...]


def _rope_tables(pos_f, tm):
    m = pos_f.shape[0]
    inv = ROPE_THETA ** (-2.0 * jnp.arange(HALF, dtype=F32) / DH)
    inv = jnp.concatenate([inv, inv])[None, :]
    sgn = jnp.concatenate([-jnp.ones((HALF,), F32), jnp.ones((HALF,), F32)])[None, :]
    return pl.pallas_call(
        _rope_tab_kernel,
        grid=(m // tm,),
        in_specs=[pl.BlockSpec((tm, 1), lambda i: (i, 0)),
                  pl.BlockSpec((1, DH), lambda i: (0, 0)),
                  pl.BlockSpec((1, DH), lambda i: (0, 0))],
        out_specs=[pl.BlockSpec((tm, DH), lambda i: (i, 0)),
                   pl.BlockSpec((tm, DH), lambda i: (i, 0))],
        out_shape=[jax.ShapeDtypeStruct((m, DH), F32)] * 2,
        compiler_params=_cparams(("arbitrary",)),
        name="rope_tables",
    )(pos_f, inv, sgn)


def _rope(a, cos, sin_signed):
    return a * cos + pltpu.roll(a, HALF, 1) * sin_signed


def _rms(x, g):
    ms = jnp.mean(x * x, axis=-1, keepdims=True)
    return x * lax.rsqrt(ms + NORM_EPS) * g


T_AROPE = 2 * W_QA // TN_IN
T_A = N_A // TN_IN
T_R = T_A + N_R // TN_IN
T_P = T_R + N_P // TN_IN
T_C = T_P + N_C // TN_IN
T_ALL = N_ALL // TN_IN


def _inproj_kernel(x_ref, g_ref, w_ref, cs_ref, cos_ref, sin_ref,
                   oa_ref, or_ref, op_ref, oc_ref, og_ref, h_ref):
    j = pl.program_id(1)

    @pl.when(j == 0)
    def _():
        h_ref[...] = _rms(x_ref[...], g_ref[...]).astype(BF16)

    acc = _dot(h_ref[...], w_ref[...])

    def roped(out_ref):
        cos = cos_ref[...]
        sin = sin_ref[...]
        for hh in range(TN_IN // DH):
            sl = slice(hh * DH, (hh + 1) * DH)
            r = _rope(acc[:, sl], cos, sin) * cs_ref[:, sl]
            out_ref[:, sl] = r.astype(out_ref.dtype)

    @pl.when(j < T_AROPE)
    def _():
        roped(oa_ref)

    @pl.when((j >= T_AROPE) & (j < T_A))
    def _():
        oa_ref[...] = acc

    @pl.when((j >= T_A) & (j < T_R))
    def _():
        roped(or_ref)

    @pl.when((j >= T_R) & (j < T_P))
    def _():
        op_ref[...] = acc.astype(BF16)

    @pl.when((j >= T_P) & (j < T_C))
    def _():
        oc_ref[...] = acc

    @pl.when(j >= T_C)
    def _():
        og_ref[...] = jax.nn.sigmoid(acc)


def _inproj(x2, g, w, cs, cos, sin, tm):
    m = x2.shape[0]

    def region(lo, hi):
        return pl.BlockSpec((tm, TN_IN), lambda i, j: (i, jnp.clip(j - lo, 0, hi - lo - 1)))

    return pl.pallas_call(
        _inproj_kernel,
        grid=(m // tm, T_ALL),
        in_specs=[pl.BlockSpec((tm, D_MODEL), lambda i, j: (i, 0)),
                  pl.BlockSpec((1, D_MODEL), lambda i, j: (0, 0)),
                  pl.BlockSpec((D_MODEL, TN_IN), lambda i, j: (0, j)),
                  pl.BlockSpec((1, TN_IN), lambda i, j: (0, j)),
                  pl.BlockSpec((tm, DH), lambda i, j: (i, 0)),
                  pl.BlockSpec((tm, DH), lambda i, j: (i, 0))],
        out_specs=[region(0, T_A), region(T_A, T_R), region(T_R, T_P), region(T_P, T_C),
                   region(T_C, T_ALL)],
        out_shape=[jax.ShapeDtypeStruct((m, N_A), F32),
                   jax.ShapeDtypeStruct((m, N_R), BF16),
                   jax.ShapeDtypeStruct((m, N_P), BF16),
                   jax.ShapeDtypeStruct((m, N_C), F32),
                   jax.ShapeDtypeStruct((m, N_G), F32)],
        scratch_shapes=[pltpu.VMEM((tm, D_MODEL), BF16)],
        compiler_params=_cparams(("arbitrary", "arbitrary")),
        name="inproj",
    )(x2, g, w, cs, cos, sin)


def _rows(start, dil):
    return pl.ds(start, BLOCK) if dil == 1 else pl.ds(start, BLOCK, stride=dil)


def _dil_kernel(*refs):
    out_ref, o_scr, lse_scr = refs[-3:]
    u = pl.program_id(1)
    row = lax.broadcasted_iota(jnp.int32, (BLOCK, 2 * BLOCK), 0)
    col = lax.broadcasted_iota(jnp.int32, (BLOCK, 2 * BLOCK), 1)
    band = ((col < BLOCK) & (col >= row)) | ((col >= BLOCK) & (col - BLOCK <= row))
    band_first = band & ((col >= BLOCK) | (u > 0))
    for gi, (_, dil) in enumerate(DIL_GROUPS):
        q_ref, kc_ref, kp_ref, vc_ref, vp_ref = refs[5 * gi:5 * gi + 5]
        span = BLOCK * dil
        for rho in range(dil):
            for ub in range(DIL_UNIT // span):
                cur = _rows(ub * span + rho, dil)
                if ub == 0:
                    kp, vp, mask = kp_ref[_rows(rho, dil), :], vp_ref[_rows(rho, dil), :], band_first
                else:
                    prv = _rows((ub - 1) * span + rho, dil)
                    kp, vp, mask = kc_ref[prv, :], vc_ref[prv, :], band
                q = q_ref[cur, :].astype(BF16)
                k = jnp.concatenate([kp, kc_ref[cur, :]], axis=0).astype(BF16)
                v = jnp.concatenate([vp, vc_ref[cur, :]], axis=0).astype(BF16)
                s = jnp.where(mask, _dot_nt(q, k), NEG_INF)
                m = jnp.max(s, axis=-1, keepdims=True)
                p = jnp.exp2(s - m)
                l = jnp.sum(p, axis=-1, keepdims=True)
                o_scr[gi, cur, :] = _dot(p.astype(BF16), v) / l
                lse_scr[gi, cur, :] = jnp.broadcast_to(m + jnp.log2(l), (BLOCK, DH))
    l0, l1, l2 = lse_scr[0], lse_scr[1], lse_scr[2]
    m = jnp.maximum(jnp.maximum(l0, l1), l2)
    e0, e1, e2 = jnp.exp2(l0 - m), jnp.exp2(l1 - m), jnp.exp2(l2 - m)
    ya = (e0 * o_scr[0] + e1 * o_scr[1] + e2 * o_scr[2]) / (e0 + e1 + e2)
    out_ref[...] = ya.astype(BF16)


def _dilated(o_a, b, s):
    a_view = o_a.reshape(b, s, N_A)
    in_specs, args = [], []
    for gi, (_, dil) in enumerate(DIL_GROUPS):
        span = BLOCK * dil
        per = DIL_UNIT // span

        def cur(colbase, gi=gi):
            return pl.BlockSpec((None, DIL_UNIT, DH),
                                lambda bi, u, j: (bi, u, colbase + gi * DIL_HPG + j))

        def prev(colbase, gi=gi, span=span, per=per):
            return pl.BlockSpec((None, span, DH),
                                lambda bi, u, j: (bi, jnp.maximum(u * per - 1, 0),
                                                  colbase + gi * DIL_HPG + j))

        kcol, vcol = W_QA // DH, 2 * W_QA // DH
        in_specs += [cur(0), cur(kcol), prev(kcol), cur(vcol), prev(vcol)]
        args += [a_view] * 5
    out = pl.pallas_call(
        _dil_kernel,
        grid=(b, s // DIL_UNIT, DIL_HPG),
        in_specs=in_specs,
        out_specs=pl.BlockSpec((None, DIL_UNIT, DH), lambda bi, u, j: (bi, u, j)),
        out_shape=jax.ShapeDtypeStruct((b, s, DIL_OUT), BF16),
        scratch_shapes=[pltpu.VMEM((len(DIL_GROUPS), DIL_UNIT, DH), F32),
                        pltpu.VMEM((len(DIL_GROUPS), DIL_UNIT, DH), F32)],
        compiler_params=_cparams(("arbitrary", "arbitrary", "arbitrary")),
        name="dilated",
    )(*args)
    return out.reshape(b * s, DIL_OUT)


def _cmp_kernel(x_ref, posk_ref, posv_ref, w1k_ref, w1v_ref, w2k_ref, w2v_ref,
                cos_ref, sin_ref, kc_ref, vc_ref, *, ncp):
    half = CMP_LEN // 2
    for kind in range(2):
        pos_ref, w1_ref, w2_ref, o_ref = ((posk_ref, w1k_ref, w2k_ref, kc_ref) if kind == 0
                                          else (posv_ref, w1v_ref, w2v_ref, vc_ref))
        for g in range(NSA_G):
            off = kind * W_KV + g * DH
            a = jnp.zeros((ncp, CMP_HIDDEN), F32)
            bm = jnp.zeros((ncp, CMP_HIDDEN), F32)
            for l in range(half):
                x = x_ref[:, l * N_C + off:l * N_C + off + DH]
                a = a + _dot((x + pos_ref[l:l + 1, :]).astype(BF16), w1_ref[l])
                bm = bm + _dot((x + pos_ref[half + l:half + l + 1, :]).astype(BF16),
                               w1_ref[half + l])
            hid = a + pltpu.roll(bm, ncp - 1, 0)
            hid = jax.nn.gelu(hid, approximate=True).astype(BF16)
            out = _dot(hid, w2_ref[...])
            if kind == 0:
                out = _rope(out, cos_ref[...], sin_ref[...])
            o_ref[g] = out.astype(BF16)


def _compress(o_c, b, s, posk, posv, w1k, w1v, w2k, w2v, cosc, sinc):
    ncp = s // CMP_STRIDE
    x_view = o_c.reshape(b, ncp, CMP_STRIDE * N_C)
    full = lambda shape: pl.BlockSpec(shape, lambda bi: (0,) * len(shape))
    return pl.pallas_call(
        functools.partial(_cmp_kernel, ncp=ncp),
        grid=(b,),
        in_specs=[pl.BlockSpec((None, ncp, CMP_STRIDE * N_C), lambda bi: (bi, 0, 0)),
                  full((CMP_LEN, DH)), full((CMP_LEN, DH)),
                  full((CMP_LEN, DH, CMP_HIDDEN)), full((CMP_LEN, DH, CMP_HIDDEN)),
                  full((CMP_HIDDEN, DH)), full((CMP_HIDDEN, DH)),
                  pl.BlockSpec((None, ncp, DH), lambda bi: (bi, 0, 0)),
                  pl.BlockSpec((None, ncp, DH), lambda bi: (bi, 0, 0))],
        out_specs=[pl.BlockSpec((None, NSA_G, ncp, DH), lambda bi: (bi, 0, 0, 0)),
                   pl.BlockSpec((None, NSA_G, ncp, DH), lambda bi: (bi, 0, 0, 0))],
        out_shape=[jax.ShapeDtypeStruct((b, NSA_G, ncp, DH), BF16)] * 2,
        compiler_params=_cparams(("arbitrary",)),
        name="nsa_compress",
    )(x_view, posk, posv, w1k, w1v, w2k, w2v, cosc, sinc)


def _nsa_kernel(q_ref, ks_ref, vs_ref, kw_ref, vw_ref, kc_ref, vc_ref, gate_ref, ovt_ref, et_ref,
                out_ref, qa_ref, ka_ref, va_ref, wa_ref, p_ref, m_ref, al_ref, acc_ref,
                *, tq, tk, s_len, ncp, n_s):
    R = NSA_R
    qi = pl.program_id(2)
    t0 = qi * tq

    @pl.when(qi == 0)
    def _():
        ones = jnp.ones((s_len, DH), BF16)
        ka_ref[:, :DH] = ks_ref[...]
        ka_ref[:, DH:] = et_ref[...]
        va_ref[:, :DH] = vs_ref[...]
        va_ref[:, DH:] = ones
        wa_ref[:, :DH] = vw_ref[...]
        wa_ref[:, DH:] = ones

    for r in range(R):
        qa_ref[r * tq:(r + 1) * tq, :DH] = q_ref[:, r * DH:(r + 1) * DH]
    q = qa_ref[:, :DH]
    trow = t0 + lax.broadcasted_iota(jnp.int32, (tq, 1), 0)
    tlane = t0 + lax.broadcasted_iota(jnp.int32, (1, tq), 1)
    head = lambda a, r: a[r * tq:(r + 1) * tq]

    cend = lax.broadcasted_iota(jnp.int32, (1, ncp), 1) * CMP_STRIDE + (CMP_LEN - 1)
    valid_c = cend <= trow
    bias_c = jnp.where(valid_c, 0.0, NEG_INF)
    keep_c = jnp.where(valid_c, 1.0, 0.0)
    s = _dot_nt(q, kc_ref[...])
    psum = jnp.zeros((tq, ncp), F32)
    for r in range(R):
        s_r = head(s, r) + bias_c
        p = jnp.exp2(s_r - jnp.max(s_r, axis=-1, keepdims=True)) * keep_c
        p = p / jnp.maximum(jnp.sum(p, axis=-1, keepdims=True), 1e-30)
        psum = psum + p
        p_ref[r * tq:(r + 1) * tq, :ncp] = p.astype(BF16)
    o_cmp = _dot(p_ref[:, :ncp], vc_ref[...])

    hi = psum.astype(BF16)
    rem = psum - hi.astype(F32)
    mid = rem.astype(BF16)
    lo = (rem - mid.astype(F32)).astype(BF16)
    ovt = ovt_ref[...]
    p_slc = _dot_nt(ovt, hi) + _dot_nt(ovt, mid) + _dot_nt(ovt, lo)
    jj = lax.broadcasted_iota(jnp.int32, (n_s, 1), 0)
    blk_t = tlane // SEL_LEN
    forced = (jj == 0) | (jj == blk_t) | (jj == blk_t - 1)
    score = jnp.where(jj <= blk_t, p_slc + jnp.where(forced, FORCE_BONUS, 0.0), -1.0)
    nch = n_s // 8
    chunks = [score[c * 8:(c + 1) * 8] for c in range(nch)]
    cnt = [jnp.zeros((8, tq), F32) for _ in range(nch)]
    sub = lax.broadcasted_iota(jnp.int32, (8, 1), 0)
    for i in range(n_s):
        row_i = score[i:i + 1]
        for c in range(nch):
            if c * 8 > i:
                beats = row_i >= chunks[c]
            elif c * 8 + 7 < i:
                beats = row_i > chunks[c]
            else:
                beats = (row_i > chunks[c]) | ((row_i == chunks[c]) & (sub + c * 8 > i))
            cnt[c] = cnt[c] + jnp.where(beats, 1.0, 0.0)
    k_sel = float(min(SEL_TOPK, n_s))
    bias_t = jnp.concatenate([jnp.where(cc < k_sel, 0.0, NEG_INF) for cc in cnt] +
                             [jnp.zeros((DH - n_s, tq), F32)], axis=0)
    bias_q = bias_t.T.astype(BF16)
    for r in range(R):
        qa_ref[r * tq:(r + 1) * tq, DH:] = bias_q

    m_ref[...] = jnp.full(m_ref.shape, NEG_INF, F32)
    acc_ref[...] = jnp.zeros(acc_ref.shape, F32)

    def tile(kt, causal):
        k0 = pl.multiple_of(kt * tk, tk)
        s = _dot_nt(qa_ref[...], ka_ref[pl.ds(k0, tk), :])
        if causal:
            kpos = k0 + lax.broadcasted_iota(jnp.int32, (1, tk), 1)
            bias_d = jnp.where(kpos <= trow, 0.0, NEG_INF)
        for r in range(R):
            s_r = head(s, r)
            if causal:
                s_r = s_r + bias_d
            m_old = m_ref[r]
            m_new = jnp.maximum(m_old, jnp.max(s_r, axis=-1, keepdims=True))
            p_ref[r * tq:(r + 1) * tq, :tk] = jnp.exp2(s_r - m_new).astype(BF16)
            al_ref[r] = jnp.exp2(m_old - m_new)
            m_ref[r] = m_new
        pv = _dot(p_ref[:, :tk], va_ref[pl.ds(k0, tk), :])
        acc_ref[...] = al_ref[...] * acc_ref[...] + pv.reshape(R, tq, 2 * DH)

    kd = t0 // tk
    lax.fori_loop(0, kd, lambda kt, c: (tile(kt, False), c)[1], 0)
    tile(kd, True)
    acc = acc_ref[...]
    o_slc = (acc[:, :, :DH] / acc[:, :, DH:DH + 1]).reshape(R * tq, DH)

    wlen = (-(-(WIN_LEN - 1) // BLOCK)) * BLOCK + tq
    w0 = pl.multiple_of(jnp.maximum(t0 + tq - wlen, 0), BLOCK)
    kpos = w0 + lax.broadcasted_iota(jnp.int32, (1, wlen), 1)
    bias_w = jnp.where((kpos <= trow) & (trow - kpos <= WIN_LEN - 1), 0.0, NEG_INF)
    s = _dot_nt(q, kw_ref[pl.ds(w0, wlen), :])
    for r in range(R):
        s_r = head(s, r) + bias_w
        p_ref[r * tq:(r + 1) * tq, :wlen] = jnp.exp2(
            s_r - jnp.max(s_r, axis=-1, keepdims=True)).astype(BF16)
    ow = _dot(p_ref[:, :wlen], wa_ref[pl.ds(w0, wlen), :])
    o_win = ow[:, :DH] / ow[:, DH:DH + 1]

    gate = gate_ref[...]
    for r in range(R):
        y = (gate[:, r:r + 1] * head(o_cmp, r) +
             gate[:, R + r:R + r + 1] * head(o_slc, r) +
             gate[:, 2 * R + r:2 * R + r + 1] * head(o_win, r))
        out_ref[:, r * DH:(r + 1) * DH] = y.astype(BF16)


def _nsa(o_r, o_p, o_g, kcmp, vcmp, b, s, tq, tk):
    ncp = s // CMP_STRIDE
    n_s = s // SEL_LEN
    assert n_s % 8 == 0 and n_s <= DH
    r_view = o_r.reshape(b, s, N_R)
    p_view = o_p.reshape(b, s, N_P)
    g_view = o_g.reshape(b, s, N_G)
    c = np.arange(ncp)[None, :]
    j = np.arange(n_s)[:, None]
    n_c = (s - CMP_LEN) // CMP_STRIDE + 1
    ovt = ((c * CMP_STRIDE <= j * SEL_LEN + SEL_LEN - 1) &
           (c * CMP_STRIDE + CMP_LEN - 1 >= j * SEL_LEN) & (c < n_c))
    ovt = jnp.asarray(ovt.astype(np.float32), BF16)
    et = jnp.asarray((np.arange(s)[:, None] // SEL_LEN == np.arange(DH)[None, :])
                     .astype(np.float32), BF16)
    wq = NSA_R * DH
    ks_col = W_QB // DH
    kw_col = ks_col + NSA_G
    gate_col = (2 * D_MODEL) // DH
    wlen = (-(-(WIN_LEN - 1) // BLOCK)) * BLOCK + tq
    kv = (None, s, DH)
    out = pl.pallas_call(
        functools.partial(_nsa_kernel, tq=tq, tk=tk, s_len=s, ncp=ncp, n_s=n_s),
        grid=(b, NSA_G, s // tq),
        in_specs=[pl.BlockSpec((None, tq, wq), lambda bi, g, qi: (bi, qi, g)),
                  pl.BlockSpec(kv, lambda bi, g, qi: (bi, 0, ks_col + g)),
                  pl.BlockSpec(kv, lambda bi, g, qi: (bi, 0, g)),
                  pl.BlockSpec(kv, lambda bi, g, qi: (bi, 0, kw_col + g)),
                  pl.BlockSpec(kv, lambda bi, g, qi: (bi, 0, NSA_G + g)),
                  pl.BlockSpec((None, None, ncp, DH), lambda bi, g, qi: (bi, g, 0, 0)),
                  pl.BlockSpec((None, None, ncp, DH), lambda bi, g, qi: (bi, g, 0, 0)),
                  pl.BlockSpec((None, tq, DH), lambda bi, g, qi: (bi, qi, gate_col + g)),
                  pl.BlockSpec((n_s, ncp), lambda bi, g, qi: (0, 0)),
                  pl.BlockSpec((s, DH), lambda bi, g, qi: (0, 0))],
        out_specs=pl.BlockSpec((None, tq, wq), lambda bi, g, qi: (bi, qi, g)),
        out_shape=jax.ShapeDtypeStruct((b, s, W_QB), BF16),
        scratch_shapes=[pltpu.VMEM((NSA_R * tq, 2 * DH), BF16),
                        pltpu.VMEM((s, 2 * DH), BF16),
                        pltpu.VMEM((s, 2 * DH), BF16),
                        pltpu.VMEM((s, 2 * DH), BF16),
                        pltpu.VMEM((NSA_R * tq, max(tk, wlen, ncp)), BF16),
                        pltpu.VMEM((NSA_R, tq, 1), F32),
                        pltpu.VMEM((NSA_R, tq, 1), F32),
                        pltpu.VMEM((NSA_R, tq, 2 * DH), F32)],
        compiler_params=_cparams(("arbitrary", "arbitrary", "arbitrary")),
        name="nsa_attention",
    )(r_view, r_view, p_view, r_view, p_view, kcmp, vcmp, g_view, ovt, et)
    return out.reshape(b * s, W_QB)


def _out_kernel(ya_ref, yb_ref, ga_ref, gb_ref, x_ref, woa_ref, wob_ref, wo_ref, out_ref):
    pa = _dot(ya_ref[...], woa_ref[...])
    pb = _dot(yb_ref[...], wob_ref[...])
    y = (ga_ref[...] * pa + gb_ref[...] * pb).astype(BF16)
    out_ref[...] = x_ref[...] + _dot(y, wo_ref[...])


def _merge_out(ya, yb, o_g, x2, woa, wob, wo, tm):
    m = x2.shape[0]
    row = lambda w: pl.BlockSpec((tm, w), lambda i: (i, 0))
    const = lambda shape: pl.BlockSpec(shape, lambda i: (0, 0), pipeline_mode=pl.Buffered(1))
    return pl.pallas_call(
        _out_kernel,
        grid=(m // tm,),
        in_specs=[row(DIL_OUT), row(W_QB),
                  pl.BlockSpec((tm, D_MODEL), lambda i: (i, 0)),
                  pl.BlockSpec((tm, D_MODEL), lambda i: (i, 1)),
                  row(D_MODEL),
                  const((DIL_OUT, D_MODEL)), const((W_QB, D_MODEL)), const((D_MODEL, D_MODEL))],
        out_specs=row(D_MODEL),
        out_shape=jax.ShapeDtypeStruct((m, D_MODEL), F32),
        compiler_params=_cparams(("arbitrary",)),
        name="merge_out",
    )(ya, yb, o_g, o_g, x2, woa, wob, wo)


def _ffn_kernel(x_ref, g_ref, wg_ref, wu_ref, wd_ref, gf_ref, out_ref, h_ref, *, nf, final):
    f = pl.program_id(1)

    @pl.when(f == 0)
    def _():
        x = x_ref[...]
        h_ref[...] = _rms(x, g_ref[...]).astype(BF16)
        out_ref[...] = x

    h = h_ref[...]
    a = _dot(h, wg_ref[...])
    u = _dot(h, wu_ref[...])
    act = (a * jax.nn.sigmoid(a) * u).astype(BF16)
    out_ref[...] += _dot(act, wd_ref[...])

    if final:
        @pl.when(f == nf - 1)
        def _():
            out_ref[...] = _rms(out_ref[...], gf_ref[...])


def _ffn(x2, g, wg, wu, wd, gf, tm, tf, final):
    m = x2.shape[0]
    nf = D_FF // tf
    return pl.pallas_call(
        functools.partial(_ffn_kernel, nf=nf, final=final),
        grid=(m // tm, nf),
        in_specs=[pl.BlockSpec((tm, D_MODEL), lambda i, f: (i, 0)),
                  pl.BlockSpec((1, D_MODEL), lambda i, f: (0, 0)),
                  pl.BlockSpec((D_MODEL, tf), lambda i, f: (0, f)),
                  pl.BlockSpec((D_MODEL, tf), lambda i, f: (0, f)),
                  pl.BlockSpec((tf, D_MODEL), lambda i, f: (f, 0)),
                  pl.BlockSpec((1, D_MODEL), lambda i, f: (0, 0))],
        out_specs=pl.BlockSpec((tm, D_MODEL), lambda i, f: (i, 0)),
        out_shape=jax.ShapeDtypeStruct((m, D_MODEL), F32),
        scratch_shapes=[pltpu.VMEM((tm, D_MODEL), BF16)],
        compiler_params=_cparams(("arbitrary", "arbitrary")),
        name="ffn",
    )(x2, g, wg, wu, wd, gf)


def _prep_w_in(w):
    cols = (W_QA, W_QA, W_QA, W_QB, W_KV, W_KV, W_KV, W_KV, W_KV, W_KV,
            NSA_Q_HEADS * 3, D_MODEL, D_MODEL)
    offs = np.concatenate([[0], np.cumsum(cols)])
    seg = [w[:, int(offs[i]):int(offs[i + 1])] for i in range(len(cols))]
    qa, ka, va, qb, kc, vc, ks, vs, kw, vw, gate_b, gam, gbm = seg
    gate_b = gate_b.reshape(D_MODEL, NSA_G, NSA_R, 3).transpose(0, 1, 3, 2).reshape(D_MODEL, NSA_G, 3 * NSA_R)
    gate_b = jnp.pad(gate_b, ((0, 0), (0, 0), (0, DH - 3 * NSA_R))).reshape(D_MODEL, NSA_G * DH)
    return jnp.concatenate([qa, ka, va, qb, ks, kw, vs, vw, kc, vc, gam, gbm, gate_b],
                           axis=1).astype(BF16)


def kernel(x, positions, ln_mix, w_in, cmp_pos_k, cmp_pos_v, cmp_w1_k, cmp_w2_k, cmp_w1_v, cmp_w2_v,
           w_out_a, w_out_b, w_out, ln_ffn, w_ffn_gate, w_ffn_up, w_ffn_down, ln_final):
    b, s, d = x.shape
    depth = w_in.shape[0]
    assert d == D_MODEL and s % DIL_UNIT == 0
    m = b * s
    tm_in = min(1024, m)
    tm_out = min(256, m)
    tm_ffn = min(512, m)
    tq, tk = 128, min(512, s)
    ncp = s // CMP_STRIDE

    pos_f = positions.astype(F32)
    cos, sin = _rope_tables(pos_f.reshape(m, 1), tm_in)
    blk_end = np.minimum(np.arange(ncp) * CMP_STRIDE + CMP_LEN - 1, s - 1)
    cosc, sinc = _rope_tables(pos_f[:, blk_end].reshape(b * ncp, 1), ncp)
    cosc = cosc.reshape(b, ncp, DH)
    sinc = sinc.reshape(b, ncp, DH)

    cs = np.ones((1, N_ALL), np.float32)
    cs[:, :W_QA] = Q_SCALE
    cs[:, N_A:N_A + W_QB] = Q_SCALE
    cs = jnp.asarray(cs)

    x2 = x.reshape(m, d)
    for l in range(depth):
        w_l = _prep_w_in(w_in[l])
        o_a, o_r, o_p, o_c, o_g = _inproj(x2, ln_mix[l][None, :], w_l, cs, cos, sin, tm_in)
        ya = _dilated(o_a, b, s)
        kcmp, vcmp = _compress(
            o_c, b, s, cmp_pos_k[l], cmp_pos_v[l],
            cmp_w1_k[l].reshape(CMP_LEN, DH, CMP_HIDDEN).astype(BF16),
            cmp_w1_v[l].reshape(CMP_LEN, DH, CMP_HIDDEN).astype(BF16),
            cmp_w2_k[l].astype(BF16), cmp_w2_v[l].astype(BF16), cosc, sinc)
        yb = _nsa(o_r, o_p, o_g, kcmp, vcmp, b, s, tq, tk)
        x2 = _merge_out(ya, yb, o_g, x2, w_out_a[l].astype(BF16), w_out_b[l].astype(BF16),
                        w_out[l].astype(BF16), tm_out)
        x2 = _ffn(x2, ln_ffn[l][None, :], w_ffn_gate[l].astype(BF16), w_ffn_up[l].astype(BF16),
                  w_ffn_down[l].astype(BF16), ln_final[None, :], tm_ffn, 512, l == depth - 1)
    return x2.reshape(b, s, d)
```

```python
import functools
import math

import numpy as np
import jax
import jax.numpy as jnp
from jax import lax
from jax.experimental import pallas as pl
from jax.experimental.pallas import tpu as pltpu

F32 = jnp.float32
BF16 = jnp.bfloat16

D_MODEL = 2048
DH = 128
HALF = DH // 2
ROPE_THETA = 10000.0
NORM_EPS = 1e-6
NEG_INF = -1e30
BLOCK = 128

DIL_GROUPS = ((128, 1), (512, 4), (2048, 16))
DIL_HPG = 4
DIL_HEADS = DIL_HPG * len(DIL_GROUPS)
DIL_OUT = DIL_HPG * DH
DIL_UNIT = DIL_GROUPS[-1][1] * BLOCK

NSA_Q_HEADS = 16
NSA_G = 2
NSA_R = NSA_Q_HEADS // NSA_G
CMP_LEN = 32
CMP_STRIDE = 16
CMP_HIDDEN = 256
SEL_LEN = 64
SEL_TOPK = 16
WIN_LEN = 512
FORCE_BONUS = 1e4
D_FF = 5632

W_QA = DIL_HEADS * DH
W_QB = NSA_Q_HEADS * DH
W_KV = NSA_G * DH
N_A = 3 * W_QA
N_R = W_QB + 2 * W_KV
N_P = 2 * W_KV
N_C = 2 * W_KV
TN_IN = 512
N_G = 2 * D_MODEL + TN_IN
N_ALL = N_A + N_R + N_P + N_C + N_G
VMEM_LIMIT = 56 * 1024 * 1024
Q_SCALE = DH ** -0.5 * math.log2(math.e)


def _cparams(sem):
    return pltpu.CompilerParams(dimension_semantics=sem, vmem_limit_bytes=VMEM_LIMIT)


def _dot_nt(a, b):
    return lax.dot_general(a, b, (((1,), (1,)), ((), ())), preferred_element_type=F32)


def _dot(a, b):
    return jnp.dot(a, b, preferred_element_type=F32)


def _rope_tab_kernel(pos_ref, inv_ref, sgn_ref, cos_ref, sin_ref):
    ang = pos_ref[...] * inv_ref[...]
    cos_ref[...] = jnp.cos(ang)
    sin_ref[...] = jnp.sin(ang) * sgn_ref[...]


def _rope_tables(pos_f, tm):
    m = pos_f.shape[0]
    inv = ROPE_THETA ** (-2.0 * jnp.arange(HALF, dtype=F32) / DH)
    inv = jnp.concatenate([inv, inv])[None, :]
    sgn = jnp.concatenate([-jnp.ones((HALF,), F32), jnp.ones((HALF,), F32)])[None, :]
    return pl.pallas_call(
        _rope_tab_kernel,
        grid=(m // tm,),
        in_specs=[pl.BlockSpec((tm, 1), lambda i: (i, 0)),
                  pl.BlockSpec((1, DH), lambda i: (0, 0)),
                  pl.BlockSpec((1, DH), lambda i: (0, 0))],
        out_specs=[pl.BlockSpec((tm, DH), lambda i: (i, 0)),
                   pl.BlockSpec((tm, DH), lambda i: (i, 0))],
        out_shape=[jax.ShapeDtypeStruct((m, DH), F32)] * 2,
        compiler_params=_cparams(("arbitrary",)),
        name="rope_tables",
    )(pos_f, inv, sgn)


def _rope(a, cos, sin_signed):
    return a * cos + pltpu.roll(a, HALF, 1) * sin_signed


def _rms(x, g):
    ms = jnp.mean(x * x, axis=-1, keepdims=True)
    return x * lax.rsqrt(ms + NORM_EPS) * g


T_AROPE = 2 * W_QA // TN_IN
T_A = N_A // TN_IN
T_R = T_A + N_R // TN_IN
T_P = T_R + N_P // TN_IN
T_C = T_P + N_C // TN_IN
T_ALL = N_ALL // TN_IN


def _inproj_kernel(x_ref, g_ref, w_ref, cs_ref, cos_ref, sin_ref,
                   oa_ref, or_ref, op_ref, oc_ref, og_ref, h_ref):
    j = pl.program_id(1)

    @pl.when(j == 0)
    def _():
        h_ref[...] = _rms(x_ref[...], g_ref[...]).astype(BF16)

    acc = _dot(h_ref[...], w_ref[...])

    def roped(out_ref):
        cos = cos_ref[...]
        sin = sin_ref[...]
        for hh in range(TN_IN // DH):
            sl = slice(hh * DH, (hh + 1) * DH)
            r = _rope(acc[:, sl], cos, sin) * cs_ref[:, sl]
            out_ref[:, sl] = r.astype(out_ref.dtype)

    @pl.when(j < T_AROPE)
    def _():
        roped(oa_ref)

    @pl.when((j >= T_AROPE) & (j < T_A))
    def _():
        oa_ref[...] = acc

    @pl.when((j >= T_A) & (j < T_R))
    def _():
        roped(or_ref)

    @pl.when((j >= T_R) & (j < T_P))
    def _():
        op_ref[...] = acc.astype(BF16)

    @pl.when((j >= T_P) & (j < T_C))
    def _():
        oc_ref[...] = acc

    @pl.when(j >= T_C)
    def _():
        og_ref[...] = jax.nn.sigmoid(acc)


def _inproj(x2, g, w, cs, cos, sin, tm):
    m = x2.shape[0]

    def region(lo, hi):
        return pl.BlockSpec((tm, TN_IN), lambda i, j: (i, jnp.clip(j - lo, 0, hi - lo - 1)))

    return pl.pallas_call(
        _inproj_kernel,
        grid=(m // tm, T_ALL),
        in_specs=[pl.BlockSpec((tm, D_MODEL), lambda i, j: (i, 0)),
                  pl.BlockSpec((1, D_MODEL), lambda i, j: (0, 0)),
                  pl.BlockSpec((D_MODEL, TN_IN), lambda i, j: (0, j)),
                  pl.BlockSpec((1, TN_IN), lambda i, j: (0, j)),
                  pl.BlockSpec((tm, DH), lambda i, j: (i, 0)),
                  pl.BlockSpec((tm, DH), lambda i, j: (i, 0))],
        out_specs=[region(0, T_A), region(T_A, T_R), region(T_R, T_P), region(T_P, T_C),
                   region(T_C, T_ALL)],
        out_shape=[jax.ShapeDtypeStruct((m, N_A), F32),
                   jax.ShapeDtypeStruct((m, N_R), BF16),
                   jax.ShapeDtypeStruct((m, N_P), BF16),
                   jax.ShapeDtypeStruct((m, N_C), F32),
                   jax.ShapeDtypeStruct((m, N_G), F32)],
        scratch_shapes=[pltpu.VMEM((tm, D_MODEL), BF16)],
        compiler_params=_cparams(("arbitrary", "arbitrary")),
        name="inproj",
    )(x2, g, w, cs, cos, sin)


def _rows(start, dil):
    return pl.ds(start, BLOCK) if dil == 1 else pl.ds(start, BLOCK, stride=dil)


def _dil_kernel(*refs):
    out_ref, o_scr, lse_scr = refs[-3:]
    u = pl.program_id(1)
    row = lax.broadcasted_iota(jnp.int32, (BLOCK, 2 * BLOCK), 0)
    col = lax.broadcasted_iota(jnp.int32, (BLOCK, 2 * BLOCK), 1)
    band = ((col < BLOCK) & (col >= row)) | ((col >= BLOCK) & (col - BLOCK <= row))
    band_first = band & ((col >= BLOCK) | (u > 0))
    for gi, (_, dil) in enumerate(DIL_GROUPS):
        q_ref, kc_ref, kp_ref, vc_ref, vp_ref = refs[5 * gi:5 * gi + 5]
        span = BLOCK * dil
        for rho in range(dil):
            for ub in range(DIL_UNIT // span):
                cur = _rows(ub * span + rho, dil)
                if ub == 0:
                    kp, vp, mask = kp_ref[_rows(rho, dil), :], vp_ref[_rows(rho, dil), :], band_first
                else:
                    prv = _rows((ub - 1) * span + rho, dil)
                    kp, vp, mask = kc_ref[prv, :], vc_ref[prv, :], band
                q = q_ref[cur, :].astype(BF16)
                k = jnp.concatenate([kp, kc_ref[cur, :]], axis=0).astype(BF16)
                v = jnp.concatenate([vp, vc_ref[cur, :]], axis=0).astype(BF16)
                s = jnp.where(mask, _dot_nt(q, k), NEG_INF)
                m = jnp.max(s, axis=-1, keepdims=True)
                p = jnp.exp2(s - m)
                l = jnp.sum(p, axis=-1, keepdims=True)
                o_scr[gi, cur, :] = _dot(p.astype(BF16), v) / l
                lse_scr[gi, cur, :] = jnp.broadcast_to(m + jnp.log2(l), (BLOCK, DH))
    l0, l1, l2 = lse_scr[0], lse_scr[1], lse_scr[2]
    m = jnp.maximum(jnp.maximum(l0, l1), l2)
    e0, e1, e2 = jnp.exp2(l0 - m), jnp.exp2(l1 - m), jnp.exp2(l2 - m)
    ya = (e0 * o_scr[0] + e1 * o_scr[1] + e2 * o_scr[2]) / (e0 + e1 + e2)
    out_ref[...] = ya.astype(BF16)


def _dilated(o_a, b, s):
    a_view = o_a.reshape(b, s, N_A)
    in_specs, args = [], []
    for gi, (_, dil) in enumerate(DIL_GROUPS):
        span = BLOCK * dil
        per = DIL_UNIT // span

        def cur(colbase, gi=gi):
            return pl.BlockSpec((None, DIL_UNIT, DH),
                                lambda bi, u, j: (bi, u, colbase + gi * DIL_HPG + j))

        def prev(colbase, gi=gi, span=span, per=per):
            return pl.BlockSpec((None, span, DH),
                                lambda bi, u, j: (bi, jnp.maximum(u * per - 1, 0),
                                                  colbase + gi * DIL_HPG + j))

        kcol, vcol = W_QA // DH, 2 * W_QA // DH
        in_specs += [cur(0), cur(kcol), prev(kcol), cur(vcol), prev(vcol)]
        args += [a_view] * 5
    out = pl.pallas_call(
        _dil_kernel,
        grid=(b, s // DIL_UNIT, DIL_HPG),
        in_specs=in_specs,
        out_specs=pl.BlockSpec((None, DIL_UNIT, DH), lambda bi, u, j: (bi, u, j)),
        out_shape=jax.ShapeDtypeStruct((b, s, DIL_OUT), BF16),
        scratch_shapes=[pltpu.VMEM((len(DIL_GROUPS), DIL_UNIT, DH), F32),
                        pltpu.VMEM((len(DIL_GROUPS), DIL_UNIT, DH), F32)],
        compiler_params=_cparams(("arbitrary", "arbitrary", "arbitrary")),
        name="dilated",
    )(*args)
    return out.reshape(b * s, DIL_OUT)


def _cmp_kernel(x_ref, posk_ref, posv_ref, w1k_ref, w1v_ref, w2k_ref, w2v_ref,
                cos_ref, sin_ref, kc_ref, vc_ref, *, ncp):
    half = CMP_LEN // 2
    for kind in range(2):
        pos_ref, w1_ref, w2_ref, o_ref = ((posk_ref, w1k_ref, w2k_ref, kc_ref) if kind == 0
                                          else (posv_ref, w1v_ref, w2v_ref, vc_ref))
        for g in range(NSA_G):
            off = kind * W_KV + g * DH
            a = jnp.zeros((ncp, CMP_HIDDEN), F32)
            bm = jnp.zeros((ncp, CMP_HIDDEN), F32)
            for l in range(half):
                x = x_ref[:, l * N_C + off:l * N_C + off + DH]
                a = a + _dot((x + pos_ref[l:l + 1, :]).astype(BF16), w1_ref[l])
                bm = bm + _dot((x + pos_ref[half + l:half + l + 1, :]).astype(BF16),
                               w1_ref[half + l])
            hid = a + pltpu.roll(bm, ncp - 1, 0)
            hid = jax.nn.gelu(hid, approximate=True).astype(BF16)
            out = _dot(hid, w2_ref[...])
            if kind == 0:
                out = _rope(out, cos_ref[...], sin_ref[...])
            o_ref[g] = out.astype(BF16)


def _compress(o_c, b, s, posk, posv, w1k, w1v, w2k, w2v, cosc, sinc):
    ncp = s // CMP_STRIDE
    x_view = o_c.reshape(b, ncp, CMP_STRIDE * N_C)
    full = lambda shape: pl.BlockSpec(shape, lambda bi: (0,) * len(shape))
    return pl.pallas_call(
        functools.partial(_cmp_kernel, ncp=ncp),
        grid=(b,),
        in_specs=[pl.BlockSpec((None, ncp, CMP_STRIDE * N_C), lambda bi: (bi, 0, 0)),
                  full((CMP_LEN, DH)), full((CMP_LEN, DH)),
                  full((CMP_LEN, DH, CMP_HIDDEN)), full((CMP_LEN, DH, CMP_HIDDEN)),
                  full((CMP_HIDDEN, DH)), full((CMP_HIDDEN, DH)),
                  pl.BlockSpec((None, ncp, DH), lambda bi: (bi, 0, 0)),
                  pl.BlockSpec((None, ncp, DH), lambda bi: (bi, 0, 0))],
        out_specs=[pl.BlockSpec((None, NSA_G, ncp, DH), lambda bi: (bi, 0, 0, 0)),
                   pl.BlockSpec((None, NSA_G, ncp, DH), lambda bi: (bi, 0, 0, 0))],
        out_shape=[jax.ShapeDtypeStruct((b, NSA_G, ncp, DH), BF16)] * 2,
        compiler_params=_cparams(("arbitrary",)),
        name="nsa_compress",
    )(x_view, posk, posv, w1k, w1v, w2k, w2v, cosc, sinc)


def _nsa_kernel(q_ref, ks_ref, vs_ref, kw_ref, vw_ref, kc_ref, vc_ref, gate_ref, ovt_ref, et_ref,
                out_ref, qa_ref, ka_ref, va_ref, wa_ref, p_ref, pw_ref, y_ref,
                s0_ref, s1_ref, p0_ref, p1_ref, m_ref, al_ref, acc_ref,
                *, tq, tk, s_len, ncp, n_s):
    R = NSA_R
    qi = pl.program_id(2)
    t0 = qi * tq

    @pl.when(qi == 0)
    def _():
        ones = jnp.ones((s_len, DH), BF16)
        ka_ref[:, :DH] = ks_ref[...]
        ka_ref[:, DH:] = et_ref[...]
        va_ref[:, :DH] = vs_ref[...]
        va_ref[:, DH:] = ones
        wa_ref[:, :DH] = vw_ref[...]
        wa_ref[:, DH:] = ones

    for r in range(R):
        qa_ref[r * tq:(r + 1) * tq, :DH] = q_ref[:, r * DH:(r + 1) * DH]
    q = qa_ref[:, :DH]
    trow = t0 + lax.broadcasted_iota(jnp.int32, (tq, 1), 0)
    tlane = t0 + lax.broadcasted_iota(jnp.int32, (1, tq), 1)
    head = lambda a, r: a[r * tq:(r + 1) * tq]

    cend = lax.broadcasted_iota(jnp.int32, (1, ncp), 1) * CMP_STRIDE + (CMP_LEN - 1)
    valid_c = cend <= trow
    bias_c = jnp.where(valid_c, 0.0, NEG_INF)
    keep_c = jnp.where(valid_c, 1.0, 0.0)
    s = _dot_nt(q, kc_ref[...])
    psum = jnp.zeros((tq, ncp), F32)
    for r in range(R):
        s_r = head(s, r) + bias_c
        p = jnp.exp2(s_r - jnp.max(s_r, axis=-1, keepdims=True)) * keep_c
        p = p / jnp.maximum(jnp.sum(p, axis=-1, keepdims=True), 1e-30)
        psum = psum + p
        p_ref[r * tq:(r + 1) * tq, :ncp] = p.astype(BF16)
    o_cmp = _dot(p_ref[:, :ncp], vc_ref[...])

    wlen = (-(-(WIN_LEN - 1) // BLOCK)) * BLOCK + tq
    w0 = pl.multiple_of(jnp.maximum(t0 + tq - wlen, 0), BLOCK)
    kpos = w0 + lax.broadcasted_iota(jnp.int32, (1, wlen), 1)
    bias_w = jnp.where((kpos <= trow) & (trow - kpos <= WIN_LEN - 1), 0.0, NEG_INF)
    s = _dot_nt(q, kw_ref[pl.ds(w0, wlen), :])
    for r in range(R):
        s_r = head(s, r) + bias_w
        pw_ref[r * tq:(r + 1) * tq, :] = jnp.exp2(
            s_r - jnp.max(s_r, axis=-1, keepdims=True)).astype(BF16)
    ow = _dot(pw_ref[...], wa_ref[pl.ds(w0, wlen), :])
    o_win = ow[:, :DH] / ow[:, DH:DH + 1]
    gate = gate_ref[...]
    for r in range(R):
        y_ref[r * tq:(r + 1) * tq, :] = (gate[:, r:r + 1] * head(o_cmp, r) +
                                         gate[:, 2 * R + r:2 * R + r + 1] * head(o_win, r))

    hi = psum.astype(BF16)
    rem = psum - hi.astype(F32)
    mid = rem.astype(BF16)
    lo = (rem - mid.astype(F32)).astype(BF16)
    ovt = ovt_ref[...]
    p_slc = _dot_nt(ovt, hi) + _dot_nt(ovt, mid) + _dot_nt(ovt, lo)
    jj = lax.broadcasted_iota(jnp.int32, (n_s, 1), 0)
    blk_t = tlane // SEL_LEN
    forced = (jj == 0) | (jj == blk_t) | (jj == blk_t - 1)
    score = jnp.where(jj <= blk_t, p_slc + jnp.where(forced, FORCE_BONUS, 0.0), -1.0)
    nch = n_s // 8
    chunks = [score[c * 8:(c + 1) * 8] for c in range(nch)]
    cnt = [jnp.zeros((8, tq), F32) for _ in range(nch)]
    sub = lax.broadcasted_iota(jnp.int32, (8, 1), 0)
    for i in range(n_s):
        row_i = score[i:i + 1]
        for c in range(nch):
            if c * 8 > i:
                beats = row_i >= chunks[c]
            elif c * 8 + 7 < i:
                beats = row_i > chunks[c]
            else:
                beats = (row_i > chunks[c]) | ((row_i == chunks[c]) & (sub + c * 8 > i))
            cnt[c] = cnt[c] + jnp.where(beats, 1.0, 0.0)
    k_sel = float(min(SEL_TOPK, n_s))
    bias_t = jnp.concatenate([jnp.where(cc < k_sel, 0.0, NEG_INF) for cc in cnt] +
                             [jnp.zeros((DH - n_s, tq), F32)], axis=0)
    bias_q = bias_t.T.astype(BF16)
    for r in range(R):
        qa_ref[r * tq:(r + 1) * tq, DH:] = bias_q

    m_ref[...] = jnp.full(m_ref.shape, NEG_INF, F32)
    acc_ref[...] = jnp.zeros(acc_ref.shape, F32)

    def scores(kt, s_ref):
        k0 = pl.multiple_of(kt * tk, tk)
        s_ref[...] = _dot_nt(qa_ref[...], ka_ref[pl.ds(k0, tk), :])

    def tile(kt, s_ref, pt_ref, causal):
        k0 = pl.multiple_of(kt * tk, tk)
        if causal:
            kpos = k0 + lax.broadcasted_iota(jnp.int32, (1, tk), 1)
            bias_d = jnp.where(kpos <= trow, 0.0, NEG_INF)
        for r in range(R):
            s_r = s_ref[r * tq:(r + 1) * tq, :]
            if causal:
                s_r = s_r + bias_d
            m_old = m_ref[r]
            m_new = jnp.maximum(m_old, jnp.max(s_r, axis=-1, keepdims=True))
            pt_ref[r * tq:(r + 1) * tq, :] = jnp.exp2(s_r - m_new).astype(BF16)
            al_ref[r] = jnp.exp2(m_old - m_new)
            m_ref[r] = m_new
        pv = _dot(pt_ref[...], va_ref[pl.ds(k0, tk), :])
        acc_ref[...] = al_ref[...] * acc_ref[...] + pv.reshape(R, tq, 2 * DH)

    kd = t0 // tk
    scores(0, s0_ref)

    def pair(i, carry):
        scores(2 * i + 1, s1_ref)
        tile(2 * i, s0_ref, p0_ref, False)
        scores(2 * i + 2, s0_ref)
        tile(2 * i + 1, s1_ref, p1_ref, False)
        return carry

    lax.fori_loop(0, kd // 2, pair, 0)

    @pl.when(kd % 2 == 0)
    def _():
        tile(kd, s0_ref, p0_ref, True)

    @pl.when(kd % 2 == 1)
    def _():
        scores(kd, s1_ref)
        tile(kd - 1, s0_ref, p0_ref, False)
        tile(kd, s1_ref, p1_ref, True)

    acc = acc_ref[...]
    o_slc = (acc[:, :, :DH] / acc[:, :, DH:DH + 1]).reshape(R * tq, DH)

    gate = gate_ref[...]
    for r in range(R):
        y = y_ref[r * tq:(r + 1) * tq, :] + gate[:, R + r:R + r + 1] * head(o_slc, r)
        out_ref[:, r * DH:(r + 1) * DH] = y.astype(BF16)


def _nsa(o_r, o_p, o_g, kcmp, vcmp, b, s, tq, tk):
    ncp = s // CMP_STRIDE
    n_s = s // SEL_LEN
    assert n_s % 8 == 0 and n_s <= DH
    r_view = o_r.reshape(b, s, N_R)
    p_view = o_p.reshape(b, s, N_P)
    g_view = o_g.reshape(b, s, N_G)
    c = np.arange(ncp)[None, :]
    j = np.arange(n_s)[:, None]
    n_c = (s - CMP_LEN) // CMP_STRIDE + 1
    ovt = ((c * CMP_STRIDE <= j * SEL_LEN + SEL_LEN - 1) &
           (c * CMP_STRIDE + CMP_LEN - 1 >= j * SEL_LEN) & (c < n_c))
    ovt = jnp.asarray(ovt.astype(np.float32), BF16)
    et = jnp.asarray((np.arange(s)[:, None] // SEL_LEN == np.arange(DH)[None, :])
                     .astype(np.float32), BF16)
    wq = NSA_R * DH
    ks_col = W_QB // DH
    kw_col = ks_col + NSA_G
    gate_col = (2 * D_MODEL) // DH
    wlen = (-(-(WIN_LEN - 1) // BLOCK)) * BLOCK + tq
    kv = (None, s, DH)
    out = pl.pallas_call(
        functools.partial(_nsa_kernel, tq=tq, tk=tk, s_len=s, ncp=ncp, n_s=n_s),
        grid=(b, NSA_G, s // tq),
        in_specs=[pl.BlockSpec((None, tq, wq), lambda bi, g, qi: (bi, qi, g)),
                  pl.BlockSpec(kv, lambda bi, g, qi: (bi, 0, ks_col + g)),
                  pl.BlockSpec(kv, lambda bi, g, qi: (bi, 0, g)),
                  pl.BlockSpec(kv, lambda bi, g, qi: (bi, 0, kw_col + g)),
                  pl.BlockSpec(kv, lambda bi, g, qi: (bi, 0, NSA_G + g)),
                  pl.BlockSpec((None, None, ncp, DH), lambda bi, g, qi: (bi, g, 0, 0)),
                  pl.BlockSpec((None, None, ncp, DH), lambda bi, g, qi: (bi, g, 0, 0)),
                  pl.BlockSpec((None, tq, DH), lambda bi, g, qi: (bi, qi, gate_col + g)),
                  pl.BlockSpec((n_s, ncp), lambda bi, g, qi: (0, 0)),
                  pl.BlockSpec((s, DH), lambda bi, g, qi: (0, 0))],
        out_specs=pl.BlockSpec((None, tq, wq), lambda bi, g, qi: (bi, qi, g)),
        out_shape=jax.ShapeDtypeStruct((b, s, W_QB), BF16),
        scratch_shapes=[pltpu.VMEM((NSA_R * tq, 2 * DH), BF16),
                        pltpu.VMEM((s, 2 * DH), BF16),
                        pltpu.VMEM((s, 2 * DH), BF16),
                        pltpu.VMEM((s, 2 * DH), BF16),
                        pltpu.VMEM((NSA_R * tq, ncp), BF16),
                        pltpu.VMEM((NSA_R * tq, wlen), BF16),
                        pltpu.VMEM((NSA_R * tq, DH), F32),
                        pltpu.VMEM((NSA_R * tq, tk), F32),
                        pltpu.VMEM((NSA_R * tq, tk), F32),
                        pltpu.VMEM((NSA_R * tq, tk), BF16),
                        pltpu.VMEM((NSA_R * tq, tk), BF16),
                        pltpu.VMEM((NSA_R, tq, 1), F32),
                        pltpu.VMEM((NSA_R, tq, 1), F32),
                        pltpu.VMEM((NSA_R, tq, 2 * DH), F32)],
        compiler_params=_cparams(("arbitrary", "arbitrary", "arbitrary")),
        name="nsa_attention",
    )(r_view, r_view, p_view, r_view, p_view, kcmp, vcmp, g_view, ovt, et)
    return out.reshape(b * s, W_QB)


def _out_kernel(ya_ref, yb_ref, ga_ref, gb_ref, x_ref, woa_ref, wob_ref, wo_ref, out_ref):
    pa = _dot(ya_ref[...], woa_ref[...])
    pb = _dot(yb_ref[...], wob_ref[...])
    y = (ga_ref[...] * pa + gb_ref[...] * pb).astype(BF16)
    out_ref[...] = x_ref[...] + _dot(y, wo_ref[...])


def _merge_out(ya, yb, o_g, x2, woa, wob, wo, tm):
    m = x2.shape[0]
    row = lambda w: pl.BlockSpec((tm, w), lambda i: (i, 0))
    const = lambda shape: pl.BlockSpec(shape, lambda i: (0, 0), pipeline_mode=pl.Buffered(1))
    return pl.pallas_call(
        _out_kernel,
        grid=(m // tm,),
        in_specs=[row(DIL_OUT), row(W_QB),
                  pl.BlockSpec((tm, D_MODEL), lambda i: (i, 0)),
                  pl.BlockSpec((tm, D_MODEL), lambda i: (i, 1)),
                  row(D_MODEL),
                  const((DIL_OUT, D_MODEL)), const((W_QB, D_MODEL)), const((D_MODEL, D_MODEL))],
        out_specs=row(D_MODEL),
        out_shape=jax.ShapeDtypeStruct((m, D_MODEL), F32),
        compiler_params=_cparams(("arbitrary",)),
        name="merge_out",
    )(ya, yb, o_g, o_g, x2, woa, wob, wo)


def _ffn_kernel(x_ref, g_ref, wg_ref, wu_ref, wd_ref, gf_ref, out_ref, h_ref, *, nf, final):
    f = pl.program_id(1)

    @pl.when(f == 0)
    def _():
        x = x_ref[...]
        h_ref[...] = _rms(x, g_ref[...]).astype(BF16)
        out_ref[...] = x

    h = h_ref[...]
    a = _dot(h, wg_ref[...])
    u = _dot(h, wu_ref[...])
    act = (a * jax.nn.sigmoid(a) * u).astype(BF16)
    out_ref[...] += _dot(act, wd_ref[...])

    if final:
        @pl.when(f == nf - 1)
        def _():
            out_ref[...] = _rms(out_ref[...], gf_ref[...])


def _ffn(x2, g, wg, wu, wd, gf, tm, tf, final):
    m = x2.shape[0]
    nf = D_FF // tf
    return pl.pallas_call(
        functools.partial(_ffn_kernel, nf=nf, final=final),
        grid=(m // tm, nf),
        in_specs=[pl.BlockSpec((tm, D_MODEL), lambda i, f: (i, 0)),
                  pl.BlockSpec((1, D_MODEL), lambda i, f: (0, 0)),
                  pl.BlockSpec((D_MODEL, tf), lambda i, f: (0, f)),
                  pl.BlockSpec((D_MODEL, tf), lambda i, f: (0, f)),
                  pl.BlockSpec((tf, D_MODEL), lambda i, f: (f, 0)),
                  pl.BlockSpec((1, D_MODEL), lambda i, f: (0, 0))],
        out_specs=pl.BlockSpec((tm, D_MODEL), lambda i, f: (i, 0)),
        out_shape=jax.ShapeDtypeStruct((m, D_MODEL), F32),
        scratch_shapes=[pltpu.VMEM((tm, D_MODEL), BF16)],
        compiler_params=_cparams(("arbitrary", "arbitrary")),
        name="ffn",
    )(x2, g, wg, wu, wd, gf)


def _prep_w_in(w):
    cols = (W_QA, W_QA, W_QA, W_QB, W_KV, W_KV, W_KV, W_KV, W_KV, W_KV,
            NSA_Q_HEADS * 3, D_MODEL, D_MODEL)
    offs = np.concatenate([[0], np.cumsum(cols)])
    seg = [w[:, int(offs[i]):int(offs[i + 1])] for i in range(len(cols))]
    qa, ka, va, qb, kc, vc, ks, vs, kw, vw, gate_b, gam, gbm = seg
    gate_b = gate_b.reshape(D_MODEL, NSA_G, NSA_R, 3).transpose(0, 1, 3, 2).reshape(D_MODEL, NSA_G, 3 * NSA_R)
    gate_b = jnp.pad(gate_b, ((0, 0), (0, 0), (0, DH - 3 * NSA_R))).reshape(D_MODEL, NSA_G * DH)
    gate_b = jnp.pad(gate_b, ((0, 0), (0, N_G - 2 * D_MODEL - NSA_G * DH)))
    return jnp.concatenate([qa, ka, va, qb, ks, kw, vs, vw, kc, vc, gam, gbm, gate_b],
                           axis=1).astype(BF16)


def kernel(x, positions, ln_mix, w_in, cmp_pos_k, cmp_pos_v, cmp_w1_k, cmp_w2_k, cmp_w1_v, cmp_w2_v,
           w_out_a, w_out_b, w_out, ln_ffn, w_ffn_gate, w_ffn_up, w_ffn_down, ln_final):
    b, s, d = x.shape
    depth = w_in.shape[0]
    assert d == D_MODEL and s % DIL_UNIT == 0
    m = b * s
    tm_in = min(1024, m)
    tm_out = min(256, m)
    tm_ffn = min(512, m)
    tq, tk = 128, min(512, s)
    ncp = s // CMP_STRIDE

    pos_f = positions.astype(F32)
    cos, sin = _rope_tables(pos_f.reshape(m, 1), tm_in)
    blk_end = np.minimum(np.arange(ncp) * CMP_STRIDE + CMP_LEN - 1, s - 1)
    cosc, sinc = _rope_tables(pos_f[:, blk_end].reshape(b * ncp, 1), ncp)
    cosc = cosc.reshape(b, ncp, DH)
    sinc = sinc.reshape(b, ncp, DH)

    cs = np.ones((1, N_ALL), np.float32)
    cs[:, :W_QA] = Q_SCALE
    cs[:, N_A:N_A + W_QB] = Q_SCALE
    cs = jnp.asarray(cs)

    x2 = x.reshape(m, d)
    for l in range(depth):
        w_l = _prep_w_in(w_in[l])
        o_a, o_r, o_p, o_c, o_g = _inproj(x2, ln_mix[l][None, :], w_l, cs, cos, sin, tm_in)
        ya = _dilated(o_a, b, s)
        kcmp, vcmp = _compress(
            o_c, b, s, cmp_pos_k[l], cmp_pos_v[l],
            cmp_w1_k[l].reshape(CMP_LEN, DH, CMP_HIDDEN).astype(BF16),
            cmp_w1_v[l].reshape(CMP_LEN, DH, CMP_HIDDEN).astype(BF16),
            cmp_w2_k[l].astype(BF16), cmp_w2_v[l].astype(BF16), cosc, sinc)
        yb = _nsa(o_r, o_p, o_g, kcmp, vcmp, b, s, tq, tk)
        x2 = _merge_out(ya, yb, o_g, x2, w_out_a[l].astype(BF16), w_out_b[l].astype(BF16),
                        w_out[l].astype(BF16), tm_out)
        x2 = _ffn(x2, ln_ffn[l][None, :], w_ffn_gate[l].astype(BF16), w_ffn_up[l].astype(BF16),
                  w_ffn_down[l].astype(BF16), ln_final[None, :], tm_ffn, 512, l == depth - 1)
    return x2.reshape(b, s, d)
```

```python
import functools
import math

import numpy as np
import jax
import jax.numpy as jnp
from jax import lax
from jax.experimental import pallas as pl
from jax.experimental.pallas import tpu as pltpu

F32 = jnp.float32
BF16 = jnp.bfloat16

D_MODEL = 2048
DH = 128
HALF = DH // 2
ROPE_THETA = 10000.0
NORM_EPS = 1e-6
NEG_INF = -1e30
BLOCK = 128

DIL_GROUPS = ((128, 1), (512, 4), (2048, 16))
DIL_HPG = 4
DIL_HEADS = DIL_HPG * len(DIL_GROUPS)
DIL_OUT = DIL_HPG * DH
DIL_UNIT = DIL_GROUPS[-1][1] * BLOCK

NSA_Q_HEADS = 16
NSA_G = 2
NSA_R = NSA_Q_HEADS // NSA_G
CMP_LEN = 32
CMP_STRIDE = 16
CMP_HIDDEN = 256
SEL_LEN = 64
SEL_TOPK = 16
WIN_LEN = 512
FORCE_BONUS = 1e4
D_FF = 5632

W_QA = DIL_HEADS * DH
W_QB = NSA_Q_HEADS * DH
W_KV = NSA_G * DH
N_A = 3 * W_QA
N_R = W_QB + 2 * W_KV
N_P = 2 * W_KV
N_C = 2 * W_KV
TN_IN = 512
N_G = 2 * D_MODEL + TN_IN
N_ALL = N_A + N_R + N_P + N_C + N_G
VMEM_LIMIT = 56 * 1024 * 1024
Q_SCALE = DH ** -0.5 * math.log2(math.e)


def _cparams(sem):
    return pltpu.CompilerParams(dimension_semantics=sem, vmem_limit_bytes=VMEM_LIMIT)


def _dot_nt(a, b):
    return lax.dot_general(a, b, (((1,), (1,)), ((), ())), preferred_element_type=F32)


def _dot(a, b):
    return jnp.dot(a, b, preferred_element_type=F32)


def _rope_tab_kernel(pos_ref, inv_ref, sgn_ref, cos_ref, sin_ref):
    ang = pos_ref[...] * inv_ref[...]
    cos_ref[...] = jnp.cos(ang)
    sin_ref[...] = jnp.sin(ang) * sgn_ref[...]


def _rope_tables(pos_f, tm):
    m = pos_f.shape[0]
    inv = ROPE_THETA ** (-2.0 * jnp.arange(HALF, dtype=F32) / DH)
    inv = jnp.concatenate([inv, inv])[None, :]
    sgn = jnp.concatenate([-jnp.ones((HALF,), F32), jnp.ones((HALF,), F32)])[None, :]
    return pl.pallas_call(
        _rope_tab_kernel,
        grid=(m // tm,),
        in_specs=[pl.BlockSpec((tm, 1), lambda i: (i, 0)),
                  pl.BlockSpec((1, DH), lambda i: (0, 0)),
                  pl.BlockSpec((1, DH), lambda i: (0, 0))],
        out_specs=[pl.BlockSpec((tm, DH), lambda i: (i, 0)),
                   pl.BlockSpec((tm, DH), lambda i: (i, 0))],
        out_shape=[jax.ShapeDtypeStruct((m, DH), F32)] * 2,
        compiler_params=_cparams(("arbitrary",)),
        name="rope_tables",
    )(pos_f, inv, sgn)


def _rope(a, cos, sin_signed):
    return a * cos + pltpu.roll(a, HALF, 1) * sin_signed


def _rms(x, g):
    ms = jnp.mean(x * x, axis=-1, keepdims=True)
    return x * lax.rsqrt(ms + NORM_EPS) * g


T_AROPE = 2 * W_QA // TN_IN
T_A = N_A // TN_IN
T_R = T_A + N_R // TN_IN
T_P = T_R + N_P // TN_IN
T_C = T_P + N_C // TN_IN
T_ALL = N_ALL // TN_IN


def _inproj_kernel(x_ref, g_ref, w_ref, cs_ref, cos_ref, sin_ref,
                   oa_ref, or_ref, op_ref, oc_ref, og_ref, h_ref):
    j = pl.program_id(1)

    @pl.when(j == 0)
    def _():
        h_ref[...] = _rms(x_ref[...], g_ref[...]).astype(BF16)

    tm = h_ref.shape[0]
    halves = [slice(k * (tm // 2), (k + 1) * (tm // 2)) for k in range(2)]

    def region(out_ref, epilogue):
        accs = [_dot(h_ref[rows, :], w_ref[...]) for rows in halves]
        for rows, acc in zip(halves, accs):
            epilogue(out_ref, rows, acc)

    def roped(out_ref, rows, acc):
        cos = cos_ref[rows, :]
        sin = sin_ref[rows, :]
        for hh in range(TN_IN // DH):
            sl = slice(hh * DH, (hh + 1) * DH)
            r = _rope(acc[:, sl], cos, sin) * cs_ref[:, sl]
            out_ref[rows, sl] = r.astype(out_ref.dtype)

    def plain(out_ref, rows, acc):
        out_ref[rows, :] = acc.astype(out_ref.dtype)

    def sigmoid(out_ref, rows, acc):
        out_ref[rows, :] = jax.nn.sigmoid(acc)

    pl.when(j < T_AROPE)(lambda: region(oa_ref, roped))
    pl.when((j >= T_AROPE) & (j < T_A))(lambda: region(oa_ref, plain))
    pl.when((j >= T_A) & (j < T_R))(lambda: region(or_ref, roped))
    pl.when((j >= T_R) & (j < T_P))(lambda: region(op_ref, plain))
    pl.when((j >= T_P) & (j < T_C))(lambda: region(oc_ref, plain))
    pl.when(j >= T_C)(lambda: region(og_ref, sigmoid))


def _inproj(x2, g, w, cs, cos, sin, tm):
    m = x2.shape[0]

    def region(lo, hi):
        return pl.BlockSpec((tm, TN_IN), lambda i, j: (i, jnp.clip(j - lo, 0, hi - lo - 1)))

    return pl.pallas_call(
        _inproj_kernel,
        grid=(m // tm, T_ALL),
        in_specs=[pl.BlockSpec((tm, D_MODEL), lambda i, j: (i, 0)),
                  pl.BlockSpec((1, D_MODEL), lambda i, j: (0, 0)),
                  pl.BlockSpec((D_MODEL, TN_IN), lambda i, j: (0, j)),
                  pl.BlockSpec((1, TN_IN), lambda i, j: (0, j)),
                  pl.BlockSpec((tm, DH), lambda i, j: (i, 0)),
                  pl.BlockSpec((tm, DH), lambda i, j: (i, 0))],
        out_specs=[region(0, T_A), region(T_A, T_R), region(T_R, T_P), region(T_P, T_C),
                   region(T_C, T_ALL)],
        out_shape=[jax.ShapeDtypeStruct((m, N_A), F32),
                   jax.ShapeDtypeStruct((m, N_R), BF16),
                   jax.ShapeDtypeStruct((m, N_P), BF16),
                   jax.ShapeDtypeStruct((m, N_C), F32),
                   jax.ShapeDtypeStruct((m, N_G), F32)],
        scratch_shapes=[pltpu.VMEM((tm, D_MODEL), BF16)],
        compiler_params=_cparams(("arbitrary", "arbitrary")),
        name="inproj",
    )(x2, g, w, cs, cos, sin)


def _rows(start, dil):
    return pl.ds(start, BLOCK) if dil == 1 else pl.ds(start, BLOCK, stride=dil)


def _dil_kernel(*refs):
    out_ref, o_scr, lse_scr = refs[-3:]
    u = pl.program_id(1)
    row = lax.broadcasted_iota(jnp.int32, (BLOCK, 2 * BLOCK), 0)
    col = lax.broadcasted_iota(jnp.int32, (BLOCK, 2 * BLOCK), 1)
    band = ((col < BLOCK) & (col >= row)) | ((col >= BLOCK) & (col - BLOCK <= row))
    band_first = band & ((col >= BLOCK) | (u > 0))
    for gi, (_, dil) in enumerate(DIL_GROUPS):
        q_ref, kc_ref, kp_ref, vc_ref, vp_ref = refs[5 * gi:5 * gi + 5]
        span = BLOCK * dil
        for rho in range(dil):
            for ub in range(DIL_UNIT // span):
                cur = _rows(ub * span + rho, dil)
                if ub == 0:
                    kp, vp, mask = kp_ref[_rows(rho, dil), :], vp_ref[_rows(rho, dil), :], band_first
                else:
                    prv = _rows((ub - 1) * span + rho, dil)
                    kp, vp, mask = kc_ref[prv, :], vc_ref[prv, :], band
                q = q_ref[cur, :].astype(BF16)
                k = jnp.concatenate([kp, kc_ref[cur, :]], axis=0).astype(BF16)
                v = jnp.concatenate([vp, vc_ref[cur, :]], axis=0).astype(BF16)
                s = jnp.where(mask, _dot_nt(q, k), NEG_INF)
                m = jnp.max(s, axis=-1, keepdims=True)
                p = jnp.exp2(s - m)
                l = jnp.sum(p, axis=-1, keepdims=True)
                o_scr[gi, cur, :] = _dot(p.astype(BF16), v) / l
                lse_scr[gi, cur, :] = jnp.broadcast_to(m + jnp.log2(l), (BLOCK, DH))
    l0, l1, l2 = lse_scr[0], lse_scr[1], lse_scr[2]
    m = jnp.maximum(jnp.maximum(l0, l1), l2)
    e0, e1, e2 = jnp.exp2(l0 - m), jnp.exp2(l1 - m), jnp.exp2(l2 - m)
    ya = (e0 * o_scr[0] + e1 * o_scr[1] + e2 * o_scr[2]) / (e0 + e1 + e2)
    out_ref[...] = ya.astype(BF16)


def _dilated(o_a, b, s):
    a_view = o_a.reshape(b, s, N_A)
    in_specs, args = [], []
    for gi, (_, dil) in enumerate(DIL_GROUPS):
        span = BLOCK * dil
        per = DIL_UNIT // span

        def cur(colbase, gi=gi):
            return pl.BlockSpec((None, DIL_UNIT, DH),
                                lambda bi, u, j: (bi, u, colbase + gi * DIL_HPG + j))

        def prev(colbase, gi=gi, span=span, per=per):
            return pl.BlockSpec((None, span, DH),
                                lambda bi, u, j: (bi, jnp.maximum(u * per - 1, 0),
                                                  colbase + gi * DIL_HPG + j))

        kcol, vcol = W_QA // DH, 2 * W_QA // DH
        in_specs += [cur(0), cur(kcol), prev(kcol), cur(vcol), prev(vcol)]
        args += [a_view] * 5
    out = pl.pallas_call(
        _dil_kernel,
        grid=(b, s // DIL_UNIT, DIL_HPG),
        in_specs=in_specs,
        out_specs=pl.BlockSpec((None, DIL_UNIT, DH), lambda bi, u, j: (bi, u, j)),
        out_shape=jax.ShapeDtypeStruct((b, s, DIL_OUT), BF16),
        scratch_shapes=[pltpu.VMEM((len(DIL_GROUPS), DIL_UNIT, DH), F32),
                        pltpu.VMEM((len(DIL_GROUPS), DIL_UNIT, DH), F32)],
        compiler_params=_cparams(("arbitrary", "arbitrary", "arbitrary")),
        name="dilated",
    )(*args)
    return out.reshape(b * s, DIL_OUT)


def _cmp_kernel(x_ref, posk_ref, posv_ref, w1k_ref, w1v_ref, w2k_ref, w2v_ref,
                cos_ref, sin_ref, kc_ref, vc_ref, *, ncp):
    half = CMP_LEN // 2
    for kind in range(2):
        pos_ref, w1_ref, w2_ref, o_ref = ((posk_ref, w1k_ref, w2k_ref, kc_ref) if kind == 0
                                          else (posv_ref, w1v_ref, w2v_ref, vc_ref))
        for g in range(NSA_G):
            off = kind * W_KV + g * DH
            a = jnp.zeros((ncp, CMP_HIDDEN), F32)
            bm = jnp.zeros((ncp, CMP_HIDDEN), F32)
            for l in range(half):
                x = x_ref[:, l * N_C + off:l * N_C + off + DH]
                a = a + _dot((x + pos_ref[l:l + 1, :]).astype(BF16), w1_ref[l])
                bm = bm + _dot((x + pos_ref[half + l:half + l + 1, :]).astype(BF16),
                               w1_ref[half + l])
            hid = a + pltpu.roll(bm, ncp - 1, 0)
            hid = jax.nn.gelu(hid, approximate=True).astype(BF16)
            out = _dot(hid, w2_ref[...])
            if kind == 0:
                out = _rope(out, cos_ref[...], sin_ref[...])
            o_ref[g] = out.astype(BF16)


def _compress(o_c, b, s, posk, posv, w1k, w1v, w2k, w2v, cosc, sinc):
    ncp = s // CMP_STRIDE
    x_view = o_c.reshape(b, ncp, CMP_STRIDE * N_C)
    full = lambda shape: pl.BlockSpec(shape, lambda bi: (0,) * len(shape))
    return pl.pallas_call(
        functools.partial(_cmp_kernel, ncp=ncp),
        grid=(b,),
        in_specs=[pl.BlockSpec((None, ncp, CMP_STRIDE * N_C), lambda bi: (bi, 0, 0)),
                  full((CMP_LEN, DH)), full((CMP_LEN, DH)),
                  full((CMP_LEN, DH, CMP_HIDDEN)), full((CMP_LEN, DH, CMP_HIDDEN)),
                  full((CMP_HIDDEN, DH)), full((CMP_HIDDEN, DH)),
                  pl.BlockSpec((None, ncp, DH), lambda bi: (bi, 0, 0)),
                  pl.BlockSpec((None, ncp, DH), lambda bi: (bi, 0, 0))],
        out_specs=[pl.BlockSpec((None, NSA_G, ncp, DH), lambda bi: (bi, 0, 0, 0)),
                   pl.BlockSpec((None, NSA_G, ncp, DH), lambda bi: (bi, 0, 0, 0))],
        out_shape=[jax.ShapeDtypeStruct((b, NSA_G, ncp, DH), BF16)] * 2,
        compiler_params=_cparams(("arbitrary",)),
        name="nsa_compress",
    )(x_view, posk, posv, w1k, w1v, w2k, w2v, cosc, sinc)


def _nsa_kernel(q_ref, ks_ref, vs_ref, kw_ref, vw_ref, kc_ref, vc_ref, gate_ref, ovt_ref, et_ref,
                out_ref, qa_ref, ka_ref, va_ref, wa_ref, p_ref, pw_ref, y_ref,
                s0_ref, s1_ref, p0_ref, p1_ref, m_ref, al_ref, acc_ref,
                *, tq, tk, s_len, ncp, n_s):
    R = NSA_R
    qi = pl.program_id(2)
    t0 = qi * tq

    @pl.when(qi == 0)
    def _():
        ones = jnp.ones((s_len, DH), BF16)
        ka_ref[:, :DH] = ks_ref[...]
        ka_ref[:, DH:] = et_ref[...]
        va_ref[:, :DH] = vs_ref[...]
        va_ref[:, DH:] = ones
        wa_ref[:, :DH] = vw_ref[...]
        wa_ref[:, DH:] = ones

    for r in range(R):
        qa_ref[r * tq:(r + 1) * tq, :DH] = q_ref[:, r * DH:(r + 1) * DH]
    q = qa_ref[:, :DH]
    trow = t0 + lax.broadcasted_iota(jnp.int32, (tq, 1), 0)
    tlane = t0 + lax.broadcasted_iota(jnp.int32, (1, tq), 1)
    head = lambda a, r: a[r * tq:(r + 1) * tq]

    cend = lax.broadcasted_iota(jnp.int32, (1, ncp), 1) * CMP_STRIDE + (CMP_LEN - 1)
    valid_c = cend <= trow
    bias_c = jnp.where(valid_c, 0.0, NEG_INF)
    keep_c = jnp.where(valid_c, 1.0, 0.0)
    s = _dot_nt(q, kc_ref[...])
    psum = jnp.zeros((tq, ncp), F32)
    for r in range(R):
        s_r = head(s, r) + bias_c
        p = jnp.exp2(s_r - jnp.max(s_r, axis=-1, keepdims=True)) * keep_c
        p = p / jnp.maximum(jnp.sum(p, axis=-1, keepdims=True), 1e-30)
        psum = psum + p
        p_ref[r * tq:(r + 1) * tq, :ncp] = p.astype(BF16)
    o_cmp = _dot(p_ref[:, :ncp], vc_ref[...])

    wlen = (-(-(WIN_LEN - 1) // BLOCK)) * BLOCK + tq
    w0 = pl.multiple_of(jnp.maximum(t0 + tq - wlen, 0), BLOCK)
    kpos = w0 + lax.broadcasted_iota(jnp.int32, (1, wlen), 1)
    bias_w = jnp.where((kpos <= trow) & (trow - kpos <= WIN_LEN - 1), 0.0, NEG_INF)
    s = _dot_nt(q, kw_ref[pl.ds(w0, wlen), :])
    for r in range(R):
        s_r = head(s, r) + bias_w
        pw_ref[r * tq:(r + 1) * tq, :] = jnp.exp2(
            s_r - jnp.max(s_r, axis=-1, keepdims=True)).astype(BF16)
    ow = _dot(pw_ref[...], wa_ref[pl.ds(w0, wlen), :])
    o_win = ow[:, :DH] / ow[:, DH:DH + 1]
    gate = gate_ref[...]
    for r in range(R):
        y_ref[r * tq:(r + 1) * tq, :] = (gate[:, r:r + 1] * head(o_cmp, r) +
                                         gate[:, 2 * R + r:2 * R + r + 1] * head(o_win, r))

    hi = psum.astype(BF16)
    rem = psum - hi.astype(F32)
    mid = rem.astype(BF16)
    lo = (rem - mid.astype(F32)).astype(BF16)
    ovt = ovt_ref[...]
    p_slc = _dot_nt(ovt, hi) + _dot_nt(ovt, mid) + _dot_nt(ovt, lo)
    jj = lax.broadcasted_iota(jnp.int32, (n_s, 1), 0)
    blk_t = tlane // SEL_LEN
    forced = (jj == 0) | (jj == blk_t) | (jj == blk_t - 1)
    score = jnp.where(jj <= blk_t, p_slc + jnp.where(forced, FORCE_BONUS, 0.0), -1.0)
    nch = n_s // 8
    chunks = [score[c * 8:(c + 1) * 8] for c in range(nch)]
    cnt = [jnp.zeros((8, tq), F32) for _ in range(nch)]
    sub = lax.broadcasted_iota(jnp.int32, (8, 1), 0)
    for i in range(n_s):
        row_i = score[i:i + 1]
        for c in range(nch):
            if c * 8 > i:
                beats = row_i >= chunks[c]
            elif c * 8 + 7 < i:
                beats = row_i > chunks[c]
            else:
                beats = (row_i > chunks[c]) | ((row_i == chunks[c]) & (sub + c * 8 > i))
            cnt[c] = cnt[c] + jnp.where(beats, 1.0, 0.0)
    k_sel = float(min(SEL_TOPK, n_s))
    bias_t = jnp.concatenate([jnp.where(cc < k_sel, 0.0, NEG_INF) for cc in cnt] +
                             [jnp.zeros((DH - n_s, tq), F32)], axis=0)
    bias_q = bias_t.T.astype(BF16)
    for r in range(R):
        qa_ref[r * tq:(r + 1) * tq, DH:] = bias_q

    m_ref[...] = jnp.full(m_ref.shape, NEG_INF, F32)
    acc_ref[...] = jnp.zeros(acc_ref.shape, F32)

    def scores(kt, s_ref):
        k0 = pl.multiple_of(kt * tk, tk)
        s_ref[...] = _dot_nt(qa_ref[...], ka_ref[pl.ds(k0, tk), :])

    def tile(kt, s_ref, pt_ref, causal):
        k0 = pl.multiple_of(kt * tk, tk)
        if causal:
            kpos = k0 + lax.broadcasted_iota(jnp.int32, (1, tk), 1)
            bias_d = jnp.where(kpos <= trow, 0.0, NEG_INF)
        for r in range(R):
            s_r = s_ref[r * tq:(r + 1) * tq, :]
            if causal:
                s_r = s_r + bias_d
            m_old = m_ref[r]
            m_new = jnp.maximum(m_old, jnp.max(s_r, axis=-1, keepdims=True))
            pt_ref[r * tq:(r + 1) * tq, :] = jnp.exp2(s_r - m_new).astype(BF16)
            al_ref[r] = jnp.exp2(m_old - m_new)
            m_ref[r] = m_new
        pv = _dot(pt_ref[...], va_ref[pl.ds(k0, tk), :])
        acc_ref[...] = al_ref[...] * acc_ref[...] + pv.reshape(R, tq, 2 * DH)

    kd = t0 // tk
    scores(0, s0_ref)

    def pair(i, carry):
        scores(2 * i + 1, s1_ref)
        tile(2 * i, s0_ref, p0_ref, False)
        scores(2 * i + 2, s0_ref)
        tile(2 * i + 1, s1_ref, p1_ref, False)
        return carry

    lax.fori_loop(0, kd // 2, pair, 0)

    @pl.when(kd % 2 == 0)
    def _():
        tile(kd, s0_ref, p0_ref, True)

    @pl.when(kd % 2 == 1)
    def _():
        scores(kd, s1_ref)
        tile(kd - 1, s0_ref, p0_ref, False)
        tile(kd, s1_ref, p1_ref, True)

    acc = acc_ref[...]
    o_slc = (acc[:, :, :DH] / acc[:, :, DH:DH + 1]).reshape(R * tq, DH)

    gate = gate_ref[...]
    for r in range(R):
        y = y_ref[r * tq:(r + 1) * tq, :] + gate[:, R + r:R + r + 1] * head(o_slc, r)
        out_ref[:, r * DH:(r + 1) * DH] = y.astype(BF16)


V_ROWS = DH + 16


def _nsa_t_kernel(q_ref, ks_ref, vs_ref, kw_ref, vw_ref, kc_ref, vc_ref, gate_ref, ovt_ref, et_ref,
                  out_ref, qt_ref, ka_ref, vt_ref, wt_ref, vct_ref, pc_ref, sw_ref, pw_ref, y_ref,
                  s0_ref, s1_ref, p0_ref, p1_ref, m_ref, al0_ref, al1_ref, acc_ref,
                  *, tq, tk, s_len, ncp, n_s):
    R = NSA_R
    qi = pl.program_id(2)
    t0 = qi * tq
    lanes = lambda r: slice(r * tq, (r + 1) * tq)

    @pl.when(qi == 0)
    def _():
        ka_ref[:, :DH] = ks_ref[...]
        ka_ref[:, DH:] = et_ref[...]
        for c in range(s_len // tk):
            vt_ref[c, :DH, :] = vs_ref[c * tk:(c + 1) * tk, :].astype(F32).T.astype(BF16)
            vt_ref[c, DH:, :] = jnp.ones((V_ROWS - DH, tk), BF16)
        for c in range(s_len // BLOCK):
            wt_ref[c, :DH, :] = vw_ref[c * BLOCK:(c + 1) * BLOCK, :].astype(F32).T.astype(BF16)
            wt_ref[c, DH:, :] = jnp.ones((V_ROWS - DH, BLOCK), BF16)
        vct_ref[...] = vc_ref[...].astype(F32).T.astype(BF16)

    for r in range(R):
        qt_ref[:DH, lanes(r)] = q_ref[:, r * DH:(r + 1) * DH].astype(F32).T.astype(BF16)
    qt = qt_ref[:DH, :]
    tlane = t0 + lax.broadcasted_iota(jnp.int32, (1, tq), 1)
    gate_t = gate_ref[...].T

    def softmax_cols(s_ref, p_ref, n, r, bias=None, m_old=None):
        def load(lo, hi):
            s = s_ref[lo:hi, lanes(r)]
            return s if bias is None else s + bias[lo:hi]
        h, q = n // 2, n // 4
        m = jnp.maximum(jnp.max(load(0, h), axis=0, keepdims=True),
                        jnp.max(load(h, n), axis=0, keepdims=True))
        if m_old is not None:
            m = jnp.maximum(m, m_old)
        for c in range(4):
            p_ref[c * q:(c + 1) * q, lanes(r)] = jnp.exp2(load(c * q, (c + 1) * q) - m).astype(BF16)
        return m

    cend = lax.broadcasted_iota(jnp.int32, (ncp, 1), 0) * CMP_STRIDE + (CMP_LEN - 1)
    valid_c = cend <= tlane
    bias_c = jnp.where(valid_c, 0.0, NEG_INF)
    keep_c = jnp.where(valid_c, 1.0, 0.0)
    s1_ref[:ncp, :] = _dot(kc_ref[...], qt)
    psum = jnp.zeros((ncp, tq), F32)
    for r in range(R):
        s_r = s1_ref[:ncp, lanes(r)] + bias_c
        p = jnp.exp2(s_r - jnp.max(s_r, axis=0, keepdims=True)) * keep_c
        p = p / jnp.maximum(jnp.sum(p, axis=0, keepdims=True), 1e-30)
        psum = psum + p
        pc_ref[:, lanes(r)] = p.astype(BF16)
    o_cmp = _dot(vct_ref[...], pc_ref[...])

    wlen = (-(-(WIN_LEN - 1) // BLOCK)) * BLOCK + tq
    w0 = pl.multiple_of(jnp.maximum(t0 + tq - wlen, 0), BLOCK)
    kpos = w0 + lax.broadcasted_iota(jnp.int32, (wlen, 1), 0)
    bias_w = jnp.where((kpos <= tlane) & (tlane - kpos <= WIN_LEN - 1), 0.0, NEG_INF)
    sw_ref[...] = _dot(kw_ref[pl.ds(w0, wlen), :], qt)
    for r in range(R):
        softmax_cols(sw_ref, pw_ref, wlen, r, bias=bias_w)
    wb = w0 // BLOCK
    ow = _dot(wt_ref[wb], pw_ref[:BLOCK, :])
    for jb in range(1, wlen // BLOCK):
        ow = ow + _dot(wt_ref[wb + jb], pw_ref[jb * BLOCK:(jb + 1) * BLOCK, :])
    o_win = ow[:DH] / ow[DH:DH + 1]
    for r in range(R):
        y_ref[:, lanes(r)] = (gate_t[r:r + 1] * o_cmp[:, lanes(r)] +
                              gate_t[2 * R + r:2 * R + r + 1] * o_win[:, lanes(r)])

    hi = psum.astype(BF16)
    rem = psum - hi.astype(F32)
    mid = rem.astype(BF16)
    lo = (rem - mid.astype(F32)).astype(BF16)
    ovt = ovt_ref[...]
    p_slc = _dot(ovt, hi) + _dot(ovt, mid) + _dot(ovt, lo)
    jj = lax.broadcasted_iota(jnp.int32, (n_s, 1), 0)
    blk_t = tlane // SEL_LEN
    forced = (jj == 0) | (jj == blk_t) | (jj == blk_t - 1)
    score = jnp.where(jj <= blk_t, p_slc + jnp.where(forced, FORCE_BONUS, 0.0), -1.0)
    nch = n_s // 8
    chunks = [score[c * 8:(c + 1) * 8] for c in range(nch)]
    cnt = [jnp.zeros((8, tq), F32) for _ in range(nch)]
    sub = lax.broadcasted_iota(jnp.int32, (8, 1), 0)
    for i in range(n_s):
        row_i = score[i:i + 1]
        for c in range(nch):
            if c * 8 > i:
                beats = row_i >= chunks[c]
            elif c * 8 + 7 < i:
                beats = row_i > chunks[c]
            else:
                beats = (row_i > chunks[c]) | ((row_i == chunks[c]) & (sub + c * 8 > i))
            cnt[c] = cnt[c] + jnp.where(beats, 1.0, 0.0)
    k_sel = float(min(SEL_TOPK, n_s))
    bias_t = jnp.concatenate([jnp.where(cc < k_sel, 0.0, NEG_INF) for cc in cnt] +
                             [jnp.zeros((DH - n_s, tq), F32)], axis=0).astype(BF16)
    for r in range(R):
        qt_ref[DH:, lanes(r)] = bias_t

    m_ref[...] = jnp.full(m_ref.shape, NEG_INF, F32)
    acc_ref[...] = jnp.zeros(acc_ref.shape, F32)

    def scores(kt, s_ref):
        k0 = pl.multiple_of(kt * tk, tk)
        s_ref[...] = _dot(ka_ref[pl.ds(k0, tk), :], qt_ref[...])

    def tile(kt, bufs, causal):
        s_ref, pt_ref, al_ref = bufs
        if causal:
            kpos = kt * tk + lax.broadcasted_iota(jnp.int32, (tk, 1), 0)
            bias_d = jnp.where(kpos <= tlane, 0.0, NEG_INF)
        for r in range(R):
            m_old = m_ref[:, lanes(r)]
            m_new = softmax_cols(s_ref, pt_ref, tk, r, bias=bias_d if causal else None, m_old=m_old)
            al_ref[:, lanes(r)] = jnp.exp2(m_old - m_new)
            m_ref[:, lanes(r)] = m_new
        pv = _dot(vt_ref[kt], pt_ref[...])
        acc_ref[...] = al_ref[...] * acc_ref[...] + pv

    even = (s0_ref, p0_ref, al0_ref)
    odd = (s1_ref, p1_ref, al1_ref)
    kd = t0 // tk
    scores(0, s0_ref)

    def pair(i, carry):
        scores(2 * i + 1, s1_ref)
        tile(2 * i, even, False)
        scores(2 * i + 2, s0_ref)
        tile(2 * i + 1, odd, False)
        return carry

    lax.fori_loop(0, kd // 2, pair, 0)

    @pl.when(kd % 2 == 0)
    def _():
        tile(kd, even, True)

    @pl.when(kd % 2 == 1)
    def _():
        scores(kd, s1_ref)
        tile(kd - 1, even, False)
        tile(kd, odd, True)

    acc = acc_ref[...]
    o_slc = acc[:DH] / acc[DH:DH + 1]
    for r in range(R):
        y = y_ref[:, lanes(r)] + gate_t[R + r:R + r + 1] * o_slc[:, lanes(r)]
        out_ref[:, r * DH:(r + 1) * DH] = y.T.astype(BF16)


def _nsa(o_r, o_p, o_g, kcmp, vcmp, b, s, tq, tk):
    ncp = s // CMP_STRIDE
    n_s = s // SEL_LEN
    assert n_s % 8 == 0 and n_s <= DH
    r_view = o_r.reshape(b, s, N_R)
    p_view = o_p.reshape(b, s, N_P)
    g_view = o_g.reshape(b, s, N_G)
    c = np.arange(ncp)[None, :]
    j = np.arange(n_s)[:, None]
    n_c = (s - CMP_LEN) // CMP_STRIDE + 1
    ovt = ((c * CMP_STRIDE <= j * SEL_LEN + SEL_LEN - 1) &
           (c * CMP_STRIDE + CMP_LEN - 1 >= j * SEL_LEN) & (c < n_c))
    ovt = jnp.asarray(ovt.astype(np.float32), BF16)
    et = jnp.asarray((np.arange(s)[:, None] // SEL_LEN == np.arange(DH)[None, :])
                     .astype(np.float32), BF16)
    wq = NSA_R * DH
    ks_col = W_QB // DH
    kw_col = ks_col + NSA_G
    gate_col = (2 * D_MODEL) // DH
    wlen = (-(-(WIN_LEN - 1) // BLOCK)) * BLOCK + tq
    kv = (None, s, DH)
    out = pl.pallas_call(
        functools.partial(_nsa_t_kernel, tq=tq, tk=tk, s_len=s, ncp=ncp, n_s=n_s),
        grid=(b, NSA_G, s // tq),
        in_specs=[pl.BlockSpec((None, tq, wq), lambda bi, g, qi: (bi, qi, g)),
                  pl.BlockSpec(kv, lambda bi, g, qi: (bi, 0, ks_col + g)),
                  pl.BlockSpec(kv, lambda bi, g, qi: (bi, 0, g)),
                  pl.BlockSpec(kv, lambda bi, g, qi: (bi, 0, kw_col + g)),
                  pl.BlockSpec(kv, lambda bi, g, qi: (bi, 0, NSA_G + g)),
                  pl.BlockSpec((None, None, ncp, DH), lambda bi, g, qi: (bi, g, 0, 0)),
                  pl.BlockSpec((None, None, ncp, DH), lambda bi, g, qi: (bi, g, 0, 0)),
                  pl.BlockSpec((None, tq, DH), lambda bi, g, qi: (bi, qi, gate_col + g)),
                  pl.BlockSpec((n_s, ncp), lambda bi, g, qi: (0, 0)),
                  pl.BlockSpec((s, DH), lambda bi, g, qi: (0, 0))],
        out_specs=pl.BlockSpec((None, tq, wq), lambda bi, g, qi: (bi, qi, g)),
        out_shape=jax.ShapeDtypeStruct((b, s, W_QB), BF16),
        scratch_shapes=[pltpu.VMEM((2 * DH, NSA_R * tq), BF16),
                        pltpu.VMEM((s, 2 * DH), BF16),
                        pltpu.VMEM((s // tk, V_ROWS, tk), BF16),
                        pltpu.VMEM((s // BLOCK, V_ROWS, BLOCK), BF16),
                        pltpu.VMEM((DH, ncp), BF16),
                        pltpu.VMEM((ncp, NSA_R * tq), BF16),
                        pltpu.VMEM((wlen, NSA_R * tq), F32),
                        pltpu.VMEM((wlen, NSA_R * tq), BF16),
                        pltpu.VMEM((DH, NSA_R * tq), F32),
                        pltpu.VMEM((tk, NSA_R * tq), F32),
                        pltpu.VMEM((tk, NSA_R * tq), F32),
                        pltpu.VMEM((tk, NSA_R * tq), BF16),
                        pltpu.VMEM((tk, NSA_R * tq), BF16),
                        pltpu.VMEM((1, NSA_R * tq), F32),
                        pltpu.VMEM((1, NSA_R * tq), F32),
                        pltpu.VMEM((1, NSA_R * tq), F32),
                        pltpu.VMEM((V_ROWS, NSA_R * tq), F32)],
        compiler_params=_cparams(("arbitrary", "arbitrary", "arbitrary")),
        name="nsa_attention",
    )(r_view, r_view, p_view, r_view, p_view, kcmp, vcmp, g_view, ovt, et)
    return out.reshape(b * s, W_QB)


def _out_kernel(ya_ref, yb_ref, ga_ref, gb_ref, x_ref, woa_ref, wob_ref, wo_ref, out_ref):
    pa = _dot(ya_ref[...], woa_ref[...])
    pb = _dot(yb_ref[...], wob_ref[...])
    y = (ga_ref[...] * pa + gb_ref[...] * pb).astype(BF16)
    out_ref[...] = x_ref[...] + _dot(y, wo_ref[...])


def _merge_out(ya, yb, o_g, x2, woa, wob, wo, tm):
    m = x2.shape[0]
    row = lambda w: pl.BlockSpec((tm, w), lambda i: (i, 0))
    const = lambda shape: pl.BlockSpec(shape, lambda i: (0, 0), pipeline_mode=pl.Buffered(1))
    return pl.pallas_call(
        _out_kernel,
        grid=(m // tm,),
        in_specs=[row(DIL_OUT), row(W_QB),
                  pl.BlockSpec((tm, D_MODEL), lambda i: (i, 0)),
                  pl.BlockSpec((tm, D_MODEL), lambda i: (i, 1)),
                  row(D_MODEL),
                  const((DIL_OUT, D_MODEL)), const((W_QB, D_MODEL)), const((D_MODEL, D_MODEL))],
        out_specs=row(D_MODEL),
        out_shape=jax.ShapeDtypeStruct((m, D_MODEL), F32),
        compiler_params=_cparams(("arbitrary",)),
        name="merge_out",
    )(ya, yb, o_g, o_g, x2, woa, wob, wo)


def _ffn_kernel(x_ref, g_ref, wg_ref, wu_ref, wd_ref, gf_ref, out_ref, h_ref, *, nf, final):
    f = pl.program_id(1)

    @pl.when(f == 0)
    def _():
        x = x_ref[...]
        h_ref[...] = _rms(x, g_ref[...]).astype(BF16)
        out_ref[...] = x

    h = h_ref[...]
    a = _dot(h, wg_ref[...])
    u = _dot(h, wu_ref[...])
    act = (a * jax.nn.sigmoid(a) * u).astype(BF16)
    out_ref[...] += _dot(act, wd_ref[...])

    if final:
        @pl.when(f == nf - 1)
        def _():
            out_ref[...] = _rms(out_ref[...], gf_ref[...])


def _ffn(x2, g, wg, wu, wd, gf, tm, tf, final):
    m = x2.shape[0]
    nf = D_FF // tf
    return pl.pallas_call(
        functools.partial(_ffn_kernel, nf=nf, final=final),
        grid=(m // tm, nf),
        in_specs=[pl.BlockSpec((tm, D_MODEL), lambda i, f: (i, 0)),
                  pl.BlockSpec((1, D_MODEL), lambda i, f: (0, 0)),
                  pl.BlockSpec((D_MODEL, tf), lambda i, f: (0, f)),
                  pl.BlockSpec((D_MODEL, tf), lambda i, f: (0, f)),
                  pl.BlockSpec((tf, D_MODEL), lambda i, f: (f, 0)),
                  pl.BlockSpec((1, D_MODEL), lambda i, f: (0, 0))],
        out_specs=pl.BlockSpec((tm, D_MODEL), lambda i, f: (i, 0)),
        out_shape=jax.ShapeDtypeStruct((m, D_MODEL), F32),
        scratch_shapes=[pltpu.VMEM((tm, D_MODEL), BF16)],
        compiler_params=_cparams(("arbitrary", "arbitrary")),
        name="ffn",
    )(x2, g, wg, wu, wd, gf)


def _prep_w_in(w):
    cols = (W_QA, W_QA, W_QA, W_QB, W_KV, W_KV, W_KV, W_KV, W_KV, W_KV,
            NSA_Q_HEADS * 3, D_MODEL, D_MODEL)
    offs = np.concatenate([[0], np.cumsum(cols)])
    seg = [w[:, int(offs[i]):int(offs[i + 1])] for i in range(len(cols))]
    qa, ka, va, qb, kc, vc, ks, vs, kw, vw, gate_b, gam, gbm = seg
    gate_b = gate_b.reshape(D_MODEL, NSA_G, NSA_R, 3).transpose(0, 1, 3, 2).reshape(D_MODEL, NSA_G, 3 * NSA_R)
    gate_b = jnp.pad(gate_b, ((0, 0), (0, 0), (0, DH - 3 * NSA_R))).reshape(D_MODEL, NSA_G * DH)
    gate_b = jnp.pad(gate_b, ((0, 0), (0, N_G - 2 * D_MODEL - NSA_G * DH)))
    return jnp.concatenate([qa, ka, va, qb, ks, kw, vs, vw, kc, vc, gam, gbm, gate_b],
                           axis=1).astype(BF16)


def kernel(x, positions, ln_mix, w_in, cmp_pos_k, cmp_pos_v, cmp_w1_k, cmp_w2_k, cmp_w1_v, cmp_w2_v,
           w_out_a, w_out_b, w_out, ln_ffn, w_ffn_gate, w_ffn_up, w_ffn_down, ln_final):
    b, s, d = x.shape
    depth = w_in.shape[0]
    assert d == D_MODEL and s % DIL_UNIT == 0
    m = b * s
    tm_in = min(1024, m)
    tm_out = min(256, m)
    tm_ffn = min(512, m)
    tq, tk = 128, min(512, s)
    ncp = s // CMP_STRIDE

    pos_f = positions.astype(F32)
    cos, sin = _rope_tables(pos_f.reshape(m, 1), tm_in)
    blk_end = np.minimum(np.arange(ncp) * CMP_STRIDE + CMP_LEN - 1, s - 1)
    cosc, sinc = _rope_tables(pos_f[:, blk_end].reshape(b * ncp, 1), ncp)
    cosc = cosc.reshape(b, ncp, DH)
    sinc = sinc.reshape(b, ncp, DH)

    cs = np.ones((1, N_ALL), np.float32)
    cs[:, :W_QA] = Q_SCALE
    cs[:, N_A:N_A + W_QB] = Q_SCALE
    cs = jnp.asarray(cs)

    x2 = x.reshape(m, d)
    for l in range(depth):
        w_l = _prep_w_in(w_in[l])
        o_a, o_r, o_p, o_c, o_g = _inproj(x2, ln_mix[l][None, :], w_l, cs, cos, sin, tm_in)
        ya = _dilated(o_a, b, s)
        kcmp, vcmp = _compress(
            o_c, b, s, cmp_pos_k[l], cmp_pos_v[l],
            cmp_w1_k[l].reshape(CMP_LEN, DH, CMP_HIDDEN).astype(BF16),
            cmp_w1_v[l].reshape(CMP_LEN, DH, CMP_HIDDEN).astype(BF16),
            cmp_w2_k[l].astype(BF16), cmp_w2_v[l].astype(BF16), cosc, sinc)
        yb = _nsa(o_r, o_p, o_g, kcmp, vcmp, b, s, tq, tk)
        x2 = _merge_out(ya, yb, o_g, x2, w_out_a[l].astype(BF16), w_out_b[l].astype(BF16),
                        w_out[l].astype(BF16), tm_out)
        x2 = _ffn(x2, ln_ffn[l][None, :], w_ffn_gate[l].astype(BF16), w_ffn_up[l].astype(BF16),
                  w_ffn_down[l].astype(BF16), ln_final[None, :], tm_ffn, 512, l == depth - 1)
    return x2.reshape(b, s, d)
```

```python
import functools
import math

import numpy as np
import jax
import jax.numpy as jnp
from jax import lax
from jax.experimental import pallas as pl
from jax.experimental.pallas import tpu as pltpu

F32 = jnp.float32
BF16 = jnp.bfloat16

D_MODEL = 2048
DH = 128
HALF = DH // 2
ROPE_THETA = 10000.0
NORM_EPS = 1e-6
NEG_INF = -1e30
BLOCK = 128

DIL_GROUPS = ((128, 1), (512, 4), (2048, 16))
DIL_HPG = 4
DIL_HEADS = DIL_HPG * len(DIL_GROUPS)
DIL_OUT = DIL_HPG * DH
DIL_UNIT = DIL_GROUPS[-1][1] * BLOCK

NSA_Q_HEADS = 16
NSA_G = 2
NSA_R = NSA_Q_HEADS // NSA_G
CMP_LEN = 32
CMP_STRIDE = 16
CMP_HIDDEN = 256
SEL_LEN = 64
SEL_TOPK = 16
WIN_LEN = 512
FORCE_BONUS = 1e4
D_FF = 5632

W_QA = DIL_HEADS * DH
W_QB = NSA_Q_HEADS * DH
W_KV = NSA_G * DH
N_A = 3 * W_QA
N_R = W_QB + 2 * W_KV
N_P = 2 * W_KV
N_C = 2 * W_KV
TN_IN = 512
N_G = 2 * D_MODEL + TN_IN
N_ALL = N_A + N_R + N_P + N_C + N_G
VMEM_LIMIT = 56 * 1024 * 1024
Q_SCALE = DH ** -0.5 * math.log2(math.e)


def _cparams(sem):
    return pltpu.CompilerParams(dimension_semantics=sem, vmem_limit_bytes=VMEM_LIMIT)


def _dot_nt(a, b):
    return lax.dot_general(a, b, (((1,), (1,)), ((), ())), preferred_element_type=F32)


def _dot(a, b):
    return jnp.dot(a, b, preferred_element_type=F32)


def _rope_tab_kernel(pos_ref, inv_ref, sgn_ref, cos_ref, sin_ref):
    ang = pos_ref[...] * inv_ref[...]
    cos_ref[...] = jnp.cos(ang)
    sin_ref[...] = jnp.sin(ang) * sgn_ref[...]


def _rope_tables(pos_f, tm):
    m = pos_f.shape[0]
    inv = ROPE_THETA ** (-2.0 * jnp.arange(HALF, dtype=F32) / DH)
    inv = jnp.concatenate([inv, inv])[None, :]
    sgn = jnp.concatenate([-jnp.ones((HALF,), F32), jnp.ones((HALF,), F32)])[None, :]
    return pl.pallas_call(
        _rope_tab_kernel,
        grid=(m // tm,),
        in_specs=[pl.BlockSpec((tm, 1), lambda i: (i, 0)),
                  pl.BlockSpec((1, DH), lambda i: (0, 0)),
                  pl.BlockSpec((1, DH), lambda i: (0, 0))],
        out_specs=[pl.BlockSpec((tm, DH), lambda i: (i, 0)),
                   pl.BlockSpec((tm, DH), lambda i: (i, 0))],
        out_shape=[jax.ShapeDtypeStruct((m, DH), F32)] * 2,
        compiler_params=_cparams(("arbitrary",)),
        name="rope_tables",
    )(pos_f, inv, sgn)


def _rope(a, cos, sin_signed):
    return a * cos + pltpu.roll(a, HALF, 1) * sin_signed


def _rms(x, g):
    ms = jnp.mean(x * x, axis=-1, keepdims=True)
    return x * lax.rsqrt(ms + NORM_EPS) * g


T_AROPE = 2 * W_QA // TN_IN
T_A = N_A // TN_IN
T_R = T_A + N_R // TN_IN
T_P = T_R + N_P // TN_IN
T_C = T_P + N_C // TN_IN
T_ALL = N_ALL // TN_IN


def _inproj_kernel(x_ref, g_ref, w_ref, cs_ref, cos_ref, sin_ref,
                   oa_ref, or_ref, op_ref, oc_ref, og_ref, h_ref):
    j = pl.program_id(1)

    @pl.when(j == 0)
    def _():
        h_ref[...] = _rms(x_ref[...], g_ref[...]).astype(BF16)

    tm = h_ref.shape[0]
    halves = [slice(k * (tm // 2), (k + 1) * (tm // 2)) for k in range(2)]

    def region(out_ref, epilogue):
        accs = [_dot(h_ref[rows, :], w_ref[...]) for rows in halves]
        for rows, acc in zip(halves, accs):
            epilogue(out_ref, rows, acc)

    def roped(out_ref, rows, acc):
        cos = cos_ref[rows, :]
        sin = sin_ref[rows, :]
        for hh in range(TN_IN // DH):
            sl = slice(hh * DH, (hh + 1) * DH)
            r = _rope(acc[:, sl], cos, sin) * cs_ref[:, sl]
            out_ref[rows, sl] = r.astype(out_ref.dtype)

    def plain(out_ref, rows, acc):
        out_ref[rows, :] = acc.astype(out_ref.dtype)

    def sigmoid(out_ref, rows, acc):
        out_ref[rows, :] = jax.nn.sigmoid(acc)

    pl.when(j < T_AROPE)(lambda: region(oa_ref, roped))
    pl.when((j >= T_AROPE) & (j < T_A))(lambda: region(oa_ref, plain))
    pl.when((j >= T_A) & (j < T_R))(lambda: region(or_ref, roped))
    pl.when((j >= T_R) & (j < T_P))(lambda: region(op_ref, plain))
    pl.when((j >= T_P) & (j < T_C))(lambda: region(oc_ref, plain))
    pl.when(j >= T_C)(lambda: region(og_ref, sigmoid))


def _inproj(x2, g, w, cs, cos, sin, tm):
    m = x2.shape[0]

    def region(lo, hi):
        return pl.BlockSpec((tm, TN_IN), lambda i, j: (i, jnp.clip(j - lo, 0, hi - lo - 1)))

    return pl.pallas_call(
        _inproj_kernel,
        grid=(m // tm, T_ALL),
        in_specs=[pl.BlockSpec((tm, D_MODEL), lambda i, j: (i, 0)),
                  pl.BlockSpec((1, D_MODEL), lambda i, j: (0, 0)),
                  pl.BlockSpec((D_MODEL, TN_IN), lambda i, j: (0, j)),
                  pl.BlockSpec((1, TN_IN), lambda i, j: (0, j)),
                  pl.BlockSpec((tm, DH), lambda i, j: (i, 0)),
                  pl.BlockSpec((tm, DH), lambda i, j: (i, 0))],
        out_specs=[region(0, T_A), region(T_A, T_R), region(T_R, T_P), region(T_P, T_C),
                   region(T_C, T_ALL)],
        out_shape=[jax.ShapeDtypeStruct((m, N_A), F32),
                   jax.ShapeDtypeStruct((m, N_R), BF16),
                   jax.ShapeDtypeStruct((m, N_P), BF16),
                   jax.ShapeDtypeStruct((m, N_C), F32),
                   jax.ShapeDtypeStruct((m, N_G), F32)],
        scratch_shapes=[pltpu.VMEM((tm, D_MODEL), BF16)],
        compiler_params=_cparams(("arbitrary", "arbitrary")),
        name="inproj",
    )(x2, g, w, cs, cos, sin)


def _rows(start, dil):
    return pl.ds(start, BLOCK) if dil == 1 else pl.ds(start, BLOCK, stride=dil)


def _dil_kernel(*refs):
    out_ref, o_scr, lse_scr = refs[-3:]
    u = pl.program_id(1)
    row = lax.broadcasted_iota(jnp.int32, (BLOCK, 2 * BLOCK), 0)
    col = lax.broadcasted_iota(jnp.int32, (BLOCK, 2 * BLOCK), 1)
    band = ((col < BLOCK) & (col >= row)) | ((col >= BLOCK) & (col - BLOCK <= row))
    band_first = band & ((col >= BLOCK) | (u > 0))
    for gi, (_, dil) in enumerate(DIL_GROUPS):
        q_ref, kc_ref, kp_ref, vc_ref, vp_ref = refs[5 * gi:5 * gi + 5]
        span = BLOCK * dil
        for rho in range(dil):
            for ub in range(DIL_UNIT // span):
                cur = _rows(ub * span + rho, dil)
                if ub == 0:
                    kp, vp, mask = kp_ref[_rows(rho, dil), :], vp_ref[_rows(rho, dil), :], band_first
                else:
                    prv = _rows((ub - 1) * span + rho, dil)
                    kp, vp, mask = kc_ref[prv, :], vc_ref[prv, :], band
                q = q_ref[cur, :].astype(BF16)
                k = jnp.concatenate([kp, kc_ref[cur, :]], axis=0).astype(BF16)
                v = jnp.concatenate([vp, vc_ref[cur, :]], axis=0).astype(BF16)
                s = jnp.where(mask, _dot_nt(q, k), NEG_INF)
                m = jnp.max(s, axis=-1, keepdims=True)
                p = jnp.exp2(s - m)
                l = jnp.sum(p, axis=-1, keepdims=True)
                o_scr[gi, cur, :] = _dot(p.astype(BF16), v) / l
                lse_scr[gi, cur, :] = jnp.broadcast_to(m + jnp.log2(l), (BLOCK, DH))
    l0, l1, l2 = lse_scr[0], lse_scr[1], lse_scr[2]
    m = jnp.maximum(jnp.maximum(l0, l1), l2)
    e0, e1, e2 = jnp.exp2(l0 - m), jnp.exp2(l1 - m), jnp.exp2(l2 - m)
    ya = (e0 * o_scr[0] + e1 * o_scr[1] + e2 * o_scr[2]) / (e0 + e1 + e2)
    out_ref[...] = ya.astype(BF16)


def _dilated(o_a, b, s):
    a_view = o_a.reshape(b, s, N_A)
    in_specs, args = [], []
    for gi, (_, dil) in enumerate(DIL_GROUPS):
        span = BLOCK * dil
        per = DIL_UNIT // span

        def cur(colbase, gi=gi):
            return pl.BlockSpec((None, DIL_UNIT, DH),
                                lambda bi, u, j: (bi, u, colbase + gi * DIL_HPG + j))

        def prev(colbase, gi=gi, span=span, per=per):
            return pl.BlockSpec((None, span, DH),
                                lambda bi, u, j: (bi, jnp.maximum(u * per - 1, 0),
                                                  colbase + gi * DIL_HPG + j))

        kcol, vcol = W_QA // DH, 2 * W_QA // DH
        in_specs += [cur(0), cur(kcol), prev(kcol), cur(vcol), prev(vcol)]
        args += [a_view] * 5
    out = pl.pallas_call(
        _dil_kernel,
        grid=(b, s // DIL_UNIT, DIL_HPG),
        in_specs=in_specs,
        out_specs=pl.BlockSpec((None, DIL_UNIT, DH), lambda bi, u, j: (bi, u, j)),
        out_shape=jax.ShapeDtypeStruct((b, s, DIL_OUT), BF16),
        scratch_shapes=[pltpu.VMEM((len(DIL_GROUPS), DIL_UNIT, DH), F32),
                        pltpu.VMEM((len(DIL_GROUPS), DIL_UNIT, DH), F32)],
        compiler_params=_cparams(("arbitrary", "arbitrary", "arbitrary")),
        name="dilated",
    )(*args)
    return out.reshape(b * s, DIL_OUT)


def _cmp_kernel(x_ref, posk_ref, posv_ref, w1k_ref, w1v_ref, w2k_ref, w2v_ref,
                cos_ref, sin_ref, kc_ref, vc_ref, *, ncp):
    half = CMP_LEN // 2
    for kind in range(2):
        pos_ref, w1_ref, w2_ref, o_ref = ((posk_ref, w1k_ref, w2k_ref, kc_ref) if kind == 0
                                          else (posv_ref, w1v_ref, w2v_ref, vc_ref))
        for g in range(NSA_G):
            off = kind * W_KV + g * DH
            a = jnp.zeros((ncp, CMP_HIDDEN), F32)
            bm = jnp.zeros((ncp, CMP_HIDDEN), F32)
            for l in range(half):
                x = x_ref[:, l * N_C + off:l * N_C + off + DH]
                a = a + _dot((x + pos_ref[l:l + 1, :]).astype(BF16), w1_ref[l])
                bm = bm + _dot((x + pos_ref[half + l:half + l + 1, :]).astype(BF16),
                               w1_ref[half + l])
            hid = a + pltpu.roll(bm, ncp - 1, 0)
            hid = jax.nn.gelu(hid, approximate=True).astype(BF16)
            out = _dot(hid, w2_ref[...])
            if kind == 0:
                out = _rope(out, cos_ref[...], sin_ref[...])
            o_ref[g] = out.astype(BF16)


def _compress(o_c, b, s, posk, posv, w1k, w1v, w2k, w2v, cosc, sinc):
    ncp = s // CMP_STRIDE
    x_view = o_c.reshape(b, ncp, CMP_STRIDE * N_C)
    full = lambda shape: pl.BlockSpec(shape, lambda bi: (0,) * len(shape))
    return pl.pallas_call(
        functools.partial(_cmp_kernel, ncp=ncp),
        grid=(b,),
        in_specs=[pl.BlockSpec((None, ncp, CMP_STRIDE * N_C), lambda bi: (bi, 0, 0)),
                  full((CMP_LEN, DH)), full((CMP_LEN, DH)),
                  full((CMP_LEN, DH, CMP_HIDDEN)), full((CMP_LEN, DH, CMP_HIDDEN)),
                  full((CMP_HIDDEN, DH)), full((CMP_HIDDEN, DH)),
                  pl.BlockSpec((None, ncp, DH), lambda bi: (bi, 0, 0)),
                  pl.BlockSpec((None, ncp, DH), lambda bi: (bi, 0, 0))],
        out_specs=[pl.BlockSpec((None, NSA_G, ncp, DH), lambda bi: (bi, 0, 0, 0)),
                   pl.BlockSpec((None, NSA_G, ncp, DH), lambda bi: (bi, 0, 0, 0))],
        out_shape=[jax.ShapeDtypeStruct((b, NSA_G, ncp, DH), BF16)] * 2,
        compiler_params=_cparams(("arbitrary",)),
        name="nsa_compress",
    )(x_view, posk, posv, w1k, w1v, w2k, w2v, cosc, sinc)


def _nsa_kernel(q_ref, ks_ref, vs_ref, kw_ref, vw_ref, kc_ref, vc_ref, gate_ref, ovt_ref, et_ref,
                out_ref, qa_ref, ka_ref, va_ref, wa_ref, p_ref, pw_ref, y_ref,
                s0_ref, s1_ref, p0_ref, p1_ref, m_ref, al0_ref, al1_ref, acc_ref,
                *, tq, tk, s_len, ncp, n_s):
    R = NSA_R
    qi = pl.program_id(2)
    t0 = qi * tq

    @pl.when(qi == 0)
    def _():
        ones = jnp.ones((s_len, DH), BF16)
        ka_ref[:, :DH] = ks_ref[...]
        ka_ref[:, DH:] = et_ref[...]
        va_ref[:, :DH] = vs_ref[...]
        va_ref[:, DH:] = ones
        wa_ref[:, :DH] = vw_ref[...]
        wa_ref[:, DH:] = ones

    for r in range(R):
        qa_ref[r * tq:(r + 1) * tq, :DH] = q_ref[:, r * DH:(r + 1) * DH]
    q = qa_ref[:, :DH]
    trow = t0 + lax.broadcasted_iota(jnp.int32, (tq, 1), 0)
    tlane = t0 + lax.broadcasted_iota(jnp.int32, (1, tq), 1)
    head = lambda a, r: a[r * tq:(r + 1) * tq]

    cend = lax.broadcasted_iota(jnp.int32, (1, ncp), 1) * CMP_STRIDE + (CMP_LEN - 1)
    valid_c = cend <= trow
    bias_c = jnp.where(valid_c, 0.0, NEG_INF)
    keep_c = jnp.where(valid_c, 1.0, 0.0)
    s = _dot_nt(q, kc_ref[...])
    psum = jnp.zeros((tq, ncp), F32)
    for r in range(R):
        s_r = head(s, r) + bias_c
        p = jnp.exp2(s_r - jnp.max(s_r, axis=-1, keepdims=True)) * keep_c
        p = p / jnp.maximum(jnp.sum(p, axis=-1, keepdims=True), 1e-30)
        psum = psum + p
        p_ref[r * tq:(r + 1) * tq, :ncp] = p.astype(BF16)
    o_cmp = _dot(p_ref[:, :ncp], vc_ref[...])

    wlen = (-(-(WIN_LEN - 1) // BLOCK)) * BLOCK + tq
    w0 = pl.multiple_of(jnp.maximum(t0 + tq - wlen, 0), BLOCK)
    kpos = w0 + lax.broadcasted_iota(jnp.int32, (1, wlen), 1)
    bias_w = jnp.where((kpos <= trow) & (trow - kpos <= WIN_LEN - 1), 0.0, NEG_INF)
    s = _dot_nt(q, kw_ref[pl.ds(w0, wlen), :])
    for r in range(R):
        s_r = head(s, r) + bias_w
        pw_ref[r * tq:(r + 1) * tq, :] = jnp.exp2(
            s_r - jnp.max(s_r, axis=-1, keepdims=True)).astype(BF16)
    ow = _dot(pw_ref[...], wa_ref[pl.ds(w0, wlen), :])
    o_win = ow[:, :DH] / ow[:, DH:DH + 1]
    gate = gate_ref[...]
    for r in range(R):
        y_ref[r * tq:(r + 1) * tq, :] = (gate[:, r:r + 1] * head(o_cmp, r) +
                                         gate[:, 2 * R + r:2 * R + r + 1] * head(o_win, r))

    hi = psum.astype(BF16)
    rem = psum - hi.astype(F32)
    mid = rem.astype(BF16)
    lo = (rem - mid.astype(F32)).astype(BF16)
    ovt = ovt_ref[...]
    p_slc = _dot_nt(ovt, hi) + _dot_nt(ovt, mid) + _dot_nt(ovt, lo)
    jj = lax.broadcasted_iota(jnp.int32, (n_s, 1), 0)
    blk_t = tlane // SEL_LEN
    forced = (jj == 0) | (jj == blk_t) | (jj == blk_t - 1)
    score = jnp.where(jj <= blk_t, p_slc + jnp.where(forced, FORCE_BONUS, 0.0), -1.0)
    nch = n_s // 8
    chunks = [score[c * 8:(c + 1) * 8] for c in range(nch)]
    cnt = [jnp.zeros((8, tq), F32) for _ in range(nch)]
    sub = lax.broadcasted_iota(jnp.int32, (8, 1), 0)
    for i in range(n_s):
        row_i = score[i:i + 1]
        for c in range(nch):
            if c * 8 > i:
                beats = row_i >= chunks[c]
            elif c * 8 + 7 < i:
                beats = row_i > chunks[c]
            else:
                beats = (row_i > chunks[c]) | ((row_i == chunks[c]) & (sub + c * 8 > i))
            cnt[c] = cnt[c] + jnp.where(beats, 1.0, 0.0)
    k_sel = float(min(SEL_TOPK, n_s))
    bias_t = jnp.concatenate([jnp.where(cc < k_sel, 0.0, NEG_INF) for cc in cnt] +
                             [jnp.zeros((DH - n_s, tq), F32)], axis=0)
    bias_q = bias_t.T.astype(BF16)
    for r in range(R):
        qa_ref[r * tq:(r + 1) * tq, DH:] = bias_q

    m_ref[...] = jnp.full(m_ref.shape, NEG_INF, F32)
    acc_ref[...] = jnp.zeros(acc_ref.shape, F32)

    def scores(kt, s_ref):
        k0 = pl.multiple_of(kt * tk, tk)
        s_ref[...] = _dot_nt(qa_ref[...], ka_ref[pl.ds(k0, tk), :])

    def tile(kt, bufs, causal):
        s_ref, pt_ref, al_ref = bufs
        if causal:
            kpos = kt * tk + lax.broadcasted_iota(jnp.int32, (1, tk), 1)
            bias_d = jnp.where(kpos <= trow, 0.0, NEG_INF)
        for r in range(R):
            s_r = s_ref[r * tq:(r + 1) * tq, :]
            if causal:
                s_r = s_r + bias_d
            m_old = m_ref[r]
            m_new = jnp.maximum(m_old, jnp.max(s_r, axis=-1, keepdims=True))
            pt_ref[r * tq:(r + 1) * tq, :] = jnp.exp2(s_r - m_new).astype(BF16)
            al_ref[r] = jnp.exp2(m_old - m_new)
            m_ref[r] = m_new
        k0 = pl.multiple_of(kt * tk, tk)
        pv = _dot(pt_ref[...], va_ref[pl.ds(k0, tk), :])
        acc_ref[...] = al_ref[...] * acc_ref[...] + pv.reshape(R, tq, 2 * DH)

    even = (s0_ref, p0_ref, al0_ref)
    odd = (s1_ref, p1_ref, al1_ref)
    kd = t0 // tk
    scores(0, s0_ref)

    def pair(i, carry):
        scores(2 * i + 1, s1_ref)
        tile(2 * i, even, False)
        scores(2 * i + 2, s0_ref)
        tile(2 * i + 1, odd, False)
        return carry

    lax.fori_loop(0, kd // 2, pair, 0)

    @pl.when(kd % 2 == 0)
    def _():
        tile(kd, even, True)

    @pl.when(kd % 2 == 1)
    def _():
        scores(kd, s1_ref)
        tile(kd - 1, even, False)
        tile(kd, odd, True)

    acc = acc_ref[...]
    o_slc = (acc[:, :, :DH] / acc[:, :, DH:DH + 1]).reshape(R * tq, DH)

    gate = gate_ref[...]
    for r in range(R):
        y = y_ref[r * tq:(r + 1) * tq, :] + gate[:, R + r:R + r + 1] * head(o_slc, r)
        out_ref[:, r * DH:(r + 1) * DH] = y.astype(BF16)


V_ROWS = DH + 16


def _nsa_t_kernel(q_ref, ks_ref, vs_ref, kw_ref, vw_ref, kc_ref, vc_ref, gate_ref, ovt_ref, et_ref,
                  out_ref, qt_ref, ka_ref, vt_ref, wt_ref, vct_ref, pc_ref, sw_ref, pw_ref, y_ref,
                  s0_ref, s1_ref, p0_ref, p1_ref, m_ref, al0_ref, al1_ref, acc_ref,
                  *, tq, tk, s_len, ncp, n_s):
    R = NSA_R
    qi = pl.program_id(2)
    t0 = qi * tq
    lanes = lambda r: slice(r * tq, (r + 1) * tq)

    @pl.when(qi == 0)
    def _():
        ka_ref[:, :DH] = ks_ref[...]
        ka_ref[:, DH:] = et_ref[...]
        for c in range(s_len // tk):
            vt_ref[c, :DH, :] = vs_ref[c * tk:(c + 1) * tk, :].astype(F32).T.astype(BF16)
            vt_ref[c, DH:, :] = jnp.ones((V_ROWS - DH, tk), BF16)
        for c in range(s_len // BLOCK):
            wt_ref[c, :DH, :] = vw_ref[c * BLOCK:(c + 1) * BLOCK, :].astype(F32).T.astype(BF16)
            wt_ref[c, DH:, :] = jnp.ones((V_ROWS - DH, BLOCK), BF16)
        vct_ref[...] = vc_ref[...].astype(F32).T.astype(BF16)

    for r in range(R):
        qt_ref[:DH, lanes(r)] = q_ref[:, r * DH:(r + 1) * DH].astype(F32).T.astype(BF16)
    qt = qt_ref[:DH, :]
    tlane = t0 + lax.broadcasted_iota(jnp.int32, (1, tq), 1)
    gate_t = gate_ref[...].T

    def softmax_cols(s_ref, p_ref, n, r, bias=None, m_old=None):
        def load(lo, hi):
            s = s_ref[lo:hi, lanes(r)]
            return s if bias is None else s + bias[lo:hi]
        h, q = n // 2, n // 4
        m = jnp.maximum(jnp.max(load(0, h), axis=0, keepdims=True),
                        jnp.max(load(h, n), axis=0, keepdims=True))
        if m_old is not None:
            m = jnp.maximum(m, m_old)
        for c in range(4):
            p_ref[c * q:(c + 1) * q, lanes(r)] = jnp.exp2(load(c * q, (c + 1) * q) - m).astype(BF16)
        return m

    cend = lax.broadcasted_iota(jnp.int32, (ncp, 1), 0) * CMP_STRIDE + (CMP_LEN - 1)
    valid_c = cend <= tlane
    bias_c = jnp.where(valid_c, 0.0, NEG_INF)
    keep_c = jnp.where(valid_c, 1.0, 0.0)
    s1_ref[:ncp, :] = _dot(kc_ref[...], qt)
    psum = jnp.zeros((ncp, tq), F32)
    for r in range(R):
        s_r = s1_ref[:ncp, lanes(r)] + bias_c
        p = jnp.exp2(s_r - jnp.max(s_r, axis=0, keepdims=True)) * keep_c
        p = p / jnp.maximum(jnp.sum(p, axis=0, keepdims=True), 1e-30)
        psum = psum + p
        pc_ref[:, lanes(r)] = p.astype(BF16)
    o_cmp = _dot(vct_ref[...], pc_ref[...])

    wlen = (-(-(WIN_LEN - 1) // BLOCK)) * BLOCK + tq
    w0 = pl.multiple_of(jnp.maximum(t0 + tq - wlen, 0), BLOCK)
    kpos = w0 + lax.broadcasted_iota(jnp.int32, (wlen, 1), 0)
    bias_w = jnp.where((kpos <= tlane) & (tlane - kpos <= WIN_LEN - 1), 0.0, NEG_INF)
    sw_ref[...] = _dot(kw_ref[pl.ds(w0, wlen), :], qt)
    for r in range(R):
        softmax_cols(sw_ref, pw_ref, wlen, r, bias=bias_w)
    wb = w0 // BLOCK
    ow = _dot(wt_ref[wb], pw_ref[:BLOCK, :])
    for jb in range(1, wlen // BLOCK):
        ow = ow + _dot(wt_ref[wb + jb], pw_ref[jb * BLOCK:(jb + 1) * BLOCK, :])
    o_win = ow[:DH] / ow[DH:DH + 1]
    for r in range(R):
        y_ref[:, lanes(r)] = (gate_t[r:r + 1] * o_cmp[:, lanes(r)] +
                              gate_t[2 * R + r:2 * R + r + 1] * o_win[:, lanes(r)])

    hi = psum.astype(BF16)
    rem = psum - hi.astype(F32)
    mid = rem.astype(BF16)
    lo = (rem - mid.astype(F32)).astype(BF16)
    ovt = ovt_ref[...]
    p_slc = _dot(ovt, hi) + _dot(ovt, mid) + _dot(ovt, lo)
    jj = lax.broadcasted_iota(jnp.int32, (n_s, 1), 0)
    blk_t = tlane // SEL_LEN
    forced = (jj == 0) | (jj == blk_t) | (jj == blk_t - 1)
    score = jnp.where(jj <= blk_t, p_slc + jnp.where(forced, FORCE_BONUS, 0.0), -1.0)
    nch = n_s // 8
    chunks = [score[c * 8:(c + 1) * 8] for c in range(nch)]
    cnt = [jnp.zeros((8, tq), F32) for _ in range(nch)]
    sub = lax.broadcasted_iota(jnp.int32, (8, 1), 0)
    for i in range(n_s):
        row_i = score[i:i + 1]
        for c in range(nch):
            if c * 8 > i:
                beats = row_i >= chunks[c]
            elif c * 8 + 7 < i:
                beats = row_i > chunks[c]
            else:
                beats = (row_i > chunks[c]) | ((row_i == chunks[c]) & (sub + c * 8 > i))
            cnt[c] = cnt[c] + jnp.where(beats, 1.0, 0.0)
    k_sel = float(min(SEL_TOPK, n_s))
    bias_t = jnp.concatenate([jnp.where(cc < k_sel, 0.0, NEG_INF) for cc in cnt] +
                             [jnp.zeros((DH - n_s, tq), F32)], axis=0).astype(BF16)
    for r in range(R):
        qt_ref[DH:, lanes(r)] = bias_t

    m_ref[...] = jnp.full(m_ref.shape, NEG_INF, F32)
    acc_ref[...] = jnp.zeros(acc_ref.shape, F32)

    def scores(kt, s_ref):
        k0 = pl.multiple_of(kt * tk, tk)
        s_ref[...] = _dot(ka_ref[pl.ds(k0, tk), :], qt_ref[...])

    def tile(kt, bufs, causal):
        s_ref, pt_ref, al_ref = bufs
        if causal:
            kpos = kt * tk + lax.broadcasted_iota(jnp.int32, (tk, 1), 0)
            bias_d = jnp.where(kpos <= tlane, 0.0, NEG_INF)
        for r in range(R):
            m_old = m_ref[:, lanes(r)]
            m_new = softmax_cols(s_ref, pt_ref, tk, r, bias=bias_d if causal else None, m_old=m_old)
            al_ref[:, lanes(r)] = jnp.exp2(m_old - m_new)
            m_ref[:, lanes(r)] = m_new
        pv = _dot(vt_ref[kt], pt_ref[...])
        acc_ref[...] = al_ref[...] * acc_ref[...] + pv

    even = (s0_ref, p0_ref, al0_ref)
    odd = (s1_ref, p1_ref, al1_ref)
    kd = t0 // tk
    scores(0, s0_ref)

    def pair(i, carry):
        scores(2 * i + 1, s1_ref)
        tile(2 * i, even, False)
        scores(2 * i + 2, s0_ref)
        tile(2 * i + 1, odd, False)
        return carry

    lax.fori_loop(0, kd // 2, pair, 0)

    @pl.when(kd % 2 == 0)
    def _():
        tile(kd, even, True)

    @pl.when(kd % 2 == 1)
    def _():
        scores(kd, s1_ref)
        tile(kd - 1, even, False)
        tile(kd, odd, True)

    acc = acc_ref[...]
    o_slc = acc[:DH] / acc[DH:DH + 1]
    for r in range(R):
        y = y_ref[:, lanes(r)] + gate_t[R + r:R + r + 1] * o_slc[:, lanes(r)]
        out_ref[:, r * DH:(r + 1) * DH] = y.T.astype(BF16)


def _nsa(o_r, o_p, o_g, kcmp, vcmp, b, s, tq, tk):
    ncp = s // CMP_STRIDE
    n_s = s // SEL_LEN
    assert n_s % 8 == 0 and n_s <= DH
    r_view = o_r.reshape(b, s, N_R)
    p_view = o_p.reshape(b, s, N_P)
    g_view = o_g.reshape(b, s, N_G)
    c = np.arange(ncp)[None, :]
    j = np.arange(n_s)[:, None]
    n_c = (s - CMP_LEN) // CMP_STRIDE + 1
    ovt = ((c * CMP_STRIDE <= j * SEL_LEN + SEL_LEN - 1) &
           (c * CMP_STRIDE + CMP_LEN - 1 >= j * SEL_LEN) & (c < n_c))
    ovt = jnp.asarray(ovt.astype(np.float32), BF16)
    et = jnp.asarray((np.arange(s)[:, None] // SEL_LEN == np.arange(DH)[None, :])
                     .astype(np.float32), BF16)
    wq = NSA_R * DH
    ks_col = W_QB // DH
    kw_col = ks_col + NSA_G
    gate_col = (2 * D_MODEL) // DH
    wlen = (-(-(WIN_LEN - 1) // BLOCK)) * BLOCK + tq
    kv = (None, s, DH)
    out = pl.pallas_call(
        functools.partial(_nsa_kernel, tq=tq, tk=tk, s_len=s, ncp=ncp, n_s=n_s),
        grid=(b, NSA_G, s // tq),
        in_specs=[pl.BlockSpec((None, tq, wq), lambda bi, g, qi: (bi, qi, g)),
                  pl.BlockSpec(kv, lambda bi, g, qi: (bi, 0, ks_col + g)),
                  pl.BlockSpec(kv, lambda bi, g, qi: (bi, 0, g)),
                  pl.BlockSpec(kv, lambda bi, g, qi: (bi, 0, kw_col + g)),
                  pl.BlockSpec(kv, lambda bi, g, qi: (bi, 0, NSA_G + g)),
                  pl.BlockSpec((None, None, ncp, DH), lambda bi, g, qi: (bi, g, 0, 0)),
                  pl.BlockSpec((None, None, ncp, DH), lambda bi, g, qi: (bi, g, 0, 0)),
                  pl.BlockSpec((None, tq, DH), lambda bi, g, qi: (bi, qi, gate_col + g)),
                  pl.BlockSpec((n_s, ncp), lambda bi, g, qi: (0, 0)),
                  pl.BlockSpec((s, DH), lambda bi, g, qi: (0, 0))],
        out_specs=pl.BlockSpec((None, tq, wq), lambda bi, g, qi: (bi, qi, g)),
        out_shape=jax.ShapeDtypeStruct((b, s, W_QB), BF16),
        scratch_shapes=[pltpu.VMEM((NSA_R * tq, 2 * DH), BF16),
                        pltpu.VMEM((s, 2 * DH), BF16),
                        pltpu.VMEM((s, 2 * DH), BF16),
                        pltpu.VMEM((s, 2 * DH), BF16),
                        pltpu.VMEM((NSA_R * tq, ncp), BF16),
                        pltpu.VMEM((NSA_R * tq, wlen), BF16),
                        pltpu.VMEM((NSA_R * tq, DH), F32),
                        pltpu.VMEM((NSA_R * tq, tk), F32),
                        pltpu.VMEM((NSA_R * tq, tk), F32),
                        pltpu.VMEM((NSA_R * tq, tk), BF16),
                        pltpu.VMEM((NSA_R * tq, tk), BF16),
                        pltpu.VMEM((NSA_R, tq, 1), F32),
                        pltpu.VMEM((NSA_R, tq, 1), F32),
                        pltpu.VMEM((NSA_R, tq, 1), F32),
                        pltpu.VMEM((NSA_R, tq, 2 * DH), F32)],
        compiler_params=_cparams(("arbitrary", "arbitrary", "arbitrary")),
        name="nsa_attention",
    )(r_view, r_view, p_view, r_view, p_view, kcmp, vcmp, g_view, ovt, et)
    return out.reshape(b * s, W_QB)


def _out_kernel(ya_ref, yb_ref, ga_ref, gb_ref, x_ref, woa_ref, wob_ref, wo_ref, out_ref):
    pa = _dot(ya_ref[...], woa_ref[...])
    pb = _dot(yb_ref[...], wob_ref[...])
    y = (ga_ref[...] * pa + gb_ref[...] * pb).astype(BF16)
    out_ref[...] = x_ref[...] + _dot(y, wo_ref[...])


def _merge_out(ya, yb, o_g, x2, woa, wob, wo, tm):
    m = x2.shape[0]
    row = lambda w: pl.BlockSpec((tm, w), lambda i: (i, 0))
    const = lambda shape: pl.BlockSpec(shape, lambda i: (0, 0), pipeline_mode=pl.Buffered(1))
    return pl.pallas_call(
        _out_kernel,
        grid=(m // tm,),
        in_specs=[row(DIL_OUT), row(W_QB),
                  pl.BlockSpec((tm, D_MODEL), lambda i: (i, 0)),
                  pl.BlockSpec((tm, D_MODEL), lambda i: (i, 1)),
                  row(D_MODEL),
                  const((DIL_OUT, D_MODEL)), const((W_QB, D_MODEL)), const((D_MODEL, D_MODEL))],
        out_specs=row(D_MODEL),
        out_shape=jax.ShapeDtypeStruct((m, D_MODEL), F32),
        compiler_params=_cparams(("arbitrary",)),
        name="merge_out",
    )(ya, yb, o_g, o_g, x2, woa, wob, wo)


def _ffn_kernel(x_ref, g_ref, wg_ref, wu_ref, wd_ref, gf_ref, out_ref, h_ref, *, nf, final):
    f = pl.program_id(1)

    @pl.when(f == 0)
    def _():
        x = x_ref[...]
        h_ref[...] = _rms(x, g_ref[...]).astype(BF16)
        out_ref[...] = x

    h = h_ref[...]
    a = _dot(h, wg_ref[...])
    u = _dot(h, wu_ref[...])
    act = (a * jax.nn.sigmoid(a) * u).astype(BF16)
    out_ref[...] += _dot(act, wd_ref[...])

    if final:
        @pl.when(f == nf - 1)
        def _():
            out_ref[...] = _rms(out_ref[...], gf_ref[...])


def _ffn(x2, g, wg, wu, wd, gf, tm, tf, final):
    m = x2.shape[0]
    nf = D_FF // tf
    return pl.pallas_call(
        functools.partial(_ffn_kernel, nf=nf, final=final),
        grid=(m // tm, nf),
        in_specs=[pl.BlockSpec((tm, D_MODEL), lambda i, f: (i, 0)),
                  pl.BlockSpec((1, D_MODEL), lambda i, f: (0, 0)),
                  pl.BlockSpec((D_MODEL, tf), lambda i, f: (0, f)),
                  pl.BlockSpec((D_MODEL, tf), lambda i, f: (0, f)),
                  pl.BlockSpec((tf, D_MODEL), lambda i, f: (f, 0)),
                  pl.BlockSpec((1, D_MODEL), lambda i, f: (0, 0))],
        out_specs=pl.BlockSpec((tm, D_MODEL), lambda i, f: (i, 0)),
        out_shape=jax.ShapeDtypeStruct((m, D_MODEL), F32),
        scratch_shapes=[pltpu.VMEM((tm, D_MODEL), BF16)],
        compiler_params=_cparams(("arbitrary", "arbitrary")),
        name="ffn",
    )(x2, g, wg, wu, wd, gf)


def _prep_w_in(w):
    cols = (W_QA, W_QA, W_QA, W_QB, W_KV, W_KV, W_KV, W_KV, W_KV, W_KV,
            NSA_Q_HEADS * 3, D_MODEL, D_MODEL)
    offs = np.concatenate([[0], np.cumsum(cols)])
    seg = [w[:, int(offs[i]):int(offs[i + 1])] for i in range(len(cols))]
    qa, ka, va, qb, kc, vc, ks, vs, kw, vw, gate_b, gam, gbm = seg
    gate_b = gate_b.reshape(D_MODEL, NSA_G, NSA_R, 3).transpose(0, 1, 3, 2).reshape(D_MODEL, NSA_G, 3 * NSA_R)
    gate_b = jnp.pad(gate_b, ((0, 0), (0, 0), (0, DH - 3 * NSA_R))).reshape(D_MODEL, NSA_G * DH)
    gate_b = jnp.pad(gate_b, ((0, 0), (0, N_G - 2 * D_MODEL - NSA_G * DH)))
    return jnp.concatenate([qa, ka, va, qb, ks, kw, vs, vw, kc, vc, gam, gbm, gate_b],
                           axis=1).astype(BF16)


def kernel(x, positions, ln_mix, w_in, cmp_pos_k, cmp_pos_v, cmp_w1_k, cmp_w2_k, cmp_w1_v, cmp_w2_v,
           w_out_a, w_out_b, w_out, ln_ffn, w_ffn_gate, w_ffn_up, w_ffn_down, ln_final):
    b, s, d = x.shape
    depth = w_in.shape[0]
    assert d == D_MODEL and s % DIL_UNIT == 0
    m = b * s
    tm_in = min(1024, m)
    tm_out = min(256, m)
    tm_ffn = min(512, m)
    tq, tk = 128, min(512, s)
    ncp = s // CMP_STRIDE

    pos_f = positions.astype(F32)
    cos, sin = _rope_tables(pos_f.reshape(m, 1), tm_in)
    blk_end = np.minimum(np.arange(ncp) * CMP_STRIDE + CMP_LEN - 1, s - 1)
    cosc, sinc = _rope_tables(pos_f[:, blk_end].reshape(b * ncp, 1), ncp)
    cosc = cosc.reshape(b, ncp, DH)
    sinc = sinc.reshape(b, ncp, DH)

    cs = np.ones((1, N_ALL), np.float32)
    cs[:, :W_QA] = Q_SCALE
    cs[:, N_A:N_A + W_QB] = Q_SCALE
    cs = jnp.asarray(cs)

    x2 = x.reshape(m, d)
    for l in range(depth):
        w_l = _prep_w_in(w_in[l])
        o_a, o_r, o_p, o_c, o_g = _inproj(x2, ln_mix[l][None, :], w_l, cs, cos, sin, tm_in)
        ya = _dilated(o_a, b, s)
        kcmp, vcmp = _compress(
            o_c, b, s, cmp_pos_k[l], cmp_pos_v[l],
            cmp_w1_k[l].reshape(CMP_LEN, DH, CMP_HIDDEN).astype(BF16),
            cmp_w1_v[l].reshape(CMP_LEN, DH, CMP_HIDDEN).astype(BF16),
            cmp_w2_k[l].astype(BF16), cmp_w2_v[l].astype(BF16), cosc, sinc)
        yb = _nsa(o_r, o_p, o_g, kcmp, vcmp, b, s, tq, tk)
        x2 = _merge_out(ya, yb, o_g, x2, w_out_a[l].astype(BF16), w_out_b[l].astype(BF16),
                        w_out[l].astype(BF16), tm_out)
        x2 = _ffn(x2, ln_ffn[l][None, :], w_ffn_gate[l].astype(BF16), w_ffn_up[l].astype(BF16),
                  w_ffn_down[l].astype(BF16), ln_final[None, :], tm_ffn, 512, l == depth - 1)
    return x2.reshape(b, s, d)
```

```python
import functools
import math

import numpy as np
import jax
import jax.numpy as jnp
from jax import lax
from jax.experimental import pallas as pl
from jax.experimental.pallas import tpu as pltpu

F32 = jnp.float32
BF16 = jnp.bfloat16

D_MODEL = 2048
DH = 128
HALF = DH // 2
ROPE_THETA = 10000.0
NORM_EPS = 1e-6
NEG_INF = -1e30
BLOCK = 128

DIL_GROUPS = ((128, 1), (512, 4), (2048, 16))
DIL_HPG = 4
DIL_HEADS = DIL_HPG * len(DIL_GROUPS)
DIL_OUT = DIL_HPG * DH
DIL_UNIT = DIL_GROUPS[-1][1] * BLOCK

NSA_Q_HEADS = 16
NSA_G = 2
NSA_R = NSA_Q_HEADS // NSA_G
CMP_LEN = 32
CMP_STRIDE = 16
CMP_HIDDEN = 256
SEL_LEN = 64
SEL_TOPK = 16
WIN_LEN = 512
FORCE_BONUS = 1e4
D_FF = 5632

W_QA = DIL_HEADS * DH
W_QB = NSA_Q_HEADS * DH
W_KV = NSA_G * DH
N_A = 3 * W_QA
N_R = W_QB + 2 * W_KV
N_P = 2 * W_KV
N_C = 2 * W_KV
TN_IN = 512
N_G = 2 * D_MODEL + TN_IN
N_ALL = N_A + N_R + N_P + N_C + N_G
VMEM_LIMIT = 56 * 1024 * 1024
Q_SCALE = DH ** -0.5 * math.log2(math.e)


def _cparams(sem):
    return pltpu.CompilerParams(dimension_semantics=sem, vmem_limit_bytes=VMEM_LIMIT)


def _dot_nt(a, b):
    return lax.dot_general(a, b, (((1,), (1,)), ((), ())), preferred_element_type=F32)


def _dot(a, b):
    return jnp.dot(a, b, preferred_element_type=F32)


def _rope_tab_kernel(pos_ref, inv_ref, sgn_ref, cos_ref, sin_ref):
    ang = pos_ref[...] * inv_ref[...]
    cos_ref[...] = jnp.cos(ang)
    sin_ref[...] = jnp.sin(ang) * sgn_ref[...]


def _rope_tables(pos_f, tm):
    m = pos_f.shape[0]
    inv = ROPE_THETA ** (-2.0 * jnp.arange(HALF, dtype=F32) / DH)
    inv = jnp.concatenate([inv, inv])[None, :]
    sgn = jnp.concatenate([-jnp.ones((HALF,), F32), jnp.ones((HALF,), F32)])[None, :]
    return pl.pallas_call(
        _rope_tab_kernel,
        grid=(m // tm,),
        in_specs=[pl.BlockSpec((tm, 1), lambda i: (i, 0)),
                  pl.BlockSpec((1, DH), lambda i: (0, 0)),
                  pl.BlockSpec((1, DH), lambda i: (0, 0))],
        out_specs=[pl.BlockSpec((tm, DH), lambda i: (i, 0)),
                   pl.BlockSpec((tm, DH), lambda i: (i, 0))],
        out_shape=[jax.ShapeDtypeStruct((m, DH), F32)] * 2,
        compiler_params=_cparams(("arbitrary",)),
        name="rope_tables",
    )(pos_f, inv, sgn)


def _rope(a, cos, sin_signed):
    return a * cos + pltpu.roll(a, HALF, 1) * sin_signed


def _rms(x, g):
    ms = jnp.mean(x * x, axis=-1, keepdims=True)
    return x * lax.rsqrt(ms + NORM_EPS) * g


T_AROPE = 2 * W_QA // TN_IN
T_A = N_A // TN_IN
T_R = T_A + N_R // TN_IN
T_P = T_R + N_P // TN_IN
T_C = T_P + N_C // TN_IN
T_ALL = N_ALL // TN_IN


def _inproj_kernel(x_ref, g_ref, w_ref, cs_ref, cos_ref, sin_ref,
                   oa_ref, or_ref, op_ref, oc_ref, og_ref, h_ref):
    j = pl.program_id(1)

    @pl.when(j == 0)
    def _():
        h_ref[...] = _rms(x_ref[...], g_ref[...]).astype(BF16)

    tm = h_ref.shape[0]
    halves = [slice(k * (tm // 2), (k + 1) * (tm // 2)) for k in range(2)]

    def region(out_ref, epilogue):
        accs = [_dot(h_ref[rows, :], w_ref[...]) for rows in halves]
        for rows, acc in zip(halves, accs):
            epilogue(out_ref, rows, acc)

    def roped(out_ref, rows, acc):
        cos = cos_ref[rows, :]
        sin = sin_ref[rows, :]
        for hh in range(TN_IN // DH):
            sl = slice(hh * DH, (hh + 1) * DH)
            r = _rope(acc[:, sl], cos, sin) * cs_ref[:, sl]
            out_ref[rows, sl] = r.astype(out_ref.dtype)

    def plain(out_ref, rows, acc):
        out_ref[rows, :] = acc.astype(out_ref.dtype)

    def sigmoid(out_ref, rows, acc):
        out_ref[rows, :] = jax.nn.sigmoid(acc)

    pl.when(j < T_AROPE)(lambda: region(oa_ref, roped))
    pl.when((j >= T_AROPE) & (j < T_A))(lambda: region(oa_ref, plain))
    pl.when((j >= T_A) & (j < T_R))(lambda: region(or_ref, roped))
    pl.when((j >= T_R) & (j < T_P))(lambda: region(op_ref, plain))
    pl.when((j >= T_P) & (j < T_C))(lambda: region(oc_ref, plain))
    pl.when(j >= T_C)(lambda: region(og_ref, sigmoid))


def _inproj(x2, g, w, layer, cs, cos, sin, tm):
    m = x2.shape[0]

    def region(lo, hi):
        return pl.BlockSpec((tm, TN_IN), lambda i, j: (i, jnp.clip(j - lo, 0, hi - lo - 1)))

    return pl.pallas_call(
        _inproj_kernel,
        grid=(m // tm, T_ALL),
        in_specs=[pl.BlockSpec((tm, D_MODEL), lambda i, j: (i, 0)),
                  pl.BlockSpec((1, D_MODEL), lambda i, j: (0, 0)),
                  pl.BlockSpec((None, D_MODEL, TN_IN), lambda i, j: (layer, 0, j)),
                  pl.BlockSpec((1, TN_IN), lambda i, j: (0, j)),
                  pl.BlockSpec((tm, DH), lambda i, j: (i, 0)),
                  pl.BlockSpec((tm, DH), lambda i, j: (i, 0))],
        out_specs=[region(0, T_A), region(T_A, T_R), region(T_R, T_P), region(T_P, T_C),
                   region(T_C, T_ALL)],
        out_shape=[jax.ShapeDtypeStruct((m, N_A), F32),
                   jax.ShapeDtypeStruct((m, N_R), BF16),
                   jax.ShapeDtypeStruct((m, N_P), BF16),
                   jax.ShapeDtypeStruct((m, N_C), F32),
                   jax.ShapeDtypeStruct((m, N_G), F32)],
        scratch_shapes=[pltpu.VMEM((tm, D_MODEL), BF16)],
        compiler_params=_cparams(("arbitrary", "arbitrary")),
        name="inproj",
    )(x2, g, w, cs, cos, sin)


def _rows(start, dil):
    return pl.ds(start, BLOCK) if dil == 1 else pl.ds(start, BLOCK, stride=dil)


def _dil_kernel(*refs):
    out_ref, o_scr, lse_scr = refs[-3:]
    u = pl.program_id(1)
    row = lax.broadcasted_iota(jnp.int32, (BLOCK, 2 * BLOCK), 0)
    col = lax.broadcasted_iota(jnp.int32, (BLOCK, 2 * BLOCK), 1)
    band = ((col < BLOCK) & (col >= row)) | ((col >= BLOCK) & (col - BLOCK <= row))
    band_first = band & ((col >= BLOCK) | (u > 0))
    for gi, (_, dil) in enumerate(DIL_GROUPS):
        q_ref, kc_ref, kp_ref, vc_ref, vp_ref = refs[5 * gi:5 * gi + 5]
        span = BLOCK * dil
        for rho in range(dil):
            for ub in range(DIL_UNIT // span):
                cur = _rows(ub * span + rho, dil)
                if ub == 0:
                    kp, vp, mask = kp_ref[_rows(rho, dil), :], vp_ref[_rows(rho, dil), :], band_first
                else:
                    prv = _rows((ub - 1) * span + rho, dil)
                    kp, vp, mask = kc_ref[prv, :], vc_ref[prv, :], band
                q = q_ref[cur, :].astype(BF16)
                k = jnp.concatenate([kp, kc_ref[cur, :]], axis=0).astype(BF16)
                v = jnp.concatenate([vp, vc_ref[cur, :]], axis=0).astype(BF16)
                s = jnp.where(mask, _dot_nt(q, k), NEG_INF)
                m = jnp.max(s, axis=-1, keepdims=True)
                p = jnp.exp2(s - m)
                l = jnp.sum(p, axis=-1, keepdims=True)
                o_scr[gi, cur, :] = _dot(p.astype(BF16), v) / l
                lse_scr[gi, cur, :] = jnp.broadcast_to(m + jnp.log2(l), (BLOCK, DH))
    l0, l1, l2 = lse_scr[0], lse_scr[1], lse_scr[2]
    m = jnp.maximum(jnp.maximum(l0, l1), l2)
    e0, e1, e2 = jnp.exp2(l0 - m), jnp.exp2(l1 - m), jnp.exp2(l2 - m)
    ya = (e0 * o_scr[0] + e1 * o_scr[1] + e2 * o_scr[2]) / (e0 + e1 + e2)
    out_ref[...] = ya.astype(BF16)


def _dilated(o_a, b, s):
    a_view = o_a.reshape(b, s, N_A)
    in_specs, args = [], []
    for gi, (_, dil) in enumerate(DIL_GROUPS):
        span = BLOCK * dil
        per = DIL_UNIT // span

        def cur(colbase, gi=gi):
            return pl.BlockSpec((None, DIL_UNIT, DH),
                                lambda bi, u, j: (bi, u, colbase + gi * DIL_HPG + j))

        def prev(colbase, gi=gi, span=span, per=per):
            return pl.BlockSpec((None, span, DH),
                                lambda bi, u, j: (bi, jnp.maximum(u * per - 1, 0),
                                                  colbase + gi * DIL_HPG + j))

        kcol, vcol = W_QA // DH, 2 * W_QA // DH
        in_specs += [cur(0), cur(kcol), prev(kcol), cur(vcol), prev(vcol)]
        args += [a_view] * 5
    out = pl.pallas_call(
        _dil_kernel,
        grid=(b, s // DIL_UNIT, DIL_HPG),
        in_specs=in_specs,
        out_specs=pl.BlockSpec((None, DIL_UNIT, DH), lambda bi, u, j: (bi, u, j)),
        out_shape=jax.ShapeDtypeStruct((b, s, DIL_OUT), BF16),
        scratch_shapes=[pltpu.VMEM((len(DIL_GROUPS), DIL_UNIT, DH), F32),
                        pltpu.VMEM((len(DIL_GROUPS), DIL_UNIT, DH), F32)],
        compiler_params=_cparams(("arbitrary", "arbitrary", "arbitrary")),
        name="dilated",
    )(*args)
    return out.reshape(b * s, DIL_OUT)


def _cmp_kernel(xk0_ref, xk1_ref, xv0_ref, xv1_ref, posk_ref, posv_ref, w1k_ref, w1v_ref,
                w2k_ref, w2v_ref, cos_ref, sin_ref, kc_ref, vc_ref, *, ncp):
    half = CMP_LEN // 2
    x_refs = ((xk0_ref, xk1_ref), (xv0_ref, xv1_ref))
    for kind in range(2):
        pos_ref, w1_ref, w2_ref, o_ref = ((posk_ref, w1k_ref, w2k_ref, kc_ref) if kind == 0
                                          else (posv_ref, w1v_ref, w2v_ref, vc_ref))
        for g in range(NSA_G):
            x_ref = x_refs[kind][g]
            a = jnp.zeros((ncp, CMP_HIDDEN), F32)
            bm = jnp.zeros((ncp, CMP_HIDDEN), F32)
            for l in range(half):
                x = x_ref[pl.ds(l, ncp, stride=CMP_STRIDE), :]
                a = a + _dot((x + pos_ref[l:l + 1, :]).astype(BF16), w1_ref[l])
                bm = bm + _dot((x + pos_ref[half + l:half + l + 1, :]).astype(BF16),
                               w1_ref[half + l])
            hid = a + pltpu.roll(bm, ncp - 1, 0)
            hid = jax.nn.gelu(hid, approximate=True).astype(BF16)
            out = _dot(hid, w2_ref[...])
            if kind == 0:
                out = _rope(out, cos_ref[...], sin_ref[...])
            o_ref[g] = out.astype(BF16)


def _compress(o_c, b, s, posk, posv, w1k, w1v, w2k, w2v, cosc, sinc):
    ncp = s // CMP_STRIDE
    x_view = o_c.reshape(b, s, N_C)
    full = lambda shape: pl.BlockSpec(shape, lambda bi: (0,) * len(shape))
    return pl.pallas_call(
        functools.partial(_cmp_kernel, ncp=ncp),
        grid=(b,),
        in_specs=[pl.BlockSpec((None, s, DH), lambda bi, c=c: (bi, 0, c)) for c in range(N_C // DH)] +
                 [full((CMP_LEN, DH)), full((CMP_LEN, DH)),
                  full((CMP_LEN, DH, CMP_HIDDEN)), full((CMP_LEN, DH, CMP_HIDDEN)),
                  full((CMP_HIDDEN, DH)), full((CMP_HIDDEN, DH)),
                  pl.BlockSpec((None, ncp, DH), lambda bi: (bi, 0, 0)),
                  pl.BlockSpec((None, ncp, DH), lambda bi: (bi, 0, 0))],
        out_specs=[pl.BlockSpec((None, NSA_G, ncp, DH), lambda bi: (bi, 0, 0, 0)),
                   pl.BlockSpec((None, NSA_G, ncp, DH), lambda bi: (bi, 0, 0, 0))],
        out_shape=[jax.ShapeDtypeStruct((b, NSA_G, ncp, DH), BF16)] * 2,
        compiler_params=_cparams(("arbitrary",)),
        name="nsa_compress",
    )(x_view, x_view, x_view, x_view, posk, posv, w1k, w1v, w2k, w2v, cosc, sinc)


def _nsa_kernel(q_ref, ks_ref, vs_ref, kw_ref, vw_ref, kc_ref, vc_ref, gate_ref, ovt_ref, et_ref,
                out_ref, qa_ref, ka_ref, va_ref, wa_ref, p_ref, pw_ref, y_ref,
                s0_ref, s1_ref, p0_ref, p1_ref, m_ref, al0_ref, al1_ref, acc_ref,
                *, tq, tk, s_len, ncp, n_s):
    R = NSA_R
    qi = pl.program_id(2)
    t0 = qi * tq

    @pl.when(qi == 0)
    def _():
        ones = jnp.ones((s_len, DH), BF16)
        ka_ref[:, :DH] = ks_ref[...]
        ka_ref[:, DH:] = et_ref[...]
        va_ref[:, :DH] = vs_ref[...]
        va_ref[:, DH:] = ones
        wa_ref[:, :DH] = vw_ref[...]
        wa_ref[:, DH:] = ones

    for r in range(R):
        qa_ref[r * tq:(r + 1) * tq, :DH] = q_ref[:, r * DH:(r + 1) * DH]
    q = qa_ref[:, :DH]
    trow = t0 + lax.broadcasted_iota(jnp.int32, (tq, 1), 0)
    tlane = t0 + lax.broadcasted_iota(jnp.int32, (1, tq), 1)
    head = lambda a, r: a[r * tq:(r + 1) * tq]

    cend = lax.broadcasted_iota(jnp.int32, (1, ncp), 1) * CMP_STRIDE + (CMP_LEN - 1)
    valid_c = cend <= trow
    bias_c = jnp.where(valid_c, 0.0, NEG_INF)
    keep_c = jnp.where(valid_c, 1.0, 0.0)
    s = _dot_nt(q, kc_ref[...])
    psum = jnp.zeros((tq, ncp), F32)
    for r in range(R):
        s_r = head(s, r) + bias_c
        p = jnp.exp2(s_r - jnp.max(s_r, axis=-1, keepdims=True)) * keep_c
        p = p / jnp.maximum(jnp.sum(p, axis=-1, keepdims=True), 1e-30)
        psum = psum + p
        p_ref[r * tq:(r + 1) * tq, :ncp] = p.astype(BF16)
    o_cmp = _dot(p_ref[:, :ncp], vc_ref[...])

    wlen = (-(-(WIN_LEN - 1) // BLOCK)) * BLOCK + tq
    w0 = pl.multiple_of(jnp.maximum(t0 + tq - wlen, 0), BLOCK)
    kpos = w0 + lax.broadcasted_iota(jnp.int32, (1, wlen), 1)
    bias_w = jnp.where((kpos <= trow) & (trow - kpos <= WIN_LEN - 1), 0.0, NEG_INF)
    s = _dot_nt(q, kw_ref[pl.ds(w0, wlen), :])
    for r in range(R):
        s_r = head(s, r) + bias_w
        pw_ref[r * tq:(r + 1) * tq, :] = jnp.exp2(
            s_r - jnp.max(s_r, axis=-1, keepdims=True)).astype(BF16)
    ow = _dot(pw_ref[...], wa_ref[pl.ds(w0, wlen), :])
    o_win = ow[:, :DH] / ow[:, DH:DH + 1]
    gate = gate_ref[...]
    for r in range(R):
        y_ref[r * tq:(r + 1) * tq, :] = (gate[:, r:r + 1] * head(o_cmp, r) +
                                         gate[:, 2 * R + r:2 * R + r + 1] * head(o_win, r))

    hi = psum.astype(BF16)
    rem = psum - hi.astype(F32)
    mid = rem.astype(BF16)
    lo = (rem - mid.astype(F32)).astype(BF16)
    ovt = ovt_ref[...]
    p_slc = _dot_nt(ovt, hi) + _dot_nt(ovt, mid) + _dot_nt(ovt, lo)
    jj = lax.broadcasted_iota(jnp.int32, (n_s, 1), 0)
    blk_t = tlane // SEL_LEN
    forced = (jj == 0) | (jj == blk_t) | (jj == blk_t - 1)
    score = jnp.where(jj <= blk_t, p_slc + jnp.where(forced, FORCE_BONUS, 0.0), -1.0)
    nch = n_s // 8
    chunks = [score[c * 8:(c + 1) * 8] for c in range(nch)]
    cnt = [jnp.zeros((8, tq), F32) for _ in range(nch)]
    sub = lax.broadcasted_iota(jnp.int32, (8, 1), 0)
    for i in range(n_s):
        row_i = score[i:i + 1]
        for c in range(nch):
            if c * 8 > i:
                beats = row_i >= chunks[c]
            elif c * 8 + 7 < i:
                beats = row_i > chunks[c]
            else:
                beats = (row_i > chunks[c]) | ((row_i == chunks[c]) & (sub + c * 8 > i))
            cnt[c] = cnt[c] + jnp.where(beats, 1.0, 0.0)
    k_sel = float(min(SEL_TOPK, n_s))
    bias_t = jnp.concatenate([jnp.where(cc < k_sel, 0.0, NEG_INF) for cc in cnt] +
                             [jnp.zeros((DH - n_s, tq), F32)], axis=0)
    bias_q = bias_t.T.astype(BF16)
    for r in range(R):
        qa_ref[r * tq:(r + 1) * tq, DH:] = bias_q

    m_ref[...] = jnp.full(m_ref.shape, NEG_INF, F32)
    acc_ref[...] = jnp.zeros(acc_ref.shape, F32)

    def scores(kt, s_ref):
        k0 = pl.multiple_of(kt * tk, tk)
        s_ref[...] = _dot_nt(qa_ref[...], ka_ref[pl.ds(k0, tk), :])

    def tile(kt, bufs, causal):
        s_ref, pt_ref, al_ref = bufs
        if causal:
            kpos = kt * tk + lax.broadcasted_iota(jnp.int32, (1, tk), 1)
            bias_d = jnp.where(kpos <= trow, 0.0, NEG_INF)
        for r in range(R):
            s_r = s_ref[r * tq:(r + 1) * tq, :]
            if causal:
                s_r = s_r + bias_d
            m_old = m_ref[r]
            m_new = jnp.maximum(m_old, jnp.max(s_r, axis=-1, keepdims=True))
            pt_ref[r * tq:(r + 1) * tq, :] = jnp.exp2(s_r - m_new).astype(BF16)
            al_ref[r] = jnp.exp2(m_old - m_new)
            m_ref[r] = m_new
        k0 = pl.multiple_of(kt * tk, tk)
        pv = _dot(pt_ref[...], va_ref[pl.ds(k0, tk), :])
        acc_ref[...] = al_ref[...] * acc_ref[...] + pv.reshape(R, tq, 2 * DH)

    even = (s0_ref, p0_ref, al0_ref)
    odd = (s1_ref, p1_ref, al1_ref)
    kd = t0 // tk
    scores(0, s0_ref)

    def pair(i, carry):
        scores(2 * i + 1, s1_ref)
        tile(2 * i, even, False)
        scores(2 * i + 2, s0_ref)
        tile(2 * i + 1, odd, False)
        return carry

    lax.fori_loop(0, kd // 2, pair, 0)

    @pl.when(kd % 2 == 0)
    def _():
        tile(kd, even, True)

    @pl.when(kd % 2 == 1)
    def _():
        scores(kd, s1_ref)
        tile(kd - 1, even, False)
        tile(kd, odd, True)

    acc = acc_ref[...]
    o_slc = (acc[:, :, :DH] / acc[:, :, DH:DH + 1]).reshape(R * tq, DH)

    gate = gate_ref[...]
    for r in range(R):
        y = y_ref[r * tq:(r + 1) * tq, :] + gate[:, R + r:R + r + 1] * head(o_slc, r)
        out_ref[:, r * DH:(r + 1) * DH] = y.astype(BF16)


V_ROWS = DH + 16


def _nsa_t_kernel(q_ref, ks_ref, vs_ref, kw_ref, vw_ref, kc_ref, vc_ref, gate_ref, ovt_ref, et_ref,
                  out_ref, qt_ref, ka_ref, vt_ref, wt_ref, vct_ref, pc_ref, sw_ref, pw_ref, y_ref,
                  s0_ref, s1_ref, p0_ref, p1_ref, m_ref, al0_ref, al1_ref, acc_ref,
                  *, tq, tk, s_len, ncp, n_s):
    R = NSA_R
    qi = pl.program_id(2)
    t0 = qi * tq
    lanes = lambda r: slice(r * tq, (r + 1) * tq)

    @pl.when(qi == 0)
    def _():
        ka_ref[:, :DH] = ks_ref[...]
        ka_ref[:, DH:] = et_ref[...]
        for c in range(s_len // tk):
            vt_ref[c, :DH, :] = vs_ref[c * tk:(c + 1) * tk, :].astype(F32).T.astype(BF16)
            vt_ref[c, DH:, :] = jnp.ones((V_ROWS - DH, tk), BF16)
        for c in range(s_len // BLOCK):
            wt_ref[c, :DH, :] = vw_ref[c * BLOCK:(c + 1) * BLOCK, :].astype(F32).T.astype(BF16)
            wt_ref[c, DH:, :] = jnp.ones((V_ROWS - DH, BLOCK), BF16)
        vct_ref[...] = vc_ref[...].astype(F32).T.astype(BF16)

    for r in range(R):
        qt_ref[:DH, lanes(r)] = q_ref[:, r * DH:(r + 1) * DH].astype(F32).T.astype(BF16)
    qt = qt_ref[:DH, :]
    tlane = t0 + lax.broadcasted_iota(jnp.int32, (1, tq), 1)
    gate_t = gate_ref[...].T

    def softmax_cols(s_ref, p_ref, n, r, bias=None, m_old=None):
        def load(lo, hi):
            s = s_ref[lo:hi, lanes(r)]
            return s if bias is None else s + bias[lo:hi]
        h, q = n // 2, n // 4
        m = jnp.maximum(jnp.max(load(0, h), axis=0, keepdims=True),
                        jnp.max(load(h, n), axis=0, keepdims=True))
        if m_old is not None:
            m = jnp.maximum(m, m_old)
        for c in range(4):
            p_ref[c * q:(c + 1) * q, lanes(r)] = jnp.exp2(load(c * q, (c + 1) * q) - m).astype(BF16)
        return m

    cend = lax.broadcasted_iota(jnp.int32, (ncp, 1), 0) * CMP_STRIDE + (CMP_LEN - 1)
    valid_c = cend <= tlane
    bias_c = jnp.where(valid_c, 0.0, NEG_INF)
    keep_c = jnp.where(valid_c, 1.0, 0.0)
    s1_ref[:ncp, :] = _dot(kc_ref[...], qt)
    psum = jnp.zeros((ncp, tq), F32)
    for r in range(R):
        s_r = s1_ref[:ncp, lanes(r)] + bias_c
        p = jnp.exp2(s_r - jnp.max(s_r, axis=0, keepdims=True)) * keep_c
        p = p / jnp.maximum(jnp.sum(p, axis=0, keepdims=True), 1e-30)
        psum = psum + p
        pc_ref[:, lanes(r)] = p.astype(BF16)
    o_cmp = _dot(vct_ref[...], pc_ref[...])

    wlen = (-(-(WIN_LEN - 1) // BLOCK)) * BLOCK + tq
    w0 = pl.multiple_of(jnp.maximum(t0 + tq - wlen, 0), BLOCK)
    kpos = w0 + lax.broadcasted_iota(jnp.int32, (wlen, 1), 0)
    bias_w = jnp.where((kpos <= tlane) & (tlane - kpos <= WIN_LEN - 1), 0.0, NEG_INF)
    sw_ref[...] = _dot(kw_ref[pl.ds(w0, wlen), :], qt)
    for r in range(R):
        softmax_cols(sw_ref, pw_ref, wlen, r, bias=bias_w)
    wb = w0 // BLOCK
    ow = _dot(wt_ref[wb], pw_ref[:BLOCK, :])
    for jb in range(1, wlen // BLOCK):
        ow = ow + _dot(wt_ref[wb + jb], pw_ref[jb * BLOCK:(jb + 1) * BLOCK, :])
    o_win = ow[:DH] / ow[DH:DH + 1]
    for r in range(R):
        y_ref[:, lanes(r)] = (gate_t[r:r + 1] * o_cmp[:, lanes(r)] +
                              gate_t[2 * R + r:2 * R + r + 1] * o_win[:, lanes(r)])

    hi = psum.astype(BF16)
    rem = psum - hi.astype(F32)
    mid = rem.astype(BF16)
    lo = (rem - mid.astype(F32)).astype(BF16)
    ovt = ovt_ref[...]
    p_slc = _dot(ovt, hi) + _dot(ovt, mid) + _dot(ovt, lo)
    jj = lax.broadcasted_iota(jnp.int32, (n_s, 1), 0)
    blk_t = tlane // SEL_LEN
    forced = (jj == 0) | (jj == blk_t) | (jj == blk_t - 1)
    score = jnp.where(jj <= blk_t, p_slc + jnp.where(forced, FORCE_BONUS, 0.0), -1.0)
    nch = n_s // 8
    chunks = [score[c * 8:(c + 1) * 8] for c in range(nch)]
    cnt = [jnp.zeros((8, tq), F32) for _ in range(nch)]
    sub = lax.broadcasted_iota(jnp.int32, (8, 1), 0)
    for i in range(n_s):
        row_i = score[i:i + 1]
        for c in range(nch):
            if c * 8 > i:
                beats = row_i >= chunks[c]
            elif c * 8 + 7 < i:
                beats = row_i > chunks[c]
            else:
                beats = (row_i > chunks[c]) | ((row_i == chunks[c]) & (sub + c * 8 > i))
            cnt[c] = cnt[c] + jnp.where(beats, 1.0, 0.0)
    k_sel = float(min(SEL_TOPK, n_s))
    bias_t = jnp.concatenate([jnp.where(cc < k_sel, 0.0, NEG_INF) for cc in cnt] +
                             [jnp.zeros((DH - n_s, tq), F32)], axis=0).astype(BF16)
    for r in range(R):
        qt_ref[DH:, lanes(r)] = bias_t

    m_ref[...] = jnp.full(m_ref.shape, NEG_INF, F32)
    acc_ref[...] = jnp.zeros(acc_ref.shape, F32)

    def scores(kt, s_ref):
        k0 = pl.multiple_of(kt * tk, tk)
        s_ref[...] = _dot(ka_ref[pl.ds(k0, tk), :], qt_ref[...])

    def tile(kt, bufs, causal):
        s_ref, pt_ref, al_ref = bufs
        if causal:
            kpos = kt * tk + lax.broadcasted_iota(jnp.int32, (tk, 1), 0)
            bias_d = jnp.where(kpos <= tlane, 0.0, NEG_INF)
        for r in range(R):
            m_old = m_ref[:, lanes(r)]
            m_new = softmax_cols(s_ref, pt_ref, tk, r, bias=bias_d if causal else None, m_old=m_old)
            al_ref[:, lanes(r)] = jnp.exp2(m_old - m_new)
            m_ref[:, lanes(r)] = m_new
        pv = _dot(vt_ref[kt], pt_ref[...])
        acc_ref[...] = al_ref[...] * acc_ref[...] + pv

    even = (s0_ref, p0_ref, al0_ref)
    odd = (s1_ref, p1_ref, al1_ref)
    kd = t0 // tk
    scores(0, s0_ref)

    def pair(i, carry):
        scores(2 * i + 1, s1_ref)
        tile(2 * i, even, False)
        scores(2 * i + 2, s0_ref)
        tile(2 * i + 1, odd, False)
        return carry

    lax.fori_loop(0, kd // 2, pair, 0)

    @pl.when(kd % 2 == 0)
    def _():
        tile(kd, even, True)

    @pl.when(kd % 2 == 1)
    def _():
        scores(kd, s1_ref)
        tile(kd - 1, even, False)
        tile(kd, odd, True)

    acc = acc_ref[...]
    o_slc = acc[:DH] / acc[DH:DH + 1]
    for r in range(R):
        y = y_ref[:, lanes(r)] + gate_t[R + r:R + r + 1] * o_slc[:, lanes(r)]
        out_ref[:, r * DH:(r + 1) * DH] = y.T.astype(BF16)


def _nsa(o_r, o_p, o_g, kcmp, vcmp, b, s, tq, tk):
    ncp = s // CMP_STRIDE
    n_s = s // SEL_LEN
    assert n_s % 8 == 0 and n_s <= DH
    r_view = o_r.reshape(b, s, N_R)
    p_view = o_p.reshape(b, s, N_P)
    g_view = o_g.reshape(b, s, N_G)
    c = np.arange(ncp)[None, :]
    j = np.arange(n_s)[:, None]
    n_c = (s - CMP_LEN) // CMP_STRIDE + 1
    ovt = ((c * CMP_STRIDE <= j * SEL_LEN + SEL_LEN - 1) &
           (c * CMP_STRIDE + CMP_LEN - 1 >= j * SEL_LEN) & (c < n_c))
    ovt = jnp.asarray(ovt.astype(np.float32), BF16)
    et = jnp.asarray((np.arange(s)[:, None] // SEL_LEN == np.arange(DH)[None, :])
                     .astype(np.float32), BF16)
    wq = NSA_R * DH
    ks_col = W_QB // DH
    kw_col = ks_col + NSA_G
    gate_col = (2 * D_MODEL) // DH
    wlen = (-(-(WIN_LEN - 1) // BLOCK)) * BLOCK + tq
    kv = (None, s, DH)
    out = pl.pallas_call(
        functools.partial(_nsa_kernel, tq=tq, tk=tk, s_len=s, ncp=ncp, n_s=n_s),
        grid=(b, NSA_G, s // tq),
        in_specs=[pl.BlockSpec((None, tq, wq), lambda bi, g, qi: (bi, qi, g)),
                  pl.BlockSpec(kv, lambda bi, g, qi: (bi, 0, ks_col + g)),
                  pl.BlockSpec(kv, lambda bi, g, qi: (bi, 0, g)),
                  pl.BlockSpec(kv, lambda bi, g, qi: (bi, 0, kw_col + g)),
                  pl.BlockSpec(kv, lambda bi, g, qi: (bi, 0, NSA_G + g)),
                  pl.BlockSpec((None, None, ncp, DH), lambda bi, g, qi: (bi, g, 0, 0)),
                  pl.BlockSpec((None, None, ncp, DH), lambda bi, g, qi: (bi, g, 0, 0)),
                  pl.BlockSpec((None, tq, DH), lambda bi, g, qi: (bi, qi, gate_col + g)),
                  pl.BlockSpec((n_s, ncp), lambda bi, g, qi: (0, 0)),
                  pl.BlockSpec((s, DH), lambda bi, g, qi: (0, 0))],
        out_specs=pl.BlockSpec((None, tq, wq), lambda bi, g, qi: (bi, qi, g)),
        out_shape=jax.ShapeDtypeStruct((b, s, W_QB), BF16),
        scratch_shapes=[pltpu.VMEM((NSA_R * tq, 2 * DH), BF16),
                        pltpu.VMEM((s, 2 * DH), BF16),
                        pltpu.VMEM((s, 2 * DH), BF16),
                        pltpu.VMEM((s, 2 * DH), BF16),
                        pltpu.VMEM((NSA_R * tq, ncp), BF16),
                        pltpu.VMEM((NSA_R * tq, wlen), BF16),
                        pltpu.VMEM((NSA_R * tq, DH), F32),
                        pltpu.VMEM((NSA_R * tq, tk), F32),
                        pltpu.VMEM((NSA_R * tq, tk), F32),
                        pltpu.VMEM((NSA_R * tq, tk), BF16),
                        pltpu.VMEM((NSA_R * tq, tk), BF16),
                        pltpu.VMEM((NSA_R, tq, 1), F32),
                        pltpu.VMEM((NSA_R, tq, 1), F32),
                        pltpu.VMEM((NSA_R, tq, 1), F32),
                        pltpu.VMEM((NSA_R, tq, 2 * DH), F32)],
        compiler_params=_cparams(("arbitrary", "arbitrary", "arbitrary")),
        name="nsa_attention",
    )(r_view, r_view, p_view, r_view, p_view, kcmp, vcmp, g_view, ovt, et)
    return out.reshape(b * s, W_QB)


def _out_kernel(ya_ref, yb_ref, ga_ref, gb_ref, x_ref, woa_ref, wob_ref, wo_ref, out_ref):
    pa = _dot(ya_ref[...], woa_ref[...])
    pb = _dot(yb_ref[...], wob_ref[...])
    y = (ga_ref[...] * pa + gb_ref[...] * pb).astype(BF16)
    out_ref[...] = x_ref[...] + _dot(y, wo_ref[...])


def _merge_out(ya, yb, o_g, x2, woa, wob, wo, layer, tm):
    m = x2.shape[0]
    row = lambda w: pl.BlockSpec((tm, w), lambda i: (i, 0))
    const = lambda shape: pl.BlockSpec((None,) + shape, lambda i: (layer, 0, 0),
                                       pipeline_mode=pl.Buffered(1))
    return pl.pallas_call(
        _out_kernel,
        grid=(m // tm,),
        in_specs=[row(DIL_OUT), row(W_QB),
                  pl.BlockSpec((tm, D_MODEL), lambda i: (i, 0)),
                  pl.BlockSpec((tm, D_MODEL), lambda i: (i, 1)),
                  row(D_MODEL),
                  const((DIL_OUT, D_MODEL)), const((W_QB, D_MODEL)), const((D_MODEL, D_MODEL))],
        out_specs=row(D_MODEL),
        out_shape=jax.ShapeDtypeStruct((m, D_MODEL), F32),
        compiler_params=_cparams(("arbitrary",)),
        name="merge_out",
    )(ya, yb, o_g, o_g, x2, woa, wob, wo)


def _ffn_kernel(x_ref, g_ref, wg_ref, wu_ref, wd_ref, gf_ref, out_ref, h_ref, *, nf, final):
    f = pl.program_id(1)

    @pl.when(f == 0)
    def _():
        x = x_ref[...]
        h_ref[...] = _rms(x, g_ref[...]).astype(BF16)
        out_ref[...] = x

    h = h_ref[...]
    a = _dot(h, wg_ref[...])
    u = _dot(h, wu_ref[...])
    act = (a * jax.nn.sigmoid(a) * u).astype(BF16)
    out_ref[...] += _dot(act, wd_ref[...])

    if final:
        @pl.when(f == nf - 1)
        def _():
            out_ref[...] = _rms(out_ref[...], gf_ref[...])


def _ffn(x2, g, wg, wu, wd, layer, gf, tm, tf, final):
    m = x2.shape[0]
    nf = D_FF // tf
    return pl.pallas_call(
        functools.partial(_ffn_kernel, nf=nf, final=final),
        grid=(m // tm, nf),
        in_specs=[pl.BlockSpec((tm, D_MODEL), lambda i, f: (i, 0)),
                  pl.BlockSpec((1, D_MODEL), lambda i, f: (0, 0)),
                  pl.BlockSpec((None, D_MODEL, tf), lambda i, f: (layer, 0, f)),
                  pl.BlockSpec((None, D_MODEL, tf), lambda i, f: (layer, 0, f)),
                  pl.BlockSpec((None, tf, D_MODEL), lambda i, f: (layer, f, 0)),
                  pl.BlockSpec((1, D_MODEL), lambda i, f: (0, 0))],
        out_specs=pl.BlockSpec((tm, D_MODEL), lambda i, f: (i, 0)),
        out_shape=jax.ShapeDtypeStruct((m, D_MODEL), F32),
        scratch_shapes=[pltpu.VMEM((tm, D_MODEL), BF16)],
        compiler_params=_cparams(("arbitrary", "arbitrary")),
        name="ffn",
    )(x2, g, wg, wu, wd, gf)


def _prep_w_in(w):
    depth = w.shape[0]
    cols = (W_QA, W_QA, W_QA, W_QB, W_KV, W_KV, W_KV, W_KV, W_KV, W_KV,
            NSA_Q_HEADS * 3, D_MODEL, D_MODEL)
    offs = np.concatenate([[0], np.cumsum(cols)])
    seg = [w[:, :, int(offs[i]):int(offs[i + 1])] for i in range(len(cols))]
    qa, ka, va, qb, kc, vc, ks, vs, kw, vw, gate_b, gam, gbm = seg
    gate_b = gate_b.reshape(depth, D_MODEL, NSA_G, NSA_R, 3).transpose(0, 1, 2, 4, 3)
    gate_b = gate_b.reshape(depth, D_MODEL, NSA_G, 3 * NSA_R)
    gate_b = jnp.pad(gate_b, ((0, 0), (0, 0), (0, 0), (0, DH - 3 * NSA_R)))
    gate_b = gate_b.reshape(depth, D_MODEL, NSA_G * DH)
    gate_b = jnp.pad(gate_b, ((0, 0), (0, 0), (0, N_G - 2 * D_MODEL - NSA_G * DH)))
    return jnp.concatenate([qa, ka, va, qb, ks, kw, vs, vw, kc, vc, gam, gbm, gate_b],
                           axis=2).astype(BF16)


def kernel(x, positions, ln_mix, w_in, cmp_pos_k, cmp_pos_v, cmp_w1_k, cmp_w2_k, cmp_w1_v, cmp_w2_v,
           w_out_a, w_out_b, w_out, ln_ffn, w_ffn_gate, w_ffn_up, w_ffn_down, ln_final):
    b, s, d = x.shape
    depth = w_in.shape[0]
    assert d == D_MODEL and s % DIL_UNIT == 0
    m = b * s
    tm_in = min(1024, m)
    tm_out = min(256, m)
    tm_ffn = min(512, m)
    tq, tk = 128, min(512, s)
    ncp = s // CMP_STRIDE

    pos_f = positions.astype(F32)
    cos, sin = _rope_tables(pos_f.reshape(m, 1), tm_in)
    blk_end = np.minimum(np.arange(ncp) * CMP_STRIDE + CMP_LEN - 1, s - 1)
    cosc, sinc = _rope_tables(pos_f[:, blk_end].reshape(b * ncp, 1), ncp)
    cosc = cosc.reshape(b, ncp, DH)
    sinc = sinc.reshape(b, ncp, DH)

    cs = np.ones((1, N_ALL), np.float32)
    cs[:, :W_QA] = Q_SCALE
    cs[:, N_A:N_A + W_QB] = Q_SCALE
    cs = jnp.asarray(cs)

    w_in_b = _prep_w_in(w_in)
    woa_b, wob_b, wo_b = w_out_a.astype(BF16), w_out_b.astype(BF16), w_out.astype(BF16)
    wg_b, wu_b, wd_b = w_ffn_gate.astype(BF16), w_ffn_up.astype(BF16), w_ffn_down.astype(BF16)

    x2 = x.reshape(m, d)
    for l in range(depth):
        o_a, o_r, o_p, o_c, o_g = _inproj(x2, ln_mix[l][None, :], w_in_b, l, cs, cos, sin, tm_in)
        ya = _dilated(o_a, b, s)
        kcmp, vcmp = _compress(
            o_c, b, s, cmp_pos_k[l], cmp_pos_v[l],
            cmp_w1_k[l].reshape(CMP_LEN, DH, CMP_HIDDEN).astype(BF16),
            cmp_w1_v[l].reshape(CMP_LEN, DH, CMP_HIDDEN).astype(BF16),
            cmp_w2_k[l].astype(BF16), cmp_w2_v[l].astype(BF16), cosc, sinc)
        yb = _nsa(o_r, o_p, o_g, kcmp, vcmp, b, s, tq, tk)
        x2 = _merge_out(ya, yb, o_g, x2, woa_b, wob_b, wo_b, l, tm_out)
        x2 = _ffn(x2, ln_ffn[l][None, :], wg_b, wu_b, wd_b, l, ln_final[None, :], tm_ffn, 512,
                  l == depth - 1)
    return x2.reshape(b, s, d)
```

```python
import functools
import math

import numpy as np
import jax
import jax.numpy as jnp
from jax import lax
from jax.experimental import pallas as pl
from jax.experimental.pallas import tpu as pltpu

F32 = jnp.float32
BF16 = jnp.bfloat16

D_MODEL = 2048
DH = 128
HALF = DH // 2
ROPE_THETA = 10000.0
NORM_EPS = 1e-6
NEG_INF = -1e30
BLOCK = 128

DIL_GROUPS = ((128, 1), (512, 4), (2048, 16))
DIL_HPG = 4
DIL_HEADS = DIL_HPG * len(DIL_GROUPS)
DIL_OUT = DIL_HPG * DH
DIL_UNIT = DIL_GROUPS[-1][1] * BLOCK

NSA_Q_HEADS = 16
NSA_G = 2
NSA_R = NSA_Q_HEADS // NSA_G
CMP_LEN = 32
CMP_STRIDE = 16
CMP_HIDDEN = 256
SEL_LEN = 64
SEL_TOPK = 16
WIN_LEN = 512
FORCE_BONUS = 1e4
D_FF = 5632

W_QA = DIL_HEADS * DH
W_QB = NSA_Q_HEADS * DH
W_KV = NSA_G * DH
N_A = 3 * W_QA
N_R = W_QB + 2 * W_KV
N_P = 2 * W_KV
N_C = 2 * W_KV
TN_IN = 512
N_G = 2 * D_MODEL + TN_IN
N_ALL = N_A + N_R + N_P + N_C + N_G
VMEM_LIMIT = 56 * 1024 * 1024
Q_SCALE = DH ** -0.5 * math.log2(math.e)


def _cparams(sem):
    return pltpu.CompilerParams(dimension_semantics=sem, vmem_limit_bytes=VMEM_LIMIT)


def _dot_nt(a, b):
    return lax.dot_general(a, b, (((1,), (1,)), ((), ())), preferred_element_type=F32)


def _dot(a, b):
    return jnp.dot(a, b, preferred_element_type=F32)


def _rope_tab_kernel(pos_ref, inv_ref, sgn_ref, cos_ref, sin_ref):
    ang = pos_ref[...] * inv_ref[...]
    cos_ref[...] = jnp.cos(ang)
    sin_ref[...] = jnp.sin(ang) * sgn_ref[...]


def _rope_tables(pos_f, tm):
    m = pos_f.shape[0]
    inv = ROPE_THETA ** (-2.0 * jnp.arange(HALF, dtype=F32) / DH)
    inv = jnp.concatenate([inv, inv])[None, :]
    sgn = jnp.concatenate([-jnp.ones((HALF,), F32), jnp.ones((HALF,), F32)])[None, :]
    return pl.pallas_call(
        _rope_tab_kernel,
        grid=(m // tm,),
        in_specs=[pl.BlockSpec((tm, 1), lambda i: (i, 0)),
                  pl.BlockSpec((1, DH), lambda i: (0, 0)),
                  pl.BlockSpec((1, DH), lambda i: (0, 0))],
        out_specs=[pl.BlockSpec((tm, DH), lambda i: (i, 0)),
                   pl.BlockSpec((tm, DH), lambda i: (i, 0))],
        out_shape=[jax.ShapeDtypeStruct((m, DH), F32)] * 2,
        compiler_params=_cparams(("arbitrary",)),
        name="rope_tables",
    )(pos_f, inv, sgn)


def _rope(a, cos, sin_signed):
    return a * cos + pltpu.roll(a, HALF, 1) * sin_signed


def _rms(x, g):
    ms = jnp.mean(x * x, axis=-1, keepdims=True)
    return x * lax.rsqrt(ms + NORM_EPS) * g


T_AROPE = 2 * W_QA // TN_IN
T_A = N_A // TN_IN
T_R = T_A + N_R // TN_IN
T_P = T_R + N_P // TN_IN
T_C = T_P + N_C // TN_IN
T_ALL = N_ALL // TN_IN


def _inproj_kernel(x_ref, g_ref, w_ref, cs_ref, cos_ref, sin_ref,
                   oa_ref, or_ref, op_ref, oc_ref, og_ref, h_ref):
    j = pl.program_id(1)

    @pl.when(j == 0)
    def _():
        h_ref[...] = _rms(x_ref[...], g_ref[...]).astype(BF16)

    tm = h_ref.shape[0]
    halves = [slice(k * (tm // 2), (k + 1) * (tm // 2)) for k in range(2)]

    def region(out_ref, epilogue):
        accs = [_dot(h_ref[rows, :], w_ref[...]) for rows in halves]
        for rows, acc in zip(halves, accs):
            epilogue(out_ref, rows, acc)

    def roped(out_ref, rows, acc):
        cos = cos_ref[rows, :]
        sin = sin_ref[rows, :]
        for hh in range(TN_IN // DH):
            sl = slice(hh * DH, (hh + 1) * DH)
            r = _rope(acc[:, sl], cos, sin) * cs_ref[:, sl]
            out_ref[rows, sl] = r.astype(out_ref.dtype)

    def plain(out_ref, rows, acc):
        out_ref[rows, :] = acc.astype(out_ref.dtype)

    def sigmoid(out_ref, rows, acc):
        out_ref[rows, :] = jax.nn.sigmoid(acc)

    pl.when(j < T_AROPE)(lambda: region(oa_ref, roped))
    pl.when((j >= T_AROPE) & (j < T_A))(lambda: region(oa_ref, plain))
    pl.when((j >= T_A) & (j < T_R))(lambda: region(or_ref, roped))
    pl.when((j >= T_R) & (j < T_P))(lambda: region(op_ref, plain))
    pl.when((j >= T_P) & (j < T_C))(lambda: region(oc_ref, plain))
    pl.when(j >= T_C)(lambda: region(og_ref, sigmoid))


def _inproj(x2, g, w, layer, cs, cos, sin, tm):
    m = x2.shape[0]

    def region(lo, hi):
        return pl.BlockSpec((tm, TN_IN), lambda i, j: (i, jnp.clip(j - lo, 0, hi - lo - 1)))

    return pl.pallas_call(
        _inproj_kernel,
        grid=(m // tm, T_ALL),
        in_specs=[pl.BlockSpec((tm, D_MODEL), lambda i, j: (i, 0)),
                  pl.BlockSpec((1, D_MODEL), lambda i, j: (0, 0)),
                  pl.BlockSpec((None, D_MODEL, TN_IN), lambda i, j: (layer, 0, j)),
                  pl.BlockSpec((1, TN_IN), lambda i, j: (0, j)),
                  pl.BlockSpec((tm, DH), lambda i, j: (i, 0)),
                  pl.BlockSpec((tm, DH), lambda i, j: (i, 0))],
        out_specs=[region(0, T_A), region(T_A, T_R), region(T_R, T_P), region(T_P, T_C),
                   region(T_C, T_ALL)],
        out_shape=[jax.ShapeDtypeStruct((m, N_A), F32),
                   jax.ShapeDtypeStruct((m, N_R), BF16),
                   jax.ShapeDtypeStruct((m, N_P), BF16),
                   jax.ShapeDtypeStruct((m, N_C), F32),
                   jax.ShapeDtypeStruct((m, N_G), F32)],
        scratch_shapes=[pltpu.VMEM((tm, D_MODEL), BF16)],
        compiler_params=_cparams(("arbitrary", "arbitrary")),
        name="inproj",
    )(x2, g, w, cs, cos, sin)


def _rows(start, dil):
    return pl.ds(start, BLOCK) if dil == 1 else pl.ds(start, BLOCK, stride=dil)


def _dil_kernel(*refs):
    out_ref, acc_scr, l_scr, m_scr = refs[-4:]
    u = pl.program_id(1)
    row = lax.broadcasted_iota(jnp.int32, (BLOCK, 2 * BLOCK), 0)
    col = lax.broadcasted_iota(jnp.int32, (BLOCK, 2 * BLOCK), 1)
    band = ((col < BLOCK) & (col >= row)) | ((col >= BLOCK) & (col - BLOCK <= row))
    bias = jnp.where(band, 0.0, NEG_INF)
    bias_first = jnp.where(band & ((col >= BLOCK) | (u > 0)), 0.0, NEG_INF)
    ones = jnp.ones((2 * BLOCK, DH), BF16)
    for gi, (_, dil) in enumerate(DIL_GROUPS):
        q_ref, kc_ref, kp_ref, vc_ref, vp_ref = refs[5 * gi:5 * gi + 5]
        span = BLOCK * dil
        for rho in range(dil):
            for ub in range(DIL_UNIT // span):
                cur = _rows(ub * span + rho, dil)
                if ub == 0:
                    kp, vp, b_add = kp_ref[_rows(rho, dil), :], vp_ref[_rows(rho, dil), :], bias_first
                else:
                    prv = _rows((ub - 1) * span + rho, dil)
                    kp, vp, b_add = kc_ref[prv, :], vc_ref[prv, :], bias
                q = q_ref[cur, :].astype(BF16)
                k = jnp.concatenate([kp, kc_ref[cur, :]], axis=0).astype(BF16)
                v = jnp.concatenate([vp, vc_ref[cur, :]], axis=0).astype(BF16)
                s = _dot_nt(q, k) + b_add
                m = jnp.broadcast_to(jnp.max(s, axis=-1, keepdims=True), (BLOCK, DH))
                p = jnp.concatenate([jnp.exp2(s[:, :BLOCK] - m), jnp.exp2(s[:, BLOCK:] - m)], axis=1)
                pv = _dot(p.astype(BF16), jnp.concatenate([v, ones], axis=1))
                acc_scr[gi, cur, :] = pv[:, :DH]
                l_scr[gi, cur, :] = pv[:, DH:]
                m_scr[gi, cur, :] = m
    m0, m1, m2 = m_scr[0], m_scr[1], m_scr[2]
    m = jnp.maximum(jnp.maximum(m0, m1), m2)
    e0, e1, e2 = jnp.exp2(m0 - m), jnp.exp2(m1 - m), jnp.exp2(m2 - m)
    num = e0 * acc_scr[0] + e1 * acc_scr[1] + e2 * acc_scr[2]
    den = e0 * l_scr[0] + e1 * l_scr[1] + e2 * l_scr[2]
    out_ref[...] = (num / den).astype(BF16)


def _dilated(o_a, b, s):
    a_view = o_a.reshape(b, s, N_A)
    in_specs, args = [], []
    for gi, (_, dil) in enumerate(DIL_GROUPS):
        span = BLOCK * dil
        per = DIL_UNIT // span

        def cur(colbase, gi=gi):
            return pl.BlockSpec((None, DIL_UNIT, DH),
                                lambda bi, u, j: (bi, u, colbase + gi * DIL_HPG + j))

        def prev(colbase, gi=gi, span=span, per=per):
            return pl.BlockSpec((None, span, DH),
                                lambda bi, u, j: (bi, jnp.maximum(u * per - 1, 0),
                                                  colbase + gi * DIL_HPG + j))

        kcol, vcol = W_QA // DH, 2 * W_QA // DH
        in_specs += [cur(0), cur(kcol), prev(kcol), cur(vcol), prev(vcol)]
        args += [a_view] * 5
    out = pl.pallas_call(
        _dil_kernel,
        grid=(b, s // DIL_UNIT, DIL_HPG),
        in_specs=in_specs,
        out_specs=pl.BlockSpec((None, DIL_UNIT, DH), lambda bi, u, j: (bi, u, j)),
        out_shape=jax.ShapeDtypeStruct((b, s, DIL_OUT), BF16),
        scratch_shapes=[pltpu.VMEM((len(DIL_GROUPS), DIL_UNIT, DH), F32)] * 3,
        compiler_params=_cparams(("arbitrary", "arbitrary", "arbitrary")),
        name="dilated",
    )(*args)
    return out.reshape(b * s, DIL_OUT)


def _cmp_kernel(xk0_ref, xk1_ref, xv0_ref, xv1_ref, posk_ref, posv_ref, w1k_ref, w1v_ref,
                w2k_ref, w2v_ref, cos_ref, sin_ref, kc_ref, vc_ref, *, ncp):
    half = CMP_LEN // 2
    x_refs = ((xk0_ref, xk1_ref), (xv0_ref, xv1_ref))
    for kind in range(2):
        pos_ref, w1_ref, w2_ref, o_ref = ((posk_ref, w1k_ref, w2k_ref, kc_ref) if kind == 0
                                          else (posv_ref, w1v_ref, w2v_ref, vc_ref))
        for g in range(NSA_G):
            x_ref = x_refs[kind][g]
            a = jnp.zeros((ncp, CMP_HIDDEN), F32)
            bm = jnp.zeros((ncp, CMP_HIDDEN), F32)
            for l in range(half):
                x = x_ref[pl.ds(l, ncp, stride=CMP_STRIDE), :]
                a = a + _dot((x + pos_ref[l:l + 1, :]).astype(BF16), w1_ref[l])
                bm = bm + _dot((x + pos_ref[half + l:half + l + 1, :]).astype(BF16),
                               w1_ref[half + l])
            hid = a + pltpu.roll(bm, ncp - 1, 0)
            hid = jax.nn.gelu(hid, approximate=True).astype(BF16)
            out = _dot(hid, w2_ref[...])
            if kind == 0:
                out = _rope(out, cos_ref[...], sin_ref[...])
            o_ref[g] = out.astype(BF16)


def _compress(o_c, b, s, posk, posv, w1k, w1v, w2k, w2v, cosc, sinc):
    ncp = s // CMP_STRIDE
    x_view = o_c.reshape(b, s, N_C)
    full = lambda shape: pl.BlockSpec(shape, lambda bi: (0,) * len(shape))
    return pl.pallas_call(
        functools.partial(_cmp_kernel, ncp=ncp),
        grid=(b,),
        in_specs=[pl.BlockSpec((None, s, DH), lambda bi, c=c: (bi, 0, c)) for c in range(N_C // DH)] +
                 [full((CMP_LEN, DH)), full((CMP_LEN, DH)),
                  full((CMP_LEN, DH, CMP_HIDDEN)), full((CMP_LEN, DH, CMP_HIDDEN)),
                  full((CMP_HIDDEN, DH)), full((CMP_HIDDEN, DH)),
                  pl.BlockSpec((None, ncp, DH), lambda bi: (bi, 0, 0)),
                  pl.BlockSpec((None, ncp, DH), lambda bi: (bi, 0, 0))],
        out_specs=[pl.BlockSpec((None, NSA_G, ncp, DH), lambda bi: (bi, 0, 0, 0)),
                   pl.BlockSpec((None, NSA_G, ncp, DH), lambda bi: (bi, 0, 0, 0))],
        out_shape=[jax.ShapeDtypeStruct((b, NSA_G, ncp, DH), BF16)] * 2,
        compiler_params=_cparams(("arbitrary",)),
        name="nsa_compress",
    )(x_view, x_view, x_view, x_view, posk, posv, w1k, w1v, w2k, w2v, cosc, sinc)


def _nsa_kernel(q_ref, ks_ref, vs_ref, kw_ref, vw_ref, kc_ref, vc_ref, gate_ref, ovt_ref, et_ref,
                out_ref, qa_ref, ka_ref, va_ref, wa_ref, p_ref, pw_ref, y_ref,
                s0_ref, s1_ref, p0_ref, p1_ref, m_ref, al0_ref, al1_ref, acc_ref,
                *, tq, tk, s_len, ncp, n_s):
    R = NSA_R
    qi = pl.program_id(2)
    t0 = qi * tq

    @pl.when(qi == 0)
    def _():
        ones = jnp.ones((s_len, DH), BF16)
        ka_ref[:, :DH] = ks_ref[...]
        ka_ref[:, DH:] = et_ref[...]
        va_ref[:, :DH] = vs_ref[...]
        va_ref[:, DH:] = ones
        wa_ref[:, :DH] = vw_ref[...]
        wa_ref[:, DH:] = ones

    for r in range(R):
        qa_ref[r * tq:(r + 1) * tq, :DH] = q_ref[:, r * DH:(r + 1) * DH]
    q = qa_ref[:, :DH]
    trow = t0 + lax.broadcasted_iota(jnp.int32, (tq, 1), 0)
    tlane = t0 + lax.broadcasted_iota(jnp.int32, (1, tq), 1)
    head = lambda a, r: a[r * tq:(r + 1) * tq]

    cend = lax.broadcasted_iota(jnp.int32, (1, ncp), 1) * CMP_STRIDE + (CMP_LEN - 1)
    valid_c = cend <= trow
    bias_c = jnp.where(valid_c, 0.0, NEG_INF)
    keep_c = jnp.where(valid_c, 1.0, 0.0)
    s = _dot_nt(q, kc_ref[...])
    chunks_c = [slice(c * DH, (c + 1) * DH) for c in range(ncp // DH)]
    psum = [jnp.zeros((tq, DH), F32) for _ in chunks_c]
    for r in range(R):
        s_r = head(s, r) + bias_c
        m_c = jnp.broadcast_to(jnp.max(s_r, axis=-1, keepdims=True), (tq, DH))
        e = [jnp.exp2(s_r[:, cols] - m_c) * keep_c[:, cols] for cols in chunks_c]
        l_c = jnp.maximum(jnp.sum(sum(e), axis=-1, keepdims=True), 1e-30)
        l_c = jnp.broadcast_to(l_c, (tq, DH))
        for ci, cols in enumerate(chunks_c):
            p = e[ci] / l_c
            psum[ci] = psum[ci] + p
            p_ref[r * tq:(r + 1) * tq, cols] = p.astype(BF16)
    psum = jnp.concatenate(psum, axis=1)
    o_cmp = _dot(p_ref[:, :ncp], vc_ref[...])

    wlen = (-(-(WIN_LEN - 1) // BLOCK)) * BLOCK + tq
    w0 = pl.multiple_of(jnp.maximum(t0 + tq - wlen, 0), BLOCK)
    kpos = w0 + lax.broadcasted_iota(jnp.int32, (1, wlen), 1)
    bias_w = jnp.where((kpos <= trow) & (trow - kpos <= WIN_LEN - 1), 0.0, NEG_INF)
    s = _dot_nt(q, kw_ref[pl.ds(w0, wlen), :])
    for r in range(R):
        s_r = head(s, r) + bias_w
        m_w = jnp.broadcast_to(jnp.max(s_r, axis=-1, keepdims=True), (tq, DH))
        for c in range(wlen // DH):
            cols = slice(c * DH, (c + 1) * DH)
            pw_ref[r * tq:(r + 1) * tq, cols] = jnp.exp2(s_r[:, cols] - m_w).astype(BF16)
    ow = _dot(pw_ref[...], wa_ref[pl.ds(w0, wlen), :])
    o_win = ow[:, :DH] / ow[:, DH:]
    gate = gate_ref[...]
    for r in range(R):
        y_ref[r * tq:(r + 1) * tq, :] = (gate[:, r:r + 1] * head(o_cmp, r) +
                                         gate[:, 2 * R + r:2 * R + r + 1] * head(o_win, r))

    hi = psum.astype(BF16)
    rem = psum - hi.astype(F32)
    mid = rem.astype(BF16)
    lo = (rem - mid.astype(F32)).astype(BF16)
    ovt = ovt_ref[...]
    p_slc = _dot_nt(ovt, hi) + _dot_nt(ovt, mid) + _dot_nt(ovt, lo)
    jj = lax.broadcasted_iota(jnp.int32, (n_s, 1), 0)
    blk_t = tlane // SEL_LEN
    forced = (jj == 0) | (jj == blk_t) | (jj == blk_t - 1)
    score = jnp.where(jj <= blk_t, p_slc + jnp.where(forced, FORCE_BONUS, 0.0), -1.0)
    nch = n_s // 8
    chunks = [score[c * 8:(c + 1) * 8] for c in range(nch)]
    cnt = [jnp.zeros((8, tq), F32) for _ in range(nch)]
    sub = lax.broadcasted_iota(jnp.int32, (8, 1), 0)
    for i in range(n_s):
        row_i = score[i:i + 1]
        for c in range(nch):
            if c * 8 > i:
                beats = row_i >= chunks[c]
            elif c * 8 + 7 < i:
                beats = row_i > chunks[c]
            else:
                beats = (row_i > chunks[c]) | ((row_i == chunks[c]) & (sub + c * 8 > i))
            cnt[c] = cnt[c] + jnp.where(beats, 1.0, 0.0)
    k_sel = float(min(SEL_TOPK, n_s))
    bias_t = jnp.concatenate([jnp.where(cc < k_sel, 0.0, NEG_INF) for cc in cnt] +
                             [jnp.zeros((DH - n_s, tq), F32)], axis=0)
    bias_q = bias_t.T.astype(BF16)
    for r in range(R):
        qa_ref[r * tq:(r + 1) * tq, DH:] = bias_q

    m_ref[...] = jnp.full(m_ref.shape, NEG_INF, F32)
    acc_ref[...] = jnp.zeros(acc_ref.shape, F32)

    def scores(kt, s_ref):
        k0 = pl.multiple_of(kt * tk, tk)
        s_ref[...] = _dot_nt(qa_ref[...], ka_ref[pl.ds(k0, tk), :])

    def tile(kt, bufs, causal):
        s_ref, pt_ref, al_ref = bufs
        if causal:
            kpos = kt * tk + lax.broadcasted_iota(jnp.int32, (1, tk), 1)
            bias_d = jnp.where(kpos <= trow, 0.0, NEG_INF)
        for r in range(R):
            rows = slice(r * tq, (r + 1) * tq)
            s_r = s_ref[rows, :]
            if causal:
                s_r = s_r + bias_d
            m_old = m_ref[r]
            m_new = jnp.maximum(m_old, jnp.max(s_r, axis=-1, keepdims=True))
            for c in range(tk // DH):
                cols = slice(c * DH, (c + 1) * DH)
                pt_ref[rows, cols] = jnp.exp2(s_r[:, cols] - m_new).astype(BF16)
            al_ref[r] = jnp.exp2(m_old - m_new)
            m_ref[r] = m_new
        k0 = pl.multiple_of(kt * tk, tk)
        pv = _dot(pt_ref[...], va_ref[pl.ds(k0, tk), :]).reshape(R, tq, 2 * DH)
        al = al_ref[...]
        acc_ref[:, :, :DH] = al * acc_ref[:, :, :DH] + pv[:, :, :DH]
        acc_ref[:, :, DH:] = al * acc_ref[:, :, DH:] + pv[:, :, DH:]

    even = (s0_ref, p0_ref, al0_ref)
    odd = (s1_ref, p1_ref, al1_ref)
    kd = t0 // tk
    scores(0, s0_ref)

    def pair(i, carry):
        scores(2 * i + 1, s1_ref)
        tile(2 * i, even, False)
        scores(2 * i + 2, s0_ref)
        tile(2 * i + 1, odd, False)
        return carry

    lax.fori_loop(0, kd // 2, pair, 0)

    @pl.when(kd % 2 == 0)
    def _():
        tile(kd, even, True)

    @pl.when(kd % 2 == 1)
    def _():
        scores(kd, s1_ref)
        tile(kd - 1, even, False)
        tile(kd, odd, True)

    acc = acc_ref[...]
    o_slc = (acc[:, :, :DH] / acc[:, :, DH:]).reshape(R * tq, DH)

    gate = gate_ref[...]
    for r in range(R):
        y = y_ref[r * tq:(r + 1) * tq, :] + gate[:, R + r:R + r + 1] * head(o_slc, r)
        out_ref[:, r * DH:(r + 1) * DH] = y.astype(BF16)


V_ROWS = DH + 16


def _nsa_t_kernel(q_ref, ks_ref, vs_ref, kw_ref, vw_ref, kc_ref, vc_ref, gate_ref, ovt_ref, et_ref,
                  out_ref, qt_ref, ka_ref, vt_ref, wt_ref, vct_ref, pc_ref, sw_ref, pw_ref, y_ref,
                  s0_ref, s1_ref, p0_ref, p1_ref, m_ref, al0_ref, al1_ref, acc_ref,
                  *, tq, tk, s_len, ncp, n_s):
    R = NSA_R
    qi = pl.program_id(2)
    t0 = qi * tq
    lanes = lambda r: slice(r * tq, (r + 1) * tq)

    @pl.when(qi == 0)
    def _():
        ka_ref[:, :DH] = ks_ref[...]
        ka_ref[:, DH:] = et_ref[...]
        for c in range(s_len // tk):
            vt_ref[c, :DH, :] = vs_ref[c * tk:(c + 1) * tk, :].astype(F32).T.astype(BF16)
            vt_ref[c, DH:, :] = jnp.ones((V_ROWS - DH, tk), BF16)
        for c in range(s_len // BLOCK):
            wt_ref[c, :DH, :] = vw_ref[c * BLOCK:(c + 1) * BLOCK, :].astype(F32).T.astype(BF16)
            wt_ref[c, DH:, :] = jnp.ones((V_ROWS - DH, BLOCK), BF16)
        vct_ref[...] = vc_ref[...].astype(F32).T.astype(BF16)

    for r in range(R):
        qt_ref[:DH, lanes(r)] = q_ref[:, r * DH:(r + 1) * DH].astype(F32).T.astype(BF16)
    qt = qt_ref[:DH, :]
    tlane = t0 + lax.broadcasted_iota(jnp.int32, (1, tq), 1)
    gate_t = gate_ref[...].T

    def softmax_cols(s_ref, p_ref, n, r, bias=None, m_old=None):
        def load(lo, hi):
            s = s_ref[lo:hi, lanes(r)]
            return s if bias is None else s + bias[lo:hi]
        h, q = n // 2, n // 4
        m = jnp.maximum(jnp.max(load(0, h), axis=0, keepdims=True),
                        jnp.max(load(h, n), axis=0, keepdims=True))
        if m_old is not None:
            m = jnp.maximum(m, m_old)
        for c in range(4):
            p_ref[c * q:(c + 1) * q, lanes(r)] = jnp.exp2(load(c * q, (c + 1) * q) - m).astype(BF16)
        return m

    cend = lax.broadcasted_iota(jnp.int32, (ncp, 1), 0) * CMP_STRIDE + (CMP_LEN - 1)
    valid_c = cend <= tlane
    bias_c = jnp.where(valid_c, 0.0, NEG_INF)
    keep_c = jnp.where(valid_c, 1.0, 0.0)
    s1_ref[:ncp, :] = _dot(kc_ref[...], qt)
    psum = jnp.zeros((ncp, tq), F32)
    for r in range(R):
        s_r = s1_ref[:ncp, lanes(r)] + bias_c
        p = jnp.exp2(s_r - jnp.max(s_r, axis=0, keepdims=True)) * keep_c
        p = p / jnp.maximum(jnp.sum(p, axis=0, keepdims=True), 1e-30)
        psum = psum + p
        pc_ref[:, lanes(r)] = p.astype(BF16)
    o_cmp = _dot(vct_ref[...], pc_ref[...])

    wlen = (-(-(WIN_LEN - 1) // BLOCK)) * BLOCK + tq
    w0 = pl.multiple_of(jnp.maximum(t0 + tq - wlen, 0), BLOCK)
    kpos = w0 + lax.broadcasted_iota(jnp.int32, (wlen, 1), 0)
    bias_w = jnp.where((kpos <= tlane) & (tlane - kpos <= WIN_LEN - 1), 0.0, NEG_INF)
    sw_ref[...] = _dot(kw_ref[pl.ds(w0, wlen), :], qt)
    for r in range(R):
        softmax_cols(sw_ref, pw_ref, wlen, r, bias=bias_w)
    wb = w0 // BLOCK
    ow = _dot(wt_ref[wb], pw_ref[:BLOCK, :])
    for jb in range(1, wlen // BLOCK):
        ow = ow + _dot(wt_ref[wb + jb], pw_ref[jb * BLOCK:(jb + 1) * BLOCK, :])
    o_win = ow[:DH] / ow[DH:DH + 1]
    for r in range(R):
        y_ref[:, lanes(r)] = (gate_t[r:r + 1] * o_cmp[:, lanes(r)] +
                              gate_t[2 * R + r:2 * R + r + 1] * o_win[:, lanes(r)])

    hi = psum.astype(BF16)
    rem = psum - hi.astype(F32)
    mid = rem.astype(BF16)
    lo = (rem - mid.astype(F32)).astype(BF16)
    ovt = ovt_ref[...]
    p_slc = _dot(ovt, hi) + _dot(ovt, mid) + _dot(ovt, lo)
    jj = lax.broadcasted_iota(jnp.int32, (n_s, 1), 0)
    blk_t = tlane // SEL_LEN
    forced = (jj == 0) | (jj == blk_t) | (jj == blk_t - 1)
    score = jnp.where(jj <= blk_t, p_slc + jnp.where(forced, FORCE_BONUS, 0.0), -1.0)
    nch = n_s // 8
    chunks = [score[c * 8:(c + 1) * 8] for c in range(nch)]
    cnt = [jnp.zeros((8, tq), F32) for _ in range(nch)]
    sub = lax.broadcasted_iota(jnp.int32, (8, 1), 0)
    for i in range(n_s):
        row_i = score[i:i + 1]
        for c in range(nch):
            if c * 8 > i:
                beats = row_i >= chunks[c]
            elif c * 8 + 7 < i:
                beats = row_i > chunks[c]
            else:
                beats = (row_i > chunks[c]) | ((row_i == chunks[c]) & (sub + c * 8 > i))
            cnt[c] = cnt[c] + jnp.where(beats, 1.0, 0.0)
    k_sel = float(min(SEL_TOPK, n_s))
    bias_t = jnp.concatenate([jnp.where(cc < k_sel, 0.0, NEG_INF) for cc in cnt] +
                             [jnp.zeros((DH - n_s, tq), F32)], axis=0).astype(BF16)
    for r in range(R):
        qt_ref[DH:, lanes(r)] = bias_t

    m_ref[...] = jnp.full(m_ref.shape, NEG_INF, F32)
    acc_ref[...] = jnp.zeros(acc_ref.shape, F32)

    def scores(kt, s_ref):
        k0 = pl.multiple_of(kt * tk, tk)
        s_ref[...] = _dot(ka_ref[pl.ds(k0, tk), :], qt_ref[...])

    def tile(kt, bufs, causal):
        s_ref, pt_ref, al_ref = bufs
        if causal:
            kpos = kt * tk + lax.broadcasted_iota(jnp.int32, (tk, 1), 0)
            bias_d = jnp.where(kpos <= tlane, 0.0, NEG_INF)
        for r in range(R):
            m_old = m_ref[:, lanes(r)]
            m_new = softmax_cols(s_ref, pt_ref, tk, r, bias=bias_d if causal else None, m_old=m_old)
            al_ref[:, lanes(r)] = jnp.exp2(m_old - m_new)
            m_ref[:, lanes(r)] = m_new
        pv = _dot(vt_ref[kt], pt_ref[...])
        acc_ref[...] = al_ref[...] * acc_ref[...] + pv

    even = (s0_ref, p0_ref, al0_ref)
    odd = (s1_ref, p1_ref, al1_ref)
    kd = t0 // tk
    scores(0, s0_ref)

    def pair(i, carry):
        scores(2 * i + 1, s1_ref)
        tile(2 * i, even, False)
        scores(2 * i + 2, s0_ref)
        tile(2 * i + 1, odd, False)
        return carry

    lax.fori_loop(0, kd // 2, pair, 0)

    @pl.when(kd % 2 == 0)
    def _():
        tile(kd, even, True)

    @pl.when(kd % 2 == 1)
    def _():
        scores(kd, s1_ref)
        tile(kd - 1, even, False)
        tile(kd, odd, True)

    acc = acc_ref[...]
    o_slc = acc[:DH] / acc[DH:DH + 1]
    for r in range(R):
        y = y_ref[:, lanes(r)] + gate_t[R + r:R + r + 1] * o_slc[:, lanes(r)]
        out_ref[:, r * DH:(r + 1) * DH] = y.T.astype(BF16)


def _nsa(o_r, o_p, o_g, kcmp, vcmp, b, s, tq, tk):
    ncp = s // CMP_STRIDE
    n_s = s // SEL_LEN
    assert n_s % 8 == 0 and n_s <= DH
    r_view = o_r.reshape(b, s, N_R)
    p_view = o_p.reshape(b, s, N_P)
    g_view = o_g.reshape(b, s, N_G)
    c = np.arange(ncp)[None, :]
    j = np.arange(n_s)[:, None]
    n_c = (s - CMP_LEN) // CMP_STRIDE + 1
    ovt = ((c * CMP_STRIDE <= j * SEL_LEN + SEL_LEN - 1) &
           (c * CMP_STRIDE + CMP_LEN - 1 >= j * SEL_LEN) & (c < n_c))
    ovt = jnp.asarray(ovt.astype(np.float32), BF16)
    et = jnp.asarray((np.arange(s)[:, None] // SEL_LEN == np.arange(DH)[None, :])
                     .astype(np.float32), BF16)
    wq = NSA_R * DH
    ks_col = W_QB // DH
    kw_col = ks_col + NSA_G
    gate_col = (2 * D_MODEL) // DH
    wlen = (-(-(WIN_LEN - 1) // BLOCK)) * BLOCK + tq
    kv = (None, s, DH)
    out = pl.pallas_call(
        functools.partial(_nsa_kernel, tq=tq, tk=tk, s_len=s, ncp=ncp, n_s=n_s),
        grid=(b, NSA_G, s // tq),
        in_specs=[pl.BlockSpec((None, tq, wq), lambda bi, g, qi: (bi, qi, g)),
                  pl.BlockSpec(kv, lambda bi, g, qi: (bi, 0, ks_col + g)),
                  pl.BlockSpec(kv, lambda bi, g, qi: (bi, 0, g)),
                  pl.BlockSpec(kv, lambda bi, g, qi: (bi, 0, kw_col + g)),
                  pl.BlockSpec(kv, lambda bi, g, qi: (bi, 0, NSA_G + g)),
                  pl.BlockSpec((None, None, ncp, DH), lambda bi, g, qi: (bi, g, 0, 0)),
                  pl.BlockSpec((None, None, ncp, DH), lambda bi, g, qi: (bi, g, 0, 0)),
                  pl.BlockSpec((None, tq, DH), lambda bi, g, qi: (bi, qi, gate_col + g)),
                  pl.BlockSpec((n_s, ncp), lambda bi, g, qi: (0, 0)),
                  pl.BlockSpec((s, DH), lambda bi, g, qi: (0, 0))],
        out_specs=pl.BlockSpec((None, tq, wq), lambda bi, g, qi: (bi, qi, g)),
        out_shape=jax.ShapeDtypeStruct((b, s, W_QB), BF16),
        scratch_shapes=[pltpu.VMEM((NSA_R * tq, 2 * DH), BF16),
                        pltpu.VMEM((s, 2 * DH), BF16),
                        pltpu.VMEM((s, 2 * DH), BF16),
                        pltpu.VMEM((s, 2 * DH), BF16),
                        pltpu.VMEM((NSA_R * tq, ncp), BF16),
                        pltpu.VMEM((NSA_R * tq, wlen), BF16),
                        pltpu.VMEM((NSA_R * tq, DH), F32),
                        pltpu.VMEM((NSA_R * tq, tk), F32),
                        pltpu.VMEM((NSA_R * tq, tk), F32),
                        pltpu.VMEM((NSA_R * tq, tk), BF16),
                        pltpu.VMEM((NSA_R * tq, tk), BF16),
                        pltpu.VMEM((NSA_R, tq, DH), F32),
                        pltpu.VMEM((NSA_R, tq, DH), F32),
                        pltpu.VMEM((NSA_R, tq, DH), F32),
                        pltpu.VMEM((NSA_R, tq, 2 * DH), F32)],
        compiler_params=_cparams(("arbitrary", "arbitrary", "arbitrary")),
        name="nsa_attention",
    )(r_view, r_view, p_view, r_view, p_view, kcmp, vcmp, g_view, ovt, et)
    return out.reshape(b * s, W_QB)


def _out_kernel(ya_ref, yb_ref, ga_ref, gb_ref, x_ref, woa_ref, wob_ref, wo_ref, out_ref):
    pa = _dot(ya_ref[...], woa_ref[...])
    pb = _dot(yb_ref[...], wob_ref[...])
    y = (ga_ref[...] * pa + gb_ref[...] * pb).astype(BF16)
    out_ref[...] = x_ref[...] + _dot(y, wo_ref[...])


def _merge_out(ya, yb, o_g, x2, woa, wob, wo, layer, tm):
    m = x2.shape[0]
    row = lambda w: pl.BlockSpec((tm, w), lambda i: (i, 0))
    const = lambda shape: pl.BlockSpec((None,) + shape, lambda i: (layer, 0, 0),
                                       pipeline_mode=pl.Buffered(1))
    return pl.pallas_call(
        _out_kernel,
        grid=(m // tm,),
        in_specs=[row(DIL_OUT), row(W_QB),
                  pl.BlockSpec((tm, D_MODEL), lambda i: (i, 0)),
                  pl.BlockSpec((tm, D_MODEL), lambda i: (i, 1)),
                  row(D_MODEL),
                  const((DIL_OUT, D_MODEL)), const((W_QB, D_MODEL)), const((D_MODEL, D_MODEL))],
        out_specs=row(D_MODEL),
        out_shape=jax.ShapeDtypeStruct((m, D_MODEL), F32),
        compiler_params=_cparams(("arbitrary",)),
        name="merge_out",
    )(ya, yb, o_g, o_g, x2, woa, wob, wo)


def _ffn_kernel(x_ref, g_ref, wg_ref, wu_ref, wd_ref, gf_ref, out_ref, h_ref, *, nf, final):
    f = pl.program_id(1)

    @pl.when(f == 0)
    def _():
        x = x_ref[...]
        h_ref[...] = _rms(x, g_ref[...]).astype(BF16)
        out_ref[...] = x

    h = h_ref[...]
    a = _dot(h, wg_ref[...])
    u = _dot(h, wu_ref[...])
    act = (a * jax.nn.sigmoid(a) * u).astype(BF16)
    out_ref[...] += _dot(act, wd_ref[...])

    if final:
        @pl.when(f == nf - 1)
        def _():
            out_ref[...] = _rms(out_ref[...], gf_ref[...])


def _ffn(x2, g, wg, wu, wd, layer, gf, tm, tf, final):
    m = x2.shape[0]
    nf = D_FF // tf
    return pl.pallas_call(
        functools.partial(_ffn_kernel, nf=nf, final=final),
        grid=(m // tm, nf),
        in_specs=[pl.BlockSpec((tm, D_MODEL), lambda i, f: (i, 0)),
                  pl.BlockSpec((1, D_MODEL), lambda i, f: (0, 0)),
                  pl.BlockSpec((None, D_MODEL, tf), lambda i, f: (layer, 0, f)),
                  pl.BlockSpec((None, D_MODEL, tf), lambda i, f: (layer, 0, f)),
                  pl.BlockSpec((None, tf, D_MODEL), lambda i, f: (layer, f, 0)),
                  pl.BlockSpec((1, D_MODEL), lambda i, f: (0, 0))],
        out_specs=pl.BlockSpec((tm, D_MODEL), lambda i, f: (i, 0)),
        out_shape=jax.ShapeDtypeStruct((m, D_MODEL), F32),
        scratch_shapes=[pltpu.VMEM((tm, D_MODEL), BF16)],
        compiler_params=_cparams(("arbitrary", "arbitrary")),
        name="ffn",
    )(x2, g, wg, wu, wd, gf)


def _prep_w_in(w):
    depth = w.shape[0]
    cols = (W_QA, W_QA, W_QA, W_QB, W_KV, W_KV, W_KV, W_KV, W_KV, W_KV,
            NSA_Q_HEADS * 3, D_MODEL, D_MODEL)
    offs = np.concatenate([[0], np.cumsum(cols)])
    seg = [w[:, :, int(offs[i]):int(offs[i + 1])] for i in range(len(cols))]
    qa, ka, va, qb, kc, vc, ks, vs, kw, vw, gate_b, gam, gbm = seg
    gate_b = gate_b.reshape(depth, D_MODEL, NSA_G, NSA_R, 3).transpose(0, 1, 2, 4, 3)
    gate_b = gate_b.reshape(depth, D_MODEL, NSA_G, 3 * NSA_R)
    gate_b = jnp.pad(gate_b, ((0, 0), (0, 0), (0, 0), (0, DH - 3 * NSA_R)))
    gate_b = gate_b.reshape(depth, D_MODEL, NSA_G * DH)
    gate_b = jnp.pad(gate_b, ((0, 0), (0, 0), (0, N_G - 2 * D_MODEL - NSA_G * DH)))
    return jnp.concatenate([qa, ka, va, qb, ks, kw, vs, vw, kc, vc, gam, gbm, gate_b],
                           axis=2).astype(BF16)


def kernel(x, positions, ln_mix, w_in, cmp_pos_k, cmp_pos_v, cmp_w1_k, cmp_w2_k, cmp_w1_v, cmp_w2_v,
           w_out_a, w_out_b, w_out, ln_ffn, w_ffn_gate, w_ffn_up, w_ffn_down, ln_final):
    b, s, d = x.shape
    depth = w_in.shape[0]
    assert d == D_MODEL and s % DIL_UNIT == 0
    m = b * s
    tm_in = min(1024, m)
    tm_out = min(256, m)
    tm_ffn = min(512, m)
    tq, tk = 128, min(512, s)
    ncp = s // CMP_STRIDE

    pos_f = positions.astype(F32)
    cos, sin = _rope_tables(pos_f.reshape(m, 1), tm_in)
    blk_end = np.minimum(np.arange(ncp) * CMP_STRIDE + CMP_LEN - 1, s - 1)
    cosc, sinc = _rope_tables(pos_f[:, blk_end].reshape(b * ncp, 1), ncp)
    cosc = cosc.reshape(b, ncp, DH)
    sinc = sinc.reshape(b, ncp, DH)

    cs = np.ones((1, N_ALL), np.float32)
    cs[:, :W_QA] = Q_SCALE
    cs[:, N_A:N_A + W_QB] = Q_SCALE
    cs = jnp.asarray(cs)

    w_in_b = _prep_w_in(w_in)
    woa_b, wob_b, wo_b = w_out_a.astype(BF16), w_out_b.astype(BF16), w_out.astype(BF16)
    wg_b, wu_b, wd_b = w_ffn_gate.astype(BF16), w_ffn_up.astype(BF16), w_ffn_down.astype(BF16)

    x2 = x.reshape(m, d)
    for l in range(depth):
        o_a, o_r, o_p, o_c, o_g = _inproj(x2, ln_mix[l][None, :], w_in_b, l, cs, cos, sin, tm_in)
        ya = _dilated(o_a, b, s)
        kcmp, vcmp = _compress(
            o_c, b, s, cmp_pos_k[l], cmp_pos_v[l],
            cmp_w1_k[l].reshape(CMP_LEN, DH, CMP_HIDDEN).astype(BF16),
            cmp_w1_v[l].reshape(CMP_LEN, DH, CMP_HIDDEN).astype(BF16),
            cmp_w2_k[l].astype(BF16), cmp_w2_v[l].astype(BF16), cosc, sinc)
        yb = _nsa(o_r, o_p, o_g, kcmp, vcmp, b, s, tq, tk)
        x2 = _merge_out(ya, yb, o_g, x2, woa_b, wob_b, wo_b, l, tm_out)
        x2 = _ffn(x2, ln_ffn[l][None, :], wg_b, wu_b, wd_b, l, ln_final[None, :], tm_ffn, 512,
                  l == depth - 1)
    return x2.reshape(b, s, d)
```

```python
import functools
import math

import numpy as np
import jax
import jax.numpy as jnp
from jax import lax
from jax.experimental import pallas as pl
from jax.experimental.pallas import tpu as pltpu

F32 = jnp.float32
BF16 = jnp.bfloat16

D_MODEL = 2048
DH = 128
HALF = DH // 2
ROPE_THETA = 10000.0
NORM_EPS = 1e-6
NEG_INF = -1e30
BLOCK = 128

DIL_GROUPS = ((128, 1), (512, 4), (2048, 16))
DIL_HPG = 4
DIL_HEADS = DIL_HPG * len(DIL_GROUPS)
DIL_OUT = DIL_HPG * DH
DIL_UNIT = DIL_GROUPS[-1][1] * BLOCK

NSA_Q_HEADS = 16
NSA_G = 2
NSA_R = NSA_Q_HEADS // NSA_G
CMP_LEN = 32
CMP_STRIDE = 16
CMP_HIDDEN = 256
SEL_LEN = 64
SEL_TOPK = 16
WIN_LEN = 512
FORCE_BONUS = 1e4
D_FF = 5632

W_QA = DIL_HEADS * DH
W_QB = NSA_Q_HEADS * DH
W_KV = NSA_G * DH
N_A = 3 * W_QA
N_R = W_QB + 2 * W_KV
N_P = 2 * W_KV
N_C = 2 * W_KV
TN_IN = 512
IN_SPLIT = 4
N_G = 2 * D_MODEL + TN_IN
N_ALL = N_A + N_R + N_P + N_C + N_G
VMEM_LIMIT = 56 * 1024 * 1024
Q_SCALE = DH ** -0.5 * math.log2(math.e)


def _cparams(sem):
    return pltpu.CompilerParams(dimension_semantics=sem, vmem_limit_bytes=VMEM_LIMIT)


def _dot_nt(a, b):
    return lax.dot_general(a, b, (((1,), (1,)), ((), ())), preferred_element_type=F32)


def _dot(a, b):
    return jnp.dot(a, b, preferred_element_type=F32)


def _rope_tab_kernel(pos_ref, inv_ref, sgn_ref, cos_ref, sin_ref):
    ang = pos_ref[...] * inv_ref[...]
    cos_ref[...] = jnp.cos(ang)
    sin_ref[...] = jnp.sin(ang) * sgn_ref[...]


def _rope_tables(pos_f, tm):
    m = pos_f.shape[0]
    inv = ROPE_THETA ** (-2.0 * jnp.arange(HALF, dtype=F32) / DH)
    inv = jnp.concatenate([inv, inv])[None, :]
    sgn = jnp.concatenate([-jnp.ones((HALF,), F32), jnp.ones((HALF,), F32)])[None, :]
    return pl.pallas_call(
        _rope_tab_kernel,
        grid=(m // tm,),
        in_specs=[pl.BlockSpec((tm, 1), lambda i: (i, 0)),
                  pl.BlockSpec((1, DH), lambda i: (0, 0)),
                  pl.BlockSpec((1, DH), lambda i: (0, 0))],
        out_specs=[pl.BlockSpec((tm, DH), lambda i: (i, 0)),
                   pl.BlockSpec((tm, DH), lambda i: (i, 0))],
        out_shape=[jax.ShapeDtypeStruct((m, DH), F32)] * 2,
        compiler_params=_cparams(("arbitrary",)),
        name="rope_tables",
    )(pos_f, inv, sgn)


def _rope(a, cos, sin_signed):
    return a * cos + pltpu.roll(a, HALF, 1) * sin_signed


def _rms(x, g):
    ms = jnp.mean(x * x, axis=-1, keepdims=True)
    return x * lax.rsqrt(ms + NORM_EPS) * g


T_AROPE = 2 * W_QA // TN_IN
T_A = N_A // TN_IN
T_R = T_A + N_R // TN_IN
T_P = T_R + N_P // TN_IN
T_C = T_P + N_C // TN_IN
T_ALL = N_ALL // TN_IN


def _inproj_kernel(x_ref, g_ref, w_ref, cs_ref, cos_ref, sin_ref,
                   oa_ref, or_ref, op_ref, oc_ref, og_ref, h_ref):
    j = pl.program_id(1)

    @pl.when(j == 0)
    def _():
        h_ref[...] = _rms(x_ref[...], g_ref[...]).astype(BF16)

    tm = h_ref.shape[0]
    groups = [slice(k * (tm // IN_SPLIT), (k + 1) * (tm // IN_SPLIT)) for k in range(IN_SPLIT)]

    def region(out_ref, epilogue):
        accs = [_dot(h_ref[rows, :], w_ref[...]) for rows in groups]
        for rows, acc in zip(groups, accs):
            epilogue(out_ref, rows, acc)

    def roped(out_ref, rows, acc):
        cos = cos_ref[rows, :]
        sin = sin_ref[rows, :]
        for hh in range(TN_IN // DH):
            sl = slice(hh * DH, (hh + 1) * DH)
            r = _rope(acc[:, sl], cos, sin) * cs_ref[:, sl]
            out_ref[rows, sl] = r.astype(out_ref.dtype)

    def plain(out_ref, rows, acc):
        out_ref[rows, :] = acc.astype(out_ref.dtype)

    def sigmoid(out_ref, rows, acc):
        out_ref[rows, :] = jax.nn.sigmoid(acc)

    pl.when(j < T_AROPE)(lambda: region(oa_ref, roped))
    pl.when((j >= T_AROPE) & (j < T_A))(lambda: region(oa_ref, plain))
    pl.when((j >= T_A) & (j < T_R))(lambda: region(or_ref, roped))
    pl.when((j >= T_R) & (j < T_P))(lambda: region(op_ref, plain))
    pl.when((j >= T_P) & (j < T_C))(lambda: region(oc_ref, plain))
    pl.when(j >= T_C)(lambda: region(og_ref, sigmoid))


def _inproj(x2, g, w, layer, cs, cos, sin, tm):
    m = x2.shape[0]

    def region(lo, hi):
        return pl.BlockSpec((tm, TN_IN), lambda i, j: (i, jnp.clip(j - lo, 0, hi - lo - 1)))

    return pl.pallas_call(
        _inproj_kernel,
        grid=(m // tm, T_ALL),
        in_specs=[pl.BlockSpec((tm, D_MODEL), lambda i, j: (i, 0)),
                  pl.BlockSpec((1, D_MODEL), lambda i, j: (0, 0)),
                  pl.BlockSpec((None, D_MODEL, TN_IN), lambda i, j: (layer, 0, j)),
                  pl.BlockSpec((1, TN_IN), lambda i, j: (0, j)),
                  pl.BlockSpec((tm, DH), lambda i, j: (i, 0)),
                  pl.BlockSpec((tm, DH), lambda i, j: (i, 0))],
        out_specs=[region(0, T_A), region(T_A, T_R), region(T_R, T_P), region(T_P, T_C),
                   region(T_C, T_ALL)],
        out_shape=[jax.ShapeDtypeStruct((m, N_A), F32),
                   jax.ShapeDtypeStruct((m, N_R), BF16),
                   jax.ShapeDtypeStruct((m, N_P), BF16),
                   jax.ShapeDtypeStruct((m, N_C), F32),
                   jax.ShapeDtypeStruct((m, N_G), F32)],
        scratch_shapes=[pltpu.VMEM((tm, D_MODEL), BF16)],
        compiler_params=_cparams(("arbitrary", "arbitrary")),
        name="inproj",
    )(x2, g, w, cs, cos, sin)


def _rows(start, dil):
    return pl.ds(start, BLOCK) if dil == 1 else pl.ds(start, BLOCK, stride=dil)


def _dil_kernel(*refs):
    out_ref, acc_scr, l_scr, m_scr = refs[-4:]
    u = pl.program_id(1)
    row = lax.broadcasted_iota(jnp.int32, (BLOCK, 2 * BLOCK), 0)
    col = lax.broadcasted_iota(jnp.int32, (BLOCK, 2 * BLOCK), 1)
    band = ((col < BLOCK) & (col >= row)) | ((col >= BLOCK) & (col - BLOCK <= row))
    bias = jnp.where(band, 0.0, NEG_INF)
    bias_first = jnp.where(band & ((col >= BLOCK) | (u > 0)), 0.0, NEG_INF)
    ones = jnp.ones((2 * BLOCK, DH), BF16)
    for gi, (_, dil) in enumerate(DIL_GROUPS):
        q_ref, kc_ref, kp_ref, vc_ref, vp_ref = refs[5 * gi:5 * gi + 5]
        span = BLOCK * dil
        for rho in range(dil):
            for ub in range(DIL_UNIT // span):
                cur = _rows(ub * span + rho, dil)
                if ub == 0:
                    kp, vp, b_add = kp_ref[_rows(rho, dil), :], vp_ref[_rows(rho, dil), :], bias_first
                else:
                    prv = _rows((ub - 1) * span + rho, dil)
                    kp, vp, b_add = kc_ref[prv, :], vc_ref[prv, :], bias
                q = q_ref[cur, :].astype(BF16)
                k = jnp.concatenate([kp, kc_ref[cur, :]], axis=0).astype(BF16)
                v = jnp.concatenate([vp, vc_ref[cur, :]], axis=0).astype(BF16)
                s = _dot_nt(q, k) + b_add
                m = jnp.broadcast_to(jnp.max(s, axis=-1, keepdims=True), (BLOCK, DH))
                p = jnp.concatenate([jnp.exp2(s[:, :BLOCK] - m), jnp.exp2(s[:, BLOCK:] - m)], axis=1)
                pv = _dot(p.astype(BF16), jnp.concatenate([v, ones], axis=1))
                acc_scr[gi, cur, :] = pv[:, :DH]
                l_scr[gi, cur, :] = pv[:, DH:]
                m_scr[gi, cur, :] = m
    m0, m1, m2 = m_scr[0], m_scr[1], m_scr[2]
    m = jnp.maximum(jnp.maximum(m0, m1), m2)
    e0, e1, e2 = jnp.exp2(m0 - m), jnp.exp2(m1 - m), jnp.exp2(m2 - m)
    num = e0 * acc_scr[0] + e1 * acc_scr[1] + e2 * acc_scr[2]
    den = e0 * l_scr[0] + e1 * l_scr[1] + e2 * l_scr[2]
    out_ref[...] = (num / den).astype(BF16)


def _dilated(o_a, b, s):
    a_view = o_a.reshape(b, s, N_A)
    in_specs, args = [], []
    for gi, (_, dil) in enumerate(DIL_GROUPS):
        span = BLOCK * dil
        per = DIL_UNIT // span

        def cur(colbase, gi=gi):
            return pl.BlockSpec((None, DIL_UNIT, DH),
                                lambda bi, u, j: (bi, u, colbase + gi * DIL_HPG + j))

        def prev(colbase, gi=gi, span=span, per=per):
            return pl.BlockSpec((None, span, DH),
                                lambda bi, u, j: (bi, jnp.maximum(u * per - 1, 0),
                                                  colbase + gi * DIL_HPG + j))

        kcol, vcol = W_QA // DH, 2 * W_QA // DH
        in_specs += [cur(0), cur(kcol), prev(kcol), cur(vcol), prev(vcol)]
        args += [a_view] * 5
    out = pl.pallas_call(
        _dil_kernel,
        grid=(b, s // DIL_UNIT, DIL_HPG),
        in_specs=in_specs,
        out_specs=pl.BlockSpec((None, DIL_UNIT, DH), lambda bi, u, j: (bi, u, j)),
        out_shape=jax.ShapeDtypeStruct((b, s, DIL_OUT), BF16),
        scratch_shapes=[pltpu.VMEM((len(DIL_GROUPS), DIL_UNIT, DH), F32)] * 3,
        compiler_params=_cparams(("arbitrary", "arbitrary", "arbitrary")),
        name="dilated",
    )(*args)
    return out.reshape(b * s, DIL_OUT)


def _cmp_kernel(xk0_ref, xk1_ref, xv0_ref, xv1_ref, posk_ref, posv_ref, w1k_ref, w1v_ref,
                w2k_ref, w2v_ref, cos_ref, sin_ref, kc_ref, vc_ref, *, ncp):
    half = CMP_LEN // 2
    x_refs = ((xk0_ref, xk1_ref), (xv0_ref, xv1_ref))
    for kind in range(2):
        pos_ref, w1_ref, w2_ref, o_ref = ((posk_ref, w1k_ref, w2k_ref, kc_ref) if kind == 0
                                          else (posv_ref, w1v_ref, w2v_ref, vc_ref))
        for g in range(NSA_G):
            x_ref = x_refs[kind][g]
            a = jnp.zeros((ncp, CMP_HIDDEN), F32)
            bm = jnp.zeros((ncp, CMP_HIDDEN), F32)
            for l in range(half):
                x = x_ref[pl.ds(l, ncp, stride=CMP_STRIDE), :]
                a = a + _dot((x + pos_ref[l:l + 1, :]).astype(BF16), w1_ref[l])
                bm = bm + _dot((x + pos_ref[half + l:half + l + 1, :]).astype(BF16),
                               w1_ref[half + l])
            hid = a + pltpu.roll(bm, ncp - 1, 0)
            hid = jax.nn.gelu(hid, approximate=True).astype(BF16)
            out = _dot(hid, w2_ref[...])
            if kind == 0:
                out = _rope(out, cos_ref[...], sin_ref[...])
            o_ref[g] = out.astype(BF16)


def _compress(o_c, b, s, posk, posv, w1k, w1v, w2k, w2v, cosc, sinc):
    ncp = s // CMP_STRIDE
    x_view = o_c.reshape(b, s, N_C)
    full = lambda shape: pl.BlockSpec(shape, lambda bi: (0,) * len(shape))
    return pl.pallas_call(
        functools.partial(_cmp_kernel, ncp=ncp),
        grid=(b,),
        in_specs=[pl.BlockSpec((None, s, DH), lambda bi, c=c: (bi, 0, c)) for c in range(N_C // DH)] +
                 [full((CMP_LEN, DH)), full((CMP_LEN, DH)),
                  full((CMP_LEN, DH, CMP_HIDDEN)), full((CMP_LEN, DH, CMP_HIDDEN)),
                  full((CMP_HIDDEN, DH)), full((CMP_HIDDEN, DH)),
                  pl.BlockSpec((None, ncp, DH), lambda bi: (bi, 0, 0)),
                  pl.BlockSpec((None, ncp, DH), lambda bi: (bi, 0, 0))],
        out_specs=[pl.BlockSpec((None, NSA_G, ncp, DH), lambda bi: (bi, 0, 0, 0)),
                   pl.BlockSpec((None, NSA_G, ncp, DH), lambda bi: (bi, 0, 0, 0))],
        out_shape=[jax.ShapeDtypeStruct((b, NSA_G, ncp, DH), BF16)] * 2,
        compiler_params=_cparams(("arbitrary",)),
        name="nsa_compress",
    )(x_view, x_view, x_view, x_view, posk, posv, w1k, w1v, w2k, w2v, cosc, sinc)


def _nsa_kernel(q_ref, ks_ref, vs_ref, kw_ref, vw_ref, kc_ref, vc_ref, gate_ref, ovt_ref, et_ref,
                out_ref, qa_ref, ka_ref, va_ref, wa_ref, p_ref, pw_ref, y_ref,
                s0_ref, s1_ref, p0_ref, p1_ref, m_ref, al0_ref, al1_ref, acc_ref,
                *, tq, tk, s_len, ncp, n_s):
    R = NSA_R
    qi = pl.program_id(2)
    t0 = qi * tq

    @pl.when(qi == 0)
    def _():
        ones = jnp.ones((s_len, DH), BF16)
        ka_ref[:, :DH] = ks_ref[...]
        ka_ref[:, DH:] = et_ref[...]
        va_ref[:, :DH] = vs_ref[...]
        va_ref[:, DH:] = ones
        wa_ref[:, :DH] = vw_ref[...]
        wa_ref[:, DH:] = ones

    for r in range(R):
        qa_ref[r * tq:(r + 1) * tq, :DH] = q_ref[:, r * DH:(r + 1) * DH]
    q = qa_ref[:, :DH]
    trow = t0 + lax.broadcasted_iota(jnp.int32, (tq, 1), 0)
    tlane = t0 + lax.broadcasted_iota(jnp.int32, (1, tq), 1)
    head = lambda a, r: a[r * tq:(r + 1) * tq]

    cend = lax.broadcasted_iota(jnp.int32, (1, ncp), 1) * CMP_STRIDE + (CMP_LEN - 1)
    valid_c = cend <= trow
    bias_c = jnp.where(valid_c, 0.0, NEG_INF)
    keep_c = jnp.where(valid_c, 1.0, 0.0)
    s = _dot_nt(q, kc_ref[...])
    chunks_c = [slice(c * DH, (c + 1) * DH) for c in range(ncp // DH)]
    psum = [jnp.zeros((tq, DH), F32) for _ in chunks_c]
    for r in range(R):
        s_r = head(s, r) + bias_c
        m_c = jnp.broadcast_to(jnp.max(s_r, axis=-1, keepdims=True), (tq, DH))
        e = [jnp.exp2(s_r[:, cols] - m_c) * keep_c[:, cols] for cols in chunks_c]
        l_c = jnp.maximum(jnp.sum(sum(e), axis=-1, keepdims=True), 1e-30)
        l_c = jnp.broadcast_to(l_c, (tq, DH))
        for ci, cols in enumerate(chunks_c):
            p = e[ci] / l_c
            psum[ci] = psum[ci] + p
            p_ref[r * tq:(r + 1) * tq, cols] = p.astype(BF16)
    psum = jnp.concatenate(psum, axis=1)
    o_cmp = _dot(p_ref[:, :ncp], vc_ref[...])

    wlen = (-(-(WIN_LEN - 1) // BLOCK)) * BLOCK + tq
    w0 = pl.multiple_of(jnp.maximum(t0 + tq - wlen, 0), BLOCK)
    kpos = w0 + lax.broadcasted_iota(jnp.int32, (1, wlen), 1)
    bias_w = jnp.where((kpos <= trow) & (trow - kpos <= WIN_LEN - 1), 0.0, NEG_INF)
    s = _dot_nt(q, kw_ref[pl.ds(w0, wlen), :])
    for r in range(R):
        s_r = head(s, r) + bias_w
        m_w = jnp.broadcast_to(jnp.max(s_r, axis=-1, keepdims=True), (tq, DH))
        for c in range(wlen // DH):
            cols = slice(c * DH, (c + 1) * DH)
            pw_ref[r * tq:(r + 1) * tq, cols] = jnp.exp2(s_r[:, cols] - m_w).astype(BF16)
    ow = _dot(pw_ref[...], wa_ref[pl.ds(w0, wlen), :])
    o_win = ow[:, :DH] / ow[:, DH:]
    gate = gate_ref[...]
    for r in range(R):
        y_ref[r * tq:(r + 1) * tq, :] = (gate[:, r:r + 1] * head(o_cmp, r) +
                                         gate[:, 2 * R + r:2 * R + r + 1] * head(o_win, r))

    hi = psum.astype(BF16)
    rem = psum - hi.astype(F32)
    mid = rem.astype(BF16)
    lo = (rem - mid.astype(F32)).astype(BF16)
    ovt = ovt_ref[...]
    p_slc = _dot_nt(ovt, hi) + _dot_nt(ovt, mid) + _dot_nt(ovt, lo)
    jj = lax.broadcasted_iota(jnp.int32, (n_s, 1), 0)
    blk_t = tlane // SEL_LEN
    forced = (jj == 0) | (jj == blk_t) | (jj == blk_t - 1)
    score = jnp.where(jj <= blk_t, p_slc + jnp.where(forced, FORCE_BONUS, 0.0), -1.0)
    nch = n_s // 8
    chunks = [score[c * 8:(c + 1) * 8] for c in range(nch)]
    cnt = [jnp.zeros((8, tq), F32) for _ in range(nch)]
    sub = lax.broadcasted_iota(jnp.int32, (8, 1), 0)
    for i in range(n_s):
        row_i = score[i:i + 1]
        for c in range(nch):
            if c * 8 > i:
                beats = row_i >= chunks[c]
            elif c * 8 + 7 < i:
                beats = row_i > chunks[c]
            else:
                beats = (row_i > chunks[c]) | ((row_i == chunks[c]) & (sub + c * 8 > i))
            cnt[c] = cnt[c] + jnp.where(beats, 1.0, 0.0)
    k_sel = float(min(SEL_TOPK, n_s))
    bias_t = jnp.concatenate([jnp.where(cc < k_sel, 0.0, NEG_INF) for cc in cnt] +
                             [jnp.zeros((DH - n_s, tq), F32)], axis=0)
    bias_q = bias_t.T.astype(BF16)
    for r in range(R):
        qa_ref[r * tq:(r + 1) * tq, DH:] = bias_q

    m_ref[...] = jnp.full(m_ref.shape, NEG_INF, F32)
    acc_ref[...] = jnp.zeros(acc_ref.shape, F32)

    def scores(kt, s_ref):
        k0 = pl.multiple_of(kt * tk, tk)
        s_ref[...] = _dot_nt(qa_ref[...], ka_ref[pl.ds(k0, tk), :])

    def tile(kt, bufs, causal):
        s_ref, pt_ref, al_ref = bufs
        if causal:
            kpos = kt * tk + lax.broadcasted_iota(jnp.int32, (1, tk), 1)
            bias_d = jnp.where(kpos <= trow, 0.0, NEG_INF)
        for r in range(R):
            rows = slice(r * tq, (r + 1) * tq)
            s_r = s_ref[rows, :]
            if causal:
                s_r = s_r + bias_d
            m_old = m_ref[r]
            m_new = jnp.maximum(m_old, jnp.max(s_r, axis=-1, keepdims=True))
            for c in range(tk // DH):
                cols = slice(c * DH, (c + 1) * DH)
                pt_ref[rows, cols] = jnp.exp2(s_r[:, cols] - m_new).astype(BF16)
            al_ref[r] = jnp.exp2(m_old - m_new)
            m_ref[r] = m_new
        k0 = pl.multiple_of(kt * tk, tk)
        pv = _dot(pt_ref[...], va_ref[pl.ds(k0, tk), :]).reshape(R, tq, 2 * DH)
        al = al_ref[...]
        acc_ref[:, :, :DH] = al * acc_ref[:, :, :DH] + pv[:, :, :DH]
        acc_ref[:, :, DH:] = al * acc_ref[:, :, DH:] + pv[:, :, DH:]

    even = (s0_ref, p0_ref, al0_ref)
    odd = (s1_ref, p1_ref, al1_ref)
    kd = t0 // tk
    scores(0, s0_ref)

    def pair(i, carry):
        scores(2 * i + 1, s1_ref)
        tile(2 * i, even, False)
        scores(2 * i + 2, s0_ref)
        tile(2 * i + 1, odd, False)
        return carry

    lax.fori_loop(0, kd // 2, pair, 0)

    @pl.when(kd % 2 == 0)
    def _():
        tile(kd, even, True)

    @pl.when(kd % 2 == 1)
    def _():
        scores(kd, s1_ref)
        tile(kd - 1, even, False)
        tile(kd, odd, True)

    acc = acc_ref[...]
    o_slc = (acc[:, :, :DH] / acc[:, :, DH:]).reshape(R * tq, DH)

    gate = gate_ref[...]
    for r in range(R):
        y = y_ref[r * tq:(r + 1) * tq, :] + gate[:, R + r:R + r + 1] * head(o_slc, r)
        out_ref[:, r * DH:(r + 1) * DH] = y.astype(BF16)


def _nsa(o_r, o_p, o_g, kcmp, vcmp, b, s, tq, tk):
    ncp = s // CMP_STRIDE
    n_s = s // SEL_LEN
    assert n_s % 8 == 0 and n_s <= DH
    r_view = o_r.reshape(b, s, N_R)
    p_view = o_p.reshape(b, s, N_P)
    g_view = o_g.reshape(b, s, N_G)
    c = np.arange(ncp)[None, :]
    j = np.arange(n_s)[:, None]
    n_c = (s - CMP_LEN) // CMP_STRIDE + 1
    ovt = ((c * CMP_STRIDE <= j * SEL_LEN + SEL_LEN - 1) &
           (c * CMP_STRIDE + CMP_LEN - 1 >= j * SEL_LEN) & (c < n_c))
    ovt = jnp.asarray(ovt.astype(np.float32), BF16)
    et = jnp.asarray((np.arange(s)[:, None] // SEL_LEN == np.arange(DH)[None, :])
                     .astype(np.float32), BF16)
    wq = NSA_R * DH
    ks_col = W_QB // DH
    kw_col = ks_col + NSA_G
    gate_col = (2 * D_MODEL) // DH
    wlen = (-(-(WIN_LEN - 1) // BLOCK)) * BLOCK + tq
    kv = (None, s, DH)
    out = pl.pallas_call(
        functools.partial(_nsa_kernel, tq=tq, tk=tk, s_len=s, ncp=ncp, n_s=n_s),
        grid=(b, NSA_G, s // tq),
        in_specs=[pl.BlockSpec((None, tq, wq), lambda bi, g, qi: (bi, qi, g)),
                  pl.BlockSpec(kv, lambda bi, g, qi: (bi, 0, ks_col + g)),
                  pl.BlockSpec(kv, lambda bi, g, qi: (bi, 0, g)),
                  pl.BlockSpec(kv, lambda bi, g, qi: (bi, 0, kw_col + g)),
                  pl.BlockSpec(kv, lambda bi, g, qi: (bi, 0, NSA_G + g)),
                  pl.BlockSpec((None, None, ncp, DH), lambda bi, g, qi: (bi, g, 0, 0)),
                  pl.BlockSpec((None, None, ncp, DH), lambda bi, g, qi: (bi, g, 0, 0)),
                  pl.BlockSpec((None, tq, DH), lambda bi, g, qi: (bi, qi, gate_col + g)),
                  pl.BlockSpec((n_s, ncp), lambda bi, g, qi: (0, 0)),
                  pl.BlockSpec((s, DH), lambda bi, g, qi: (0, 0))],
        out_specs=pl.BlockSpec((None, tq, wq), lambda bi, g, qi: (bi, qi, g)),
        out_shape=jax.ShapeDtypeStruct((b, s, W_QB), BF16),
        scratch_shapes=[pltpu.VMEM((NSA_R * tq, 2 * DH), BF16),
                        pltpu.VMEM((s, 2 * DH), BF16),
                        pltpu.VMEM((s, 2 * DH), BF16),
                        pltpu.VMEM((s, 2 * DH), BF16),
                        pltpu.VMEM((NSA_R * tq, ncp), BF16),
                        pltpu.VMEM((NSA_R * tq, wlen), BF16),
                        pltpu.VMEM((NSA_R * tq, DH), F32),
                        pltpu.VMEM((NSA_R * tq, tk), F32),
                        pltpu.VMEM((NSA_R * tq, tk), F32),
                        pltpu.VMEM((NSA_R * tq, tk), BF16),
                        pltpu.VMEM((NSA_R * tq, tk), BF16),
                        pltpu.VMEM((NSA_R, tq, DH), F32),
                        pltpu.VMEM((NSA_R, tq, DH), F32),
                        pltpu.VMEM((NSA_R, tq, DH), F32),
                        pltpu.VMEM((NSA_R, tq, 2 * DH), F32)],
        compiler_params=_cparams(("arbitrary", "arbitrary", "arbitrary")),
        name="nsa_attention",
    )(r_view, r_view, p_view, r_view, p_view, kcmp, vcmp, g_view, ovt, et)
    return out.reshape(b * s, W_QB)


def _out_kernel(ya_ref, yb_ref, ga_ref, gb_ref, x_ref, woa_ref, wob_ref, wo_ref, out_ref):
    pa = _dot(ya_ref[...], woa_ref[...])
    pb = _dot(yb_ref[...], wob_ref[...])
    y = (ga_ref[...] * pa + gb_ref[...] * pb).astype(BF16)
    out_ref[...] = x_ref[...] + _dot(y, wo_ref[...])


def _merge_out(ya, yb, o_g, x2, woa, wob, wo, layer, tm):
    m = x2.shape[0]
    row = lambda w: pl.BlockSpec((tm, w), lambda i: (i, 0))
    const = lambda shape: pl.BlockSpec((None,) + shape, lambda i: (layer, 0, 0),
                                       pipeline_mode=pl.Buffered(1))
    return pl.pallas_call(
        _out_kernel,
        grid=(m // tm,),
        in_specs=[row(DIL_OUT), row(W_QB),
                  pl.BlockSpec((tm, D_MODEL), lambda i: (i, 0)),
                  pl.BlockSpec((tm, D_MODEL), lambda i: (i, 1)),
                  row(D_MODEL),
                  const((DIL_OUT, D_MODEL)), const((W_QB, D_MODEL)), const((D_MODEL, D_MODEL))],
        out_specs=row(D_MODEL),
        out_shape=jax.ShapeDtypeStruct((m, D_MODEL), F32),
        compiler_params=_cparams(("arbitrary",)),
        name="merge_out",
    )(ya, yb, o_g, o_g, x2, woa, wob, wo)


def _ffn_kernel(x_ref, g_ref, wg_ref, wu_ref, wd_ref, gf_ref, out_ref, h_ref, *, nf, final):
    f = pl.program_id(1)

    @pl.when(f == 0)
    def _():
        x = x_ref[...]
        h_ref[...] = _rms(x, g_ref[...]).astype(BF16)
        out_ref[...] = x

    h = h_ref[...]
    a = _dot(h, wg_ref[...])
    u = _dot(h, wu_ref[...])
    act = (a * jax.nn.sigmoid(a) * u).astype(BF16)
    out_ref[...] += _dot(act, wd_ref[...])

    if final:
        @pl.when(f == nf - 1)
        def _():
            out_ref[...] = _rms(out_ref[...], gf_ref[...])


def _ffn(x2, g, wg, wu, wd, layer, gf, tm, tf, final):
    m = x2.shape[0]
    nf = D_FF // tf
    return pl.pallas_call(
        functools.partial(_ffn_kernel, nf=nf, final=final),
        grid=(m // tm, nf),
        in_specs=[pl.BlockSpec((tm, D_MODEL), lambda i, f: (i, 0)),
                  pl.BlockSpec((1, D_MODEL), lambda i, f: (0, 0)),
                  pl.BlockSpec((None, D_MODEL, tf), lambda i, f: (layer, 0, f)),
                  pl.BlockSpec((None, D_MODEL, tf), lambda i, f: (layer, 0, f)),
                  pl.BlockSpec((None, tf, D_MODEL), lambda i, f: (layer, f, 0)),
                  pl.BlockSpec((1, D_MODEL), lambda i, f: (0, 0))],
        out_specs=pl.BlockSpec((tm, D_MODEL), lambda i, f: (i, 0)),
        out_shape=jax.ShapeDtypeStruct((m, D_MODEL), F32),
        scratch_shapes=[pltpu.VMEM((tm, D_MODEL), BF16)],
        compiler_params=_cparams(("arbitrary", "arbitrary")),
        name="ffn",
    )(x2, g, wg, wu, wd, gf)


def _prep_w_in(w):
    depth = w.shape[0]
    cols = (W_QA, W_QA, W_QA, W_QB, W_KV, W_KV, W_KV, W_KV, W_KV, W_KV,
            NSA_Q_HEADS * 3, D_MODEL, D_MODEL)
    offs = np.concatenate([[0], np.cumsum(cols)])
    seg = [w[:, :, int(offs[i]):int(offs[i + 1])] for i in range(len(cols))]
    qa, ka, va, qb, kc, vc, ks, vs, kw, vw, gate_b, gam, gbm = seg
    gate_b = gate_b.reshape(depth, D_MODEL, NSA_G, NSA_R, 3).transpose(0, 1, 2, 4, 3)
    gate_b = gate_b.reshape(depth, D_MODEL, NSA_G, 3 * NSA_R)
    gate_b = jnp.pad(gate_b, ((0, 0), (0, 0), (0, 0), (0, DH - 3 * NSA_R)))
    gate_b = gate_b.reshape(depth, D_MODEL, NSA_G * DH)
    gate_b = jnp.pad(gate_b, ((0, 0), (0, 0), (0, N_G - 2 * D_MODEL - NSA_G * DH)))
    return jnp.concatenate([qa, ka, va, qb, ks, kw, vs, vw, kc, vc, gam, gbm, gate_b],
                           axis=2).astype(BF16)


def kernel(x, positions, ln_mix, w_in, cmp_pos_k, cmp_pos_v, cmp_w1_k, cmp_w2_k, cmp_w1_v, cmp_w2_v,
           w_out_a, w_out_b, w_out, ln_ffn, w_ffn_gate, w_ffn_up, w_ffn_down, ln_final):
    b, s, d = x.shape
    depth = w_in.shape[0]
    assert d == D_MODEL and s % DIL_UNIT == 0
    m = b * s
    tm_in = min(1024, m)
    tm_out = min(256, m)
    tm_ffn = min(1024, m)
    tq, tk = 128, min(512, s)
    ncp = s // CMP_STRIDE

    pos_f = positions.astype(F32)
    cos, sin = _rope_tables(pos_f.reshape(m, 1), tm_in)
    blk_end = np.minimum(np.arange(ncp) * CMP_STRIDE + CMP_LEN - 1, s - 1)
    cosc, sinc = _rope_tables(pos_f[:, blk_end].reshape(b * ncp, 1), ncp)
    cosc = cosc.reshape(b, ncp, DH)
    sinc = sinc.reshape(b, ncp, DH)

    cs = np.ones((1, N_ALL), np.float32)
    cs[:, :W_QA] = Q_SCALE
    cs[:, N_A:N_A + W_QB] = Q_SCALE
    cs = jnp.asarray(cs)

    w_in_b = _prep_w_in(w_in)
    woa_b, wob_b, wo_b = w_out_a.astype(BF16), w_out_b.astype(BF16), w_out.astype(BF16)
    wg_b, wu_b, wd_b = w_ffn_gate.astype(BF16), w_ffn_up.astype(BF16), w_ffn_down.astype(BF16)

    x2 = x.reshape(m, d)
    for l in range(depth):
        o_a, o_r, o_p, o_c, o_g = _inproj(x2, ln_mix[l][None, :], w_in_b, l, cs, cos, sin, tm_in)
        ya = _dilated(o_a, b, s)
        kcmp, vcmp = _compress(
            o_c, b, s, cmp_pos_k[l], cmp_pos_v[l],
            cmp_w1_k[l].reshape(CMP_LEN, DH, CMP_HIDDEN).astype(BF16),
            cmp_w1_v[l].reshape(CMP_LEN, DH, CMP_HIDDEN).astype(BF16),
            cmp_w2_k[l].astype(BF16), cmp_w2_v[l].astype(BF16), cosc, sinc)
        yb = _nsa(o_r, o_p, o_g, kcmp, vcmp, b, s, tq, tk)
        x2 = _merge_out(ya, yb, o_g, x2, woa_b, wob_b, wo_b, l, tm_out)
        x2 = _ffn(x2, ln_ffn[l][None, :], wg_b, wu_b, wd_b, l, ln_final[None, :], tm_ffn, 512,
                  l == depth - 1)
    return x2.reshape(b, s, d)
```

```python
import functools
import math

import numpy as np
import jax
import jax.numpy as jnp
from jax import lax
from jax.experimental import pallas as pl
from jax.experimental.pallas import tpu as pltpu

F32 = jnp.float32
BF16 = jnp.bfloat16

D_MODEL = 2048
DH = 128
HALF = DH // 2
ROPE_THETA = 10000.0
NORM_EPS = 1e-6
NEG_INF = -1e30
BLOCK = 128

DIL_GROUPS = ((128, 1), (512, 4), (2048, 16))
DIL_HPG = 4
DIL_HEADS = DIL_HPG * len(DIL_GROUPS)
DIL_OUT = DIL_HPG * DH
DIL_UNIT = DIL_GROUPS[-1][1] * BLOCK

NSA_Q_HEADS = 16
NSA_G = 2
NSA_R = NSA_Q_HEADS // NSA_G
CMP_LEN = 32
CMP_STRIDE = 16
CMP_HIDDEN = 256
SEL_LEN = 64
SEL_TOPK = 16
WIN_LEN = 512
FORCE_BONUS = 1e4
D_FF = 5632

W_QA = DIL_HEADS * DH
W_QB = NSA_Q_HEADS * DH
W_KV = NSA_G * DH
N_A = 3 * W_QA
N_R = W_QB + 2 * W_KV
N_P = 2 * W_KV
N_C = 2 * W_KV
TN_IN = 512
IN_SPLIT = 4
N_G = 2 * D_MODEL + TN_IN
N_ALL = N_A + N_R + N_P + N_C + N_G
VMEM_LIMIT = 56 * 1024 * 1024
Q_SCALE = DH ** -0.5 * math.log2(math.e)


def _cparams(sem):
    return pltpu.CompilerParams(dimension_semantics=sem, vmem_limit_bytes=VMEM_LIMIT)


def _dot(a, b):
    return jnp.dot(a, b, preferred_element_type=F32)


def _rope_tab_kernel(pos_ref, inv_ref, sgn_ref, cos_ref, sin_ref):
    ang = pos_ref[...] * inv_ref[...]
    cos_ref[...] = jnp.cos(ang)
    sin_ref[...] = jnp.sin(ang) * sgn_ref[...]


def _rope_tables(pos_f, tm):
    m = pos_f.shape[0]
    inv = ROPE_THETA ** (-2.0 * jnp.arange(HALF, dtype=F32) / DH)
    inv = jnp.concatenate([inv, inv])[None, :]
    sgn = jnp.concatenate([-jnp.ones((HALF,), F32), jnp.ones((HALF,), F32)])[None, :]
    return pl.pallas_call(
        _rope_tab_kernel,
        grid=(m // tm,),
        in_specs=[pl.BlockSpec((tm, 1), lambda i: (i, 0)),
                  pl.BlockSpec((1, DH), lambda i: (0, 0)),
                  pl.BlockSpec((1, DH), lambda i: (0, 0))],
        out_specs=[pl.BlockSpec((tm, DH), lambda i: (i, 0)),
                   pl.BlockSpec((tm, DH), lambda i: (i, 0))],
        out_shape=[jax.ShapeDtypeStruct((m, DH), F32)] * 2,
        compiler_params=_cparams(("arbitrary",)),
        name="rope_tables",
    )(pos_f, inv, sgn)


def _rope(a, cos, sin_signed):
    return a * cos + pltpu.roll(a, HALF, 1) * sin_signed


def _rms(x, g):
    ms = jnp.mean(x * x, axis=-1, keepdims=True)
    return x * lax.rsqrt(ms + NORM_EPS) * g


T_AROPE = 2 * W_QA // TN_IN
T_A = N_A // TN_IN
T_R = T_A + N_R // TN_IN
T_P = T_R + N_P // TN_IN
T_C = T_P + N_C // TN_IN
T_ALL = N_ALL // TN_IN


def _inproj_kernel(x_ref, g_ref, w_ref, cs_ref, cos_ref, sin_ref,
                   oa_ref, or_ref, op_ref, oc_ref, og_ref, h_ref):
    j = pl.program_id(1)

    @pl.when(j == 0)
    def _():
        h_ref[...] = _rms(x_ref[...], g_ref[...]).astype(BF16)

    tm = h_ref.shape[0]
    groups = [slice(k * (tm // IN_SPLIT), (k + 1) * (tm // IN_SPLIT)) for k in range(IN_SPLIT)]

    def region(out_ref, epilogue):
        accs = [_dot(h_ref[rows, :], w_ref[...]) for rows in groups]
        for rows, acc in zip(groups, accs):
            epilogue(out_ref, rows, acc)

    def roped(out_ref, rows, acc):
        cos = cos_ref[rows, :]
        sin = sin_ref[rows, :]
        for hh in range(TN_IN // DH):
            sl = slice(hh * DH, (hh + 1) * DH)
            r = _rope(acc[:, sl], cos, sin) * cs_ref[:, sl]
            out_ref[rows, sl] = r.astype(out_ref.dtype)

    def plain(out_ref, rows, acc):
        out_ref[rows, :] = acc.astype(out_ref.dtype)

    def sigmoid(out_ref, rows, acc):
        out_ref[rows, :] = jax.nn.sigmoid(acc)

    pl.when(j < T_AROPE)(lambda: region(oa_ref, roped))
    pl.when((j >= T_AROPE) & (j < T_A))(lambda: region(oa_ref, plain))
    pl.when((j >= T_A) & (j < T_R))(lambda: region(or_ref, roped))
    pl.when((j >= T_R) & (j < T_P))(lambda: region(op_ref, plain))
    pl.when((j >= T_P) & (j < T_C))(lambda: region(oc_ref, plain))
    pl.when(j >= T_C)(lambda: region(og_ref, sigmoid))


def _inproj(x2, g, w, layer, cs, cos, sin, tm):
    m = x2.shape[0]

    def region(lo, hi):
        return pl.BlockSpec((tm, TN_IN), lambda i, j: (i, jnp.clip(j - lo, 0, hi - lo - 1)))

    return pl.pallas_call(
        _inproj_kernel,
        grid=(m // tm, T_ALL),
        in_specs=[pl.BlockSpec((tm, D_MODEL), lambda i, j: (i, 0)),
                  pl.BlockSpec((1, D_MODEL), lambda i, j: (0, 0)),
                  pl.BlockSpec((None, D_MODEL, TN_IN), lambda i, j: (layer, 0, j)),
                  pl.BlockSpec((1, TN_IN), lambda i, j: (0, j)),
                  pl.BlockSpec((tm, DH), lambda i, j: (i, 0)),
                  pl.BlockSpec((tm, DH), lambda i, j: (i, 0))],
        out_specs=[region(0, T_A), region(T_A, T_R), region(T_R, T_P), region(T_P, T_C),
                   region(T_C, T_ALL)],
        out_shape=[jax.ShapeDtypeStruct((m, N_A), F32),
                   jax.ShapeDtypeStruct((m, N_R), BF16),
                   jax.ShapeDtypeStruct((m, N_P), BF16),
                   jax.ShapeDtypeStruct((m, N_C), F32),
                   jax.ShapeDtypeStruct((m, N_G), F32)],
        scratch_shapes=[pltpu.VMEM((tm, D_MODEL), BF16)],
        compiler_params=_cparams(("arbitrary", "arbitrary")),
        name="inproj",
    )(x2, g, w, cs, cos, sin)


def _rows(start, dil):
    return pl.ds(start, BLOCK) if dil == 1 else pl.ds(start, BLOCK, stride=dil)


def _dil_kernel(*refs):
    out_ref, acc_scr, l_scr, m_scr = refs[-4:]
    u = pl.program_id(1)
    row = lax.broadcasted_iota(jnp.int32, (BLOCK, 2 * BLOCK), 0)
    col = lax.broadcasted_iota(jnp.int32, (BLOCK, 2 * BLOCK), 1)
    band = ((col < BLOCK) & (col >= row)) | ((col >= BLOCK) & (col - BLOCK <= row))
    bias = jnp.where(band, 0.0, NEG_INF)
    bias_first = jnp.where(band & ((col >= BLOCK) | (u > 0)), 0.0, NEG_INF)
    ones = jnp.ones((2 * BLOCK, DH), BF16)
    for gi, (_, dil) in enumerate(DIL_GROUPS):
        q_ref, kc_ref, kp_ref, vc_ref, vp_ref = refs[5 * gi:5 * gi + 5]
        span = BLOCK * dil
        for rho in range(dil):
            for ub in range(DIL_UNIT // span):
                cur = _rows(ub * span + rho, dil)
                if ub == 0:
                    kp, vp, b_add = kp_ref[_rows(rho, dil), :], vp_ref[_rows(rho, dil), :], bias_first
                else:
                    prv = _rows((ub - 1) * span + rho, dil)
                    kp, vp, b_add = kc_ref[prv, :], vc_ref[prv, :], bias
                q = q_ref[cur, :].astype(BF16)
                kt = jnp.concatenate([kp.T, kc_ref[cur, :].T], axis=1).astype(BF16)
                v = jnp.concatenate([vp, vc_ref[cur, :]], axis=0).astype(BF16)
                s = _dot(q, kt) + b_add
                m = jnp.broadcast_to(jnp.max(s, axis=-1, keepdims=True), (BLOCK, DH))
                p = jnp.concatenate([jnp.exp2(s[:, :BLOCK] - m), jnp.exp2(s[:, BLOCK:] - m)], axis=1)
                pv = _dot(p.astype(BF16), jnp.concatenate([v, ones], axis=1))
                acc_scr[gi, cur, :] = pv[:, :DH]
                l_scr[gi, cur, :] = pv[:, DH:]
                m_scr[gi, cur, :] = m
    m0, m1, m2 = m_scr[0], m_scr[1], m_scr[2]
    m = jnp.maximum(jnp.maximum(m0, m1), m2)
    e0, e1, e2 = jnp.exp2(m0 - m), jnp.exp2(m1 - m), jnp.exp2(m2 - m)
    num = e0 * acc_scr[0] + e1 * acc_scr[1] + e2 * acc_scr[2]
    den = e0 * l_scr[0] + e1 * l_scr[1] + e2 * l_scr[2]
    out_ref[...] = (num / den).astype(BF16)


def _dilated(o_a, b, s):
    a_view = o_a.reshape(b, s, N_A)
    in_specs, args = [], []
    for gi, (_, dil) in enumerate(DIL_GROUPS):
        span = BLOCK * dil
        per = DIL_UNIT // span

        def cur(colbase, gi=gi):
            return pl.BlockSpec((None, DIL_UNIT, DH),
                                lambda bi, u, j: (bi, u, colbase + gi * DIL_HPG + j))

        def prev(colbase, gi=gi, span=span, per=per):
            return pl.BlockSpec((None, span, DH),
                                lambda bi, u, j: (bi, jnp.maximum(u * per - 1, 0),
                                                  colbase + gi * DIL_HPG + j))

        kcol, vcol = W_QA // DH, 2 * W_QA // DH
        in_specs += [cur(0), cur(kcol), prev(kcol), cur(vcol), prev(vcol)]
        args += [a_view] * 5
    out = pl.pallas_call(
        _dil_kernel,
        grid=(b, s // DIL_UNIT, DIL_HPG),
        in_specs=in_specs,
        out_specs=pl.BlockSpec((None, DIL_UNIT, DH), lambda bi, u, j: (bi, u, j)),
        out_shape=jax.ShapeDtypeStruct((b, s, DIL_OUT), BF16),
        scratch_shapes=[pltpu.VMEM((len(DIL_GROUPS), DIL_UNIT, DH), F32)] * 3,
        compiler_params=_cparams(("arbitrary", "arbitrary", "arbitrary")),
        name="dilated",
    )(*args)
    return out.reshape(b * s, DIL_OUT)


def _cmp_kernel(xk0_ref, xk1_ref, xv0_ref, xv1_ref, posk_ref, posv_ref, w1k_ref, w1v_ref,
                w2k_ref, w2v_ref, cos_ref, sin_ref, kc_ref, vc_ref, *, ncp):
    half = CMP_LEN // 2
    x_refs = ((xk0_ref, xk1_ref), (xv0_ref, xv1_ref))
    for kind in range(2):
        pos_ref, w1_ref, w2_ref, o_ref = ((posk_ref, w1k_ref, w2k_ref, kc_ref) if kind == 0
                                          else (posv_ref, w1v_ref, w2v_ref, vc_ref))
        for g in range(NSA_G):
            x_ref = x_refs[kind][g]
            a = jnp.zeros((ncp, CMP_HIDDEN), F32)
            bm = jnp.zeros((ncp, CMP_HIDDEN), F32)
            for l in range(half):
                x = x_ref[pl.ds(l, ncp, stride=CMP_STRIDE), :]
                a = a + _dot((x + pos_ref[l:l + 1, :]).astype(BF16), w1_ref[l])
                bm = bm + _dot((x + pos_ref[half + l:half + l + 1, :]).astype(BF16),
                               w1_ref[half + l])
            hid = a + pltpu.roll(bm, ncp - 1, 0)
            hid = jax.nn.gelu(hid, approximate=True).astype(BF16)
            out = _dot(hid, w2_ref[...])
            if kind == 0:
                out = _rope(out, cos_ref[...], sin_ref[...])
            o_ref[g] = out.astype(BF16)


def _compress(o_c, b, s, posk, posv, w1k, w1v, w2k, w2v, cosc, sinc):
    ncp = s // CMP_STRIDE
    x_view = o_c.reshape(b, s, N_C)
    full = lambda shape: pl.BlockSpec(shape, lambda bi: (0,) * len(shape))
    return pl.pallas_call(
        functools.partial(_cmp_kernel, ncp=ncp),
        grid=(b,),
        in_specs=[pl.BlockSpec((None, s, DH), lambda bi, c=c: (bi, 0, c)) for c in range(N_C // DH)] +
                 [full((CMP_LEN, DH)), full((CMP_LEN, DH)),
                  full((CMP_LEN, DH, CMP_HIDDEN)), full((CMP_LEN, DH, CMP_HIDDEN)),
                  full((CMP_HIDDEN, DH)), full((CMP_HIDDEN, DH)),
                  pl.BlockSpec((None, ncp, DH), lambda bi: (bi, 0, 0)),
                  pl.BlockSpec((None, ncp, DH), lambda bi: (bi, 0, 0))],
        out_specs=[pl.BlockSpec((None, NSA_G, ncp, DH), lambda bi: (bi, 0, 0, 0)),
                   pl.BlockSpec((None, NSA_G, ncp, DH), lambda bi: (bi, 0, 0, 0))],
        out_shape=[jax.ShapeDtypeStruct((b, NSA_G, ncp, DH), BF16)] * 2,
        compiler_params=_cparams(("arbitrary",)),
        name="nsa_compress",
    )(x_view, x_view, x_view, x_view, posk, posv, w1k, w1v, w2k, w2v, cosc, sinc)


def _nsa_kernel(q_ref, ks_ref, vs_ref, kw_ref, vw_ref, kc_ref, vc_ref, gate_ref, ov_ref, et_ref,
                out_ref, qa_ref, kat_ref, kwt_ref, kct_ref, va_ref, wa_ref, p_ref, pw_ref, y_ref,
                s0_ref, s1_ref, p0_ref, p1_ref, m_ref, al0_ref, al1_ref, acc_ref,
                *, tq, tk, s_len, ncp, n_s):
    R = NSA_R
    qi = pl.program_id(2)
    t0 = qi * tq
    transposed = lambda a: a.astype(F32).T.astype(BF16)

    @pl.when(qi == 0)
    def _():
        ones = jnp.ones((s_len, DH), BF16)
        for c in range(s_len // tk):
            kat_ref[c, :DH, :] = transposed(ks_ref[c * tk:(c + 1) * tk, :])
            kat_ref[c, DH:, :] = et_ref[:, c * tk:(c + 1) * tk]
        for c in range(s_len // BLOCK):
            kwt_ref[c] = transposed(kw_ref[c * BLOCK:(c + 1) * BLOCK, :])
        kct_ref[...] = transposed(kc_ref[...])
        va_ref[:, :DH] = vs_ref[...]
        va_ref[:, DH:] = ones
        wa_ref[:, :DH] = vw_ref[...]
        wa_ref[:, DH:] = ones

    for r in range(R):
        qa_ref[r * tq:(r + 1) * tq, :DH] = q_ref[:, r * DH:(r + 1) * DH]
    q = qa_ref[:, :DH]
    trow = t0 + lax.broadcasted_iota(jnp.int32, (tq, 1), 0)
    tlane = t0 + lax.broadcasted_iota(jnp.int32, (1, tq), 1)
    head = lambda a, r: a[r * tq:(r + 1) * tq]

    cend = lax.broadcasted_iota(jnp.int32, (1, ncp), 1) * CMP_STRIDE + (CMP_LEN - 1)
    valid_c = cend <= trow
    bias_c = jnp.where(valid_c, 0.0, NEG_INF)
    keep_c = jnp.where(valid_c, 1.0, 0.0)
    s = _dot(q, kct_ref[...])
    chunks_c = [slice(c * DH, (c + 1) * DH) for c in range(ncp // DH)]
    psum = [jnp.zeros((tq, DH), F32) for _ in chunks_c]
    for r in range(R):
        s_r = head(s, r) + bias_c
        m_c = jnp.broadcast_to(jnp.max(s_r, axis=-1, keepdims=True), (tq, DH))
        e = [jnp.exp2(s_r[:, cols] - m_c) * keep_c[:, cols] for cols in chunks_c]
        l_c = jnp.maximum(jnp.sum(sum(e), axis=-1, keepdims=True), 1e-30)
        l_c = jnp.broadcast_to(l_c, (tq, DH))
        for ci, cols in enumerate(chunks_c):
            p = e[ci] / l_c
            psum[ci] = psum[ci] + p
            p_ref[r * tq:(r + 1) * tq, cols] = p.astype(BF16)
    psum = jnp.concatenate(psum, axis=1)
    o_cmp = _dot(p_ref[:, :ncp], vc_ref[...])

    wlen = (-(-(WIN_LEN - 1) // BLOCK)) * BLOCK + tq
    w0 = pl.multiple_of(jnp.maximum(t0 + tq - wlen, 0), BLOCK)
    kpos = w0 + lax.broadcasted_iota(jnp.int32, (1, wlen), 1)
    bias_w = jnp.where((kpos <= trow) & (trow - kpos <= WIN_LEN - 1), 0.0, NEG_INF)
    wb = w0 // BLOCK
    kwt = jnp.concatenate([kwt_ref[wb + jb] for jb in range(wlen // BLOCK)], axis=1)
    s = _dot(q, kwt)
    for r in range(R):
        s_r = head(s, r) + bias_w
        m_w = jnp.broadcast_to(jnp.max(s_r, axis=-1, keepdims=True), (tq, DH))
        for c in range(wlen // DH):
            cols = slice(c * DH, (c + 1) * DH)
            pw_ref[r * tq:(r + 1) * tq, cols] = jnp.exp2(s_r[:, cols] - m_w).astype(BF16)
    ow = _dot(pw_ref[...], wa_ref[pl.ds(w0, wlen), :])
    o_win = ow[:, :DH] / ow[:, DH:]
    gate = gate_ref[...]
    for r in range(R):
        y_ref[r * tq:(r + 1) * tq, :] = (gate[:, r:r + 1] * head(o_cmp, r) +
                                         gate[:, 2 * R + r:2 * R + r + 1] * head(o_win, r))

    hi = psum.astype(BF16)
    rem = psum - hi.astype(F32)
    mid = rem.astype(BF16)
    lo = (rem - mid.astype(F32)).astype(BF16)
    ov = ov_ref[...]
    p_slc = (_dot(hi, ov) + _dot(mid, ov) + _dot(lo, ov)).T[:n_s]
    jj = lax.broadcasted_iota(jnp.int32, (n_s, 1), 0)
    blk_t = tlane // SEL_LEN
    forced = (jj == 0) | (jj == blk_t) | (jj == blk_t - 1)
    score = jnp.where(jj <= blk_t, p_slc + jnp.where(forced, FORCE_BONUS, 0.0), -1.0)
    nch = n_s // 8
    chunks = [score[c * 8:(c + 1) * 8] for c in range(nch)]
    cnt = [jnp.zeros((8, tq), F32) for _ in range(nch)]
    sub = lax.broadcasted_iota(jnp.int32, (8, 1), 0)
    for i in range(n_s):
        row_i = score[i:i + 1]
        for c in range(nch):
            if c * 8 > i:
                beats = row_i >= chunks[c]
            elif c * 8 + 7 < i:
                beats = row_i > chunks[c]
            else:
                beats = (row_i > chunks[c]) | ((row_i == chunks[c]) & (sub + c * 8 > i))
            cnt[c] = cnt[c] + jnp.where(beats, 1.0, 0.0)
    k_sel = float(min(SEL_TOPK, n_s))
    bias_t = jnp.concatenate([jnp.where(cc < k_sel, 0.0, NEG_INF) for cc in cnt] +
                             [jnp.zeros((DH - n_s, tq), F32)], axis=0)
    bias_q = bias_t.T.astype(BF16)
    for r in range(R):
        qa_ref[r * tq:(r + 1) * tq, DH:] = bias_q

    m_ref[...] = jnp.full(m_ref.shape, NEG_INF, F32)
    acc_ref[...] = jnp.zeros(acc_ref.shape, F32)

    def scores(kt, s_ref):
        s_ref[...] = _dot(qa_ref[...], kat_ref[kt])

    def tile(kt, bufs, causal):
        s_ref, pt_ref, al_ref = bufs
        if causal:
            kpos = kt * tk + lax.broadcasted_iota(jnp.int32, (1, tk), 1)
            bias_d = jnp.where(kpos <= trow, 0.0, NEG_INF)
        for r in range(R):
            rows = slice(r * tq, (r + 1) * tq)
            s_r = s_ref[rows, :]
            if causal:
                s_r = s_r + bias_d
            m_old = m_ref[r]
            m_new = jnp.maximum(m_old, jnp.max(s_r, axis=-1, keepdims=True))
            for c in range(tk // DH):
                cols = slice(c * DH, (c + 1) * DH)
                pt_ref[rows, cols] = jnp.exp2(s_r[:, cols] - m_new).astype(BF16)
            al_ref[r] = jnp.exp2(m_old - m_new)
            m_ref[r] = m_new
        k0 = pl.multiple_of(kt * tk, tk)
        pv = _dot(pt_ref[...], va_ref[pl.ds(k0, tk), :]).reshape(R, tq, 2 * DH)
        al = al_ref[...]
        acc_ref[:, :, :DH] = al * acc_ref[:, :, :DH] + pv[:, :, :DH]
        acc_ref[:, :, DH:] = al * acc_ref[:, :, DH:] + pv[:, :, DH:]

    even = (s0_ref, p0_ref, al0_ref)
    odd = (s1_ref, p1_ref, al1_ref)
    kd = t0 // tk
    scores(0, s0_ref)

    def pair(i, carry):
        scores(2 * i + 1, s1_ref)
        tile(2 * i, even, False)
        scores(2 * i + 2, s0_ref)
        tile(2 * i + 1, odd, False)
        return carry

    lax.fori_loop(0, kd // 2, pair, 0)

    @pl.when(kd % 2 == 0)
    def _():
        tile(kd, even, True)

    @pl.when(kd % 2 == 1)
    def _():
        scores(kd, s1_ref)
        tile(kd - 1, even, False)
        tile(kd, odd, True)

    acc = acc_ref[...]
    o_slc = (acc[:, :, :DH] / acc[:, :, DH:]).reshape(R * tq, DH)

    gate = gate_ref[...]
    for r in range(R):
        y = y_ref[r * tq:(r + 1) * tq, :] + gate[:, R + r:R + r + 1] * head(o_slc, r)
        out_ref[:, r * DH:(r + 1) * DH] = y.astype(BF16)


def _nsa(o_r, o_p, o_g, kcmp, vcmp, b, s, tq, tk):
    ncp = s // CMP_STRIDE
    n_s = s // SEL_LEN
    assert n_s % 8 == 0 and n_s <= DH
    r_view = o_r.reshape(b, s, N_R)
    p_view = o_p.reshape(b, s, N_P)
    g_view = o_g.reshape(b, s, N_G)
    c = np.arange(ncp)[:, None]
    j = np.arange(DH)[None, :]
    n_c = (s - CMP_LEN) // CMP_STRIDE + 1
    ov = ((c * CMP_STRIDE <= j * SEL_LEN + SEL_LEN - 1) &
          (c * CMP_STRIDE + CMP_LEN - 1 >= j * SEL_LEN) & (c < n_c) & (j < n_s))
    ov = jnp.asarray(ov.astype(np.float32), BF16)
    et = jnp.asarray((np.arange(s)[None, :] // SEL_LEN == np.arange(DH)[:, None])
                     .astype(np.float32), BF16)
    wq = NSA_R * DH
    ks_col = W_QB // DH
    kw_col = ks_col + NSA_G
    gate_col = (2 * D_MODEL) // DH
    wlen = (-(-(WIN_LEN - 1) // BLOCK)) * BLOCK + tq
    kv = (None, s, DH)
    out = pl.pallas_call(
        functools.partial(_nsa_kernel, tq=tq, tk=tk, s_len=s, ncp=ncp, n_s=n_s),
        grid=(b, NSA_G, s // tq),
        in_specs=[pl.BlockSpec((None, tq, wq), lambda bi, g, qi: (bi, qi, g)),
                  pl.BlockSpec(kv, lambda bi, g, qi: (bi, 0, ks_col + g)),
                  pl.BlockSpec(kv, lambda bi, g, qi: (bi, 0, g)),
                  pl.BlockSpec(kv, lambda bi, g, qi: (bi, 0, kw_col + g)),
                  pl.BlockSpec(kv, lambda bi, g, qi: (bi, 0, NSA_G + g)),
                  pl.BlockSpec((None, None, ncp, DH), lambda bi, g, qi: (bi, g, 0, 0)),
                  pl.BlockSpec((None, None, ncp, DH), lambda bi, g, qi: (bi, g, 0, 0)),
                  pl.BlockSpec((None, tq, DH), lambda bi, g, qi: (bi, qi, gate_col + g)),
                  pl.BlockSpec((ncp, DH), lambda bi, g, qi: (0, 0)),
                  pl.BlockSpec((DH, s), lambda bi, g, qi: (0, 0))],
        out_specs=pl.BlockSpec((None, tq, wq), lambda bi, g, qi: (bi, qi, g)),
        out_shape=jax.ShapeDtypeStruct((b, s, W_QB), BF16),
        scratch_shapes=[pltpu.VMEM((NSA_R * tq, 2 * DH), BF16),
                        pltpu.VMEM((s // tk, 2 * DH, tk), BF16),
                        pltpu.VMEM((s // BLOCK, DH, BLOCK), BF16),
                        pltpu.VMEM((DH, ncp), BF16),
                        pltpu.VMEM((s, 2 * DH), BF16),
                        pltpu.VMEM((s, 2 * DH), BF16),
                        pltpu.VMEM((NSA_R * tq, ncp), BF16),
                        pltpu.VMEM((NSA_R * tq, wlen), BF16),
                        pltpu.VMEM((NSA_R * tq, DH), F32),
                        pltpu.VMEM((NSA_R * tq, tk), F32),
                        pltpu.VMEM((NSA_R * tq, tk), F32),
                        pltpu.VMEM((NSA_R * tq, tk), BF16),
                        pltpu.VMEM((NSA_R * tq, tk), BF16),
                        pltpu.VMEM((NSA_R, tq, DH), F32),
                        pltpu.VMEM((NSA_R, tq, DH), F32),
                        pltpu.VMEM((NSA_R, tq, DH), F32),
                        pltpu.VMEM((NSA_R, tq, 2 * DH), F32)],
        compiler_params=_cparams(("arbitrary", "arbitrary", "arbitrary")),
        name="nsa_attention",
    )(r_view, r_view, p_view, r_view, p_view, kcmp, vcmp, g_view, ov, et)
    return out.reshape(b * s, W_QB)


def _out_kernel(ya_ref, yb_ref, ga_ref, gb_ref, x_ref, woa_ref, wob_ref, wo_ref, out_ref):
    pa = _dot(ya_ref[...], woa_ref[...])
    pb = _dot(yb_ref[...], wob_ref[...])
    y = (ga_ref[...] * pa + gb_ref[...] * pb).astype(BF16)
    out_ref[...] = x_ref[...] + _dot(y, wo_ref[...])


def _merge_out(ya, yb, o_g, x2, woa, wob, wo, layer, tm):
    m = x2.shape[0]
    row = lambda w: pl.BlockSpec((tm, w), lambda i: (i, 0))
    const = lambda shape: pl.BlockSpec((None,) + shape, lambda i: (layer, 0, 0),
                                       pipeline_mode=pl.Buffered(1))
    return pl.pallas_call(
        _out_kernel,
        grid=(m // tm,),
        in_specs=[row(DIL_OUT), row(W_QB),
                  pl.BlockSpec((tm, D_MODEL), lambda i: (i, 0)),
                  pl.BlockSpec((tm, D_MODEL), lambda i: (i, 1)),
                  row(D_MODEL),
                  const((DIL_OUT, D_MODEL)), const((W_QB, D_MODEL)), const((D_MODEL, D_MODEL))],
        out_specs=row(D_MODEL),
        out_shape=jax.ShapeDtypeStruct((m, D_MODEL), F32),
        compiler_params=_cparams(("arbitrary",)),
        name="merge_out",
    )(ya, yb, o_g, o_g, x2, woa, wob, wo)


def _ffn_kernel(x_ref, g_ref, wg_ref, wu_ref, wd_ref, gf_ref, out_ref, h_ref, *, nf, final):
    f = pl.program_id(1)

    @pl.when(f == 0)
    def _():
        x = x_ref[...]
        h_ref[...] = _rms(x, g_ref[...]).astype(BF16)
        out_ref[...] = x

    h = h_ref[...]
    a = _dot(h, wg_ref[...])
    u = _dot(h, wu_ref[...])
    act = (a * jax.nn.sigmoid(a) * u).astype(BF16)
    out_ref[...] += _dot(act, wd_ref[...])

    if final:
        @pl.when(f == nf - 1)
        def _():
            out_ref[...] = _rms(out_ref[...], gf_ref[...])


def _ffn(x2, g, wg, wu, wd, layer, gf, tm, tf, final):
    m = x2.shape[0]
    nf = D_FF // tf
    return pl.pallas_call(
        functools.partial(_ffn_kernel, nf=nf, final=final),
        grid=(m // tm, nf),
        in_specs=[pl.BlockSpec((tm, D_MODEL), lambda i, f: (i, 0)),
                  pl.BlockSpec((1, D_MODEL), lambda i, f: (0, 0)),
                  pl.BlockSpec((None, D_MODEL, tf), lambda i, f: (layer, 0, f)),
                  pl.BlockSpec((None, D_MODEL, tf), lambda i, f: (layer, 0, f)),
                  pl.BlockSpec((None, tf, D_MODEL), lambda i, f: (layer, f, 0)),
                  pl.BlockSpec((1, D_MODEL), lambda i, f: (0, 0))],
        out_specs=pl.BlockSpec((tm, D_MODEL), lambda i, f: (i, 0)),
        out_shape=jax.ShapeDtypeStruct((m, D_MODEL), F32),
        scratch_shapes=[pltpu.VMEM((tm, D_MODEL), BF16)],
        compiler_params=_cparams(("arbitrary", "arbitrary")),
        name="ffn",
    )(x2, g, wg, wu, wd, gf)


def _prep_w_in(w):
    depth = w.shape[0]
    cols = (W_QA, W_QA, W_QA, W_QB, W_KV, W_KV, W_KV, W_KV, W_KV, W_KV,
            NSA_Q_HEADS * 3, D_MODEL, D_MODEL)
    offs = np.concatenate([[0], np.cumsum(cols)])
    seg = [w[:, :, int(offs[i]):int(offs[i + 1])] for i in range(len(cols))]
    qa, ka, va, qb, kc, vc, ks, vs, kw, vw, gate_b, gam, gbm = seg
    gate_b = gate_b.reshape(depth, D_MODEL, NSA_G, NSA_R, 3).transpose(0, 1, 2, 4, 3)
    gate_b = gate_b.reshape(depth, D_MODEL, NSA_G, 3 * NSA_R)
    gate_b = jnp.pad(gate_b, ((0, 0), (0, 0), (0, 0), (0, DH - 3 * NSA_R)))
    gate_b = gate_b.reshape(depth, D_MODEL, NSA_G * DH)
    gate_b = jnp.pad(gate_b, ((0, 0), (0, 0), (0, N_G - 2 * D_MODEL - NSA_G * DH)))
    return jnp.concatenate([qa, ka, va, qb, ks, kw, vs, vw, kc, vc, gam, gbm, gate_b],
                           axis=2).astype(BF16)


def kernel(x, positions, ln_mix, w_in, cmp_pos_k, cmp_pos_v, cmp_w1_k, cmp_w2_k, cmp_w1_v, cmp_w2_v,
           w_out_a, w_out_b, w_out, ln_ffn, w_ffn_gate, w_ffn_up, w_ffn_down, ln_final):
    b, s, d = x.shape
    depth = w_in.shape[0]
    assert d == D_MODEL and s % DIL_UNIT == 0
    m = b * s
    tm_in = min(1024, m)
    tm_out = min(256, m)
    tm_ffn = min(1024, m)
    tq, tk = 128, min(512, s)
    ncp = s // CMP_STRIDE

    pos_f = positions.astype(F32)
    cos, sin = _rope_tables(pos_f.reshape(m, 1), tm_in)
    blk_end = np.minimum(np.arange(ncp) * CMP_STRIDE + CMP_LEN - 1, s - 1)
    cosc, sinc = _rope_tables(pos_f[:, blk_end].reshape(b * ncp, 1), ncp)
    cosc = cosc.reshape(b, ncp, DH)
    sinc = sinc.reshape(b, ncp, DH)

    cs = np.ones((1, N_ALL), np.float32)
    cs[:, :W_QA] = Q_SCALE
    cs[:, N_A:N_A + W_QB] = Q_SCALE
    cs = jnp.asarray(cs)

    w_in_b = _prep_w_in(w_in)
    woa_b, wob_b, wo_b = w_out_a.astype(BF16), w_out_b.astype(BF16), w_out.astype(BF16)
    wg_b, wu_b, wd_b = w_ffn_gate.astype(BF16), w_ffn_up.astype(BF16), w_ffn_down.astype(BF16)

    x2 = x.reshape(m, d)
    for l in range(depth):
        o_a, o_r, o_p, o_c, o_g = _inproj(x2, ln_mix[l][None, :], w_in_b, l, cs, cos, sin, tm_in)
        ya = _dilated(o_a, b, s)
        kcmp, vcmp = _compress(
            o_c, b, s, cmp_pos_k[l], cmp_pos_v[l],
            cmp_w1_k[l].reshape(CMP_LEN, DH, CMP_HIDDEN).astype(BF16),
            cmp_w1_v[l].reshape(CMP_LEN, DH, CMP_HIDDEN).astype(BF16),
            cmp_w2_k[l].astype(BF16), cmp_w2_v[l].astype(BF16), cosc, sinc)
        yb = _nsa(o_r, o_p, o_g, kcmp, vcmp, b, s, tq, tk)
        x2 = _merge_out(ya, yb, o_g, x2, woa_b, wob_b, wo_b, l, tm_out)
        x2 = _ffn(x2, ln_ffn[l][None, :], wg_b, wu_b, wd_b, l, ln_final[None, :], tm_ffn, 512,
                  l == depth - 1)
    return x2.reshape(b, s, d)
```

```python
import functools
import math

import numpy as np
import jax
import jax.numpy as jnp
from jax import lax
from jax.experimental import pallas as pl
from jax.experimental.pallas import tpu as pltpu

F32 = jnp.float32
BF16 = jnp.bfloat16

D_MODEL = 2048
DH = 128
HALF = DH // 2
ROPE_THETA = 10000.0
NORM_EPS = 1e-6
NEG_INF = -1e30
BLOCK = 128

DIL_GROUPS = ((128, 1), (512, 4), (2048, 16))
DIL_HPG = 4
DIL_HEADS = DIL_HPG * len(DIL_GROUPS)
DIL_OUT = DIL_HPG * DH
DIL_UNIT = DIL_GROUPS[-1][1] * BLOCK

NSA_Q_HEADS = 16
NSA_G = 2
NSA_R = NSA_Q_HEADS // NSA_G
CMP_LEN = 32
CMP_STRIDE = 16
CMP_HIDDEN = 256
SEL_LEN = 64
SEL_TOPK = 16
WIN_LEN = 512
FORCE_BONUS = 1e4
D_FF = 5632

W_QA = DIL_HEADS * DH
W_QB = NSA_Q_HEADS * DH
W_KV = NSA_G * DH
TN_IN = 512
IN_SPLIT = 4
N_AROPE = 2 * W_QA
N_A = 3 * W_QA + 2 * W_KV
N_RROPE = W_QB + 2 * W_KV
N_R = N_RROPE + 2 * W_KV
N_G = 2 * D_MODEL + TN_IN
N_ALL = N_A + N_R + N_G
COL_KC = 3 * W_QA // DH
VMEM_LIMIT = 56 * 1024 * 1024
Q_SCALE = DH ** -0.5 * math.log2(math.e)


def _cparams(sem):
    return pltpu.CompilerParams(dimension_semantics=sem, vmem_limit_bytes=VMEM_LIMIT)


def _dot(a, b):
    return jnp.dot(a, b, preferred_element_type=F32)


def _rope_tab_kernel(pos_ref, inv_ref, sgn_ref, cos_ref, sin_ref):
    ang = pos_ref[...] * inv_ref[...]
    cos_ref[...] = jnp.cos(ang)
    sin_ref[...] = jnp.sin(ang) * sgn_ref[...]


def _rope_tables(pos_f, tm):
    m = pos_f.shape[0]
    inv = ROPE_THETA ** (-2.0 * jnp.arange(HALF, dtype=F32) / DH)
    inv = jnp.concatenate([inv, inv])[None, :]
    sgn = jnp.concatenate([-jnp.ones((HALF,), F32), jnp.ones((HALF,), F32)])[None, :]
    return pl.pallas_call(
        _rope_tab_kernel,
        grid=(m // tm,),
        in_specs=[pl.BlockSpec((tm, 1), lambda i: (i, 0)),
                  pl.BlockSpec((1, DH), lambda i: (0, 0)),
                  pl.BlockSpec((1, DH), lambda i: (0, 0))],
        out_specs=[pl.BlockSpec((tm, DH), lambda i: (i, 0)),
                   pl.BlockSpec((tm, DH), lambda i: (i, 0))],
        out_shape=[jax.ShapeDtypeStruct((m, DH), F32)] * 2,
        compiler_params=_cparams(("arbitrary",)),
        name="rope_tables",
    )(pos_f, inv, sgn)


def _rope(a, cos, sin_signed):
    return a * cos + pltpu.roll(a, HALF, 1) * sin_signed


def _rms(x, g):
    ms = jnp.mean(x * x, axis=-1, keepdims=True)
    return x * lax.rsqrt(ms + NORM_EPS) * g


T_AROPE = N_AROPE // TN_IN
T_A = N_A // TN_IN
T_RROPE = T_A + N_RROPE // TN_IN
T_R = T_A + N_R // TN_IN
T_ALL = N_ALL // TN_IN


def _norm_kernel(x_ref, g_ref, h_ref):
    h_ref[...] = _rms(x_ref[...], g_ref[...]).astype(BF16)


def _norm(x2, g, tm):
    m = x2.shape[0]
    return pl.pallas_call(
        _norm_kernel,
        grid=(m // tm,),
        in_specs=[pl.BlockSpec((tm, D_MODEL), lambda i: (i, 0)),
                  pl.BlockSpec((1, D_MODEL), lambda i: (0, 0))],
        out_specs=pl.BlockSpec((tm, D_MODEL), lambda i: (i, 0)),
        out_shape=jax.ShapeDtypeStruct((m, D_MODEL), BF16),
        compiler_params=_cparams(("arbitrary",)),
        name="rmsnorm",
    )(x2, g)


def _inproj_kernel(h_ref, w_ref, cs_ref, cos_ref, sin_ref, oa_ref, or_ref, og_ref):
    j = pl.program_id(1)

    tm = h_ref.shape[0]
    groups = [slice(k * (tm // IN_SPLIT), (k + 1) * (tm // IN_SPLIT)) for k in range(IN_SPLIT)]

    def region(out_ref, epilogue):
        accs = [_dot(h_ref[rows, :], w_ref[...]) for rows in groups]
        for rows, acc in zip(groups, accs):
            epilogue(out_ref, rows, acc)

    def roped(out_ref, rows, acc):
        cos = cos_ref[rows, :]
        sin = sin_ref[rows, :]
        for hh in range(TN_IN // DH):
            sl = slice(hh * DH, (hh + 1) * DH)
            r = _rope(acc[:, sl], cos, sin) * cs_ref[:, sl]
            out_ref[rows, sl] = r.astype(out_ref.dtype)

    def plain(out_ref, rows, acc):
        out_ref[rows, :] = acc.astype(out_ref.dtype)

    def sigmoid(out_ref, rows, acc):
        out_ref[rows, :] = jax.nn.sigmoid(acc)

    pl.when(j < T_AROPE)(lambda: region(oa_ref, roped))
    pl.when((j >= T_AROPE) & (j < T_A))(lambda: region(oa_ref, plain))
    pl.when((j >= T_A) & (j < T_RROPE))(lambda: region(or_ref, roped))
    pl.when((j >= T_RROPE) & (j < T_R))(lambda: region(or_ref, plain))
    pl.when(j >= T_R)(lambda: region(og_ref, sigmoid))


def _inproj(h, w, layer, cs, cos, sin, tm):
    m = h.shape[0]

    def region(lo, hi):
        return pl.BlockSpec((tm, TN_IN), lambda i, j: (i, jnp.clip(j - lo, 0, hi - lo - 1)))

    return pl.pallas_call(
        _inproj_kernel,
        grid=(m // tm, T_ALL),
        in_specs=[pl.BlockSpec((tm, D_MODEL), lambda i, j: (i, 0)),
                  pl.BlockSpec((None, D_MODEL, TN_IN), lambda i, j: (layer, 0, j)),
                  pl.BlockSpec((1, TN_IN), lambda i, j: (0, j)),
                  pl.BlockSpec((tm, DH), lambda i, j: (i, 0)),
                  pl.BlockSpec((tm, DH), lambda i, j: (i, 0))],
        out_specs=[region(0, T_A), region(T_A, T_R), region(T_R, T_ALL)],
        out_shape=[jax.ShapeDtypeStruct((m, N_A), F32),
                   jax.ShapeDtypeStruct((m, N_R), BF16),
                   jax.ShapeDtypeStruct((m, N_G), F32)],
        compiler_params=_cparams(("arbitrary", "arbitrary")),
        name="inproj",
    )(h, w, cs, cos, sin)


def _rows(start, dil):
    return pl.ds(start, BLOCK) if dil == 1 else pl.ds(start, BLOCK, stride=dil)


def _dil_kernel(*refs):
    out_ref, acc_scr, l_scr, m_scr = refs[-4:]
    u = pl.program_id(1)
    row = lax.broadcasted_iota(jnp.int32, (BLOCK, 2 * BLOCK), 0)
    col = lax.broadcasted_iota(jnp.int32, (BLOCK, 2 * BLOCK), 1)
    band = ((col < BLOCK) & (col >= row)) | ((col >= BLOCK) & (col - BLOCK <= row))
    bias = jnp.where(band, 0.0, NEG_INF)
    bias_first = jnp.where(band & ((col >= BLOCK) | (u > 0)), 0.0, NEG_INF)
    ones = jnp.ones((2 * BLOCK, DH), BF16)
    for gi, (_, dil) in enumerate(DIL_GROUPS):
        q_ref, kc_ref, kp_ref, vc_ref, vp_ref = refs[5 * gi:5 * gi + 5]
        span = BLOCK * dil
        for rho in range(dil):
            for ub in range(DIL_UNIT // span):
                cur = _rows(ub * span + rho, dil)
                if ub == 0:
                    kp, vp, b_add = kp_ref[_rows(rho, dil), :], vp_ref[_rows(rho, dil), :], bias_first
                else:
                    prv = _rows((ub - 1) * span + rho, dil)
                    kp, vp, b_add = kc_ref[prv, :], vc_ref[prv, :], bias
                q = q_ref[cur, :].astype(BF16)
                k = jnp.concatenate([kp, kc_ref[cur, :]], axis=0).astype(BF16)
                v = jnp.concatenate([vp, vc_ref[cur, :]], axis=0).astype(BF16)
                s = lax.dot_general(q, k, (((1,), (1,)), ((), ())), preferred_element_type=F32) + b_add
                m = jnp.broadcast_to(jnp.max(s, axis=-1, keepdims=True), (BLOCK, DH))
                p = jnp.concatenate([jnp.exp2(s[:, :BLOCK] - m), jnp.exp2(s[:, BLOCK:] - m)], axis=1)
                pv = _dot(p.astype(BF16), jnp.concatenate([v, ones], axis=1))
                acc_scr[gi, cur, :] = pv[:, :DH]
                l_scr[gi, cur, :] = pv[:, DH:]
                m_scr[gi, cur, :] = m
    m0, m1, m2 = m_scr[0], m_scr[1], m_scr[2]
    m = jnp.maximum(jnp.maximum(m0, m1), m2)
    e0, e1, e2 = jnp.exp2(m0 - m), jnp.exp2(m1 - m), jnp.exp2(m2 - m)
    num = e0 * acc_scr[0] + e1 * acc_scr[1] + e2 * acc_scr[2]
    den = e0 * l_scr[0] + e1 * l_scr[1] + e2 * l_scr[2]
    out_ref[...] = (num / den).astype(BF16)


def _dilated(o_a, b, s):
    a_view = o_a.reshape(b, s, N_A)
    in_specs, args = [], []
    for gi, (_, dil) in enumerate(DIL_GROUPS):
        span = BLOCK * dil
        per = DIL_UNIT // span

        def cur(colbase, gi=gi):
            return pl.BlockSpec((None, DIL_UNIT, DH),
                                lambda bi, u, j: (bi, u, colbase + gi * DIL_HPG + j))

        def prev(colbase, gi=gi, span=span, per=per):
            return pl.BlockSpec((None, span, DH),
                                lambda bi, u, j: (bi, jnp.maximum(u * per - 1, 0),
                                                  colbase + gi * DIL_HPG + j))

        kcol, vcol = W_QA // DH, 2 * W_QA // DH
        in_specs += [cur(0), cur(kcol), prev(kcol), cur(vcol), prev(vcol)]
        args += [a_view] * 5
    out = pl.pallas_call(
        _dil_kernel,
        grid=(b, s // DIL_UNIT, DIL_HPG),
        in_specs=in_specs,
        out_specs=pl.BlockSpec((None, DIL_UNIT, DH), lambda bi, u, j: (bi, u, j)),
        out_shape=jax.ShapeDtypeStruct((b, s, DIL_OUT), BF16),
        scratch_shapes=[pltpu.VMEM((len(DIL_GROUPS), DIL_UNIT, DH), F32)] * 3,
        compiler_params=_cparams(("arbitrary", "arbitrary", "arbitrary")),
        name="dilated",
    )(*args)
    return out.reshape(b * s, DIL_OUT)


def _cmp_kernel(xk0_ref, xk1_ref, xv0_ref, xv1_ref, posk_ref, posv_ref, w1k_ref, w1v_ref,
                w2k_ref, w2v_ref, cos_ref, sin_ref, kc_ref, vc_ref, *, ncp):
    half = CMP_LEN // 2
    x_refs = ((xk0_ref, xk1_ref), (xv0_ref, xv1_ref))
    for kind in range(2):
        pos_ref, w1_ref, w2_ref, o_ref = ((posk_ref, w1k_ref, w2k_ref, kc_ref) if kind == 0
                                          else (posv_ref, w1v_ref, w2v_ref, vc_ref))
        for g in range(NSA_G):
            x_ref = x_refs[kind][g]
            a = jnp.zeros((ncp, CMP_HIDDEN), F32)
            bm = jnp.zeros((ncp, CMP_HIDDEN), F32)
            for l in range(half):
                x = x_ref[pl.ds(l, ncp, stride=CMP_STRIDE), :]
                a = a + _dot((x + pos_ref[l:l + 1, :]).astype(BF16), w1_ref[l])
                bm = bm + _dot((x + pos_ref[half + l:half + l + 1, :]).astype(BF16),
                               w1_ref[half + l])
            hid = a + pltpu.roll(bm, ncp - 1, 0)
            hid = jax.nn.gelu(hid, approximate=True).astype(BF16)
            out = _dot(hid, w2_ref[...])
            if kind == 0:
                out = _rope(out, cos_ref[...], sin_ref[...])
            o_ref[g] = out.astype(BF16)


def _compress(o_a, b, s, posk, posv, w1k, w1v, w2k, w2v, cosc, sinc):
    ncp = s // CMP_STRIDE
    x_view = o_a.reshape(b, s, N_A)
    full = lambda shape: pl.BlockSpec(shape, lambda bi: (0,) * len(shape))
    return pl.pallas_call(
        functools.partial(_cmp_kernel, ncp=ncp),
        grid=(b,),
        in_specs=[pl.BlockSpec((None, s, DH), lambda bi, c=c: (bi, 0, COL_KC + c))
                  for c in range(2 * NSA_G)] +
                 [full((CMP_LEN, DH)), full((CMP_LEN, DH)),
                  full((CMP_LEN, DH, CMP_HIDDEN)), full((CMP_LEN, DH, CMP_HIDDEN)),
                  full((CMP_HIDDEN, DH)), full((CMP_HIDDEN, DH)),
                  pl.BlockSpec((None, ncp, DH), lambda bi: (bi, 0, 0)),
                  pl.BlockSpec((None, ncp, DH), lambda bi: (bi, 0, 0))],
        out_specs=[pl.BlockSpec((None, NSA_G, ncp, DH), lambda bi: (bi, 0, 0, 0)),
                   pl.BlockSpec((None, NSA_G, ncp, DH), lambda bi: (bi, 0, 0, 0))],
        out_shape=[jax.ShapeDtypeStruct((b, NSA_G, ncp, DH), BF16)] * 2,
        compiler_params=_cparams(("arbitrary",)),
        name="nsa_compress",
    )(x_view, x_view, x_view, x_view, posk, posv, w1k, w1v, w2k, w2v, cosc, sinc)


def _nsa_kernel(q_ref, ks_ref, vs_ref, kw_ref, vw_ref, kc_ref, vc_ref, gate_ref, ov_ref, et_ref,
                out_ref, qa_ref, kat_ref, kwt_ref, kct_ref, va_ref, wa_ref, p_ref, pw_ref, y_ref,
                s0_ref, s1_ref, p0_ref, p1_ref, m_ref, al0_ref, al1_ref, acc_ref,
                *, tq, tk, s_len, ncp, n_s):
    R = NSA_R
    qi = pl.program_id(2)
    t0 = qi * tq
    transposed = lambda a: a.astype(F32).T.astype(BF16)

    @pl.when(qi == 0)
    def _():
        ones = jnp.ones((s_len, DH), BF16)
        for c in range(s_len // tk):
            kat_ref[c, :DH, :] = transposed(ks_ref[c * tk:(c + 1) * tk, :])
            kat_ref[c, DH:, :] = et_ref[:, c * tk:(c + 1) * tk]
        for c in range(s_len // BLOCK):
            kwt_ref[c] = transposed(kw_ref[c * BLOCK:(c + 1) * BLOCK, :])
        kct_ref[...] = transposed(kc_ref[...])
        va_ref[:, :DH] = vs_ref[...]
        va_ref[:, DH:] = ones
        wa_ref[:, :DH] = vw_ref[...]
        wa_ref[:, DH:] = ones

    for r in range(R):
        qa_ref[r * tq:(r + 1) * tq, :DH] = q_ref[:, r * DH:(r + 1) * DH]
    q = qa_ref[:, :DH]
    trow = t0 + lax.broadcasted_iota(jnp.int32, (tq, 1), 0)
    tlane = t0 + lax.broadcasted_iota(jnp.int32, (1, tq), 1)
    head = lambda a, r: a[r * tq:(r + 1) * tq]

    cend = lax.broadcasted_iota(jnp.int32, (1, ncp), 1) * CMP_STRIDE + (CMP_LEN - 1)
    valid_c = cend <= trow
    bias_c = jnp.where(valid_c, 0.0, NEG_INF)
    keep_c = jnp.where(valid_c, 1.0, 0.0)
    s = _dot(q, kct_ref[...])
    chunks_c = [slice(c * DH, (c + 1) * DH) for c in range(ncp // DH)]
    psum = [jnp.zeros((tq, DH), F32) for _ in chunks_c]
    for r in range(R):
        s_r = head(s, r) + bias_c
        m_c = jnp.broadcast_to(jnp.max(s_r, axis=-1, keepdims=True), (tq, DH))
        e = [jnp.exp2(s_r[:, cols] - m_c) * keep_c[:, cols] for cols in chunks_c]
        l_c = jnp.maximum(jnp.sum(sum(e), axis=-1, keepdims=True), 1e-30)
        l_c = jnp.broadcast_to(l_c, (tq, DH))
        for ci, cols in enumerate(chunks_c):
            p = e[ci] / l_c
            psum[ci] = psum[ci] + p
            p_ref[r * tq:(r + 1) * tq, cols] = p.astype(BF16)
    psum = jnp.concatenate(psum, axis=1)
    o_cmp = _dot(p_ref[:, :ncp], vc_ref[...])

    wlen = (-(-(WIN_LEN - 1) // BLOCK)) * BLOCK + tq
    w0 = pl.multiple_of(jnp.maximum(t0 + tq - wlen, 0), BLOCK)
    kpos = w0 + lax.broadcasted_iota(jnp.int32, (1, wlen), 1)
    bias_w = jnp.where((kpos <= trow) & (trow - kpos <= WIN_LEN - 1), 0.0, NEG_INF)
    wb = w0 // BLOCK
    kwt = jnp.concatenate([kwt_ref[wb + jb] for jb in range(wlen // BLOCK)], axis=1)
    s = _dot(q, kwt)
    for r in range(R):
        s_r = head(s, r) + bias_w
        m_w = jnp.broadcast_to(jnp.max(s_r, axis=-1, keepdims=True), (tq, DH))
        for c in range(wlen // DH):
            cols = slice(c * DH, (c + 1) * DH)
            pw_ref[r * tq:(r + 1) * tq, cols] = jnp.exp2(s_r[:, cols] - m_w).astype(BF16)
    ow = _dot(pw_ref[...], wa_ref[pl.ds(w0, wlen), :])
    o_win = ow[:, :DH] / ow[:, DH:]
    gate = gate_ref[...]
    for r in range(R):
        y_ref[r * tq:(r + 1) * tq, :] = (gate[:, r:r + 1] * head(o_cmp, r) +
                                         gate[:, 2 * R + r:2 * R + r + 1] * head(o_win, r))

    hi = psum.astype(BF16)
    rem = psum - hi.astype(F32)
    mid = rem.astype(BF16)
    lo = (rem - mid.astype(F32)).astype(BF16)
    ov = ov_ref[...]
    p_slc = (_dot(hi, ov) + _dot(mid, ov) + _dot(lo, ov)).T[:n_s]
    jj = lax.broadcasted_iota(jnp.int32, (n_s, 1), 0)
    blk_t = tlane // SEL_LEN
    forced = (jj == 0) | (jj == blk_t) | (jj == blk_t - 1)
    score = jnp.where(jj <= blk_t, p_slc + jnp.where(forced, FORCE_BONUS, 0.0), -1.0)
    nch = n_s // 8
    chunks = [score[c * 8:(c + 1) * 8] for c in range(nch)]
    cnt = [jnp.zeros((8, tq), F32) for _ in range(nch)]
    sub = lax.broadcasted_iota(jnp.int32, (8, 1), 0)
    for i in range(n_s):
        row_i = score[i:i + 1]
        for c in range(nch):
            if c * 8 > i:
                beats = row_i >= chunks[c]
            elif c * 8 + 7 < i:
                beats = row_i > chunks[c]
            else:
                beats = (row_i > chunks[c]) | ((row_i == chunks[c]) & (sub + c * 8 > i))
            cnt[c] = cnt[c] + jnp.where(beats, 1.0, 0.0)
    k_sel = float(min(SEL_TOPK, n_s))
    bias_t = jnp.concatenate([jnp.where(cc < k_sel, 0.0, NEG_INF) for cc in cnt] +
                             [jnp.zeros((DH - n_s, tq), F32)], axis=0)
    bias_q = bias_t.T.astype(BF16)
    for r in range(R):
        qa_ref[r * tq:(r + 1) * tq, DH:] = bias_q

    m_ref[...] = jnp.full(m_ref.shape, NEG_INF, F32)
    acc_ref[...] = jnp.zeros(acc_ref.shape, F32)

    def scores(kt, s_ref):
        s_ref[...] = _dot(qa_ref[...], kat_ref[kt])

    def tile(kt, bufs, causal):
        s_ref, pt_ref, al_ref = bufs
        if causal:
            kpos = kt * tk + lax.broadcasted_iota(jnp.int32, (1, tk), 1)
            bias_d = jnp.where(kpos <= trow, 0.0, NEG_INF)
        for r in range(R):
            rows = slice(r * tq, (r + 1) * tq)
            s_r = s_ref[rows, :]
            if causal:
                s_r = s_r + bias_d
            m_old = m_ref[r]
            m_new = jnp.maximum(m_old, jnp.max(s_r, axis=-1, keepdims=True))
            for c in range(tk // DH):
                cols = slice(c * DH, (c + 1) * DH)
                pt_ref[rows, cols] = jnp.exp2(s_r[:, cols] - m_new).astype(BF16)
            al_ref[r] = jnp.exp2(m_old - m_new)
            m_ref[r] = m_new
        k0 = pl.multiple_of(kt * tk, tk)
        pv = _dot(pt_ref[...], va_ref[pl.ds(k0, tk), :]).reshape(R, tq, 2 * DH)
        al = al_ref[...]
        acc_ref[:, :, :DH] = al * acc_ref[:, :, :DH] + pv[:, :, :DH]
        acc_ref[:, :, DH:] = al * acc_ref[:, :, DH:] + pv[:, :, DH:]

    even = (s0_ref, p0_ref, al0_ref)
    odd = (s1_ref, p1_ref, al1_ref)
    kd = t0 // tk
    scores(0, s0_ref)

    def pair(i, carry):
        scores(2 * i + 1, s1_ref)
        tile(2 * i, even, False)
        scores(2 * i + 2, s0_ref)
        tile(2 * i + 1, odd, False)
        return carry

    lax.fori_loop(0, kd // 2, pair, 0)

    @pl.when(kd % 2 == 0)
    def _():
        tile(kd, even, True)

    @pl.when(kd % 2 == 1)
    def _():
        scores(kd, s1_ref)
        tile(kd - 1, even, False)
        tile(kd, odd, True)

    acc = acc_ref[...]
    o_slc = (acc[:, :, :DH] / acc[:, :, DH:]).reshape(R * tq, DH)

    gate = gate_ref[...]
    for r in range(R):
        y = y_ref[r * tq:(r + 1) * tq, :] + gate[:, R + r:R + r + 1] * head(o_slc, r)
        out_ref[:, r * DH:(r + 1) * DH] = y.astype(BF16)


def _nsa(o_r, o_g, kcmp, vcmp, b, s, tq, tk):
    ncp = s // CMP_STRIDE
    n_s = s // SEL_LEN
    assert n_s % 8 == 0 and n_s <= DH
    r_view = o_r.reshape(b, s, N_R)
    g_view = o_g.reshape(b, s, N_G)
    c = np.arange(ncp)[:, None]
    j = np.arange(DH)[None, :]
    n_c = (s - CMP_LEN) // CMP_STRIDE + 1
    ov = ((c * CMP_STRIDE <= j * SEL_LEN + SEL_LEN - 1) &
          (c * CMP_STRIDE + CMP_LEN - 1 >= j * SEL_LEN) & (c < n_c) & (j < n_s))
    ov = jnp.asarray(ov.astype(np.float32), BF16)
    et = jnp.asarray((np.arange(s)[None, :] // SEL_LEN == np.arange(DH)[:, None])
                     .astype(np.float32), BF16)
    wq = NSA_R * DH
    ks_col = W_QB // DH
    kw_col = ks_col + NSA_G
    vs_col = kw_col + NSA_G
    vw_col = vs_col + NSA_G
    gate_col = (2 * D_MODEL) // DH
    wlen = (-(-(WIN_LEN - 1) // BLOCK)) * BLOCK + tq
    kv = (None, s, DH)
    out = pl.pallas_call(
        functools.partial(_nsa_kernel, tq=tq, tk=tk, s_len=s, ncp=ncp, n_s=n_s),
        grid=(b, NSA_G, s // tq),
        in_specs=[pl.BlockSpec((None, tq, wq), lambda bi, g, qi: (bi, qi, g)),
                  pl.BlockSpec(kv, lambda bi, g, qi: (bi, 0, ks_col + g)),
                  pl.BlockSpec(kv, lambda bi, g, qi: (bi, 0, vs_col + g)),
                  pl.BlockSpec(kv, lambda bi, g, qi: (bi, 0, kw_col + g)),
                  pl.BlockSpec(kv, lambda bi, g, qi: (bi, 0, vw_col + g)),
                  pl.BlockSpec((None, None, ncp, DH), lambda bi, g, qi: (bi, g, 0, 0)),
                  pl.BlockSpec((None, None, ncp, DH), lambda bi, g, qi: (bi, g, 0, 0)),
                  pl.BlockSpec((None, tq, DH), lambda bi, g, qi: (bi, qi, gate_col + g)),
                  pl.BlockSpec((ncp, DH), lambda bi, g, qi: (0, 0)),
                  pl.BlockSpec((DH, s), lambda bi, g, qi: (0, 0))],
        out_specs=pl.BlockSpec((None, tq, wq), lambda bi, g, qi: (bi, qi, g)),
        out_shape=jax.ShapeDtypeStruct((b, s, W_QB), BF16),
        scratch_shapes=[pltpu.VMEM((NSA_R * tq, 2 * DH), BF16),
                        pltpu.VMEM((s // tk, 2 * DH, tk), BF16),
                        pltpu.VMEM((s // BLOCK, DH, BLOCK), BF16),
                        pltpu.VMEM((DH, ncp), BF16),
                        pltpu.VMEM((s, 2 * DH), BF16),
                        pltpu.VMEM((s, 2 * DH), BF16),
                        pltpu.VMEM((NSA_R * tq, ncp), BF16),
                        pltpu.VMEM((NSA_R * tq, wlen), BF16),
                        pltpu.VMEM((NSA_R * tq, DH), F32),
                        pltpu.VMEM((NSA_R * tq, tk), F32),
                        pltpu.VMEM((NSA_R * tq, tk), F32),
                        pltpu.VMEM((NSA_R * tq, tk), BF16),
                        pltpu.VMEM((NSA_R * tq, tk), BF16),
                        pltpu.VMEM((NSA_R, tq, DH), F32),
                        pltpu.VMEM((NSA_R, tq, DH), F32),
                        pltpu.VMEM((NSA_R, tq, DH), F32),
                        pltpu.VMEM((NSA_R, tq, 2 * DH), F32)],
        compiler_params=_cparams(("arbitrary", "arbitrary", "arbitrary")),
        name="nsa_attention",
    )(r_view, r_view, r_view, r_view, r_view, kcmp, vcmp, g_view, ov, et)
    return out.reshape(b * s, W_QB)


def _out_kernel(ya_ref, yb_ref, ga_ref, gb_ref, x_ref, woa_ref, wob_ref, wo_ref, out_ref):
    pa = _dot(ya_ref[...], woa_ref[...])
    pb = _dot(yb_ref[...], wob_ref[...])
    y = (ga_ref[...] * pa + gb_ref[...] * pb).astype(BF16)
    out_ref[...] = x_ref[...] + _dot(y, wo_ref[...])


def _merge_out(ya, yb, o_g, x2, woa, wob, wo, layer, tm):
    m = x2.shape[0]
    row = lambda w: pl.BlockSpec((tm, w), lambda i: (i, 0))
    const = lambda shape: pl.BlockSpec((None,) + shape, lambda i: (layer, 0, 0),
                                       pipeline_mode=pl.Buffered(1))
    return pl.pallas_call(
        _out_kernel,
        grid=(m // tm,),
        in_specs=[row(DIL_OUT), row(W_QB),
                  pl.BlockSpec((tm, D_MODEL), lambda i: (i, 0)),
                  pl.BlockSpec((tm, D_MODEL), lambda i: (i, 1)),
                  row(D_MODEL),
                  const((DIL_OUT, D_MODEL)), const((W_QB, D_MODEL)), const((D_MODEL, D_MODEL))],
        out_specs=row(D_MODEL),
        out_shape=jax.ShapeDtypeStruct((m, D_MODEL), F32),
        compiler_params=_cparams(("arbitrary",)),
        name="merge_out",
    )(ya, yb, o_g, o_g, x2, woa, wob, wo)


def _ffn_kernel(x_ref, g_ref, wg_ref, wu_ref, wd_ref, gf_ref, out_ref, h_ref, *, nf, final):
    f = pl.program_id(1)

    @pl.when(f == 0)
    def _():
        x = x_ref[...]
        h_ref[...] = _rms(x, g_ref[...]).astype(BF16)
        out_ref[...] = x

    h = h_ref[...]
    a = _dot(h, wg_ref[...])
    u = _dot(h, wu_ref[...])
    act = (a * jax.nn.sigmoid(a) * u).astype(BF16)
    out_ref[...] += _dot(act, wd_ref[...])

    if final:
        @pl.when(f == nf - 1)
        def _():
            out_ref[...] = _rms(out_ref[...], gf_ref[...])


def _ffn(x2, g, wg, wu, wd, layer, gf, tm, tf, final):
    m = x2.shape[0]
    nf = D_FF // tf
    return pl.pallas_call(
        functools.partial(_ffn_kernel, nf=nf, final=final),
        grid=(m // tm, nf),
        in_specs=[pl.BlockSpec((tm, D_MODEL), lambda i, f: (i, 0)),
                  pl.BlockSpec((1, D_MODEL), lambda i, f: (0, 0)),
                  pl.BlockSpec((None, D_MODEL, tf), lambda i, f: (layer, 0, f)),
                  pl.BlockSpec((None, D_MODEL, tf), lambda i, f: (layer, 0, f)),
                  pl.BlockSpec((None, tf, D_MODEL), lambda i, f: (layer, f, 0)),
                  pl.BlockSpec((1, D_MODEL), lambda i, f: (0, 0))],
        out_specs=pl.BlockSpec((tm, D_MODEL), lambda i, f: (i, 0)),
        out_shape=jax.ShapeDtypeStruct((m, D_MODEL), F32),
        scratch_shapes=[pltpu.VMEM((tm, D_MODEL), BF16)],
        compiler_params=_cparams(("arbitrary", "arbitrary")),
        name="ffn",
    )(x2, g, wg, wu, wd, gf)


def _prep_w_in(w):
    depth = w.shape[0]
    cols = (W_QA, W_QA, W_QA, W_QB, W_KV, W_KV, W_KV, W_KV, W_KV, W_KV,
            NSA_Q_HEADS * 3, D_MODEL, D_MODEL)
    offs = np.concatenate([[0], np.cumsum(cols)])
    seg = [w[:, :, int(offs[i]):int(offs[i + 1])] for i in range(len(cols))]
    qa, ka, va, qb, kc, vc, ks, vs, kw, vw, gate_b, gam, gbm = seg
    gate_b = gate_b.reshape(depth, D_MODEL, NSA_G, NSA_R, 3).transpose(0, 1, 2, 4, 3)
    gate_b = gate_b.reshape(depth, D_MODEL, NSA_G, 3 * NSA_R)
    gate_b = jnp.pad(gate_b, ((0, 0), (0, 0), (0, 0), (0, DH - 3 * NSA_R)))
    gate_b = gate_b.reshape(depth, D_MODEL, NSA_G * DH)
    gate_b = jnp.pad(gate_b, ((0, 0), (0, 0), (0, N_G - 2 * D_MODEL - NSA_G * DH)))
    return jnp.concatenate([qa, ka, va, kc, vc, qb, ks, kw, vs, vw, gam, gbm, gate_b],
                           axis=2).astype(BF16)


def kernel(x, positions, ln_mix, w_in, cmp_pos_k, cmp_pos_v, cmp_w1_k, cmp_w2_k, cmp_w1_v, cmp_w2_v,
           w_out_a, w_out_b, w_out, ln_ffn, w_ffn_gate, w_ffn_up, w_ffn_down, ln_final):
    b, s, d = x.shape
    depth = w_in.shape[0]
    assert d == D_MODEL and s % DIL_UNIT == 0
    m = b * s
    tm_in = min(2048, m)
    tm_out = min(256, m)
    tm_ffn = min(1024, m)
    tq, tk = 128, min(512, s)
    ncp = s // CMP_STRIDE

    pos_f = positions.astype(F32)
    cos, sin = _rope_tables(pos_f.reshape(m, 1), tm_in)
    blk_end = np.minimum(np.arange(ncp) * CMP_STRIDE + CMP_LEN - 1, s - 1)
    cosc, sinc = _rope_tables(pos_f[:, blk_end].reshape(b * ncp, 1), ncp)
    cosc = cosc.reshape(b, ncp, DH)
    sinc = sinc.reshape(b, ncp, DH)

    cs = np.ones((1, N_ALL), np.float32)
    cs[:, :W_QA] = Q_SCALE
    cs[:, N_A:N_A + W_QB] = Q_SCALE
    cs = jnp.asarray(cs)

    w_in_b = _prep_w_in(w_in)
    woa_b, wob_b, wo_b = w_out_a.astype(BF16), w_out_b.astype(BF16), w_out.astype(BF16)
    wg_b, wu_b, wd_b = w_ffn_gate.astype(BF16), w_ffn_up.astype(BF16), w_ffn_down.astype(BF16)

    x2 = x.reshape(m, d)
    for l in range(depth):
        h = _norm(x2, ln_mix[l][None, :], min(1024, m))
        o_a, o_r, o_g = _inproj(h, w_in_b, l, cs, cos, sin, tm_in)
        ya = _dilated(o_a, b, s)
        kcmp, vcmp = _compress(
            o_a, b, s, cmp_pos_k[l], cmp_pos_v[l],
            cmp_w1_k[l].reshape(CMP_LEN, DH, CMP_HIDDEN).astype(BF16),
            cmp_w1_v[l].reshape(CMP_LEN, DH, CMP_HIDDEN).astype(BF16),
            cmp_w2_k[l].astype(BF16), cmp_w2_v[l].astype(BF16), cosc, sinc)
        yb = _nsa(o_r, o_g, kcmp, vcmp, b, s, tq, tk)
        x2 = _merge_out(ya, yb, o_g, x2, woa_b, wob_b, wo_b, l, tm_out)
        x2 = _ffn(x2, ln_ffn[l][None, :], wg_b, wu_b, wd_b, l, ln_final[None, :], tm_ffn, 512,
                  l == depth - 1)
    return x2.reshape(b, s, d)
```

```python
import functools
import math

import numpy as np
import jax
import jax.numpy as jnp
from jax import lax
from jax.experimental import pallas as pl
from jax.experimental.pallas import tpu as pltpu

F32 = jnp.float32
BF16 = jnp.bfloat16

D_MODEL = 2048
DH = 128
HALF = DH // 2
ROPE_THETA = 10000.0
NORM_EPS = 1e-6
NEG_INF = -1e30
BLOCK = 128

DIL_GROUPS = ((128, 1), (512, 4), (2048, 16))
DIL_HPG = 4
DIL_HEADS = DIL_HPG * len(DIL_GROUPS)
DIL_OUT = DIL_HPG * DH
DIL_UNIT = DIL_GROUPS[-1][1] * BLOCK

NSA_Q_HEADS = 16
NSA_G = 2
NSA_R = NSA_Q_HEADS // NSA_G
CMP_LEN = 32
CMP_STRIDE = 16
CMP_HIDDEN = 256
SEL_LEN = 64
SEL_TOPK = 16
WIN_LEN = 512
FORCE_BONUS = 1e4
D_FF = 5632

W_QA = DIL_HEADS * DH
W_QB = NSA_Q_HEADS * DH
W_KV = NSA_G * DH
TN_IN = 512
IN_SPLIT = 4
N_SRC = 3 * W_QA + W_QB + 6 * W_KV
N_A = 3 * W_QA + 2 * W_KV
N_R = W_QB + 4 * W_KV
N_G = 2 * D_MODEL + TN_IN
COL_KC = 3 * W_QA // DH
VMEM_LIMIT = 56 * 1024 * 1024
Q_SCALE = DH ** -0.5 * math.log2(math.e)


def _cparams(sem):
    return pltpu.CompilerParams(dimension_semantics=sem, vmem_limit_bytes=VMEM_LIMIT)


def _dot(a, b):
    return jnp.dot(a, b, preferred_element_type=F32)


def _rope_tab_kernel(pos_ref, inv_ref, sgn_ref, cos_ref, sin_ref):
    ang = pos_ref[...] * inv_ref[...]
    cos_ref[...] = jnp.cos(ang)
    sin_ref[...] = jnp.sin(ang) * sgn_ref[...]


def _rope_tables(pos_f, tm):
    m = pos_f.shape[0]
    inv = ROPE_THETA ** (-2.0 * jnp.arange(HALF, dtype=F32) / DH)
    inv = jnp.concatenate([inv, inv])[None, :]
    sgn = jnp.concatenate([-jnp.ones((HALF,), F32), jnp.ones((HALF,), F32)])[None, :]
    return pl.pallas_call(
        _rope_tab_kernel,
        grid=(m // tm,),
        in_specs=[pl.BlockSpec((tm, 1), lambda i: (i, 0)),
                  pl.BlockSpec((1, DH), lambda i: (0, 0)),
                  pl.BlockSpec((1, DH), lambda i: (0, 0))],
        out_specs=[pl.BlockSpec((tm, DH), lambda i: (i, 0)),
                   pl.BlockSpec((tm, DH), lambda i: (i, 0))],
        out_shape=[jax.ShapeDtypeStruct((m, DH), F32)] * 2,
        compiler_params=_cparams(("arbitrary",)),
        name="rope_tables",
    )(pos_f, inv, sgn)


def _rope(a, cos, sin_signed):
    return a * cos + pltpu.roll(a, HALF, 1) * sin_signed


def _rms(x, g):
    ms = jnp.mean(x * x, axis=-1, keepdims=True)
    return x * lax.rsqrt(ms + NORM_EPS) * g


T_QK = 2 * W_QA // TN_IN
T_VA = 3 * W_QA // TN_IN
T_QB = T_VA + W_QB // TN_IN
T_KC = T_QB + 2 * W_KV // TN_IN
T_SRC = N_SRC // TN_IN
T_ALL = T_SRC + N_G // TN_IN


def _norm_kernel(x_ref, g_ref, h_ref):
    h_ref[...] = _rms(x_ref[...], g_ref[...]).astype(BF16)


def _norm(x2, g, tm):
    m = x2.shape[0]
    return pl.pallas_call(
        _norm_kernel,
        grid=(m // tm,),
        in_specs=[pl.BlockSpec((tm, D_MODEL), lambda i: (i, 0)),
                  pl.BlockSpec((1, D_MODEL), lambda i: (0, 0))],
        out_specs=pl.BlockSpec((tm, D_MODEL), lambda i: (i, 0)),
        out_shape=jax.ShapeDtypeStruct((m, D_MODEL), BF16),
        compiler_params=_cparams(("arbitrary",)),
        name="rmsnorm",
    )(x2, g)


def _inproj_kernel(h_ref, w_ref, wg_ref, cs_ref, cos_ref, sin_ref, oa_ref, or_ref, og_ref):
    j = pl.program_id(1)

    tm = h_ref.shape[0]
    groups = [slice(k * (tm // IN_SPLIT), (k + 1) * (tm // IN_SPLIT)) for k in range(IN_SPLIT)]
    heads = [slice(hh * DH, (hh + 1) * DH) for hh in range(TN_IN // DH)]

    def region(out_ref, epilogue, weights=w_ref):
        accs = [_dot(h_ref[rows, :], weights[...]) for rows in groups]
        for rows, acc in zip(groups, accs):
            epilogue(out_ref, rows, acc)

    def rope_heads(out_ref, rows, acc, which):
        cos = cos_ref[rows, :]
        sin = sin_ref[rows, :]
        for sl in which:
            r = _rope(acc[:, sl], cos, sin) * cs_ref[:, sl]
            out_ref[rows, sl] = r.astype(out_ref.dtype)

    def roped(out_ref, rows, acc):
        rope_heads(out_ref, rows, acc, heads)

    def plain(out_ref, rows, acc):
        out_ref[rows, :] = acc.astype(out_ref.dtype)

    def key_value(out_ref, rows, acc):
        rope_heads(out_ref, rows, acc, heads[:len(heads) // 2])
        out_ref[rows, TN_IN // 2:] = acc[:, TN_IN // 2:].astype(out_ref.dtype)

    def sigmoid(out_ref, rows, acc):
        out_ref[rows, :] = jax.nn.sigmoid(acc)

    pl.when(j < T_QK)(lambda: region(oa_ref, roped))
    pl.when((j >= T_QK) & (j < T_VA))(lambda: region(oa_ref, plain))
    pl.when((j >= T_VA) & (j < T_QB))(lambda: region(or_ref, roped))
    pl.when((j >= T_QB) & (j < T_KC))(lambda: region(oa_ref, plain))
    pl.when((j >= T_KC) & (j < T_SRC))(lambda: region(or_ref, key_value))
    pl.when(j >= T_SRC)(lambda: region(og_ref, sigmoid, wg_ref))


def _inproj(h, w, wg, layer, cs, cos, sin, tm):
    m = h.shape[0]
    n_kc = T_KC - T_QB

    def a_tile(j):
        return jnp.where(j < T_VA, j, jnp.clip(j - T_QB, -1, n_kc - 1) + T_VA)

    def r_tile(j):
        first = jnp.clip(j - T_VA, 0, T_QB - T_VA - 1)
        return jnp.where(j < T_KC, first, jnp.minimum(j, T_SRC - 1) - T_KC + T_QB - T_VA)

    return pl.pallas_call(
        _inproj_kernel,
        grid=(m // tm, T_ALL),
        in_specs=[pl.BlockSpec((tm, D_MODEL), lambda i, j: (i, 0)),
                  pl.BlockSpec((None, D_MODEL, TN_IN),
                               lambda i, j: (layer, 0, jnp.minimum(j, T_SRC - 1))),
                  pl.BlockSpec((None, D_MODEL, TN_IN),
                               lambda i, j: (layer, 0, jnp.maximum(j - T_SRC, 0))),
                  pl.BlockSpec((1, TN_IN), lambda i, j: (0, jnp.minimum(j, T_SRC - 1))),
                  pl.BlockSpec((tm, DH), lambda i, j: (i, 0)),
                  pl.BlockSpec((tm, DH), lambda i, j: (i, 0))],
        out_specs=[pl.BlockSpec((tm, TN_IN), lambda i, j: (i, a_tile(j))),
                   pl.BlockSpec((tm, TN_IN), lambda i, j: (i, r_tile(j))),
                   pl.BlockSpec((tm, TN_IN), lambda i, j: (i, jnp.maximum(j - T_SRC, 0)))],
        out_shape=[jax.ShapeDtypeStruct((m, N_A), F32),
                   jax.ShapeDtypeStruct((m, N_R), BF16),
                   jax.ShapeDtypeStruct((m, N_G), F32)],
        compiler_params=_cparams(("arbitrary", "arbitrary")),
        name="inproj",
    )(h, w, wg, cs, cos, sin)


def _rows(start, dil):
    return pl.ds(start, BLOCK) if dil == 1 else pl.ds(start, BLOCK, stride=dil)


def _dil_kernel(*refs):
    out_ref, acc_scr, l_scr, m_scr = refs[-4:]
    u = pl.program_id(1)
    row = lax.broadcasted_iota(jnp.int32, (BLOCK, 2 * BLOCK), 0)
    col = lax.broadcasted_iota(jnp.int32, (BLOCK, 2 * BLOCK), 1)
    band = ((col < BLOCK) & (col >= row)) | ((col >= BLOCK) & (col - BLOCK <= row))
    bias = jnp.where(band, 0.0, NEG_INF)
    bias_first = jnp.where(band & ((col >= BLOCK) | (u > 0)), 0.0, NEG_INF)
    ones = jnp.ones((2 * BLOCK, DH), BF16)
    for gi, (_, dil) in enumerate(DIL_GROUPS):
        q_ref, kc_ref, kp_ref, vc_ref, vp_ref = refs[5 * gi:5 * gi + 5]
        span = BLOCK * dil
        for rho in range(dil):
            for ub in range(DIL_UNIT // span):
                cur = _rows(ub * span + rho, dil)
                if ub == 0:
                    kp, vp, b_add = kp_ref[_rows(rho, dil), :], vp_ref[_rows(rho, dil), :], bias_first
                else:
                    prv = _rows((ub - 1) * span + rho, dil)
                    kp, vp, b_add = kc_ref[prv, :], vc_ref[prv, :], bias
                q = q_ref[cur, :].astype(BF16)
                k = jnp.concatenate([kp, kc_ref[cur, :]], axis=0).astype(BF16)
                v = jnp.concatenate([vp, vc_ref[cur, :]], axis=0).astype(BF16)
                s = lax.dot_general(q, k, (((1,), (1,)), ((), ())), preferred_element_type=F32) + b_add
                m = jnp.broadcast_to(jnp.max(s, axis=-1, keepdims=True), (BLOCK, DH))
                p = jnp.concatenate([jnp.exp2(s[:, :BLOCK] - m), jnp.exp2(s[:, BLOCK:] - m)], axis=1)
                pv = _dot(p.astype(BF16), jnp.concatenate([v, ones], axis=1))
                acc_scr[gi, cur, :] = pv[:, :DH]
                l_scr[gi, cur, :] = pv[:, DH:]
                m_scr[gi, cur, :] = m
    m0, m1, m2 = m_scr[0], m_scr[1], m_scr[2]
    m = jnp.maximum(jnp.maximum(m0, m1), m2)
    e0, e1, e2 = jnp.exp2(m0 - m), jnp.exp2(m1 - m), jnp.exp2(m2 - m)
    num = e0 * acc_scr[0] + e1 * acc_scr[1] + e2 * acc_scr[2]
    den = e0 * l_scr[0] + e1 * l_scr[1] + e2 * l_scr[2]
    out_ref[...] = (num / den).astype(BF16)


def _dilated(o_a, b, s):
    a_view = o_a.reshape(b, s, N_A)
    in_specs, args = [], []
    for gi, (_, dil) in enumerate(DIL_GROUPS):
        span = BLOCK * dil
        per = DIL_UNIT // span

        def cur(colbase, gi=gi):
            return pl.BlockSpec((None, DIL_UNIT, DH),
                                lambda bi, u, j: (bi, u, colbase + gi * DIL_HPG + j))

        def prev(colbase, gi=gi, span=span, per=per):
            return pl.BlockSpec((None, span, DH),
                                lambda bi, u, j: (bi, jnp.maximum(u * per - 1, 0),
                                                  colbase + gi * DIL_HPG + j))

        kcol, vcol = W_QA // DH, 2 * W_QA // DH
        in_specs += [cur(0), cur(kcol), prev(kcol), cur(vcol), prev(vcol)]
        args += [a_view] * 5
    out = pl.pallas_call(
        _dil_kernel,
        grid=(b, s // DIL_UNIT, DIL_HPG),
        in_specs=in_specs,
        out_specs=pl.BlockSpec((None, DIL_UNIT, DH), lambda bi, u, j: (bi, u, j)),
        out_shape=jax.ShapeDtypeStruct((b, s, DIL_OUT), BF16),
        scratch_shapes=[pltpu.VMEM((len(DIL_GROUPS), DIL_UNIT, DH), F32)] * 3,
        compiler_params=_cparams(("arbitrary", "arbitrary", "arbitrary")),
        name="dilated",
    )(*args)
    return out.reshape(b * s, DIL_OUT)


def _cmp_kernel(xk0_ref, xk1_ref, xv0_ref, xv1_ref, posk_ref, posv_ref, w1k_ref, w1v_ref,
                w2k_ref, w2v_ref, cos_ref, sin_ref, kc_ref, vc_ref, *, ncp):
    half = CMP_LEN // 2
    x_refs = ((xk0_ref, xk1_ref), (xv0_ref, xv1_ref))
    for kind in range(2):
        pos_ref, w1_ref, w2_ref, o_ref = ((posk_ref, w1k_ref, w2k_ref, kc_ref) if kind == 0
                                          else (posv_ref, w1v_ref, w2v_ref, vc_ref))
        for g in range(NSA_G):
            x_ref = x_refs[kind][g]
            a = jnp.zeros((ncp, CMP_HIDDEN), F32)
            bm = jnp.zeros((ncp, CMP_HIDDEN), F32)
            for l in range(half):
                x = x_ref[pl.ds(l, ncp, stride=CMP_STRIDE), :]
                a = a + _dot((x + pos_ref[l:l + 1, :]).astype(BF16), w1_ref[l])
                bm = bm + _dot((x + pos_ref[half + l:half + l + 1, :]).astype(BF16),
                               w1_ref[half + l])
            hid = a + pltpu.roll(bm, ncp - 1, 0)
            hid = jax.nn.gelu(hid, approximate=True).astype(BF16)
            out = _dot(hid, w2_ref[...])
            if kind == 0:
                out = _rope(out, cos_ref[...], sin_ref[...])
            o_ref[g] = out.astype(BF16)


def _compress(o_a, b, s, posk, posv, w1k, w1v, w2k, w2v, cosc, sinc):
    ncp = s // CMP_STRIDE
    x_view = o_a.reshape(b, s, N_A)
    full = lambda shape: pl.BlockSpec(shape, lambda bi: (0,) * len(shape))
    return pl.pallas_call(
        functools.partial(_cmp_kernel, ncp=ncp),
        grid=(b,),
        in_specs=[pl.BlockSpec((None, s, DH), lambda bi, c=c: (bi, 0, COL_KC + c))
                  for c in range(2 * NSA_G)] +
                 [full((CMP_LEN, DH)), full((CMP_LEN, DH)),
                  full((CMP_LEN, DH, CMP_HIDDEN)), full((CMP_LEN, DH, CMP_HIDDEN)),
                  full((CMP_HIDDEN, DH)), full((CMP_HIDDEN, DH)),
                  pl.BlockSpec((None, ncp, DH), lambda bi: (bi, 0, 0)),
                  pl.BlockSpec((None, ncp, DH), lambda bi: (bi, 0, 0))],
        out_specs=[pl.BlockSpec((None, NSA_G, ncp, DH), lambda bi: (bi, 0, 0, 0)),
                   pl.BlockSpec((None, NSA_G, ncp, DH), lambda bi: (bi, 0, 0, 0))],
        out_shape=[jax.ShapeDtypeStruct((b, NSA_G, ncp, DH), BF16)] * 2,
        compiler_params=_cparams(("arbitrary",)),
        name="nsa_compress",
    )(x_view, x_view, x_view, x_view, posk, posv, w1k, w1v, w2k, w2v, cosc, sinc)


def _nsa_kernel(q_ref, ks_ref, vs_ref, kw_ref, vw_ref, kc_ref, vc_ref, gate_ref, ov_ref, et_ref,
                out_ref, qa_ref, kat_ref, kwt_ref, kct_ref, va_ref, wa_ref, p_ref, pw_ref, y_ref,
                s0_ref, s1_ref, p0_ref, p1_ref, m_ref, al0_ref, al1_ref, acc_ref,
                *, tq, tk, s_len, ncp, n_s):
    R = NSA_R
    qi = pl.program_id(2)
    t0 = qi * tq
    transposed = lambda a: a.astype(F32).T.astype(BF16)

    @pl.when(qi == 0)
    def _():
        ones = jnp.ones((s_len, DH), BF16)
        for c in range(s_len // tk):
            kat_ref[c, :DH, :] = transposed(ks_ref[c * tk:(c + 1) * tk, :])
            kat_ref[c, DH:, :] = et_ref[:, c * tk:(c + 1) * tk]
        for c in range(s_len // BLOCK):
            kwt_ref[c] = transposed(kw_ref[c * BLOCK:(c + 1) * BLOCK, :])
        kct_ref[...] = transposed(kc_ref[...])
        va_ref[:, :DH] = vs_ref[...]
        va_ref[:, DH:] = ones
        wa_ref[:, :DH] = vw_ref[...]
        wa_ref[:, DH:] = ones

    for r in range(R):
        qa_ref[r * tq:(r + 1) * tq, :DH] = q_ref[:, r * DH:(r + 1) * DH]
    q = qa_ref[:, :DH]
    trow = t0 + lax.broadcasted_iota(jnp.int32, (tq, 1), 0)
    tlane = t0 + lax.broadcasted_iota(jnp.int32, (1, tq), 1)
    head = lambda a, r: a[r * tq:(r + 1) * tq]

    cend = lax.broadcasted_iota(jnp.int32, (1, ncp), 1) * CMP_STRIDE + (CMP_LEN - 1)
    valid_c = cend <= trow
    bias_c = jnp.where(valid_c, 0.0, NEG_INF)
    keep_c = jnp.where(valid_c, 1.0, 0.0)
    s = _dot(q, kct_ref[...])
    chunks_c = [slice(c * DH, (c + 1) * DH) for c in range(ncp // DH)]
    psum = [jnp.zeros((tq, DH), F32) for _ in chunks_c]
    for r in range(R):
        s_r = head(s, r) + bias_c
        m_c = jnp.broadcast_to(jnp.max(s_r, axis=-1, keepdims=True), (tq, DH))
        e = [jnp.exp2(s_r[:, cols] - m_c) * keep_c[:, cols] for cols in chunks_c]
        l_c = jnp.maximum(jnp.sum(sum(e), axis=-1, keepdims=True), 1e-30)
        l_c = jnp.broadcast_to(l_c, (tq, DH))
        for ci, cols in enumerate(chunks_c):
            p = e[ci] / l_c
            psum[ci] = psum[ci] + p
            p_ref[r * tq:(r + 1) * tq, cols] = p.astype(BF16)
    psum = jnp.concatenate(psum, axis=1)
    o_cmp = _dot(p_ref[:, :ncp], vc_ref[...])

    wlen = (-(-(WIN_LEN - 1) // BLOCK)) * BLOCK + tq
    w0 = pl.multiple_of(jnp.maximum(t0 + tq - wlen, 0), BLOCK)
    kpos = w0 + lax.broadcasted_iota(jnp.int32, (1, wlen), 1)
    bias_w = jnp.where((kpos <= trow) & (trow - kpos <= WIN_LEN - 1), 0.0, NEG_INF)
    wb = w0 // BLOCK
    kwt = jnp.concatenate([kwt_ref[wb + jb] for jb in range(wlen // BLOCK)], axis=1)
    s = _dot(q, kwt)
    for r in range(R):
        s_r = head(s, r) + bias_w
        m_w = jnp.broadcast_to(jnp.max(s_r, axis=-1, keepdims=True), (tq, DH))
        for c in range(wlen // DH):
            cols = slice(c * DH, (c + 1) * DH)
            pw_ref[r * tq:(r + 1) * tq, cols] = jnp.exp2(s_r[:, cols] - m_w).astype(BF16)
    ow = _dot(pw_ref[...], wa_ref[pl.ds(w0, wlen), :])
    o_win = ow[:, :DH] / ow[:, DH:]
    gate = gate_ref[...]
    for r in range(R):
        y_ref[r * tq:(r + 1) * tq, :] = (gate[:, r:r + 1] * head(o_cmp, r) +
                                         gate[:, 2 * R + r:2 * R + r + 1] * head(o_win, r))

    hi = psum.astype(BF16)
    rem = psum - hi.astype(F32)
    mid = rem.astype(BF16)
    lo = (rem - mid.astype(F32)).astype(BF16)
    ov = ov_ref[...]
    p_slc = (_dot(hi, ov) + _dot(mid, ov) + _dot(lo, ov)).T[:n_s]
    jj = lax.broadcasted_iota(jnp.int32, (n_s, 1), 0)
    blk_t = tlane // SEL_LEN
    forced = (jj == 0) | (jj == blk_t) | (jj == blk_t - 1)
    score = jnp.where(jj <= blk_t, p_slc + jnp.where(forced, FORCE_BONUS, 0.0), -1.0)
    nch = n_s // 8
    chunks = [score[c * 8:(c + 1) * 8] for c in range(nch)]
    cnt = [jnp.zeros((8, tq), F32) for _ in range(nch)]
    sub = lax.broadcasted_iota(jnp.int32, (8, 1), 0)
    for i in range(n_s):
        row_i = score[i:i + 1]
        for c in range(nch):
            if c * 8 > i:
                beats = row_i >= chunks[c]
            elif c * 8 + 7 < i:
                beats = row_i > chunks[c]
            else:
                beats = (row_i > chunks[c]) | ((row_i == chunks[c]) & (sub + c * 8 > i))
            cnt[c] = cnt[c] + jnp.where(beats, 1.0, 0.0)
    k_sel = float(min(SEL_TOPK, n_s))
    bias_t = jnp.concatenate([jnp.where(cc < k_sel, 0.0, NEG_INF) for cc in cnt] +
                             [jnp.zeros((DH - n_s, tq), F32)], axis=0)
    bias_q = bias_t.T.astype(BF16)
    for r in range(R):
        qa_ref[r * tq:(r + 1) * tq, DH:] = bias_q

    m_ref[...] = jnp.full(m_ref.shape, NEG_INF, F32)
    acc_ref[...] = jnp.zeros(acc_ref.shape, F32)

    def scores(kt, s_ref):
        s_ref[...] = _dot(qa_ref[...], kat_ref[kt])

    def tile(kt, bufs, causal):
        s_ref, pt_ref, al_ref = bufs
        if causal:
            kpos = kt * tk + lax.broadcasted_iota(jnp.int32, (1, tk), 1)
            bias_d = jnp.where(kpos <= trow, 0.0, NEG_INF)
        for r in range(R):
            rows = slice(r * tq, (r + 1) * tq)
            s_r = s_ref[rows, :]
            if causal:
                s_r = s_r + bias_d
            m_old = m_ref[r]
            m_new = jnp.maximum(m_old, jnp.max(s_r, axis=-1, keepdims=True))
            for c in range(tk // DH):
                cols = slice(c * DH, (c + 1) * DH)
                pt_ref[rows, cols] = jnp.exp2(s_r[:, cols] - m_new).astype(BF16)
            al_ref[r] = jnp.exp2(m_old - m_new)
            m_ref[r] = m_new
        k0 = pl.multiple_of(kt * tk, tk)
        pv = _dot(pt_ref[...], va_ref[pl.ds(k0, tk), :]).reshape(R, tq, 2 * DH)
        al = al_ref[...]
        acc_ref[:, :, :DH] = al * acc_ref[:, :, :DH] + pv[:, :, :DH]
        acc_ref[:, :, DH:] = al * acc_ref[:, :, DH:] + pv[:, :, DH:]

    even = (s0_ref, p0_ref, al0_ref)
    odd = (s1_ref, p1_ref, al1_ref)
    kd = t0 // tk
    scores(0, s0_ref)

    def pair(i, carry):
        scores(2 * i + 1, s1_ref)
        tile(2 * i, even, False)
        scores(2 * i + 2, s0_ref)
        tile(2 * i + 1, odd, False)
        return carry

    lax.fori_loop(0, kd // 2, pair, 0)

    @pl.when(kd % 2 == 0)
    def _():
        tile(kd, even, True)

    @pl.when(kd % 2 == 1)
    def _():
        scores(kd, s1_ref)
        tile(kd - 1, even, False)
        tile(kd, odd, True)

    acc = acc_ref[...]
    o_slc = (acc[:, :, :DH] / acc[:, :, DH:]).reshape(R * tq, DH)

    gate = gate_ref[...]
    for r in range(R):
        y = y_ref[r * tq:(r + 1) * tq, :] + gate[:, R + r:R + r + 1] * head(o_slc, r)
        out_ref[:, r * DH:(r + 1) * DH] = y.astype(BF16)


def _nsa(o_r, o_g, kcmp, vcmp, b, s, tq, tk):
    ncp = s // CMP_STRIDE
    n_s = s // SEL_LEN
    assert n_s % 8 == 0 and n_s <= DH
    r_view = o_r.reshape(b, s, N_R)
    g_view = o_g.reshape(b, s, N_G)
    c = np.arange(ncp)[:, None]
    j = np.arange(DH)[None, :]
    n_c = (s - CMP_LEN) // CMP_STRIDE + 1
    ov = ((c * CMP_STRIDE <= j * SEL_LEN + SEL_LEN - 1) &
          (c * CMP_STRIDE + CMP_LEN - 1 >= j * SEL_LEN) & (c < n_c) & (j < n_s))
    ov = jnp.asarray(ov.astype(np.float32), BF16)
    et = jnp.asarray((np.arange(s)[None, :] // SEL_LEN == np.arange(DH)[:, None])
                     .astype(np.float32), BF16)
    wq = NSA_R * DH
    ks_col = W_QB // DH
    vs_col = ks_col + NSA_G
    kw_col = vs_col + NSA_G
    vw_col = kw_col + NSA_G
    gate_col = (2 * D_MODEL) // DH
    wlen = (-(-(WIN_LEN - 1) // BLOCK)) * BLOCK + tq
    kv = (None, s, DH)
    out = pl.pallas_call(
        functools.partial(_nsa_kernel, tq=tq, tk=tk, s_len=s, ncp=ncp, n_s=n_s),
        grid=(b, NSA_G, s // tq),
        in_specs=[pl.BlockSpec((None, tq, wq), lambda bi, g, qi: (bi, qi, g)),
                  pl.BlockSpec(kv, lambda bi, g, qi: (bi, 0, ks_col + g)),
                  pl.BlockSpec(kv, lambda bi, g, qi: (bi, 0, vs_col + g)),
                  pl.BlockSpec(kv, lambda bi, g, qi: (bi, 0, kw_col + g)),
                  pl.BlockSpec(kv, lambda bi, g, qi: (bi, 0, vw_col + g)),
                  pl.BlockSpec((None, None, ncp, DH), lambda bi, g, qi: (bi, g, 0, 0)),
                  pl.BlockSpec((None, None, ncp, DH), lambda bi, g, qi: (bi, g, 0, 0)),
                  pl.BlockSpec((None, tq, DH), lambda bi, g, qi: (bi, qi, gate_col + g)),
                  pl.BlockSpec((ncp, DH), lambda bi, g, qi: (0, 0)),
                  pl.BlockSpec((DH, s), lambda bi, g, qi: (0, 0))],
        out_specs=pl.BlockSpec((None, tq, wq), lambda bi, g, qi: (bi, qi, g)),
        out_shape=jax.ShapeDtypeStruct((b, s, W_QB), BF16),
        scratch_shapes=[pltpu.VMEM((NSA_R * tq, 2 * DH), BF16),
                        pltpu.VMEM((s // tk, 2 * DH, tk), BF16),
                        pltpu.VMEM((s // BLOCK, DH, BLOCK), BF16),
                        pltpu.VMEM((DH, ncp), BF16),
                        pltpu.VMEM((s, 2 * DH), BF16),
                        pltpu.VMEM((s, 2 * DH), BF16),
                        pltpu.VMEM((NSA_R * tq, ncp), BF16),
                        pltpu.VMEM((NSA_R * tq, wlen), BF16),
                        pltpu.VMEM((NSA_R * tq, DH), F32),
                        pltpu.VMEM((NSA_R * tq, tk), F32),
                        pltpu.VMEM((NSA_R * tq, tk), F32),
                        pltpu.VMEM((NSA_R * tq, tk), BF16),
                        pltpu.VMEM((NSA_R * tq, tk), BF16),
                        pltpu.VMEM((NSA_R, tq, DH), F32),
                        pltpu.VMEM((NSA_R, tq, DH), F32),
                        pltpu.VMEM((NSA_R, tq, DH), F32),
                        pltpu.VMEM((NSA_R, tq, 2 * DH), F32)],
        compiler_params=_cparams(("arbitrary", "arbitrary", "arbitrary")),
        name="nsa_attention",
    )(r_view, r_view, r_view, r_view, r_view, kcmp, vcmp, g_view, ov, et)
    return out.reshape(b * s, W_QB)


def _out_kernel(ya_ref, yb_ref, ga_ref, gb_ref, x_ref, woa_ref, wob_ref, wo_ref, out_ref):
    pa = _dot(ya_ref[...], woa_ref[...])
    pb = _dot(yb_ref[...], wob_ref[...])
    y = (ga_ref[...] * pa + gb_ref[...] * pb).astype(BF16)
    out_ref[...] = x_ref[...] + _dot(y, wo_ref[...])


def _merge_out(ya, yb, o_g, x2, woa, wob, wo, layer, tm):
    m = x2.shape[0]
    row = lambda w: pl.BlockSpec((tm, w), lambda i: (i, 0))
    const = lambda shape: pl.BlockSpec((None,) + shape, lambda i: (layer, 0, 0),
                                       pipeline_mode=pl.Buffered(1))
    return pl.pallas_call(
        _out_kernel,
        grid=(m // tm,),
        in_specs=[row(DIL_OUT), row(W_QB),
                  pl.BlockSpec((tm, D_MODEL), lambda i: (i, 0)),
                  pl.BlockSpec((tm, D_MODEL), lambda i: (i, 1)),
                  row(D_MODEL),
                  const((DIL_OUT, D_MODEL)), const((W_QB, D_MODEL)), const((D_MODEL, D_MODEL))],
        out_specs=row(D_MODEL),
        out_shape=jax.ShapeDtypeStruct((m, D_MODEL), F32),
        compiler_params=_cparams(("arbitrary",)),
        name="merge_out",
    )(ya, yb, o_g, o_g, x2, woa, wob, wo)


def _ffn_kernel(x_ref, g_ref, wg_ref, wu_ref, wd_ref, gf_ref, out_ref, h_ref, *, nf, final):
    f = pl.program_id(1)

    @pl.when(f == 0)
    def _():
        x = x_ref[...]
        h_ref[...] = _rms(x, g_ref[...]).astype(BF16)
        out_ref[...] = x

    h = h_ref[...]
    a = _dot(h, wg_ref[...])
    u = _dot(h, wu_ref[...])
    act = (a * jax.nn.sigmoid(a) * u).astype(BF16)
    out_ref[...] += _dot(act, wd_ref[...])

    if final:
        @pl.when(f == nf - 1)
        def _():
            out_ref[...] = _rms(out_ref[...], gf_ref[...])


def _ffn(x2, g, wg, wu, wd, layer, gf, tm, tf, final):
    m = x2.shape[0]
    nf = D_FF // tf
    return pl.pallas_call(
        functools.partial(_ffn_kernel, nf=nf, final=final),
        grid=(m // tm, nf),
        in_specs=[pl.BlockSpec((tm, D_MODEL), lambda i, f: (i, 0)),
                  pl.BlockSpec((1, D_MODEL), lambda i, f: (0, 0)),
                  pl.BlockSpec((None, D_MODEL, tf), lambda i, f: (layer, 0, f)),
                  pl.BlockSpec((None, D_MODEL, tf), lambda i, f: (layer, 0, f)),
                  pl.BlockSpec((None, tf, D_MODEL), lambda i, f: (layer, f, 0)),
                  pl.BlockSpec((1, D_MODEL), lambda i, f: (0, 0))],
        out_specs=pl.BlockSpec((tm, D_MODEL), lambda i, f: (i, 0)),
        out_shape=jax.ShapeDtypeStruct((m, D_MODEL), F32),
        scratch_shapes=[pltpu.VMEM((tm, D_MODEL), BF16)],
        compiler_params=_cparams(("arbitrary", "arbitrary")),
        name="ffn",
    )(x2, g, wg, wu, wd, gf)


def _prep_w_gates(w):
    depth = w.shape[0]
    n_gb = NSA_Q_HEADS * 3
    gate_b = w[:, :, N_SRC:N_SRC + n_gb]
    gam = w[:, :, N_SRC + n_gb:N_SRC + n_gb + D_MODEL]
    gbm = w[:, :, N_SRC + n_gb + D_MODEL:]
    gate_b = gate_b.reshape(depth, D_MODEL, NSA_G, NSA_R, 3).transpose(0, 1, 2, 4, 3)
    gate_b = gate_b.reshape(depth, D_MODEL, NSA_G, 3 * NSA_R)
    gate_b = jnp.pad(gate_b, ((0, 0), (0, 0), (0, 0), (0, DH - 3 * NSA_R)))
    gate_b = gate_b.reshape(depth, D_MODEL, NSA_G * DH)
    gate_b = jnp.pad(gate_b, ((0, 0), (0, 0), (0, N_G - 2 * D_MODEL - NSA_G * DH)))
    return jnp.concatenate([gam, gbm, gate_b], axis=2).astype(BF16)


def kernel(x, positions, ln_mix, w_in, cmp_pos_k, cmp_pos_v, cmp_w1_k, cmp_w2_k, cmp_w1_v, cmp_w2_v,
           w_out_a, w_out_b, w_out, ln_ffn, w_ffn_gate, w_ffn_up, w_ffn_down, ln_final):
    b, s, d = x.shape
    depth = w_in.shape[0]
    assert d == D_MODEL and s % DIL_UNIT == 0
    m = b * s
    tm_in = min(2048, m)
    tm_out = min(256, m)
    tm_ffn = min(1024, m)
    tq, tk = 128, min(512, s)
    ncp = s // CMP_STRIDE

    pos_f = positions.astype(F32)
    cos, sin = _rope_tables(pos_f.reshape(m, 1), tm_in)
    blk_end = np.minimum(np.arange(ncp) * CMP_STRIDE + CMP_LEN - 1, s - 1)
    cosc, sinc = _rope_tables(pos_f[:, blk_end].reshape(b * ncp, 1), ncp)
    cosc = cosc.reshape(b, ncp, DH)
    sinc = sinc.reshape(b, ncp, DH)

    cs = np.ones((1, N_SRC), np.float32)
    cs[:, :W_QA] = Q_SCALE
    cs[:, 3 * W_QA:3 * W_QA + W_QB] = Q_SCALE
    cs = jnp.asarray(cs)

    w_in_b = w_in[:, :, :N_SRC].astype(BF16)
    w_gate_b = _prep_w_gates(w_in)
    woa_b, wob_b, wo_b = w_out_a.astype(BF16), w_out_b.astype(BF16), w_out.astype(BF16)
    wg_b, wu_b, wd_b = w_ffn_gate.astype(BF16), w_ffn_up.astype(BF16), w_ffn_down.astype(BF16)

    x2 = x.reshape(m, d)
    for l in range(depth):
        h = _norm(x2, ln_mix[l][None, :], min(1024, m))
        o_a, o_r, o_g = _inproj(h, w_in_b, w_gate_b, l, cs, cos, sin, tm_in)
        ya = _dilated(o_a, b, s)
        kcmp, vcmp = _compress(
            o_a, b, s, cmp_pos_k[l], cmp_pos_v[l],
            cmp_w1_k[l].reshape(CMP_LEN, DH, CMP_HIDDEN).astype(BF16),
            cmp_w1_v[l].reshape(CMP_LEN, DH, CMP_HIDDEN).astype(BF16),
            cmp_w2_k[l].astype(BF16), cmp_w2_v[l].astype(BF16), cosc, sinc)
        yb = _nsa(o_r, o_g, kcmp, vcmp, b, s, tq, tk)
        x2 = _merge_out(ya, yb, o_g, x2, woa_b, wob_b, wo_b, l, tm_out)
        x2 = _ffn(x2, ln_ffn[l][None, :], wg_b, wu_b, wd_b, l, ln_final[None, :], tm_ffn, 512,
                  l == depth - 1)
    return x2.reshape(b, s, d)
```

```python
import functools
import math

import numpy as np
import jax
import jax.numpy as jnp
from jax import lax
from jax.experimental import pallas as pl
from jax.experimental.pallas import tpu as pltpu

F32 = jnp.float32
BF16 = jnp.bfloat16

D_MODEL = 2048
DH = 128
HALF = DH // 2
ROPE_THETA = 10000.0
NORM_EPS = 1e-6
NEG_INF = -1e30
BLOCK = 128

DIL_GROUPS = ((128, 1), (512, 4), (2048, 16))
DIL_HPG = 4
DIL_HEADS = DIL_HPG * len(DIL_GROUPS)
DIL_OUT = DIL_HPG * DH
DIL_UNIT = DIL_GROUPS[-1][1] * BLOCK

NSA_Q_HEADS = 16
NSA_G = 2
NSA_R = NSA_Q_HEADS // NSA_G
CMP_LEN = 32
CMP_STRIDE = 16
CMP_HIDDEN = 256
SEL_LEN = 64
SEL_TOPK = 16
WIN_LEN = 512
FORCE_BONUS = 1e4
D_FF = 5632

W_QA = DIL_HEADS * DH
W_QB = NSA_Q_HEADS * DH
W_KV = NSA_G * DH
TN_IN = 512
IN_SPLIT = 4
N_SRC = 3 * W_QA + W_QB + 6 * W_KV
N_A = 3 * W_QA + 2 * W_KV
N_R = W_QB + 4 * W_KV
N_G = 2 * D_MODEL + TN_IN
COL_KC = 3 * W_QA // DH
VMEM_LIMIT = 56 * 1024 * 1024
Q_SCALE = DH ** -0.5 * math.log2(math.e)


def _cparams(sem):
    return pltpu.CompilerParams(dimension_semantics=sem, vmem_limit_bytes=VMEM_LIMIT)


def _dot(a, b):
    return jnp.dot(a, b, preferred_element_type=F32)


def _rope_tab_kernel(pos_ref, inv_ref, sgn_ref, cos_ref, sin_ref):
    ang = pos_ref[...] * inv_ref[...]
    cos_ref[...] = jnp.cos(ang)
    sin_ref[...] = jnp.sin(ang) * sgn_ref[...]


def _rope_tables(pos_f, tm):
    m = pos_f.shape[0]
    inv = ROPE_THETA ** (-2.0 * jnp.arange(HALF, dtype=F32) / DH)
    inv = jnp.concatenate([inv, inv])[None, :]
    sgn = jnp.concatenate([-jnp.ones((HALF,), F32), jnp.ones((HALF,), F32)])[None, :]
    return pl.pallas_call(
        _rope_tab_kernel,
        grid=(m // tm,),
        in_specs=[pl.BlockSpec((tm, 1), lambda i: (i, 0)),
                  pl.BlockSpec((1, DH), lambda i: (0, 0)),
                  pl.BlockSpec((1, DH), lambda i: (0, 0))],
        out_specs=[pl.BlockSpec((tm, DH), lambda i: (i, 0)),
                   pl.BlockSpec((tm, DH), lambda i: (i, 0))],
        out_shape=[jax.ShapeDtypeStruct((m, DH), F32)] * 2,
        compiler_params=_cparams(("arbitrary",)),
        name="rope_tables",
    )(pos_f, inv, sgn)


def _rope(a, cos, sin_signed):
    return a * cos + pltpu.roll(a, HALF, 1) * sin_signed


def _rms(x, g):
    ms = jnp.mean(x * x, axis=-1, keepdims=True)
    return x * lax.rsqrt(ms + NORM_EPS) * g


T_QK = 2 * W_QA // TN_IN
T_VA = 3 * W_QA // TN_IN
T_QB = T_VA + W_QB // TN_IN
T_KC = T_QB + 2 * W_KV // TN_IN
T_SRC = N_SRC // TN_IN
T_ALL = T_SRC + N_G // TN_IN


def _norm_kernel(x_ref, g_ref, h_ref):
    h_ref[...] = _rms(x_ref[...], g_ref[...]).astype(BF16)


def _norm(x2, g, tm):
    m = x2.shape[0]
    return pl.pallas_call(
        _norm_kernel,
        grid=(m // tm,),
        in_specs=[pl.BlockSpec((tm, D_MODEL), lambda i: (i, 0)),
                  pl.BlockSpec((1, D_MODEL), lambda i: (0, 0))],
        out_specs=pl.BlockSpec((tm, D_MODEL), lambda i: (i, 0)),
        out_shape=jax.ShapeDtypeStruct((m, D_MODEL), BF16),
        compiler_params=_cparams(("arbitrary",)),
        name="rmsnorm",
    )(x2, g)


def _inproj_kernel(h_ref, w_ref, wg_ref, cs_ref, cos_ref, sin_ref, oa_ref, or_ref, og_ref):
    j = pl.program_id(1)

    tm = h_ref.shape[0]
    groups = [slice(k * (tm // IN_SPLIT), (k + 1) * (tm // IN_SPLIT)) for k in range(IN_SPLIT)]
    heads = [slice(hh * DH, (hh + 1) * DH) for hh in range(TN_IN // DH)]

    def region(out_ref, epilogue, weights=w_ref):
        accs = [_dot(h_ref[rows, :], weights[...]) for rows in groups]
        for rows, acc in zip(groups, accs):
            epilogue(out_ref, rows, acc)

    def rope_heads(out_ref, rows, acc, which):
        cos = cos_ref[rows, :]
        sin = sin_ref[rows, :]
        for sl in which:
            r = _rope(acc[:, sl], cos, sin) * cs_ref[:, sl]
            out_ref[rows, sl] = r.astype(out_ref.dtype)

    def roped(out_ref, rows, acc):
        rope_heads(out_ref, rows, acc, heads)

    def plain(out_ref, rows, acc):
        out_ref[rows, :] = acc.astype(out_ref.dtype)

    def key_value(out_ref, rows, acc):
        rope_heads(out_ref, rows, acc, heads[:len(heads) // 2])
        out_ref[rows, TN_IN // 2:] = acc[:, TN_IN // 2:].astype(out_ref.dtype)

    def sigmoid(out_ref, rows, acc):
        out_ref[rows, :] = jax.nn.sigmoid(acc)

    pl.when(j < T_QK)(lambda: region(oa_ref, roped))
    pl.when((j >= T_QK) & (j < T_VA))(lambda: region(oa_ref, plain))
    pl.when((j >= T_VA) & (j < T_QB))(lambda: region(or_ref, roped))
    pl.when((j >= T_QB) & (j < T_KC))(lambda: region(oa_ref, plain))
    pl.when((j >= T_KC) & (j < T_SRC))(lambda: region(or_ref, key_value))
    pl.when(j >= T_SRC)(lambda: region(og_ref, sigmoid, wg_ref))


def _inproj(h, w, wg, layer, cs, cos, sin, tm):
    m = h.shape[0]
    n_kc = T_KC - T_QB

    def a_tile(j):
        return jnp.where(j < T_VA, j, jnp.clip(j - T_QB, -1, n_kc - 1) + T_VA)

    def r_tile(j):
        first = jnp.clip(j - T_VA, 0, T_QB - T_VA - 1)
        return jnp.where(j < T_KC, first, jnp.minimum(j, T_SRC - 1) - T_KC + T_QB - T_VA)

    return pl.pallas_call(
        _inproj_kernel,
        grid=(m // tm, T_ALL),
        in_specs=[pl.BlockSpec((tm, D_MODEL), lambda i, j: (i, 0)),
                  pl.BlockSpec((None, D_MODEL, TN_IN),
                               lambda i, j: (layer, 0, jnp.minimum(j, T_SRC - 1))),
                  pl.BlockSpec((None, D_MODEL, TN_IN),
                               lambda i, j: (layer, 0, jnp.maximum(j - T_SRC, 0))),
                  pl.BlockSpec((1, TN_IN), lambda i, j: (0, jnp.minimum(j, T_SRC - 1))),
                  pl.BlockSpec((tm, DH), lambda i, j: (i, 0)),
                  pl.BlockSpec((tm, DH), lambda i, j: (i, 0))],
        out_specs=[pl.BlockSpec((tm, TN_IN), lambda i, j: (i, a_tile(j))),
                   pl.BlockSpec((tm, TN_IN), lambda i, j: (i, r_tile(j))),
                   pl.BlockSpec((tm, TN_IN), lambda i, j: (i, jnp.maximum(j - T_SRC, 0)))],
        out_shape=[jax.ShapeDtypeStruct((m, N_A), F32),
                   jax.ShapeDtypeStruct((m, N_R), BF16),
                   jax.ShapeDtypeStruct((m, N_G), F32)],
        compiler_params=_cparams(("arbitrary", "arbitrary")),
        name="inproj",
    )(h, w, wg, cs, cos, sin)


def _rows(start, dil):
    return pl.ds(start, BLOCK) if dil == 1 else pl.ds(start, BLOCK, stride=dil)


def _dil_kernel(*refs):
    out_ref, acc_scr, l_scr, m_scr = refs[-4:]
    u = pl.program_id(1)
    row = lax.broadcasted_iota(jnp.int32, (BLOCK, 2 * BLOCK), 0)
    col = lax.broadcasted_iota(jnp.int32, (BLOCK, 2 * BLOCK), 1)
    band = ((col < BLOCK) & (col >= row)) | ((col >= BLOCK) & (col - BLOCK <= row))
    bias = jnp.where(band, 0.0, NEG_INF)
    bias_first = jnp.where(band & ((col >= BLOCK) | (u > 0)), 0.0, NEG_INF)
    ones = jnp.ones((2 * BLOCK, DH), BF16)
    for gi, (_, dil) in enumerate(DIL_GROUPS):
        q_ref, kc_ref, kp_ref, vc_ref, vp_ref = refs[5 * gi:5 * gi + 5]
        span = BLOCK * dil
        for rho in range(dil):
            for ub in range(DIL_UNIT // span):
                cur = _rows(ub * span + rho, dil)
                if ub == 0:
                    kp, vp, b_add = kp_ref[_rows(rho, dil), :], vp_ref[_rows(rho, dil), :], bias_first
                else:
                    prv = _rows((ub - 1) * span + rho, dil)
                    kp, vp, b_add = kc_ref[prv, :], vc_ref[prv, :], bias
                q = q_ref[cur, :].astype(BF16)
                k = jnp.concatenate([kp, kc_ref[cur, :]], axis=0).astype(BF16)
                v = jnp.concatenate([vp, vc_ref[cur, :]], axis=0).astype(BF16)
                s = lax.dot_general(q, k, (((1,), (1,)), ((), ())), preferred_element_type=F32) + b_add
                m = jnp.broadcast_to(jnp.max(s, axis=-1, keepdims=True), (BLOCK, DH))
                p = jnp.concatenate([jnp.exp2(s[:, :BLOCK] - m), jnp.exp2(s[:, BLOCK:] - m)], axis=1)
                pv = _dot(p.astype(BF16), jnp.concatenate([v, ones], axis=1))
                acc_scr[gi, cur, :] = pv[:, :DH]
                l_scr[gi, cur, :] = pv[:, DH:]
                m_scr[gi, cur, :] = m
    m0, m1, m2 = m_scr[0], m_scr[1], m_scr[2]
    m = jnp.maximum(jnp.maximum(m0, m1), m2)
    e0, e1, e2 = jnp.exp2(m0 - m), jnp.exp2(m1 - m), jnp.exp2(m2 - m)
    num = e0 * acc_scr[0] + e1 * acc_scr[1] + e2 * acc_scr[2]
    den = e0 * l_scr[0] + e1 * l_scr[1] + e2 * l_scr[2]
    out_ref[...] = (num / den).astype(BF16)


def _dilated(o_a, b, s):
    a_view = o_a.reshape(b, s, N_A)
    in_specs, args = [], []
    for gi, (_, dil) in enumerate(DIL_GROUPS):
        span = BLOCK * dil
        per = DIL_UNIT // span

        def cur(colbase, gi=gi):
            return pl.BlockSpec((None, DIL_UNIT, DH),
                                lambda bi, u, j: (bi, u, colbase + gi * DIL_HPG + j))

        def prev(colbase, gi=gi, span=span, per=per):
            return pl.BlockSpec((None, span, DH),
                                lambda bi, u, j: (bi, jnp.maximum(u * per - 1, 0),
                                                  colbase + gi * DIL_HPG + j))

        kcol, vcol = W_QA // DH, 2 * W_QA // DH
        in_specs += [cur(0), cur(kcol), prev(kcol), cur(vcol), prev(vcol)]
        args += [a_view] * 5
    out = pl.pallas_call(
        _dil_kernel,
        grid=(b, s // DIL_UNIT, DIL_HPG),
        in_specs=in_specs,
        out_specs=pl.BlockSpec((None, DIL_UNIT, DH), lambda bi, u, j: (bi, u, j)),
        out_shape=jax.ShapeDtypeStruct((b, s, DIL_OUT), BF16),
        scratch_shapes=[pltpu.VMEM((len(DIL_GROUPS), DIL_UNIT, DH), F32)] * 3,
        compiler_params=_cparams(("arbitrary", "arbitrary", "arbitrary")),
        name="dilated",
    )(*args)
    return out.reshape(b * s, DIL_OUT)


def _cmp_kernel(xk0_ref, xk1_ref, xv0_ref, xv1_ref, posk_ref, posv_ref, w1k_ref, w1v_ref,
                w2k_ref, w2v_ref, cos_ref, sin_ref, kc_ref, vc_ref, *, ncp):
    half = CMP_LEN // 2
    x_refs = ((xk0_ref, xk1_ref), (xv0_ref, xv1_ref))
    for kind in range(2):
        pos_ref, w1_ref, w2_ref, o_ref = ((posk_ref, w1k_ref, w2k_ref, kc_ref) if kind == 0
                                          else (posv_ref, w1v_ref, w2v_ref, vc_ref))
        for g in range(NSA_G):
            x_ref = x_refs[kind][g]
            a = jnp.zeros((ncp, CMP_HIDDEN), F32)
            bm = jnp.zeros((ncp, CMP_HIDDEN), F32)
            for l in range(half):
                x = x_ref[pl.ds(l, ncp, stride=CMP_STRIDE), :]
                a = a + _dot((x + pos_ref[l:l + 1, :]).astype(BF16), w1_ref[l])
                bm = bm + _dot((x + pos_ref[half + l:half + l + 1, :]).astype(BF16),
                               w1_ref[half + l])
            hid = a + pltpu.roll(bm, ncp - 1, 0)
            hid = jax.nn.gelu(hid, approximate=True).astype(BF16)
            out = _dot(hid, w2_ref[...])
            if kind == 0:
                out = _rope(out, cos_ref[...], sin_ref[...])
            o_ref[g] = out.astype(BF16)


def _compress(o_a, b, s, posk, posv, w1k, w1v, w2k, w2v, cosc, sinc):
    ncp = s // CMP_STRIDE
    x_view = o_a.reshape(b, s, N_A)
    full = lambda shape: pl.BlockSpec(shape, lambda bi: (0,) * len(shape))
    return pl.pallas_call(
        functools.partial(_cmp_kernel, ncp=ncp),
        grid=(b,),
        in_specs=[pl.BlockSpec((None, s, DH), lambda bi, c=c: (bi, 0, COL_KC + c))
                  for c in range(2 * NSA_G)] +
                 [full((CMP_LEN, DH)), full((CMP_LEN, DH)),
                  full((CMP_LEN, DH, CMP_HIDDEN)), full((CMP_LEN, DH, CMP_HIDDEN)),
                  full((CMP_HIDDEN, DH)), full((CMP_HIDDEN, DH)),
                  pl.BlockSpec((None, ncp, DH), lambda bi: (bi, 0, 0)),
                  pl.BlockSpec((None, ncp, DH), lambda bi: (bi, 0, 0))],
        out_specs=[pl.BlockSpec((None, NSA_G, ncp, DH), lambda bi: (bi, 0, 0, 0)),
                   pl.BlockSpec((None, NSA_G, ncp, DH), lambda bi: (bi, 0, 0, 0))],
        out_shape=[jax.ShapeDtypeStruct((b, NSA_G, ncp, DH), BF16)] * 2,
        compiler_params=_cparams(("arbitrary",)),
        name="nsa_compress",
    )(x_view, x_view, x_view, x_view, posk, posv, w1k, w1v, w2k, w2v, cosc, sinc)


def _nsa_kernel(q_ref, ks_ref, vs_ref, kw_ref, vw_ref, kc_ref, vc_ref, gate_ref, ov_ref, et_ref,
                out_ref, qa_ref, kat_ref, kwt_ref, kct_ref, va_ref, wa_ref, p_ref, pw_ref, y_ref,
                s0_ref, s1_ref, p0_ref, p1_ref, m_ref, al0_ref, al1_ref, acc_ref,
                *, tq, tk, s_len, ncp, n_s):
    R = NSA_R
    qi = pl.program_id(2)
    t0 = qi * tq
    transposed = lambda a: a.astype(F32).T.astype(BF16)

    @pl.when(qi == 0)
    def _():
        ones = jnp.ones((s_len, DH), BF16)
        for c in range(s_len // tk):
            kat_ref[c, :DH, :] = transposed(ks_ref[c * tk:(c + 1) * tk, :])
            kat_ref[c, DH:, :] = et_ref[:, c * tk:(c + 1) * tk]
        for c in range(s_len // BLOCK):
            kwt_ref[c] = transposed(kw_ref[c * BLOCK:(c + 1) * BLOCK, :])
        kct_ref[...] = transposed(kc_ref[...])
        va_ref[:, :DH] = vs_ref[...]
        va_ref[:, DH:] = ones
        wa_ref[:, :DH] = vw_ref[...]
        wa_ref[:, DH:] = ones

    for r in range(R):
        qa_ref[r * tq:(r + 1) * tq, :DH] = q_ref[:, r * DH:(r + 1) * DH]
    q = qa_ref[:, :DH]
    trow = t0 + lax.broadcasted_iota(jnp.int32, (tq, 1), 0)
    tlane = t0 + lax.broadcasted_iota(jnp.int32, (1, tq), 1)
    head = lambda a, r: a[r * tq:(r + 1) * tq]

    cend = lax.broadcasted_iota(jnp.int32, (1, ncp), 1) * CMP_STRIDE + (CMP_LEN - 1)
    valid_c = cend <= trow
    bias_c = jnp.where(valid_c, 0.0, NEG_INF)
    keep_c = jnp.where(valid_c, 1.0, 0.0)
    s = _dot(q, kct_ref[...])
    chunks_c = [slice(c * DH, (c + 1) * DH) for c in range(ncp // DH)]
    psum = [jnp.zeros((tq, DH), F32) for _ in chunks_c]
    for r in range(R):
        s_r = head(s, r) + bias_c
        m_c = jnp.broadcast_to(jnp.max(s_r, axis=-1, keepdims=True), (tq, DH))
        e = [jnp.exp2(s_r[:, cols] - m_c) * keep_c[:, cols] for cols in chunks_c]
        l_c = jnp.maximum(jnp.sum(sum(e), axis=-1, keepdims=True), 1e-30)
        l_c = jnp.broadcast_to(l_c, (tq, DH))
        for ci, cols in enumerate(chunks_c):
            p = e[ci] / l_c
            psum[ci] = psum[ci] + p
            p_ref[r * tq:(r + 1) * tq, cols] = p.astype(BF16)
    psum = jnp.concatenate(psum, axis=1)
    o_cmp = _dot(p_ref[:, :ncp], vc_ref[...])

    wlen = (-(-(WIN_LEN - 1) // BLOCK)) * BLOCK + tq
    w0 = pl.multiple_of(jnp.maximum(t0 + tq - wlen, 0), BLOCK)
    kpos = w0 + lax.broadcasted_iota(jnp.int32, (1, wlen), 1)
    bias_w = jnp.where((kpos <= trow) & (trow - kpos <= WIN_LEN - 1), 0.0, NEG_INF)
    wb = w0 // BLOCK
    kwt = jnp.concatenate([kwt_ref[wb + jb] for jb in range(wlen // BLOCK)], axis=1)
    s = _dot(q, kwt)
    for r in range(R):
        s_r = head(s, r) + bias_w
        m_w = jnp.broadcast_to(jnp.max(s_r, axis=-1, keepdims=True), (tq, DH))
        for c in range(wlen // DH):
            cols = slice(c * DH, (c + 1) * DH)
            pw_ref[r * tq:(r + 1) * tq, cols] = jnp.exp2(s_r[:, cols] - m_w).astype(BF16)
    ow = _dot(pw_ref[...], wa_ref[pl.ds(w0, wlen), :])
    o_win = ow[:, :DH] / ow[:, DH:]
    gate = gate_ref[...]
    for r in range(R):
        y_ref[r * tq:(r + 1) * tq, :] = (gate[:, r:r + 1] * head(o_cmp, r) +
                                         gate[:, 2 * R + r:2 * R + r + 1] * head(o_win, r))

    hi = psum.astype(BF16)
    rem = psum - hi.astype(F32)
    mid = rem.astype(BF16)
    lo = (rem - mid.astype(F32)).astype(BF16)
    ov = ov_ref[...]
    p_slc = (_dot(hi, ov) + _dot(mid, ov) + _dot(lo, ov)).T[:n_s]
    jj = lax.broadcasted_iota(jnp.int32, (n_s, 1), 0)
    blk_t = tlane // SEL_LEN
    forced = (jj == 0) | (jj == blk_t) | (jj == blk_t - 1)
    score = jnp.where(jj <= blk_t, p_slc + jnp.where(forced, FORCE_BONUS, 0.0), -1.0)
    nch = n_s // 8
    chunks = [score[c * 8:(c + 1) * 8] for c in range(nch)]
    cnt = [jnp.zeros((8, tq), F32) for _ in range(nch)]
    sub = lax.broadcasted_iota(jnp.int32, (8, 1), 0)
    for i in range(n_s):
        row_i = score[i:i + 1]
        for c in range(nch):
            if c * 8 > i:
                beats = row_i >= chunks[c]
            elif c * 8 + 7 < i:
                beats = row_i > chunks[c]
            else:
                beats = (row_i > chunks[c]) | ((row_i == chunks[c]) & (sub + c * 8 > i))
            cnt[c] = cnt[c] + jnp.where(beats, 1.0, 0.0)
    k_sel = float(min(SEL_TOPK, n_s))
    bias_t = jnp.concatenate([jnp.where(cc < k_sel, 0.0, NEG_INF) for cc in cnt] +
                             [jnp.zeros((DH - n_s, tq), F32)], axis=0)
    bias_q = bias_t.T.astype(BF16)
    for r in range(R):
        qa_ref[r * tq:(r + 1) * tq, DH:] = bias_q

    m_ref[...] = jnp.full(m_ref.shape, NEG_INF, F32)
    acc_ref[...] = jnp.zeros(acc_ref.shape, F32)

    def scores(kt, s_ref):
        s_ref[...] = _dot(qa_ref[...], kat_ref[kt])

    def tile(kt, bufs, causal):
        s_ref, pt_ref, al_ref = bufs
        if causal:
            kpos = kt * tk + lax.broadcasted_iota(jnp.int32, (1, tk), 1)
            bias_d = jnp.where(kpos <= trow, 0.0, NEG_INF)
        for r in range(R):
            rows = slice(r * tq, (r + 1) * tq)
            s_r = s_ref[rows, :]
            if causal:
                s_r = s_r + bias_d
            m_old = m_ref[r]
            m_new = jnp.maximum(m_old, jnp.max(s_r, axis=-1, keepdims=True))
            for c in range(tk // DH):
                cols = slice(c * DH, (c + 1) * DH)
                pt_ref[rows, cols] = jnp.exp2(s_r[:, cols] - m_new).astype(BF16)
            al_ref[r] = jnp.exp2(m_old - m_new)
            m_ref[r] = m_new
        k0 = pl.multiple_of(kt * tk, tk)
        pv = _dot(pt_ref[...], va_ref[pl.ds(k0, tk), :]).reshape(R, tq, 2 * DH)
        al = al_ref[...]
        acc_ref[:, :, :DH] = al * acc_ref[:, :, :DH] + pv[:, :, :DH]
        acc_ref[:, :, DH:] = al * acc_ref[:, :, DH:] + pv[:, :, DH:]

    even = (s0_ref, p0_ref, al0_ref)
    odd = (s1_ref, p1_ref, al1_ref)
    kd = t0 // tk
    scores(0, s0_ref)

    def pair(i, carry):
        scores(2 * i + 1, s1_ref)
        tile(2 * i, even, False)
        scores(2 * i + 2, s0_ref)
        tile(2 * i + 1, odd, False)
        return carry

    lax.fori_loop(0, kd // 2, pair, 0)

    @pl.when(kd % 2 == 0)
    def _():
        tile(kd, even, True)

    @pl.when(kd % 2 == 1)
    def _():
        scores(kd, s1_ref)
        tile(kd - 1, even, False)
        tile(kd, odd, True)

    acc = acc_ref[...]
    o_slc = (acc[:, :, :DH] / acc[:, :, DH:]).reshape(R * tq, DH)

    gate = gate_ref[...]
    for r in range(R):
        y = y_ref[r * tq:(r + 1) * tq, :] + gate[:, R + r:R + r + 1] * head(o_slc, r)
        out_ref[:, r * DH:(r + 1) * DH] = y.astype(BF16)


def _nsa(o_r, o_g, kcmp, vcmp, b, s, tq, tk):
    ncp = s // CMP_STRIDE
    n_s = s // SEL_LEN
    assert n_s % 8 == 0 and n_s <= DH
    r_view = o_r.reshape(b, s, N_R)
    g_view = o_g.reshape(b, s, N_G)
    c = np.arange(ncp)[:, None]
    j = np.arange(DH)[None, :]
    n_c = (s - CMP_LEN) // CMP_STRIDE + 1
    ov = ((c * CMP_STRIDE <= j * SEL_LEN + SEL_LEN - 1) &
          (c * CMP_STRIDE + CMP_LEN - 1 >= j * SEL_LEN) & (c < n_c) & (j < n_s))
    ov = jnp.asarray(ov.astype(np.float32), BF16)
    et = jnp.asarray((np.arange(s)[None, :] // SEL_LEN == np.arange(DH)[:, None])
                     .astype(np.float32), BF16)
    wq = NSA_R * DH
    ks_col = W_QB // DH
    vs_col = ks_col + NSA_G
    kw_col = vs_col + NSA_G
    vw_col = kw_col + NSA_G
    gate_col = (2 * D_MODEL) // DH
    wlen = (-(-(WIN_LEN - 1) // BLOCK)) * BLOCK + tq
    kv = (None, s, DH)
    out = pl.pallas_call(
        functools.partial(_nsa_kernel, tq=tq, tk=tk, s_len=s, ncp=ncp, n_s=n_s),
        grid=(b, NSA_G, s // tq),
        in_specs=[pl.BlockSpec((None, tq, wq), lambda bi, g, qi: (bi, qi, g)),
                  pl.BlockSpec(kv, lambda bi, g, qi: (bi, 0, ks_col + g)),
                  pl.BlockSpec(kv, lambda bi, g, qi: (bi, 0, vs_col + g)),
                  pl.BlockSpec(kv, lambda bi, g, qi: (bi, 0, kw_col + g)),
                  pl.BlockSpec(kv, lambda bi, g, qi: (bi, 0, vw_col + g)),
                  pl.BlockSpec((None, None, ncp, DH), lambda bi, g, qi: (bi, g, 0, 0)),
                  pl.BlockSpec((None, None, ncp, DH), lambda bi, g, qi: (bi, g, 0, 0)),
                  pl.BlockSpec((None, tq, DH), lambda bi, g, qi: (bi, qi, gate_col + g)),
                  pl.BlockSpec((ncp, DH), lambda bi, g, qi: (0, 0)),
                  pl.BlockSpec((DH, s), lambda bi, g, qi: (0, 0))],
        out_specs=pl.BlockSpec((None, tq, wq), lambda bi, g, qi: (bi, qi, g)),
        out_shape=jax.ShapeDtypeStruct((b, s, W_QB), BF16),
        scratch_shapes=[pltpu.VMEM((NSA_R * tq, 2 * DH), BF16),
                        pltpu.VMEM((s // tk, 2 * DH, tk), BF16),
                        pltpu.VMEM((s // BLOCK, DH, BLOCK), BF16),
                        pltpu.VMEM((DH, ncp), BF16),
                        pltpu.VMEM((s, 2 * DH), BF16),
                        pltpu.VMEM((s, 2 * DH), BF16),
                        pltpu.VMEM((NSA_R * tq, ncp), BF16),
                        pltpu.VMEM((NSA_R * tq, wlen), BF16),
                        pltpu.VMEM((NSA_R * tq, DH), F32),
                        pltpu.VMEM((NSA_R * tq, tk), F32),
                        pltpu.VMEM((NSA_R * tq, tk), F32),
                        pltpu.VMEM((NSA_R * tq, tk), BF16),
                        pltpu.VMEM((NSA_R * tq, tk), BF16),
                        pltpu.VMEM((NSA_R, tq, DH), F32),
                        pltpu.VMEM((NSA_R, tq, DH), F32),
                        pltpu.VMEM((NSA_R, tq, DH), F32),
                        pltpu.VMEM((NSA_R, tq, 2 * DH), F32)],
        compiler_params=_cparams(("arbitrary", "arbitrary", "arbitrary")),
        name="nsa_attention",
    )(r_view, r_view, r_view, r_view, r_view, kcmp, vcmp, g_view, ov, et)
    return out.reshape(b * s, W_QB)


def _out_kernel(ya_ref, yb_ref, ga_ref, gb_ref, x_ref, woa_ref, wob_ref, wo_ref, out_ref):
    pa = _dot(ya_ref[...], woa_ref[...])
    pb = _dot(yb_ref[...], wob_ref[...])
    y = (ga_ref[...] * pa + gb_ref[...] * pb).astype(BF16)
    out_ref[...] = x_ref[...] + _dot(y, wo_ref[...])


def _merge_out(ya, yb, o_g, x2, woa, wob, wo, layer, tm):
    m = x2.shape[0]
    row = lambda w: pl.BlockSpec((tm, w), lambda i: (i, 0))
    const = lambda shape: pl.BlockSpec((None,) + shape, lambda i: (layer, 0, 0),
                                       pipeline_mode=pl.Buffered(1))
    return pl.pallas_call(
        _out_kernel,
        grid=(m // tm,),
        in_specs=[row(DIL_OUT), row(W_QB),
                  pl.BlockSpec((tm, D_MODEL), lambda i: (i, 0)),
                  pl.BlockSpec((tm, D_MODEL), lambda i: (i, 1)),
                  row(D_MODEL),
                  const((DIL_OUT, D_MODEL)), const((W_QB, D_MODEL)), const((D_MODEL, D_MODEL))],
        out_specs=row(D_MODEL),
        out_shape=jax.ShapeDtypeStruct((m, D_MODEL), F32),
        compiler_params=_cparams(("arbitrary",)),
        name="merge_out",
    )(ya, yb, o_g, o_g, x2, woa, wob, wo)


def _ffn_kernel(x_ref, g_ref, wg_ref, wu_ref, wd_ref, gf_ref, out_ref, h_ref, *, nf, final):
    f = pl.program_id(1)

    @pl.when(f == 0)
    def _():
        x = x_ref[...]
        h_ref[...] = _rms(x, g_ref[...]).astype(BF16)
        out_ref[...] = x

    h = h_ref[...]
    a = _dot(h, wg_ref[...])
    u = _dot(h, wu_ref[...])
    act = (a * jax.nn.sigmoid(a) * u).astype(BF16)
    out_ref[...] += _dot(act, wd_ref[...])

    if final:
        @pl.when(f == nf - 1)
        def _():
            out_ref[...] = _rms(out_ref[...], gf_ref[...])


def _ffn(x2, g, wg, wu, wd, layer, gf, tm, tf, final):
    m = x2.shape[0]
    nf = D_FF // tf
    return pl.pallas_call(
        functools.partial(_ffn_kernel, nf=nf, final=final),
        grid=(m // tm, nf),
        in_specs=[pl.BlockSpec((tm, D_MODEL), lambda i, f: (i, 0)),
                  pl.BlockSpec((1, D_MODEL), lambda i, f: (0, 0)),
                  pl.BlockSpec((None, D_MODEL, tf), lambda i, f: (layer, 0, f)),
                  pl.BlockSpec((None, D_MODEL, tf), lambda i, f: (layer, 0, f)),
                  pl.BlockSpec((None, tf, D_MODEL), lambda i, f: (layer, f, 0)),
                  pl.BlockSpec((1, D_MODEL), lambda i, f: (0, 0))],
        out_specs=pl.BlockSpec((tm, D_MODEL), lambda i, f: (i, 0)),
        out_shape=jax.ShapeDtypeStruct((m, D_MODEL), F32),
        scratch_shapes=[pltpu.VMEM((tm, D_MODEL), BF16)],
        compiler_params=_cparams(("arbitrary", "arbitrary")),
        name="ffn",
    )(x2, g, wg, wu, wd, gf)


def _prep_w_gates(w):
    depth = w.shape[0]
    n_gb = NSA_Q_HEADS * 3
    gate_b = w[:, :, N_SRC:N_SRC + n_gb]
    gam = w[:, :, N_SRC + n_gb:N_SRC + n_gb + D_MODEL]
    gbm = w[:, :, N_SRC + n_gb + D_MODEL:]
    gate_b = gate_b.reshape(depth, D_MODEL, NSA_G, NSA_R, 3).transpose(0, 1, 2, 4, 3)
    gate_b = gate_b.reshape(depth, D_MODEL, NSA_G, 3 * NSA_R)
    gate_b = jnp.pad(gate_b, ((0, 0), (0, 0), (0, 0), (0, DH - 3 * NSA_R)))
    gate_b = gate_b.reshape(depth, D_MODEL, NSA_G * DH)
    gate_b = jnp.pad(gate_b, ((0, 0), (0, 0), (0, N_G - 2 * D_MODEL - NSA_G * DH)))
    return jnp.concatenate([gam, gbm, gate_b], axis=2).astype(BF16)


def kernel(x, positions, ln_mix, w_in, cmp_pos_k, cmp_pos_v, cmp_w1_k, cmp_w2_k, cmp_w1_v, cmp_w2_v,
           w_out_a, w_out_b, w_out, ln_ffn, w_ffn_gate, w_ffn_up, w_ffn_down, ln_final):
    b, s, d = x.shape
    depth = w_in.shape[0]
    assert d == D_MODEL and s % DIL_UNIT == 0
    m = b * s
    tm_in = min(2048, m)
    tm_out = min(256, m)
    tm_ffn = min(1024, m)
    tq, tk = 128, min(512, s)
    ncp = s // CMP_STRIDE

    pos_f = positions.astype(F32)
    cos, sin = _rope_tables(pos_f.reshape(m, 1), tm_in)
    blk_end = np.minimum(np.arange(ncp) * CMP_STRIDE + CMP_LEN - 1, s - 1)
    cosc, sinc = _rope_tables(pos_f[:, blk_end].reshape(b * ncp, 1), ncp)
    cosc = cosc.reshape(b, ncp, DH)
    sinc = sinc.reshape(b, ncp, DH)

    cs = np.ones((1, N_SRC), np.float32)
    cs[:, :W_QA] = Q_SCALE
    cs[:, 3 * W_QA:3 * W_QA + W_QB] = Q_SCALE
    cs = jnp.asarray(cs)

    w_in_b = w_in.astype(BF16)
    w_gate_b = _prep_w_gates(w_in)
    woa_b, wob_b, wo_b = w_out_a.astype(BF16), w_out_b.astype(BF16), w_out.astype(BF16)
    wg_b, wu_b, wd_b = w_ffn_gate.astype(BF16), w_ffn_up.astype(BF16), w_ffn_down.astype(BF16)

    x2 = x.reshape(m, d)
    for l in range(depth):
        h = _norm(x2, ln_mix[l][None, :], min(1024, m))
        o_a, o_r, o_g = _inproj(h, w_in_b, w_gate_b, l, cs, cos, sin, tm_in)
        ya = _dilated(o_a, b, s)
        kcmp, vcmp = _compress(
            o_a, b, s, cmp_pos_k[l], cmp_pos_v[l],
            cmp_w1_k[l].reshape(CMP_LEN, DH, CMP_HIDDEN).astype(BF16),
            cmp_w1_v[l].reshape(CMP_LEN, DH, CMP_HIDDEN).astype(BF16),
            cmp_w2_k[l].astype(BF16), cmp_w2_v[l].astype(BF16), cosc, sinc)
        yb = _nsa(o_r, o_g, kcmp, vcmp, b, s, tq, tk)
        x2 = _merge_out(ya, yb, o_g, x2, woa_b, wob_b, wo_b, l, tm_out)
        x2 = _ffn(x2, ln_ffn[l][None, :], wg_b, wu_b, wd_b, l, ln_final[None, :], tm_ffn, 512,
                  l == depth - 1)
    return x2.reshape(b, s, d)
```

```python
import functools
import math

import numpy as np
import jax
import jax.numpy as jnp
from jax import lax
from jax.experimental import pallas as pl
from jax.experimental.pallas import tpu as pltpu

F32 = jnp.float32
BF16 = jnp.bfloat16

D_MODEL = 2048
DH = 128
HALF = DH // 2
ROPE_THETA = 10000.0
NORM_EPS = 1e-6
NEG_INF = -1e30
BLOCK = 128

DIL_GROUPS = ((128, 1), (512, 4), (2048, 16))
DIL_HPG = 4
DIL_HEADS = DIL_HPG * len(DIL_GROUPS)
DIL_OUT = DIL_HPG * DH
DIL_UNIT = DIL_GROUPS[-1][1] * BLOCK

NSA_Q_HEADS = 16
NSA_G = 2
NSA_R = NSA_Q_HEADS // NSA_G
CMP_LEN = 32
CMP_STRIDE = 16
CMP_HIDDEN = 256
SEL_LEN = 64
SEL_TOPK = 16
WIN_LEN = 512
FORCE_BONUS = 1e4
D_FF = 5632

W_QA = DIL_HEADS * DH
W_QB = NSA_Q_HEADS * DH
W_KV = NSA_G * DH
TN_IN = 512
IN_SPLIT = 4
N_SRC = 3 * W_QA + W_QB + 6 * W_KV
N_A = 3 * W_QA + 2 * W_KV
N_R = W_QB + 4 * W_KV
N_G = 2 * D_MODEL + TN_IN
COL_KC = 3 * W_QA // DH
VMEM_LIMIT = 56 * 1024 * 1024
Q_SCALE = DH ** -0.5 * math.log2(math.e)


def _cparams(sem):
    return pltpu.CompilerParams(dimension_semantics=sem, vmem_limit_bytes=VMEM_LIMIT)


def _dot(a, b):
    return jnp.dot(a, b, preferred_element_type=F32)


def _rope_tab_kernel(pos_ref, inv_ref, sgn_ref, cos_ref, sin_ref):
    ang = pos_ref[...] * inv_ref[...]
    cos_ref[...] = jnp.cos(ang)
    sin_ref[...] = jnp.sin(ang) * sgn_ref[...]


def _rope_tables(pos_f, tm):
    m = pos_f.shape[0]
    inv = ROPE_THETA ** (-2.0 * jnp.arange(HALF, dtype=F32) / DH)
    inv = jnp.concatenate([inv, inv])[None, :]
    sgn = jnp.concatenate([-jnp.ones((HALF,), F32), jnp.ones((HALF,), F32)])[None, :]
    return pl.pallas_call(
        _rope_tab_kernel,
        grid=(m // tm,),
        in_specs=[pl.BlockSpec((tm, 1), lambda i: (i, 0)),
                  pl.BlockSpec((1, DH), lambda i: (0, 0)),
                  pl.BlockSpec((1, DH), lambda i: (0, 0))],
        out_specs=[pl.BlockSpec((tm, DH), lambda i: (i, 0)),
                   pl.BlockSpec((tm, DH), lambda i: (i, 0))],
        out_shape=[jax.ShapeDtypeStruct((m, DH), F32)] * 2,
        compiler_params=_cparams(("arbitrary",)),
        name="rope_tables",
    )(pos_f, inv, sgn)


def _rope(a, cos, sin_signed):
    return a * cos + pltpu.roll(a, HALF, 1) * sin_signed


def _rms(x, g):
    ms = jnp.mean(x * x, axis=-1, keepdims=True)
    return x * lax.rsqrt(ms + NORM_EPS) * g


T_QK = 2 * W_QA // TN_IN
T_VA = 3 * W_QA // TN_IN
T_QB = T_VA + W_QB // TN_IN
T_KC = T_QB + 2 * W_KV // TN_IN
T_SRC = N_SRC // TN_IN
T_ALL = T_SRC + N_G // TN_IN


def _norm_kernel(x_ref, g_ref, h_ref):
    h_ref[...] = _rms(x_ref[...], g_ref[...]).astype(BF16)


def _norm(x2, g, tm):
    m = x2.shape[0]
    return pl.pallas_call(
        _norm_kernel,
        grid=(m // tm,),
        in_specs=[pl.BlockSpec((tm, D_MODEL), lambda i: (i, 0)),
                  pl.BlockSpec((1, D_MODEL), lambda i: (0, 0))],
        out_specs=pl.BlockSpec((tm, D_MODEL), lambda i: (i, 0)),
        out_shape=jax.ShapeDtypeStruct((m, D_MODEL), BF16),
        compiler_params=_cparams(("arbitrary",)),
        name="rmsnorm",
    )(x2, g)


def _inproj_kernel(h_ref, w_ref, wg_ref, cs_ref, cos_ref, sin_ref, oa_ref, or_ref, og_ref):
    j = pl.program_id(1)

    tm = h_ref.shape[0]
    groups = [slice(k * (tm // IN_SPLIT), (k + 1) * (tm // IN_SPLIT)) for k in range(IN_SPLIT)]
    heads = [slice(hh * DH, (hh + 1) * DH) for hh in range(TN_IN // DH)]

    def region(out_ref, epilogue, weights=w_ref):
        accs = [_dot(h_ref[rows, :], weights[...]) for rows in groups]
        for rows, acc in zip(groups, accs):
            epilogue(out_ref, rows, acc)

    def rope_heads(out_ref, rows, acc, which):
        cos = cos_ref[rows, :]
        sin = sin_ref[rows, :]
        for sl in which:
            r = _rope(acc[:, sl], cos, sin) * cs_ref[:, sl]
            out_ref[rows, sl] = r.astype(out_ref.dtype)

    def roped(out_ref, rows, acc):
        rope_heads(out_ref, rows, acc, heads)

    def plain(out_ref, rows, acc):
        out_ref[rows, :] = acc.astype(out_ref.dtype)

    def key_value(out_ref, rows, acc):
        rope_heads(out_ref, rows, acc, heads[:len(heads) // 2])
        out_ref[rows, TN_IN // 2:] = acc[:, TN_IN // 2:].astype(out_ref.dtype)

    def sigmoid(out_ref, rows, acc):
        out_ref[rows, :] = jax.nn.sigmoid(acc)

    pl.when(j < T_QK)(lambda: region(oa_ref, roped))
    pl.when((j >= T_QK) & (j < T_VA))(lambda: region(oa_ref, plain))
    pl.when((j >= T_VA) & (j < T_QB))(lambda: region(or_ref, roped))
    pl.when((j >= T_QB) & (j < T_KC))(lambda: region(oa_ref, plain))
    pl.when((j >= T_KC) & (j < T_SRC))(lambda: region(or_ref, key_value))
    pl.when(j >= T_SRC)(lambda: region(og_ref, sigmoid, wg_ref))


def _inproj(h, w, wg, layer, cs, cos, sin, tm):
    m = h.shape[0]
    n_kc = T_KC - T_QB

    def a_tile(j):
        return jnp.where(j < T_VA, j, jnp.clip(j - T_QB, -1, n_kc - 1) + T_VA)

    def r_tile(j):
        first = jnp.clip(j - T_VA, 0, T_QB - T_VA - 1)
        return jnp.where(j < T_KC, first, jnp.minimum(j, T_SRC - 1) - T_KC + T_QB - T_VA)

    return pl.pallas_call(
        _inproj_kernel,
        grid=(m // tm, T_ALL),
        in_specs=[pl.BlockSpec((tm, D_MODEL), lambda i, j: (i, 0)),
                  pl.BlockSpec((None, D_MODEL, TN_IN),
                               lambda i, j: (layer, 0, jnp.minimum(j, T_SRC - 1))),
                  pl.BlockSpec((None, D_MODEL, TN_IN),
                               lambda i, j: (layer, 0, jnp.maximum(j - T_SRC, 0))),
                  pl.BlockSpec((1, TN_IN), lambda i, j: (0, jnp.minimum(j, T_SRC - 1))),
                  pl.BlockSpec((tm, DH), lambda i, j: (i, 0)),
                  pl.BlockSpec((tm, DH), lambda i, j: (i, 0))],
        out_specs=[pl.BlockSpec((tm, TN_IN), lambda i, j: (i, a_tile(j))),
                   pl.BlockSpec((tm, TN_IN), lambda i, j: (i, r_tile(j))),
                   pl.BlockSpec((tm, TN_IN), lambda i, j: (i, jnp.maximum(j - T_SRC, 0)))],
        out_shape=[jax.ShapeDtypeStruct((m, N_A), F32),
                   jax.ShapeDtypeStruct((m, N_R), BF16),
                   jax.ShapeDtypeStruct((m, N_G), F32)],
        compiler_params=_cparams(("arbitrary", "arbitrary")),
        name="inproj",
    )(h, w, wg, cs, cos, sin)


def _rows(start, dil):
    return pl.ds(start, BLOCK) if dil == 1 else pl.ds(start, BLOCK, stride=dil)


def _dil_kernel(*refs):
    out_ref, acc_scr, l_scr, m_scr = refs[-4:]
    u = pl.program_id(1)
    row = lax.broadcasted_iota(jnp.int32, (BLOCK, 2 * BLOCK), 0)
    col = lax.broadcasted_iota(jnp.int32, (BLOCK, 2 * BLOCK), 1)
    band = ((col < BLOCK) & (col >= row)) | ((col >= BLOCK) & (col - BLOCK <= row))
    bias = jnp.where(band, 0.0, NEG_INF)
    bias_first = jnp.where(band & ((col >= BLOCK) | (u > 0)), 0.0, NEG_INF)
    ones = jnp.ones((2 * BLOCK, DH), BF16)
    for gi, (_, dil) in enumerate(DIL_GROUPS):
        q_ref, kc_ref, kp_ref, vc_ref, vp_ref = refs[5 * gi:5 * gi + 5]
        span = BLOCK * dil
        for rho in range(dil):
            for ub in range(DIL_UNIT // span):
                cur = _rows(ub * span + rho, dil)
                if ub == 0:
                    kp, vp, b_add = kp_ref[_rows(rho, dil), :], vp_ref[_rows(rho, dil), :], bias_first
                else:
                    prv = _rows((ub - 1) * span + rho, dil)
                    kp, vp, b_add = kc_ref[prv, :], vc_ref[prv, :], bias
                q = q_ref[cur, :].astype(BF16)
                k = jnp.concatenate([kp, kc_ref[cur, :]], axis=0).astype(BF16)
                v = jnp.concatenate([vp, vc_ref[cur, :]], axis=0).astype(BF16)
                s = lax.dot_general(q, k, (((1,), (1,)), ((), ())), preferred_element_type=F32) + b_add
                m = jnp.broadcast_to(jnp.max(s, axis=-1, keepdims=True), (BLOCK, DH))
                p = jnp.concatenate([jnp.exp2(s[:, :BLOCK] - m), jnp.exp2(s[:, BLOCK:] - m)], axis=1)
                pv = _dot(p.astype(BF16), jnp.concatenate([v, ones], axis=1))
                acc_scr[gi, cur, :] = pv[:, :DH]
                l_scr[gi, cur, :] = pv[:, DH:]
                m_scr[gi, cur, :] = m
    m0, m1, m2 = m_scr[0], m_scr[1], m_scr[2]
    m = jnp.maximum(jnp.maximum(m0, m1), m2)
    e0, e1, e2 = jnp.exp2(m0 - m), jnp.exp2(m1 - m), jnp.exp2(m2 - m)
    num = e0 * acc_scr[0] + e1 * acc_scr[1] + e2 * acc_scr[2]
    den = e0 * l_scr[0] + e1 * l_scr[1] + e2 * l_scr[2]
    out_ref[...] = (num / den).astype(BF16)


def _dilated(o_a, b, s):
    a_view = o_a.reshape(b, s, N_A)
    in_specs, args = [], []
    for gi, (_, dil) in enumerate(DIL_GROUPS):
        span = BLOCK * dil
        per = DIL_UNIT // span

        def cur(colbase, gi=gi):
            return pl.BlockSpec((None, DIL_UNIT, DH),
                                lambda bi, u, j: (bi, u, colbase + gi * DIL_HPG + j))

        def prev(colbase, gi=gi, span=span, per=per):
            return pl.BlockSpec((None, span, DH),
                                lambda bi, u, j: (bi, jnp.maximum(u * per - 1, 0),
                                                  colbase + gi * DIL_HPG + j))

        kcol, vcol = W_QA // DH, 2 * W_QA // DH
        in_specs += [cur(0), cur(kcol), prev(kcol), cur(vcol), prev(vcol)]
        args += [a_view] * 5
    out = pl.pallas_call(
        _dil_kernel,
        grid=(b, s // DIL_UNIT, DIL_HPG),
        in_specs=in_specs,
        out_specs=pl.BlockSpec((None, DIL_UNIT, DH), lambda bi, u, j: (bi, u, j)),
        out_shape=jax.ShapeDtypeStruct((b, s, DIL_OUT), BF16),
        scratch_shapes=[pltpu.VMEM((len(DIL_GROUPS), DIL_UNIT, DH), F32)] * 3,
        compiler_params=_cparams(("arbitrary", "arbitrary", "arbitrary")),
        name="dilated",
    )(*args)
    return out.reshape(b * s, DIL_OUT)


def _cmp_kernel(xk0_ref, xk1_ref, xv0_ref, xv1_ref, posk_ref, posv_ref, w1k_ref, w1v_ref,
                w2k_ref, w2v_ref, cos_ref, sin_ref, kc_ref, vc_ref, *, ncp):
    half = CMP_LEN // 2
    x_refs = ((xk0_ref, xk1_ref), (xv0_ref, xv1_ref))
    for kind in range(2):
        pos_ref, w1_ref, w2_ref, o_ref = ((posk_ref, w1k_ref, w2k_ref, kc_ref) if kind == 0
                                          else (posv_ref, w1v_ref, w2v_ref, vc_ref))
        for g in range(NSA_G):
            x_ref = x_refs[kind][g]
            a = jnp.zeros((ncp, CMP_HIDDEN), F32)
            bm = jnp.zeros((ncp, CMP_HIDDEN), F32)
            for l in range(half):
                x = x_ref[pl.ds(l, ncp, stride=CMP_STRIDE), :]
                a = a + _dot((x + pos_ref[l:l + 1, :]).astype(BF16), w1_ref[l])
                bm = bm + _dot((x + pos_ref[half + l:half + l + 1, :]).astype(BF16),
                               w1_ref[half + l])
            hid = a + pltpu.roll(bm, ncp - 1, 0)
            hid = jax.nn.gelu(hid, approximate=True).astype(BF16)
            out = _dot(hid, w2_ref[...])
            if kind == 0:
                out = _rope(out, cos_ref[...], sin_ref[...])
            o_ref[g] = out.astype(BF16)


def _compress(o_a, b, s, posk, posv, w1k, w1v, w2k, w2v, cosc, sinc):
    ncp = s // CMP_STRIDE
    x_view = o_a.reshape(b, s, N_A)
    full = lambda shape: pl.BlockSpec(shape, lambda bi: (0,) * len(shape))
    return pl.pallas_call(
        functools.partial(_cmp_kernel, ncp=ncp),
        grid=(b,),
        in_specs=[pl.BlockSpec((None, s, DH), lambda bi, c=c: (bi, 0, COL_KC + c))
                  for c in range(2 * NSA_G)] +
                 [full((CMP_LEN, DH)), full((CMP_LEN, DH)),
                  full((CMP_LEN, DH, CMP_HIDDEN)), full((CMP_LEN, DH, CMP_HIDDEN)),
                  full((CMP_HIDDEN, DH)), full((CMP_HIDDEN, DH)),
                  pl.BlockSpec((None, ncp, DH), lambda bi: (bi, 0, 0)),
                  pl.BlockSpec((None, ncp, DH), lambda bi: (bi, 0, 0))],
        out_specs=[pl.BlockSpec((None, NSA_G, ncp, DH), lambda bi: (bi, 0, 0, 0)),
                   pl.BlockSpec((None, NSA_G, ncp, DH), lambda bi: (bi, 0, 0, 0))],
        out_shape=[jax.ShapeDtypeStruct((b, NSA_G, ncp, DH), BF16)] * 2,
        compiler_params=_cparams(("arbitrary",)),
        name="nsa_compress",
    )(x_view, x_view, x_view, x_view, posk, posv, w1k, w1v, w2k, w2v, cosc, sinc)


def _nsa_kernel(q_ref, ks_ref, vs_ref, kw_ref, vw_ref, kc_ref, vc_ref, gate_ref, ov_ref, et_ref,
                out_ref, qa_ref, kat_ref, kwt_ref, kct_ref, va_ref, wa_ref, p_ref, pw_ref, y_ref,
                s0_ref, s1_ref, p0_ref, p1_ref, m_ref, al0_ref, al1_ref, acc_ref,
                *, tq, tk, s_len, ncp, n_s):
    R = NSA_R
    qi = pl.program_id(2)
    t0 = qi * tq
    transposed = lambda a: a.astype(F32).T.astype(BF16)

    @pl.when(qi == 0)
    def _():
        ones = jnp.ones((s_len, DH), BF16)
        for c in range(s_len // tk):
            kat_ref[c, :DH, :] = transposed(ks_ref[c * tk:(c + 1) * tk, :])
            kat_ref[c, DH:, :] = et_ref[:, c * tk:(c + 1) * tk]
        for c in range(s_len // BLOCK):
            kwt_ref[c] = transposed(kw_ref[c * BLOCK:(c + 1) * BLOCK, :])
        kct_ref[...] = transposed(kc_ref[...])
        va_ref[:, :DH] = vs_ref[...]
        va_ref[:, DH:] = ones
        wa_ref[:, :DH] = vw_ref[...]
        wa_ref[:, DH:] = ones

    for r in range(R):
        qa_ref[r * tq:(r + 1) * tq, :DH] = q_ref[:, r * DH:(r + 1) * DH]
    q = qa_ref[:, :DH]
    trow = t0 + lax.broadcasted_iota(jnp.int32, (tq, 1), 0)
    tlane = t0 + lax.broadcasted_iota(jnp.int32, (1, tq), 1)
    head = lambda a, r: a[r * tq:(r + 1) * tq]

    cend = lax.broadcasted_iota(jnp.int32, (1, ncp), 1) * CMP_STRIDE + (CMP_LEN - 1)
    valid_c = cend <= trow
    bias_c = jnp.where(valid_c, 0.0, NEG_INF)
    keep_c = jnp.where(valid_c, 1.0, 0.0)
    s = _dot(q, kct_ref[...])
    chunks_c = [slice(c * DH, (c + 1) * DH) for c in range(ncp // DH)]
    psum = [jnp.zeros((tq, DH), F32) for _ in chunks_c]
    for r in range(R):
        s_r = head(s, r) + bias_c
        m_c = jnp.broadcast_to(jnp.max(s_r, axis=-1, keepdims=True), (tq, DH))
        e = [jnp.exp2(s_r[:, cols] - m_c) * keep_c[:, cols] for cols in chunks_c]
        l_c = jnp.maximum(jnp.sum(sum(e), axis=-1, keepdims=True), 1e-30)
        l_c = jnp.broadcast_to(l_c, (tq, DH))
        for ci, cols in enumerate(chunks_c):
            p = e[ci] / l_c
            psum[ci] = psum[ci] + p
            p_ref[r * tq:(r + 1) * tq, cols] = p.astype(BF16)
    psum = jnp.concatenate(psum, axis=1)
    o_cmp = _dot(p_ref[:, :ncp], vc_ref[...])

    wlen = (-(-(WIN_LEN - 1) // BLOCK)) * BLOCK + tq
    w0 = pl.multiple_of(jnp.maximum(t0 + tq - wlen, 0), BLOCK)
    kpos = w0 + lax.broadcasted_iota(jnp.int32, (1, wlen), 1)
    bias_w = jnp.where((kpos <= trow) & (trow - kpos <= WIN_LEN - 1), 0.0, NEG_INF)
    wb = w0 // BLOCK
    kwt = jnp.concatenate([kwt_ref[wb + jb] for jb in range(wlen // BLOCK)], axis=1)
    s = _dot(q, kwt)
    for r in range(R):
        s_r = head(s, r) + bias_w
        m_w = jnp.broadcast_to(jnp.max(s_r, axis=-1, keepdims=True), (tq, DH))
        for c in range(wlen // DH):
            cols = slice(c * DH, (c + 1) * DH)
            pw_ref[r * tq:(r + 1) * tq, cols] = jnp.exp2(s_r[:, cols] - m_w).astype(BF16)
    ow = _dot(pw_ref[...], wa_ref[pl.ds(w0, wlen), :])
    o_win = ow[:, :DH] / ow[:, DH:]
    gate = gate_ref[...]
    for r in range(R):
        y_ref[r * tq:(r + 1) * tq, :] = (gate[:, r:r + 1] * head(o_cmp, r) +
                                         gate[:, 2 * R + r:2 * R + r + 1] * head(o_win, r))

    hi = psum.astype(BF16)
    rem = psum - hi.astype(F32)
    mid = rem.astype(BF16)
    lo = (rem - mid.astype(F32)).astype(BF16)
    ov = ov_ref[...]
    p_slc = (_dot(hi, ov) + _dot(mid, ov) + _dot(lo, ov)).T[:n_s]
    jj = lax.broadcasted_iota(jnp.int32, (n_s, 1), 0)
    blk_t = tlane // SEL_LEN
    forced = (jj == 0) | (jj == blk_t) | (jj == blk_t - 1)
    score = jnp.where(jj <= blk_t, p_slc + jnp.where(forced, FORCE_BONUS, 0.0), -1.0)
    nch = n_s // 8
    chunks = [score[c * 8:(c + 1) * 8] for c in range(nch)]
    cnt = [jnp.zeros((8, tq), F32) for _ in range(nch)]
    sub = lax.broadcasted_iota(jnp.int32, (8, 1), 0)
    for i in range(n_s):
        row_i = score[i:i + 1]
        for c in range(nch):
            if c * 8 > i:
                beats = row_i >= chunks[c]
            elif c * 8 + 7 < i:
                beats = row_i > chunks[c]
            else:
                beats = (row_i > chunks[c]) | ((row_i == chunks[c]) & (sub + c * 8 > i))
            cnt[c] = cnt[c] + jnp.where(beats, 1.0, 0.0)
    k_sel = float(min(SEL_TOPK, n_s))
    bias_t = jnp.concatenate([jnp.where(cc < k_sel, 0.0, NEG_INF) for cc in cnt] +
                             [jnp.zeros((DH - n_s, tq), F32)], axis=0)
    bias_q = bias_t.T.astype(BF16)
    for r in range(R):
        qa_ref[r * tq:(r + 1) * tq, DH:] = bias_q

    m_ref[...] = jnp.full(m_ref.shape, NEG_INF, F32)
    acc_ref[...] = jnp.zeros(acc_ref.shape, F32)

    def scores(kt, s_ref):
        s_ref[...] = _dot(qa_ref[...], kat_ref[kt])

    def tile(kt, bufs, causal):
        s_ref, pt_ref, al_ref = bufs
        if causal:
            kpos = kt * tk + lax.broadcasted_iota(jnp.int32, (1, tk), 1)
            bias_d = jnp.where(kpos <= trow, 0.0, NEG_INF)
        for r in range(R):
            rows = slice(r * tq, (r + 1) * tq)
            s_r = s_ref[rows, :]
            if causal:
                s_r = s_r + bias_d
            m_old = m_ref[r]
            m_new = jnp.maximum(m_old, jnp.max(s_r, axis=-1, keepdims=True))
            for c in range(tk // DH):
                cols = slice(c * DH, (c + 1) * DH)
                pt_ref[rows, cols] = jnp.exp2(s_r[:, cols] - m_new).astype(BF16)
            al_ref[r] = jnp.exp2(m_old - m_new)
            m_ref[r] = m_new
        k0 = pl.multiple_of(kt * tk, tk)
        pv = _dot(pt_ref[...], va_ref[pl.ds(k0, tk), :]).reshape(R, tq, 2 * DH)
        al = al_ref[...]
        acc_ref[:, :, :DH] = al * acc_ref[:, :, :DH] + pv[:, :, :DH]
        acc_ref[:, :, DH:] = al * acc_ref[:, :, DH:] + pv[:, :, DH:]

    even = (s0_ref, p0_ref, al0_ref)
    odd = (s1_ref, p1_ref, al1_ref)
    kd = t0 // tk
    scores(0, s0_ref)

    def pair(i, carry):
        scores(2 * i + 1, s1_ref)
        tile(2 * i, even, False)
        scores(2 * i + 2, s0_ref)
        tile(2 * i + 1, odd, False)
        return carry

    lax.fori_loop(0, kd // 2, pair, 0)

    @pl.when(kd % 2 == 0)
    def _():
        tile(kd, even, True)

    @pl.when(kd % 2 == 1)
    def _():
        scores(kd, s1_ref)
        tile(kd - 1, even, False)
        tile(kd, odd, True)

    acc = acc_ref[...]
    o_slc = (acc[:, :, :DH] / acc[:, :, DH:]).reshape(R * tq, DH)

    gate = gate_ref[...]
    for r in range(R):
        y = y_ref[r * tq:(r + 1) * tq, :] + gate[:, R + r:R + r + 1] * head(o_slc, r)
        out_ref[:, r * DH:(r + 1) * DH] = y.astype(BF16)


def _nsa(o_r, o_g, kcmp, vcmp, b, s, tq, tk):
    ncp = s // CMP_STRIDE
    n_s = s // SEL_LEN
    assert n_s % 8 == 0 and n_s <= DH
    r_view = o_r.reshape(b, s, N_R)
    g_view = o_g.reshape(b, s, N_G)
    c = np.arange(ncp)[:, None]
    j = np.arange(DH)[None, :]
    n_c = (s - CMP_LEN) // CMP_STRIDE + 1
    ov = ((c * CMP_STRIDE <= j * SEL_LEN + SEL_LEN - 1) &
          (c * CMP_STRIDE + CMP_LEN - 1 >= j * SEL_LEN) & (c < n_c) & (j < n_s))
    ov = jnp.asarray(ov.astype(np.float32), BF16)
    et = jnp.asarray((np.arange(s)[None, :] // SEL_LEN == np.arange(DH)[:, None])
                     .astype(np.float32), BF16)
    wq = NSA_R * DH
    ks_col = W_QB // DH
    vs_col = ks_col + NSA_G
    kw_col = vs_col + NSA_G
    vw_col = kw_col + NSA_G
    gate_col = (2 * D_MODEL) // DH
    wlen = (-(-(WIN_LEN - 1) // BLOCK)) * BLOCK + tq
    kv = (None, s, DH)
    out = pl.pallas_call(
        functools.partial(_nsa_kernel, tq=tq, tk=tk, s_len=s, ncp=ncp, n_s=n_s),
        grid=(b, NSA_G, s // tq),
        in_specs=[pl.BlockSpec((None, tq, wq), lambda bi, g, qi: (bi, qi, g)),
                  pl.BlockSpec(kv, lambda bi, g, qi: (bi, 0, ks_col + g)),
                  pl.BlockSpec(kv, lambda bi, g, qi: (bi, 0, vs_col + g)),
                  pl.BlockSpec(kv, lambda bi, g, qi: (bi, 0, kw_col + g)),
                  pl.BlockSpec(kv, lambda bi, g, qi: (bi, 0, vw_col + g)),
                  pl.BlockSpec((None, None, ncp, DH), lambda bi, g, qi: (bi, g, 0, 0)),
                  pl.BlockSpec((None, None, ncp, DH), lambda bi, g, qi: (bi, g, 0, 0)),
                  pl.BlockSpec((None, tq, DH), lambda bi, g, qi: (bi, qi, gate_col + g)),
                  pl.BlockSpec((ncp, DH), lambda bi, g, qi: (0, 0)),
                  pl.BlockSpec((DH, s), lambda bi, g, qi: (0, 0))],
        out_specs=pl.BlockSpec((None, tq, wq), lambda bi, g, qi: (bi, qi, g)),
        out_shape=jax.ShapeDtypeStruct((b, s, W_QB), BF16),
        scratch_shapes=[pltpu.VMEM((NSA_R * tq, 2 * DH), BF16),
                        pltpu.VMEM((s // tk, 2 * DH, tk), BF16),
                        pltpu.VMEM((s // BLOCK, DH, BLOCK), BF16),
                        pltpu.VMEM((DH, ncp), BF16),
                        pltpu.VMEM((s, 2 * DH), BF16),
                        pltpu.VMEM((s, 2 * DH), BF16),
                        pltpu.VMEM((NSA_R * tq, ncp), BF16),
                        pltpu.VMEM((NSA_R * tq, wlen), BF16),
                        pltpu.VMEM((NSA_R * tq, DH), F32),
                        pltpu.VMEM((NSA_R * tq, tk), F32),
                        pltpu.VMEM((NSA_R * tq, tk), F32),
                        pltpu.VMEM((NSA_R * tq, tk), BF16),
                        pltpu.VMEM((NSA_R * tq, tk), BF16),
                        pltpu.VMEM((NSA_R, tq, DH), F32),
                        pltpu.VMEM((NSA_R, tq, DH), F32),
                        pltpu.VMEM((NSA_R, tq, DH), F32),
                        pltpu.VMEM((NSA_R, tq, 2 * DH), F32)],
        compiler_params=_cparams(("arbitrary", "arbitrary", "arbitrary")),
        name="nsa_attention",
    )(r_view, r_view, r_view, r_view, r_view, kcmp, vcmp, g_view, ov, et)
    return out.reshape(b * s, W_QB)


def _out_kernel(ya_ref, yb_ref, ga_ref, gb_ref, x_ref, woa_ref, wob_ref, wo_ref, out_ref):
    pa = _dot(ya_ref[...], woa_ref[...])
    pb = _dot(yb_ref[...], wob_ref[...])
    y = (ga_ref[...] * pa + gb_ref[...] * pb).astype(BF16)
    out_ref[...] = x_ref[...] + _dot(y, wo_ref[...])


def _merge_out(ya, yb, o_g, x2, woa, wob, wo, layer, tm):
    m = x2.shape[0]
    row = lambda w: pl.BlockSpec((tm, w), lambda i: (i, 0))
    const = lambda shape: pl.BlockSpec((None,) + shape, lambda i: (layer, 0, 0),
                                       pipeline_mode=pl.Buffered(1))
    return pl.pallas_call(
        _out_kernel,
        grid=(m // tm,),
        in_specs=[row(DIL_OUT), row(W_QB),
                  pl.BlockSpec((tm, D_MODEL), lambda i: (i, 0)),
                  pl.BlockSpec((tm, D_MODEL), lambda i: (i, 1)),
                  row(D_MODEL),
                  const((DIL_OUT, D_MODEL)), const((W_QB, D_MODEL)), const((D_MODEL, D_MODEL))],
        out_specs=row(D_MODEL),
        out_shape=jax.ShapeDtypeStruct((m, D_MODEL), F32),
        compiler_params=_cparams(("arbitrary",)),
        name="merge_out",
    )(ya, yb, o_g, o_g, x2, woa, wob, wo)


def _ffn_kernel(x_ref, g_ref, wg_ref, wu_ref, wd_ref, gf_ref, out_ref, h_ref, *, nf, final):
    f = pl.program_id(1)

    @pl.when(f == 0)
    def _():
        x = x_ref[...]
        h_ref[...] = _rms(x, g_ref[...]).astype(BF16)
        out_ref[...] = x

    h = h_ref[...]
    a = _dot(h, wg_ref[...])
    u = _dot(h, wu_ref[...])
    act = (a * jax.nn.sigmoid(a) * u).astype(BF16)
    out_ref[...] += _dot(act, wd_ref[...])

    if final:
        @pl.when(f == nf - 1)
        def _():
            out_ref[...] = _rms(out_ref[...], gf_ref[...])


def _ffn(x2, g, wg, wu, wd, layer, gf, tm, tf, final):
    m = x2.shape[0]
    nf = D_FF // tf
    return pl.pallas_call(
        functools.partial(_ffn_kernel, nf=nf, final=final),
        grid=(m // tm, nf),
        in_specs=[pl.BlockSpec((tm, D_MODEL), lambda i, f: (i, 0)),
                  pl.BlockSpec((1, D_MODEL), lambda i, f: (0, 0)),
                  pl.BlockSpec((None, D_MODEL, tf), lambda i, f: (layer, 0, f)),
                  pl.BlockSpec((None, D_MODEL, tf), lambda i, f: (layer, 0, f)),
                  pl.BlockSpec((None, tf, D_MODEL), lambda i, f: (layer, f, 0)),
                  pl.BlockSpec((1, D_MODEL), lambda i, f: (0, 0))],
        out_specs=pl.BlockSpec((tm, D_MODEL), lambda i, f: (i, 0)),
        out_shape=jax.ShapeDtypeStruct((m, D_MODEL), F32),
        scratch_shapes=[pltpu.VMEM((tm, D_MODEL), BF16)],
        compiler_params=_cparams(("arbitrary", "arbitrary")),
        name="ffn",
    )(x2, g, wg, wu, wd, gf)


def _cast_kernel(w_ref, o_ref):
    o_ref[...] = w_ref[...].astype(BF16)


def _cast_leading_cols(w, n_cols, tn):
    depth, k, _ = w.shape
    return pl.pallas_call(
        _cast_kernel,
        grid=(depth, n_cols // tn),
        in_specs=[pl.BlockSpec((None, k, tn), lambda l, j: (l, 0, j))],
        out_specs=pl.BlockSpec((None, k, tn), lambda l, j: (l, 0, j)),
        out_shape=jax.ShapeDtypeStruct((depth, k, n_cols), BF16),
        compiler_params=_cparams(("arbitrary", "arbitrary")),
        name="cast_w_in",
    )(w)


def _prep_w_gates(w):
    depth = w.shape[0]
    n_gb = NSA_Q_HEADS * 3
    gate_b = w[:, :, N_SRC:N_SRC + n_gb]
    gam = w[:, :, N_SRC + n_gb:N_SRC + n_gb + D_MODEL]
    gbm = w[:, :, N_SRC + n_gb + D_MODEL:]
    gate_b = gate_b.reshape(depth, D_MODEL, NSA_G, NSA_R, 3).transpose(0, 1, 2, 4, 3)
    gate_b = gate_b.reshape(depth, D_MODEL, NSA_G, 3 * NSA_R)
    gate_b = jnp.pad(gate_b, ((0, 0), (0, 0), (0, 0), (0, DH - 3 * NSA_R)))
    gate_b = gate_b.reshape(depth, D_MODEL, NSA_G * DH)
    gate_b = jnp.pad(gate_b, ((0, 0), (0, 0), (0, N_G - 2 * D_MODEL - NSA_G * DH)))
    return jnp.concatenate([gam, gbm, gate_b], axis=2).astype(BF16)


def kernel(x, positions, ln_mix, w_in, cmp_pos_k, cmp_pos_v, cmp_w1_k, cmp_w2_k, cmp_w1_v, cmp_w2_v,
           w_out_a, w_out_b, w_out, ln_ffn, w_ffn_gate, w_ffn_up, w_ffn_down, ln_final):
    b, s, d = x.shape
    depth = w_in.shape[0]
    assert d == D_MODEL and s % DIL_UNIT == 0
    m = b * s
    tm_in = min(2048, m)
    tm_out = min(256, m)
    tm_ffn = min(1024, m)
    tq, tk = 128, min(512, s)
    ncp = s // CMP_STRIDE

    pos_f = positions.astype(F32)
    cos, sin = _rope_tables(pos_f.reshape(m, 1), tm_in)
    blk_end = np.minimum(np.arange(ncp) * CMP_STRIDE + CMP_LEN - 1, s - 1)
    cosc, sinc = _rope_tables(pos_f[:, blk_end].reshape(b * ncp, 1), ncp)
    cosc = cosc.reshape(b, ncp, DH)
    sinc = sinc.reshape(b, ncp, DH)

    cs = np.ones((1, N_SRC), np.float32)
    cs[:, :W_QA] = Q_SCALE
    cs[:, 3 * W_QA:3 * W_QA + W_QB] = Q_SCALE
    cs = jnp.asarray(cs)

    w_in_b = _cast_leading_cols(w_in, N_SRC, 1024)
    w_gate_b = _prep_w_gates(w_in)
    woa_b, wob_b, wo_b = w_out_a.astype(BF16), w_out_b.astype(BF16), w_out.astype(BF16)
    wg_b, wu_b, wd_b = w_ffn_gate.astype(BF16), w_ffn_up.astype(BF16), w_ffn_down.astype(BF16)

    x2 = x.reshape(m, d)
    for l in range(depth):
        h = _norm(x2, ln_mix[l][None, :], min(1024, m))
        o_a, o_r, o_g = _inproj(h, w_in_b, w_gate_b, l, cs, cos, sin, tm_in)
        ya = _dilated(o_a, b, s)
        kcmp, vcmp = _compress(
            o_a, b, s, cmp_pos_k[l], cmp_pos_v[l],
            cmp_w1_k[l].reshape(CMP_LEN, DH, CMP_HIDDEN).astype(BF16),
            cmp_w1_v[l].reshape(CMP_LEN, DH, CMP_HIDDEN).astype(BF16),
            cmp_w2_k[l].astype(BF16), cmp_w2_v[l].astype(BF16), cosc, sinc)
        yb = _nsa(o_r, o_g, kcmp, vcmp, b, s, tq, tk)
        x2 = _merge_out(ya, yb, o_g, x2, woa_b, wob_b, wo_b, l, tm_out)
        x2 = _ffn(x2, ln_ffn[l][None, :], wg_b, wu_b, wd_b, l, ln_final[None, :], tm_ffn, 512,
                  l == depth - 1)
    return x2.reshape(b, s, d)
```

```python
import functools
import math

import numpy as np
import jax
import jax.numpy as jnp
from jax import lax
from jax.experimental import pallas as pl
from jax.experimental.pallas import tpu as pltpu

F32 = jnp.float32
BF16 = jnp.bfloat16

D_MODEL = 2048
DH = 128
HALF = DH // 2
ROPE_THETA = 10000.0
NORM_EPS = 1e-6
NEG_INF = -1e30
BLOCK = 128

DIL_GROUPS = ((128, 1), (512, 4), (2048, 16))
DIL_HPG = 4
DIL_HEADS = DIL_HPG * len(DIL_GROUPS)
DIL_OUT = DIL_HPG * DH
DIL_UNIT = DIL_GROUPS[-1][1] * BLOCK

NSA_Q_HEADS = 16
NSA_G = 2
NSA_R = NSA_Q_HEADS // NSA_G
CMP_LEN = 32
CMP_STRIDE = 16
CMP_HIDDEN = 256
SEL_LEN = 64
SEL_TOPK = 16
WIN_LEN = 512
FORCE_BONUS = 1e4
D_FF = 5632

W_QA = DIL_HEADS * DH
W_QB = NSA_Q_HEADS * DH
W_KV = NSA_G * DH
TN_IN = 512
IN_SPLIT = 8
N_SRC = 3 * W_QA + W_QB + 6 * W_KV
N_A = 3 * W_QA + 2 * W_KV
N_R = W_QB + 4 * W_KV
N_G = 2 * D_MODEL + TN_IN
COL_KC = 3 * W_QA // DH
VMEM_LIMIT = 56 * 1024 * 1024
Q_SCALE = DH ** -0.5 * math.log2(math.e)


def _cparams(sem):
    return pltpu.CompilerParams(dimension_semantics=sem, vmem_limit_bytes=VMEM_LIMIT)


def _dot(a, b):
    return jnp.dot(a, b, preferred_element_type=F32)


def _rope_tab_kernel(pos_ref, inv_ref, sgn_ref, cos_ref, sin_ref):
    ang = pos_ref[...] * inv_ref[...]
    cos_ref[...] = jnp.cos(ang)
    sin_ref[...] = jnp.sin(ang) * sgn_ref[...]


def _rope_tables(pos_f, tm):
    m = pos_f.shape[0]
    inv = ROPE_THETA ** (-2.0 * jnp.arange(HALF, dtype=F32) / DH)
    inv = jnp.concatenate([inv, inv])[None, :]
    sgn = jnp.concatenate([-jnp.ones((HALF,), F32), jnp.ones((HALF,), F32)])[None, :]
    return pl.pallas_call(
        _rope_tab_kernel,
        grid=(m // tm,),
        in_specs=[pl.BlockSpec((tm, 1), lambda i: (i, 0)),
                  pl.BlockSpec((1, DH), lambda i: (0, 0)),
                  pl.BlockSpec((1, DH), lambda i: (0, 0))],
        out_specs=[pl.BlockSpec((tm, DH), lambda i: (i, 0)),
                   pl.BlockSpec((tm, DH), lambda i: (i, 0))],
        out_shape=[jax.ShapeDtypeStruct((m, DH), F32)] * 2,
        compiler_params=_cparams(("arbitrary",)),
        name="rope_tables",
    )(pos_f, inv, sgn)


def _rope(a, cos, sin_signed):
    return a * cos + pltpu.roll(a, HALF, 1) * sin_signed


def _rms(x, g):
    ms = jnp.mean(x * x, axis=-1, keepdims=True)
    return x * lax.rsqrt(ms + NORM_EPS) * g


T_QK = 2 * W_QA // TN_IN
T_VA = 3 * W_QA // TN_IN
T_QB = T_VA + W_QB // TN_IN
T_KC = T_QB + 2 * W_KV // TN_IN
T_SRC = N_SRC // TN_IN
T_ALL = T_SRC + N_G // TN_IN


def _norm_kernel(x_ref, g_ref, h_ref):
    h_ref[...] = _rms(x_ref[...], g_ref[...]).astype(BF16)


def _norm(x2, g, tm):
    m = x2.shape[0]
    return pl.pallas_call(
        _norm_kernel,
        grid=(m // tm,),
        in_specs=[pl.BlockSpec((tm, D_MODEL), lambda i: (i, 0)),
                  pl.BlockSpec((1, D_MODEL), lambda i: (0, 0))],
        out_specs=pl.BlockSpec((tm, D_MODEL), lambda i: (i, 0)),
        out_shape=jax.ShapeDtypeStruct((m, D_MODEL), BF16),
        compiler_params=_cparams(("arbitrary",)),
        name="rmsnorm",
    )(x2, g)


def _inproj_kernel(h_ref, w_ref, wg_ref, cs_ref, cos_ref, sin_ref, oa_ref, or_ref, og_ref):
    j = pl.program_id(1)

    tm = h_ref.shape[0]
    groups = [slice(k * (tm // IN_SPLIT), (k + 1) * (tm // IN_SPLIT)) for k in range(IN_SPLIT)]
    heads = [slice(hh * DH, (hh + 1) * DH) for hh in range(TN_IN // DH)]

    def region(out_ref, epilogue, weights=w_ref):
        accs = [_dot(h_ref[rows, :], weights[...]) for rows in groups]
        for rows, acc in zip(groups, accs):
            epilogue(out_ref, rows, acc)

    def rope_heads(out_ref, rows, acc, which):
        cos = cos_ref[rows, :]
        sin = sin_ref[rows, :]
        for sl in which:
            r = _rope(acc[:, sl], cos, sin) * cs_ref[:, sl]
            out_ref[rows, sl] = r.astype(out_ref.dtype)

    def roped(out_ref, rows, acc):
        rope_heads(out_ref, rows, acc, heads)

    def plain(out_ref, rows, acc):
        out_ref[rows, :] = acc.astype(out_ref.dtype)

    def key_value(out_ref, rows, acc):
        rope_heads(out_ref, rows, acc, heads[:len(heads) // 2])
        out_ref[rows, TN_IN // 2:] = acc[:, TN_IN // 2:].astype(out_ref.dtype)

    def sigmoid(out_ref, rows, acc):
        out_ref[rows, :] = jax.nn.sigmoid(acc)

    pl.when(j < T_QK)(lambda: region(oa_ref, roped))
    pl.when((j >= T_QK) & (j < T_VA))(lambda: region(oa_ref, plain))
    pl.when((j >= T_VA) & (j < T_QB))(lambda: region(or_ref, roped))
    pl.when((j >= T_QB) & (j < T_KC))(lambda: region(oa_ref, plain))
    pl.when((j >= T_KC) & (j < T_SRC))(lambda: region(or_ref, key_value))
    pl.when(j >= T_SRC)(lambda: region(og_ref, sigmoid, wg_ref))


def _inproj(h, w, wg, layer, cs, cos, sin, tm):
    m = h.shape[0]
    n_kc = T_KC - T_QB

    def a_tile(j):
        return jnp.where(j < T_VA, j, jnp.clip(j - T_QB, -1, n_kc - 1) + T_VA)

    def r_tile(j):
        first = jnp.clip(j - T_VA, 0, T_QB - T_VA - 1)
        return jnp.where(j < T_KC, first, jnp.minimum(j, T_SRC - 1) - T_KC + T_QB - T_VA)

    return pl.pallas_call(
        _inproj_kernel,
        grid=(m // tm, T_ALL),
        in_specs=[pl.BlockSpec((tm, D_MODEL), lambda i, j: (i, 0)),
                  pl.BlockSpec((None, D_MODEL, TN_IN),
                               lambda i, j: (layer, 0, jnp.minimum(j, T_SRC - 1))),
                  pl.BlockSpec((None, D_MODEL, TN_IN),
                               lambda i, j: (layer, 0, jnp.maximum(j - T_SRC, 0))),
                  pl.BlockSpec((1, TN_IN), lambda i, j: (0, jnp.minimum(j, T_SRC - 1))),
                  pl.BlockSpec((tm, DH), lambda i, j: (i, 0)),
                  pl.BlockSpec((tm, DH), lambda i, j: (i, 0))],
        out_specs=[pl.BlockSpec((tm, TN_IN), lambda i, j: (i, a_tile(j))),
                   pl.BlockSpec((tm, TN_IN), lambda i, j: (i, r_tile(j))),
                   pl.BlockSpec((tm, TN_IN), lambda i, j: (i, jnp.maximum(j - T_SRC, 0)))],
        out_shape=[jax.ShapeDtypeStruct((m, N_A), F32),
                   jax.ShapeDtypeStruct((m, N_R), BF16),
                   jax.ShapeDtypeStruct((m, N_G), F32)],
        compiler_params=_cparams(("arbitrary", "arbitrary")),
        name="inproj",
    )(h, w, wg, cs, cos, sin)


def _rows(start, dil):
    return pl.ds(start, BLOCK) if dil == 1 else pl.ds(start, BLOCK, stride=dil)


def _dil_kernel(*refs):
    out_ref, acc_scr, l_scr, m_scr = refs[-4:]
    u = pl.program_id(1)
    row = lax.broadcasted_iota(jnp.int32, (BLOCK, 2 * BLOCK), 0)
    col = lax.broadcasted_iota(jnp.int32, (BLOCK, 2 * BLOCK), 1)
    band = ((col < BLOCK) & (col >= row)) | ((col >= BLOCK) & (col - BLOCK <= row))
    bias = jnp.where(band, 0.0, NEG_INF)
    bias_first = jnp.where(band & ((col >= BLOCK) | (u > 0)), 0.0, NEG_INF)
    ones = jnp.ones((2 * BLOCK, DH), BF16)
    for gi, (_, dil) in enumerate(DIL_GROUPS):
        q_ref, kc_ref, kp_ref, vc_ref, vp_ref = refs[5 * gi:5 * gi + 5]
        span = BLOCK * dil
        for rho in range(dil):
            for ub in range(DIL_UNIT // span):
                cur = _rows(ub * span + rho, dil)
                if ub == 0:
                    kp = kp_ref[_rows(rho, dil), :].astype(BF16)
                    vp = vp_ref[_rows(rho, dil), :].astype(BF16)
                    b_add = bias_first
                else:
                    kp, vp, b_add = kc, vc, bias
                q = q_ref[cur, :].astype(BF16)
                kc = kc_ref[cur, :].astype(BF16)
                vc = vc_ref[cur, :].astype(BF16)
                k = jnp.concatenate([kp, kc], axis=0)
                v = jnp.concatenate([vp, vc], axis=0)
                s = lax.dot_general(q, k, (((1,), (1,)), ((), ())), preferred_element_type=F32) + b_add
                m = jnp.broadcast_to(jnp.max(s, axis=-1, keepdims=True), (BLOCK, DH))
                p = jnp.concatenate([jnp.exp2(s[:, :BLOCK] - m), jnp.exp2(s[:, BLOCK:] - m)], axis=1)
                pv = _dot(p.astype(BF16), jnp.concatenate([v, ones], axis=1))
                acc_scr[gi, cur, :] = pv[:, :DH]
                l_scr[gi, cur, :] = pv[:, DH:]
                m_scr[gi, cur, :] = m
    m0, m1, m2 = m_scr[0], m_scr[1], m_scr[2]
    m = jnp.maximum(jnp.maximum(m0, m1), m2)
    e0, e1, e2 = jnp.exp2(m0 - m), jnp.exp2(m1 - m), jnp.exp2(m2 - m)
    num = e0 * acc_scr[0] + e1 * acc_scr[1] + e2 * acc_scr[2]
    den = e0 * l_scr[0] + e1 * l_scr[1] + e2 * l_scr[2]
    out_ref[...] = (num / den).astype(BF16)


def _dilated(o_a, b, s):
    a_view = o_a.reshape(b, s, N_A)
    in_specs, args = [], []
    for gi, (_, dil) in enumerate(DIL_GROUPS):
        span = BLOCK * dil
        per = DIL_UNIT // span

        def cur(colbase, gi=gi):
            return pl.BlockSpec((None, DIL_UNIT, DH),
                                lambda bi, u, j: (bi, u, colbase + gi * DIL_HPG + j))

        def prev(colbase, gi=gi, span=span, per=per):
            return pl.BlockSpec((None, span, DH),
                                lambda bi, u, j: (bi, jnp.maximum(u * per - 1, 0),
                                                  colbase + gi * DIL_HPG + j))

        kcol, vcol = W_QA // DH, 2 * W_QA // DH
        in_specs += [cur(0), cur(kcol), prev(kcol), cur(vcol), prev(vcol)]
        args += [a_view] * 5
    out = pl.pallas_call(
        _dil_kernel,
        grid=(b, s // DIL_UNIT, DIL_HPG),
        in_specs=in_specs,
        out_specs=pl.BlockSpec((None, DIL_UNIT, DH), lambda bi, u, j: (bi, u, j)),
        out_shape=jax.ShapeDtypeStruct((b, s, DIL_OUT), BF16),
        scratch_shapes=[pltpu.VMEM((len(DIL_GROUPS), DIL_UNIT, DH), F32)] * 3,
        compiler_params=_cparams(("arbitrary", "arbitrary", "arbitrary")),
        name="dilated",
    )(*args)
    return out.reshape(b * s, DIL_OUT)


def _cmp_kernel(xk0_ref, xk1_ref, xv0_ref, xv1_ref, posk_ref, posv_ref, w1k_ref, w1v_ref,
                w2k_ref, w2v_ref, cos_ref, sin_ref, kc_ref, vc_ref, *, ncp):
    half = CMP_LEN // 2
    x_refs = ((xk0_ref, xk1_ref), (xv0_ref, xv1_ref))
    for kind in range(2):
        pos_ref, w1_ref, w2_ref, o_ref = ((posk_ref, w1k_ref, w2k_ref, kc_ref) if kind == 0
                                          else (posv_ref, w1v_ref, w2v_ref, vc_ref))
        for g in range(NSA_G):
            x_ref = x_refs[kind][g]
            a = jnp.zeros((ncp, CMP_HIDDEN), F32)
            bm = jnp.zeros((ncp, CMP_HIDDEN), F32)
            for l in range(half):
                x = x_ref[pl.ds(l, ncp, stride=CMP_STRIDE), :]
                a = a + _dot((x + pos_ref[l:l + 1, :]).astype(BF16), w1_ref[l])
                bm = bm + _dot((x + pos_ref[half + l:half + l + 1, :]).astype(BF16),
                               w1_ref[half + l])
            hid = a + pltpu.roll(bm, ncp - 1, 0)
            hid = jax.nn.gelu(hid, approximate=True).astype(BF16)
            out = _dot(hid, w2_ref[...])
            if kind == 0:
                out = _rope(out, cos_ref[...], sin_ref[...])
            o_ref[g] = out.astype(BF16)


def _compress(o_a, b, s, posk, posv, w1k, w1v, w2k, w2v, cosc, sinc):
    ncp = s // CMP_STRIDE
    x_view = o_a.reshape(b, s, N_A)
    full = lambda shape: pl.BlockSpec(shape, lambda bi: (0,) * len(shape))
    return pl.pallas_call(
        functools.partial(_cmp_kernel, ncp=ncp),
        grid=(b,),
        in_specs=[pl.BlockSpec((None, s, DH), lambda bi, c=c: (bi, 0, COL_KC + c))
                  for c in range(2 * NSA_G)] +
                 [full((CMP_LEN, DH)), full((CMP_LEN, DH)),
                  full((CMP_LEN, DH, CMP_HIDDEN)), full((CMP_LEN, DH, CMP_HIDDEN)),
                  full((CMP_HIDDEN, DH)), full((CMP_HIDDEN, DH)),
                  pl.BlockSpec((None, ncp, DH), lambda bi: (bi, 0, 0)),
                  pl.BlockSpec((None, ncp, DH), lambda bi: (bi, 0, 0))],
        out_specs=[pl.BlockSpec((None, NSA_G, ncp, DH), lambda bi: (bi, 0, 0, 0)),
                   pl.BlockSpec((None, NSA_G, ncp, DH), lambda bi: (bi, 0, 0, 0))],
        out_shape=[jax.ShapeDtypeStruct((b, NSA_G, ncp, DH), BF16)] * 2,
        compiler_params=_cparams(("arbitrary",)),
        name="nsa_compress",
    )(x_view, x_view, x_view, x_view, posk, posv, w1k, w1v, w2k, w2v, cosc, sinc)


def _nsa_kernel(q_ref, ks_ref, vs_ref, kw_ref, vw_ref, kc_ref, vc_ref, gate_ref, ov_ref, et_ref,
                out_ref, qa_ref, kat_ref, kwt_ref, kct_ref, va_ref, wa_ref, p_ref, pw_ref, y_ref,
                s0_ref, s1_ref, p0_ref, p1_ref, m_ref, al0_ref, al1_ref, acc_ref,
                *, tq, tk, s_len, ncp, n_s):
    R = NSA_R
    qi = pl.program_id(2)
    t0 = qi * tq
    transposed = lambda a: a.astype(F32).T.astype(BF16)

    @pl.when(qi == 0)
    def _():
        ones = jnp.ones((s_len, DH), BF16)
        for c in range(s_len // tk):
            kat_ref[c, :DH, :] = transposed(ks_ref[c * tk:(c + 1) * tk, :])
            kat_ref[c, DH:, :] = et_ref[:, c * tk:(c + 1) * tk]
        for c in range(s_len // BLOCK):
            kwt_ref[c] = transposed(kw_ref[c * BLOCK:(c + 1) * BLOCK, :])
        kct_ref[...] = transposed(kc_ref[...])
        va_ref[:, :DH] = vs_ref[...]
        va_ref[:, DH:] = ones
        wa_ref[:, :DH] = vw_ref[...]
        wa_ref[:, DH:] = ones

    for r in range(R):
        qa_ref[r * tq:(r + 1) * tq, :DH] = q_ref[:, r * DH:(r + 1) * DH]
    q = qa_ref[:, :DH]
    trow = t0 + lax.broadcasted_iota(jnp.int32, (tq, 1), 0)
    tlane = t0 + lax.broadcasted_iota(jnp.int32, (1, tq), 1)
    head = lambda a, r: a[r * tq:(r + 1) * tq]

    cend = lax.broadcasted_iota(jnp.int32, (1, ncp), 1) * CMP_STRIDE + (CMP_LEN - 1)
    valid_c = cend <= trow
    bias_c = jnp.where(valid_c, 0.0, NEG_INF)
    keep_c = jnp.where(valid_c, 1.0, 0.0)
    s = _dot(q, kct_ref[...])
    chunks_c = [slice(c * DH, (c + 1) * DH) for c in range(ncp // DH)]
    psum = [jnp.zeros((tq, DH), F32) for _ in chunks_c]
    for r in range(R):
        s_r = head(s, r) + bias_c
        m_c = jnp.broadcast_to(jnp.max(s_r, axis=-1, keepdims=True), (tq, DH))
        e = [jnp.exp2(s_r[:, cols] - m_c) * keep_c[:, cols] for cols in chunks_c]
        l_c = jnp.maximum(jnp.sum(sum(e), axis=-1, keepdims=True), 1e-30)
        l_c = jnp.broadcast_to(l_c, (tq, DH))
        for ci, cols in enumerate(chunks_c):
            p = e[ci] / l_c
            psum[ci] = psum[ci] + p
            p_ref[r * tq:(r + 1) * tq, cols] = p.astype(BF16)
    psum = jnp.concatenate(psum, axis=1)
    o_cmp = _dot(p_ref[:, :ncp], vc_ref[...])

    wlen = (-(-(WIN_LEN - 1) // BLOCK)) * BLOCK + tq
    w0 = pl.multiple_of(jnp.maximum(t0 + tq - wlen, 0), BLOCK)
    kpos = w0 + lax.broadcasted_iota(jnp.int32, (1, wlen), 1)
    bias_w = jnp.where((kpos <= trow) & (trow - kpos <= WIN_LEN - 1), 0.0, NEG_INF)
    wb = w0 // BLOCK
    kwt = jnp.concatenate([kwt_ref[wb + jb] for jb in range(wlen // BLOCK)], axis=1)
    s = _dot(q, kwt)
    for r in range(R):
        s_r = head(s, r) + bias_w
        m_w = jnp.broadcast_to(jnp.max(s_r, axis=-1, keepdims=True), (tq, DH))
        for c in range(wlen // DH):
            cols = slice(c * DH, (c + 1) * DH)
            pw_ref[r * tq:(r + 1) * tq, cols] = jnp.exp2(s_r[:, cols] - m_w).astype(BF16)
    ow = _dot(pw_ref[...], wa_ref[pl.ds(w0, wlen), :])
    o_win = ow[:, :DH] / ow[:, DH:]
    gate = gate_ref[...]
    for r in range(R):
        y_ref[r * tq:(r + 1) * tq, :] = (gate[:, r:r + 1] * head(o_cmp, r) +
                                         gate[:, 2 * R + r:2 * R + r + 1] * head(o_win, r))

    hi = psum.astype(BF16)
    rem = psum - hi.astype(F32)
    mid = rem.astype(BF16)
    lo = (rem - mid.astype(F32)).astype(BF16)
    ov = ov_ref[...]
    p_slc = (_dot(hi, ov) + _dot(mid, ov) + _dot(lo, ov)).T[:n_s]
    jj = lax.broadcasted_iota(jnp.int32, (n_s, 1), 0)
    blk_t = tlane // SEL_LEN
    forced = (jj == 0) | (jj == blk_t) | (jj == blk_t - 1)
    score = jnp.where(jj <= blk_t, p_slc + jnp.where(forced, FORCE_BONUS, 0.0), -1.0)
    nch = n_s // 8
    chunks = [score[c * 8:(c + 1) * 8] for c in range(nch)]
    cnt = [jnp.zeros((8, tq), F32) for _ in range(nch)]
    sub = lax.broadcasted_iota(jnp.int32, (8, 1), 0)
    for i in range(n_s):
        row_i = score[i:i + 1]
        for c in range(nch):
            if c * 8 > i:
                beats = row_i >= chunks[c]
            elif c * 8 + 7 < i:
                beats = row_i > chunks[c]
            else:
                beats = (row_i > chunks[c]) | ((row_i == chunks[c]) & (sub + c * 8 > i))
            cnt[c] = cnt[c] + jnp.where(beats, 1.0, 0.0)
    k_sel = float(min(SEL_TOPK, n_s))
    bias_t = jnp.concatenate([jnp.where(cc < k_sel, 0.0, NEG_INF) for cc in cnt] +
                             [jnp.zeros((DH - n_s, tq), F32)], axis=0)
    bias_q = bias_t.T.astype(BF16)
    for r in range(R):
        qa_ref[r * tq:(r + 1) * tq, DH:] = bias_q

    m_ref[...] = jnp.full(m_ref.shape, NEG_INF, F32)
    acc_ref[...] = jnp.zeros(acc_ref.shape, F32)

    def scores(kt, s_ref):
        s_ref[...] = _dot(qa_ref[...], kat_ref[kt])

    def tile(kt, bufs, causal):
        s_ref, pt_ref, al_ref = bufs
        if causal:
            kpos = kt * tk + lax.broadcasted_iota(jnp.int32, (1, tk), 1)
            bias_d = jnp.where(kpos <= trow, 0.0, NEG_INF)
        for r in range(R):
            rows = slice(r * tq, (r + 1) * tq)
            s_r = s_ref[rows, :]
            if causal:
                s_r = s_r + bias_d
            m_old = m_ref[r]
            m_new = jnp.maximum(m_old, jnp.max(s_r, axis=-1, keepdims=True))
            for c in range(tk // DH):
                cols = slice(c * DH, (c + 1) * DH)
                pt_ref[rows, cols] = jnp.exp2(s_r[:, cols] - m_new).astype(BF16)
            al_ref[r] = jnp.exp2(m_old - m_new)
            m_ref[r] = m_new
        k0 = pl.multiple_of(kt * tk, tk)
        pv = _dot(pt_ref[...], va_ref[pl.ds(k0, tk), :]).reshape(R, tq, 2 * DH)
        al = al_ref[...]
        acc_ref[:, :, :DH] = al * acc_ref[:, :, :DH] + pv[:, :, :DH]
        acc_ref[:, :, DH:] = al * acc_ref[:, :, DH:] + pv[:, :, DH:]

    even = (s0_ref, p0_ref, al0_ref)
    odd = (s1_ref, p1_ref, al1_ref)
    kd = t0 // tk
    scores(0, s0_ref)

    def pair(i, carry):
        scores(2 * i + 1, s1_ref)
        tile(2 * i, even, False)
        scores(2 * i + 2, s0_ref)
        tile(2 * i + 1, odd, False)
        return carry

    lax.fori_loop(0, kd // 2, pair, 0)

    @pl.when(kd % 2 == 0)
    def _():
        tile(kd, even, True)

    @pl.when(kd % 2 == 1)
    def _():
        scores(kd, s1_ref)
        tile(kd - 1, even, False)
        tile(kd, odd, True)

    acc = acc_ref[...]
    o_slc = (acc[:, :, :DH] / acc[:, :, DH:]).reshape(R * tq, DH)

    gate = gate_ref[...]
    for r in range(R):
        y = y_ref[r * tq:(r + 1) * tq, :] + gate[:, R + r:R + r + 1] * head(o_slc, r)
        out_ref[:, r * DH:(r + 1) * DH] = y.astype(BF16)


def _nsa(o_r, o_g, kcmp, vcmp, b, s, tq, tk):
    ncp = s // CMP_STRIDE
    n_s = s // SEL_LEN
    assert n_s % 8 == 0 and n_s <= DH
    r_view = o_r.reshape(b, s, N_R)
    g_view = o_g.reshape(b, s, N_G)
    c = np.arange(ncp)[:, None]
    j = np.arange(DH)[None, :]
    n_c = (s - CMP_LEN) // CMP_STRIDE + 1
    ov = ((c * CMP_STRIDE <= j * SEL_LEN + SEL_LEN - 1) &
          (c * CMP_STRIDE + CMP_LEN - 1 >= j * SEL_LEN) & (c < n_c) & (j < n_s))
    ov = jnp.asarray(ov.astype(np.float32), BF16)
    et = jnp.asarray((np.arange(s)[None, :] // SEL_LEN == np.arange(DH)[:, None])
                     .astype(np.float32), BF16)
    wq = NSA_R * DH
    ks_col = W_QB // DH
    vs_col = ks_col + NSA_G
    kw_col = vs_col + NSA_G
    vw_col = kw_col + NSA_G
    gate_col = (2 * D_MODEL) // DH
    wlen = (-(-(WIN_LEN - 1) // BLOCK)) * BLOCK + tq
    kv = (None, s, DH)
    out = pl.pallas_call(
        functools.partial(_nsa_kernel, tq=tq, tk=tk, s_len=s, ncp=ncp, n_s=n_s),
        grid=(b, NSA_G, s // tq),
        in_specs=[pl.BlockSpec((None, tq, wq), lambda bi, g, qi: (bi, qi, g)),
                  pl.BlockSpec(kv, lambda bi, g, qi: (bi, 0, ks_col + g)),
                  pl.BlockSpec(kv, lambda bi, g, qi: (bi, 0, vs_col + g)),
                  pl.BlockSpec(kv, lambda bi, g, qi: (bi, 0, kw_col + g)),
                  pl.BlockSpec(kv, lambda bi, g, qi: (bi, 0, vw_col + g)),
                  pl.BlockSpec((None, None, ncp, DH), lambda bi, g, qi: (bi, g, 0, 0)),
                  pl.BlockSpec((None, None, ncp, DH), lambda bi, g, qi: (bi, g, 0, 0)),
                  pl.BlockSpec((None, tq, DH), lambda bi, g, qi: (bi, qi, gate_col + g)),
                  pl.BlockSpec((ncp, DH), lambda bi, g, qi: (0, 0)),
                  pl.BlockSpec((DH, s), lambda bi, g, qi: (0, 0))],
        out_specs=pl.BlockSpec((None, tq, wq), lambda bi, g, qi: (bi, qi, g)),
        out_shape=jax.ShapeDtypeStruct((b, s, W_QB), BF16),
        scratch_shapes=[pltpu.VMEM((NSA_R * tq, 2 * DH), BF16),
                        pltpu.VMEM((s // tk, 2 * DH, tk), BF16),
                        pltpu.VMEM((s // BLOCK, DH, BLOCK), BF16),
                        pltpu.VMEM((DH, ncp), BF16),
                        pltpu.VMEM((s, 2 * DH), BF16),
                        pltpu.VMEM((s, 2 * DH), BF16),
                        pltpu.VMEM((NSA_R * tq, ncp), BF16),
                        pltpu.VMEM((NSA_R * tq, wlen), BF16),
                        pltpu.VMEM((NSA_R * tq, DH), F32),
                        pltpu.VMEM((NSA_R * tq, tk), F32),
                        pltpu.VMEM((NSA_R * tq, tk), F32),
                        pltpu.VMEM((NSA_R * tq, tk), BF16),
                        pltpu.VMEM((NSA_R * tq, tk), BF16),
                        pltpu.VMEM((NSA_R, tq, DH), F32),
                        pltpu.VMEM((NSA_R, tq, DH), F32),
                        pltpu.VMEM((NSA_R, tq, DH), F32),
                        pltpu.VMEM((NSA_R, tq, 2 * DH), F32)],
        compiler_params=_cparams(("arbitrary", "arbitrary", "arbitrary")),
        name="nsa_attention",
    )(r_view, r_view, r_view, r_view, r_view, kcmp, vcmp, g_view, ov, et)
    return out.reshape(b * s, W_QB)


def _out_kernel(ya_ref, yb_ref, ga_ref, gb_ref, x_ref, woa_ref, wob_ref, wo_ref, out_ref):
    pa = _dot(ya_ref[...], woa_ref[...])
    pb = _dot(yb_ref[...], wob_ref[...])
    y = (ga_ref[...] * pa + gb_ref[...] * pb).astype(BF16)
    out_ref[...] = x_ref[...] + _dot(y, wo_ref[...])


def _merge_out(ya, yb, o_g, x2, woa, wob, wo, layer, tm):
    m = x2.shape[0]
    row = lambda w: pl.BlockSpec((tm, w), lambda i: (i, 0))
    const = lambda shape: pl.BlockSpec((None,) + shape, lambda i: (layer, 0, 0),
                                       pipeline_mode=pl.Buffered(1))
    return pl.pallas_call(
        _out_kernel,
        grid=(m // tm,),
        in_specs=[row(DIL_OUT), row(W_QB),
                  pl.BlockSpec((tm, D_MODEL), lambda i: (i, 0)),
                  pl.BlockSpec((tm, D_MODEL), lambda i: (i, 1)),
                  row(D_MODEL),
                  const((DIL_OUT, D_MODEL)), const((W_QB, D_MODEL)), const((D_MODEL, D_MODEL))],
        out_specs=row(D_MODEL),
        out_shape=jax.ShapeDtypeStruct((m, D_MODEL), F32),
        compiler_params=_cparams(("arbitrary",)),
        name="merge_out",
    )(ya, yb, o_g, o_g, x2, woa, wob, wo)


def _ffn_kernel(x_ref, g_ref, wg_ref, wu_ref, wd_ref, gf_ref, out_ref, h_ref, *, nf, final):
    f = pl.program_id(1)

    @pl.when(f == 0)
    def _():
        x = x_ref[...]
        h_ref[...] = _rms(x, g_ref[...]).astype(BF16)
        out_ref[...] = x

    h = h_ref[...]
    a = _dot(h, wg_ref[...])
    u = _dot(h, wu_ref[...])
    act = (a * jax.nn.sigmoid(a) * u).astype(BF16)
    out_ref[...] += _dot(act, wd_ref[...])

    if final:
        @pl.when(f == nf - 1)
        def _():
            out_ref[...] = _rms(out_ref[...], gf_ref[...])


def _ffn(x2, g, wg, wu, wd, layer, gf, tm, tf, final):
    m = x2.shape[0]
    nf = D_FF // tf
    return pl.pallas_call(
        functools.partial(_ffn_kernel, nf=nf, final=final),
        grid=(m // tm, nf),
        in_specs=[pl.BlockSpec((tm, D_MODEL), lambda i, f: (i, 0)),
                  pl.BlockSpec((1, D_MODEL), lambda i, f: (0, 0)),
                  pl.BlockSpec((None, D_MODEL, tf), lambda i, f: (layer, 0, f)),
                  pl.BlockSpec((None, D_MODEL, tf), lambda i, f: (layer, 0, f)),
                  pl.BlockSpec((None, tf, D_MODEL), lambda i, f: (layer, f, 0)),
                  pl.BlockSpec((1, D_MODEL), lambda i, f: (0, 0))],
        out_specs=pl.BlockSpec((tm, D_MODEL), lambda i, f: (i, 0)),
        out_shape=jax.ShapeDtypeStruct((m, D_MODEL), F32),
        scratch_shapes=[pltpu.VMEM((tm, D_MODEL), BF16)],
        compiler_params=_cparams(("arbitrary", "arbitrary")),
        name="ffn",
    )(x2, g, wg, wu, wd, gf)


def _prep_w_gates(w):
    depth = w.shape[0]
    n_gb = NSA_Q_HEADS * 3
    gate_b = w[:, :, N_SRC:N_SRC + n_gb]
    gam = w[:, :, N_SRC + n_gb:N_SRC + n_gb + D_MODEL]
    gbm = w[:, :, N_SRC + n_gb + D_MODEL:]
    gate_b = gate_b.reshape(depth, D_MODEL, NSA_G, NSA_R, 3).transpose(0, 1, 2, 4, 3)
    gate_b = gate_b.reshape(depth, D_MODEL, NSA_G, 3 * NSA_R)
    gate_b = jnp.pad(gate_b, ((0, 0), (0, 0), (0, 0), (0, DH - 3 * NSA_R)))
    gate_b = gate_b.reshape(depth, D_MODEL, NSA_G * DH)
    gate_b = jnp.pad(gate_b, ((0, 0), (0, 0), (0, N_G - 2 * D_MODEL - NSA_G * DH)))
    return jnp.concatenate([gam, gbm, gate_b], axis=2).astype(BF16)


def kernel(x, positions, ln_mix, w_in, cmp_pos_k, cmp_pos_v, cmp_w1_k, cmp_w2_k, cmp_w1_v, cmp_w2_v,
           w_out_a, w_out_b, w_out, ln_ffn, w_ffn_gate, w_ffn_up, w_ffn_down, ln_final):
    b, s, d = x.shape
    depth = w_in.shape[0]
    assert d == D_MODEL and s % DIL_UNIT == 0
    m = b * s
    tm_in = min(2048, m)
    tm_out = min(256, m)
    tm_ffn = min(1024, m)
    tq, tk = 128, min(512, s)
    ncp = s // CMP_STRIDE

    pos_f = positions.astype(F32)
    cos, sin = _rope_tables(pos_f.reshape(m, 1), tm_in)
    blk_end = np.minimum(np.arange(ncp) * CMP_STRIDE + CMP_LEN - 1, s - 1)
    cosc, sinc = _rope_tables(pos_f[:, blk_end].reshape(b * ncp, 1), ncp)
    cosc = cosc.reshape(b, ncp, DH)
    sinc = sinc.reshape(b, ncp, DH)

    cs = np.ones((1, N_SRC), np.float32)
    cs[:, :W_QA] = Q_SCALE
    cs[:, 3 * W_QA:3 * W_QA + W_QB] = Q_SCALE
    cs = jnp.asarray(cs)

    w_in_b = w_in.astype(BF16)
    w_gate_b = _prep_w_gates(w_in)
    woa_b, wob_b, wo_b = w_out_a.astype(BF16), w_out_b.astype(BF16), w_out.astype(BF16)
    wg_b, wu_b, wd_b = w_ffn_gate.astype(BF16), w_ffn_up.astype(BF16), w_ffn_down.astype(BF16)

    x2 = x.reshape(m, d)
    for l in range(depth):
        h = _norm(x2, ln_mix[l][None, :], min(1024, m))
        o_a, o_r, o_g = _inproj(h, w_in_b, w_gate_b, l, cs, cos, sin, tm_in)
        ya = _dilated(o_a, b, s)
        kcmp, vcmp = _compress(
            o_a, b, s, cmp_pos_k[l], cmp_pos_v[l],
            cmp_w1_k[l].reshape(CMP_LEN, DH, CMP_HIDDEN).astype(BF16),
            cmp_w1_v[l].reshape(CMP_LEN, DH, CMP_HIDDEN).astype(BF16),
            cmp_w2_k[l].astype(BF16), cmp_w2_v[l].astype(BF16), cosc, sinc)
        yb = _nsa(o_r, o_g, kcmp, vcmp, b, s, tq, tk)
        x2 = _merge_out(ya, yb, o_g, x2, woa_b, wob_b, wo_b, l, tm_out)
        x2 = _ffn(x2, ln_ffn[l][None, :], wg_b, wu_b, wd_b, l, ln_final[None, :], tm_ffn, 512,
                  l == depth - 1)
    return x2.reshape(b, s, d)
```

```python
import functools
import math

import numpy as np
import jax
import jax.numpy as jnp
from jax import lax
from jax.experimental import pallas as pl
from jax.experimental.pallas import tpu as pltpu

F32 = jnp.float32
BF16 = jnp.bfloat16

D_MODEL = 2048
DH = 128
HALF = DH // 2
ROPE_THETA = 10000.0
NORM_EPS = 1e-6
NEG_INF = -1e30
BLOCK = 128

DIL_GROUPS = ((128, 1), (512, 4), (2048, 16))
DIL_HPG = 4
DIL_HEADS = DIL_HPG * len(DIL_GROUPS)
DIL_OUT = DIL_HPG * DH
DIL_UNIT = DIL_GROUPS[-1][1] * BLOCK

NSA_Q_HEADS = 16
NSA_G = 2
NSA_R = NSA_Q_HEADS // NSA_G
CMP_LEN = 32
CMP_STRIDE = 16
CMP_HIDDEN = 256
SEL_LEN = 64
SEL_TOPK = 16
WIN_LEN = 512
FORCE_BONUS = 1e4
D_FF = 5632

W_QA = DIL_HEADS * DH
W_QB = NSA_Q_HEADS * DH
W_KV = NSA_G * DH
TN_IN = 512
IN_SPLIT = 8
N_SRC = 3 * W_QA + W_QB + 6 * W_KV
N_A = 3 * W_QA + 2 * W_KV
N_R = W_QB + 4 * W_KV
N_G = 2 * D_MODEL + TN_IN
COL_KC = 3 * W_QA // DH
VMEM_LIMIT = 56 * 1024 * 1024
Q_SCALE = DH ** -0.5 * math.log2(math.e)


def _cparams(sem):
    return pltpu.CompilerParams(dimension_semantics=sem, vmem_limit_bytes=VMEM_LIMIT)


def _dot(a, b):
    return jnp.dot(a, b, preferred_element_type=F32)


def _rope_tab_kernel(pos_ref, inv_ref, sgn_ref, cos_ref, sin_ref):
    ang = pos_ref[...] * inv_ref[...]
    cos_ref[...] = jnp.cos(ang)
    sin_ref[...] = jnp.sin(ang) * sgn_ref[...]


def _rope_tables(pos_f, tm):
    m = pos_f.shape[0]
    inv = ROPE_THETA ** (-2.0 * jnp.arange(HALF, dtype=F32) / DH)
    inv = jnp.concatenate([inv, inv])[None, :]
    sgn = jnp.concatenate([-jnp.ones((HALF,), F32), jnp.ones((HALF,), F32)])[None, :]
    return pl.pallas_call(
        _rope_tab_kernel,
        grid=(m // tm,),
        in_specs=[pl.BlockSpec((tm, 1), lambda i: (i, 0)),
                  pl.BlockSpec((1, DH), lambda i: (0, 0)),
                  pl.BlockSpec((1, DH), lambda i: (0, 0))],
        out_specs=[pl.BlockSpec((tm, DH), lambda i: (i, 0)),
                   pl.BlockSpec((tm, DH), lambda i: (i, 0))],
        out_shape=[jax.ShapeDtypeStruct((m, DH), F32)] * 2,
        compiler_params=_cparams(("arbitrary",)),
        name="rope_tables",
    )(pos_f, inv, sgn)


def _rope(a, cos, sin_signed):
    return a * cos + pltpu.roll(a, HALF, 1) * sin_signed


def _rms(x, g):
    ms = jnp.mean(x * x, axis=-1, keepdims=True)
    return x * lax.rsqrt(ms + NORM_EPS) * g


T_QK = 2 * W_QA // TN_IN
T_VA = 3 * W_QA // TN_IN
T_QB = T_VA + W_QB // TN_IN
T_KC = T_QB + 2 * W_KV // TN_IN
T_SRC = N_SRC // TN_IN
T_ALL = T_SRC + N_G // TN_IN


def _norm_kernel(x_ref, g_ref, h_ref):
    h_ref[...] = _rms(x_ref[...], g_ref[...]).astype(BF16)


def _norm(x2, g, tm):
    m = x2.shape[0]
    return pl.pallas_call(
        _norm_kernel,
        grid=(m // tm,),
        in_specs=[pl.BlockSpec((tm, D_MODEL), lambda i: (i, 0)),
                  pl.BlockSpec((1, D_MODEL), lambda i: (0, 0))],
        out_specs=pl.BlockSpec((tm, D_MODEL), lambda i: (i, 0)),
        out_shape=jax.ShapeDtypeStruct((m, D_MODEL), BF16),
        compiler_params=_cparams(("arbitrary",)),
        name="rmsnorm",
    )(x2, g)


def _inproj_kernel(h_ref, w_ref, wg_ref, cs_ref, cos_ref, sin_ref, oa_ref, or_ref, og_ref):
    j = pl.program_id(1)

    tm = h_ref.shape[0]
    groups = [slice(k * (tm // IN_SPLIT), (k + 1) * (tm // IN_SPLIT)) for k in range(IN_SPLIT)]
    heads = [slice(hh * DH, (hh + 1) * DH) for hh in range(TN_IN // DH)]

    def region(out_ref, epilogue, weights=w_ref):
        accs = [_dot(h_ref[rows, :], weights[...]) for rows in groups]
        for rows, acc in zip(groups, accs):
            epilogue(out_ref, rows, acc)

    def rope_heads(out_ref, rows, acc, which):
        cos = cos_ref[rows, :]
        sin = sin_ref[rows, :]
        for sl in which:
            r = _rope(acc[:, sl], cos, sin) * cs_ref[:, sl]
            out_ref[rows, sl] = r.astype(out_ref.dtype)

    def roped(out_ref, rows, acc):
        rope_heads(out_ref, rows, acc, heads)

    def plain(out_ref, rows, acc):
        out_ref[rows, :] = acc.astype(out_ref.dtype)

    def key_value(out_ref, rows, acc):
        rope_heads(out_ref, rows, acc, heads[:len(heads) // 2])
        out_ref[rows, TN_IN // 2:] = acc[:, TN_IN // 2:].astype(out_ref.dtype)

    def sigmoid(out_ref, rows, acc):
        out_ref[rows, :] = jax.nn.sigmoid(acc)

    pl.when(j < T_QK)(lambda: region(oa_ref, roped))
    pl.when((j >= T_QK) & (j < T_VA))(lambda: region(oa_ref, plain))
    pl.when((j >= T_VA) & (j < T_QB))(lambda: region(or_ref, roped))
    pl.when((j >= T_QB) & (j < T_KC))(lambda: region(oa_ref, plain))
    pl.when((j >= T_KC) & (j < T_SRC))(lambda: region(or_ref, key_value))
    pl.when(j >= T_SRC)(lambda: region(og_ref, sigmoid, wg_ref))


def _inproj(h, w, wg, layer, cs, cos, sin, tm):
    m = h.shape[0]
    n_kc = T_KC - T_QB

    def a_tile(j):
        return jnp.where(j < T_VA, j, jnp.clip(j - T_QB, -1, n_kc - 1) + T_VA)

    def r_tile(j):
        first = jnp.clip(j - T_VA, 0, T_QB - T_VA - 1)
        return jnp.where(j < T_KC, first, jnp.minimum(j, T_SRC - 1) - T_KC + T_QB - T_VA)

    return pl.pallas_call(
        _inproj_kernel,
        grid=(m // tm, T_ALL),
        in_specs=[pl.BlockSpec((tm, D_MODEL), lambda i, j: (i, 0)),
                  pl.BlockSpec((None, D_MODEL, TN_IN),
                               lambda i, j: (layer, 0, jnp.minimum(j, T_SRC - 1))),
                  pl.BlockSpec((None, D_MODEL, TN_IN),
                               lambda i, j: (layer, 0, jnp.maximum(j - T_SRC, 0))),
                  pl.BlockSpec((1, TN_IN), lambda i, j: (0, jnp.minimum(j, T_SRC - 1))),
                  pl.BlockSpec((tm, DH), lambda i, j: (i, 0)),
                  pl.BlockSpec((tm, DH), lambda i, j: (i, 0))],
        out_specs=[pl.BlockSpec((tm, TN_IN), lambda i, j: (i, a_tile(j))),
                   pl.BlockSpec((tm, TN_IN), lambda i, j: (i, r_tile(j))),
                   pl.BlockSpec((tm, TN_IN), lambda i, j: (i, jnp.maximum(j - T_SRC, 0)))],
        out_shape=[jax.ShapeDtypeStruct((m, N_A), F32),
                   jax.ShapeDtypeStruct((m, N_R), BF16),
                   jax.ShapeDtypeStruct((m, N_G), F32)],
        compiler_params=_cparams(("arbitrary", "arbitrary")),
        name="inproj",
    )(h, w, wg, cs, cos, sin)


def _rows(start, dil):
    return pl.ds(start, BLOCK) if dil == 1 else pl.ds(start, BLOCK, stride=dil)


def _dil_kernel(*refs):
    out_ref, acc_scr, l_scr, m_scr = refs[-4:]
    u = pl.program_id(1)
    row = lax.broadcasted_iota(jnp.int32, (BLOCK, 2 * BLOCK), 0)
    col = lax.broadcasted_iota(jnp.int32, (BLOCK, 2 * BLOCK), 1)
    band = ((col < BLOCK) & (col >= row)) | ((col >= BLOCK) & (col - BLOCK <= row))
    bias = jnp.where(band, 0.0, NEG_INF)
    bias_first = jnp.where(band & ((col >= BLOCK) | (u > 0)), 0.0, NEG_INF)
    ones = jnp.ones((2 * BLOCK, DH), BF16)
    for gi, (_, dil) in enumerate(DIL_GROUPS):
        q_ref, kc_ref, kp_ref, vc_ref, vp_ref = refs[5 * gi:5 * gi + 5]
        span = BLOCK * dil
        for rho in range(dil):
            for ub in range(DIL_UNIT // span):
                cur = _rows(ub * span + rho, dil)
                if ub == 0:
                    kp = kp_ref[_rows(rho, dil), :].astype(BF16)
                    vp = vp_ref[_rows(rho, dil), :].astype(BF16)
                    b_add = bias_first
                else:
                    kp, vp, b_add = kc, vc, bias
                q = q_ref[cur, :].astype(BF16)
                kc = kc_ref[cur, :].astype(BF16)
                vc = vc_ref[cur, :].astype(BF16)
                k = jnp.concatenate([kp, kc], axis=0)
                v = jnp.concatenate([vp, vc], axis=0)
                s = lax.dot_general(q, k, (((1,), (1,)), ((), ())), preferred_element_type=F32) + b_add
                m = jnp.broadcast_to(jnp.max(s, axis=-1, keepdims=True), (BLOCK, DH))
                p = jnp.concatenate([jnp.exp2(s[:, :BLOCK] - m), jnp.exp2(s[:, BLOCK:] - m)], axis=1)
                pv = _dot(p.astype(BF16), jnp.concatenate([v, ones], axis=1))
                acc_scr[gi, cur, :] = pv[:, :DH]
                l_scr[gi, cur, :] = pv[:, DH:]
                m_scr[gi, cur, :] = m
    m0, m1, m2 = m_scr[0], m_scr[1], m_scr[2]
    m = jnp.maximum(jnp.maximum(m0, m1), m2)
    e0, e1, e2 = jnp.exp2(m0 - m), jnp.exp2(m1 - m), jnp.exp2(m2 - m)
    num = e0 * acc_scr[0] + e1 * acc_scr[1] + e2 * acc_scr[2]
    den = e0 * l_scr[0] + e1 * l_scr[1] + e2 * l_scr[2]
    out_ref[...] = (num / den).astype(BF16)


def _dilated(o_a, b, s):
    a_view = o_a.reshape(b, s, N_A)
    in_specs, args = [], []
    for gi, (_, dil) in enumerate(DIL_GROUPS):
        span = BLOCK * dil
        per = DIL_UNIT // span

        def cur(colbase, gi=gi):
            return pl.BlockSpec((None, DIL_UNIT, DH),
                                lambda bi, u, j: (bi, u, colbase + gi * DIL_HPG + j))

        def prev(colbase, gi=gi, span=span, per=per):
            return pl.BlockSpec((None, span, DH),
                                lambda bi, u, j: (bi, jnp.maximum(u * per - 1, 0),
                                                  colbase + gi * DIL_HPG + j))

        kcol, vcol = W_QA // DH, 2 * W_QA // DH
        in_specs += [cur(0), cur(kcol), prev(kcol), cur(vcol), prev(vcol)]
        args += [a_view] * 5
    out = pl.pallas_call(
        _dil_kernel,
        grid=(b, s // DIL_UNIT, DIL_HPG),
        in_specs=in_specs,
        out_specs=pl.BlockSpec((None, DIL_UNIT, DH), lambda bi, u, j: (bi, u, j)),
        out_shape=jax.ShapeDtypeStruct((b, s, DIL_OUT), BF16),
        scratch_shapes=[pltpu.VMEM((len(DIL_GROUPS), DIL_UNIT, DH), F32)] * 3,
        compiler_params=_cparams(("arbitrary", "arbitrary", "arbitrary")),
        name="dilated",
    )(*args)
    return out.reshape(b * s, DIL_OUT)


def _cmp_kernel(xk0_ref, xk1_ref, xv0_ref, xv1_ref, posk_ref, posv_ref, w1k_ref, w1v_ref,
                w2k_ref, w2v_ref, cos_ref, sin_ref, kc_ref, vc_ref, *, ncp):
    half = CMP_LEN // 2
    x_refs = ((xk0_ref, xk1_ref), (xv0_ref, xv1_ref))
    for kind in range(2):
        pos_ref, w1_ref, w2_ref, o_ref = ((posk_ref, w1k_ref, w2k_ref, kc_ref) if kind == 0
                                          else (posv_ref, w1v_ref, w2v_ref, vc_ref))
        for g in range(NSA_G):
            x_ref = x_refs[kind][g]
            a = jnp.zeros((ncp, CMP_HIDDEN), F32)
            bm = jnp.zeros((ncp, CMP_HIDDEN), F32)
            for l in range(half):
                x = x_ref[pl.ds(l, ncp, stride=CMP_STRIDE), :]
                a = a + _dot((x + pos_ref[l:l + 1, :]).astype(BF16), w1_ref[l])
                bm = bm + _dot((x + pos_ref[half + l:half + l + 1, :]).astype(BF16),
                               w1_ref[half + l])
            hid = a + pltpu.roll(bm, ncp - 1, 0)
            hid = jax.nn.gelu(hid, approximate=True).astype(BF16)
            out = _dot(hid, w2_ref[...])
            if kind == 0:
                out = _rope(out, cos_ref[...], sin_ref[...])
            o_ref[g] = out.astype(BF16)


def _compress(o_a, b, s, posk, posv, w1k, w1v, w2k, w2v, cosc, sinc):
    ncp = s // CMP_STRIDE
    x_view = o_a.reshape(b, s, N_A)
    full = lambda shape: pl.BlockSpec(shape, lambda bi: (0,) * len(shape))
    return pl.pallas_call(
        functools.partial(_cmp_kernel, ncp=ncp),
        grid=(b,),
        in_specs=[pl.BlockSpec((None, s, DH), lambda bi, c=c: (bi, 0, COL_KC + c))
                  for c in range(2 * NSA_G)] +
                 [full((CMP_LEN, DH)), full((CMP_LEN, DH)),
                  full((CMP_LEN, DH, CMP_HIDDEN)), full((CMP_LEN, DH, CMP_HIDDEN)),
                  full((CMP_HIDDEN, DH)), full((CMP_HIDDEN, DH)),
                  pl.BlockSpec((None, ncp, DH), lambda bi: (bi, 0, 0)),
                  pl.BlockSpec((None, ncp, DH), lambda bi: (bi, 0, 0))],
        out_specs=[pl.BlockSpec((None, NSA_G, ncp, DH), lambda bi: (bi, 0, 0, 0)),
                   pl.BlockSpec((None, NSA_G, ncp, DH), lambda bi: (bi, 0, 0, 0))],
        out_shape=[jax.ShapeDtypeStruct((b, NSA_G, ncp, DH), BF16)] * 2,
        compiler_params=_cparams(("arbitrary",)),
        name="nsa_compress",
    )(x_view, x_view, x_view, x_view, posk, posv, w1k, w1v, w2k, w2v, cosc, sinc)


def _nsa_kernel(q_ref, ks_ref, vs_ref, kw_ref, vw_ref, kc_ref, vc_ref, gate_ref, ov_ref, et_ref,
                out_ref, qa_ref, kat_ref, kwt_ref, kct_ref, va_ref, wa_ref, p_ref, pw_ref, y_ref,
                s0_ref, s1_ref, p0_ref, p1_ref, m_ref, al0_ref, al1_ref, acc_ref,
                *, tq, tk, s_len, ncp, n_s):
    R = NSA_R
    qi = pl.program_id(2)
    t0 = qi * tq
    transposed = lambda a: a.astype(F32).T.astype(BF16)

    @pl.when(qi == 0)
    def _():
        ones = jnp.ones((s_len, DH), BF16)
        for c in range(s_len // tk):
            kat_ref[c, :DH, :] = transposed(ks_ref[c * tk:(c + 1) * tk, :])
            kat_ref[c, DH:, :] = et_ref[:, c * tk:(c + 1) * tk]
        for c in range(s_len // BLOCK):
            kwt_ref[c] = transposed(kw_ref[c * BLOCK:(c + 1) * BLOCK, :])
        kct_ref[...] = transposed(kc_ref[...])
        va_ref[:, :DH] = vs_ref[...]
        va_ref[:, DH:] = ones
        wa_ref[:, :DH] = vw_ref[...]
        wa_ref[:, DH:] = ones

    for r in range(R):
        qa_ref[r * tq:(r + 1) * tq, :DH] = q_ref[:, r * DH:(r + 1) * DH]
    q = qa_ref[:, :DH]
    trow = t0 + lax.broadcasted_iota(jnp.int32, (tq, 1), 0)
    tlane = t0 + lax.broadcasted_iota(jnp.int32, (1, tq), 1)
    head = lambda a, r: a[r * tq:(r + 1) * tq]

    cend = lax.broadcasted_iota(jnp.int32, (1, ncp), 1) * CMP_STRIDE + (CMP_LEN - 1)
    valid_c = cend <= trow
    bias_c = jnp.where(valid_c, 0.0, NEG_INF)
    keep_c = jnp.where(valid_c, 1.0, 0.0)
    s = _dot(q, kct_ref[...])
    chunks_c = [slice(c * DH, (c + 1) * DH) for c in range(ncp // DH)]
    psum = [jnp.zeros((tq, DH), F32) for _ in chunks_c]
    for r in range(R):
        s_r = head(s, r) + bias_c
        m_c = jnp.broadcast_to(jnp.max(s_r, axis=-1, keepdims=True), (tq, DH))
        e = [jnp.exp2(s_r[:, cols] - m_c) * keep_c[:, cols] for cols in chunks_c]
        l_c = jnp.maximum(jnp.sum(sum(e), axis=-1, keepdims=True), 1e-30)
        l_c = jnp.broadcast_to(l_c, (tq, DH))
        for ci, cols in enumerate(chunks_c):
            p = e[ci] / l_c
            psum[ci] = psum[ci] + p
            p_ref[r * tq:(r + 1) * tq, cols] = p.astype(BF16)
    psum = jnp.concatenate(psum, axis=1)
    o_cmp = _dot(p_ref[:, :ncp], vc_ref[...])

    wlen = (-(-(WIN_LEN - 1) // BLOCK)) * BLOCK + tq
    w0 = pl.multiple_of(jnp.maximum(t0 + tq - wlen, 0), BLOCK)
    kpos = w0 + lax.broadcasted_iota(jnp.int32, (1, wlen), 1)
    bias_w = jnp.where((kpos <= trow) & (trow - kpos <= WIN_LEN - 1), 0.0, NEG_INF)
    wb = w0 // BLOCK
    kwt = jnp.concatenate([kwt_ref[wb + jb] for jb in range(wlen // BLOCK)], axis=1)
    s = _dot(q, kwt)
    for r in range(R):
        s_r = head(s, r) + bias_w
        m_w = jnp.broadcast_to(jnp.max(s_r, axis=-1, keepdims=True), (tq, DH))
        for c in range(wlen // DH):
            cols = slice(c * DH, (c + 1) * DH)
            pw_ref[r * tq:(r + 1) * tq, cols] = jnp.exp2(s_r[:, cols] - m_w).astype(BF16)
    ow = _dot(pw_ref[...], wa_ref[pl.ds(w0, wlen), :])
    o_win = ow[:, :DH] / ow[:, DH:]
    gate = gate_ref[...]
    for r in range(R):
        y_ref[r * tq:(r + 1) * tq, :] = (gate[:, r:r + 1] * head(o_cmp, r) +
                                         gate[:, 2 * R + r:2 * R + r + 1] * head(o_win, r))

    hi = psum.astype(BF16)
    rem = psum - hi.astype(F32)
    mid = rem.astype(BF16)
    lo = (rem - mid.astype(F32)).astype(BF16)
    ov = ov_ref[...]
    p_slc = (_dot(hi, ov) + _dot(mid, ov) + _dot(lo, ov)).T[:n_s]
    jj = lax.broadcasted_iota(jnp.int32, (n_s, 1), 0)
    blk_t = tlane // SEL_LEN
    forced = (jj == 0) | (jj == blk_t) | (jj == blk_t - 1)
    score = jnp.where(jj <= blk_t, p_slc + jnp.where(forced, FORCE_BONUS, 0.0), -1.0)
    nch = n_s // 8
    chunks = [score[c * 8:(c + 1) * 8] for c in range(nch)]
    cnt = [jnp.zeros((8, tq), F32) for _ in range(nch)]
    sub = lax.broadcasted_iota(jnp.int32, (8, 1), 0)
    for i in range(n_s):
        row_i = score[i:i + 1]
        for c in range(nch):
            if c * 8 > i:
                beats = row_i >= chunks[c]
            elif c * 8 + 7 < i:
                beats = row_i > chunks[c]
            else:
                beats = (row_i > chunks[c]) | ((row_i == chunks[c]) & (sub + c * 8 > i))
            cnt[c] = cnt[c] + jnp.where(beats, 1.0, 0.0)
    k_sel = float(min(SEL_TOPK, n_s))
    bias_t = jnp.concatenate([jnp.where(cc < k_sel, 0.0, NEG_INF) for cc in cnt] +
                             [jnp.zeros((DH - n_s, tq), F32)], axis=0)
    bias_q = bias_t.T.astype(BF16)
    for r in range(R):
        qa_ref[r * tq:(r + 1) * tq, DH:] = bias_q

    m_ref[...] = jnp.full(m_ref.shape, NEG_INF, F32)
    acc_ref[...] = jnp.zeros(acc_ref.shape, F32)

    def scores(kt, s_ref):
        s_ref[...] = _dot(qa_ref[...], kat_ref[kt])

    def tile(kt, bufs, causal):
        s_ref, pt_ref, al_ref = bufs
        if causal:
            kpos = kt * tk + lax.broadcasted_iota(jnp.int32, (1, tk), 1)
            bias_d = jnp.where(kpos <= trow, 0.0, NEG_INF)
        for r in range(R):
            rows = slice(r * tq, (r + 1) * tq)
            s_r = s_ref[rows, :]
            if causal:
                s_r = s_r + bias_d
            m_old = m_ref[r]
            m_new = jnp.maximum(m_old, jnp.max(s_r, axis=-1, keepdims=True))
            for c in range(tk // DH):
                cols = slice(c * DH, (c + 1) * DH)
                pt_ref[rows, cols] = jnp.exp2(s_r[:, cols] - m_new).astype(BF16)
            al_ref[r] = jnp.exp2(m_old - m_new)
            m_ref[r] = m_new
        k0 = pl.multiple_of(kt * tk, tk)
        pv = _dot(pt_ref[...], va_ref[pl.ds(k0, tk), :]).reshape(R, tq, 2 * DH)
        al = al_ref[...]
        acc_ref[:, :, :DH] = al * acc_ref[:, :, :DH] + pv[:, :, :DH]
        acc_ref[:, :, DH:] = al * acc_ref[:, :, DH:] + pv[:, :, DH:]

    even = (s0_ref, p0_ref, al0_ref)
    odd = (s1_ref, p1_ref, al1_ref)
    kd = t0 // tk
    scores(0, s0_ref)

    def pair(i, carry):
        scores(2 * i + 1, s1_ref)
        tile(2 * i, even, False)
        scores(2 * i + 2, s0_ref)
        tile(2 * i + 1, odd, False)
        return carry

    lax.fori_loop(0, kd // 2, pair, 0)

    @pl.when(kd % 2 == 0)
    def _():
        tile(kd, even, True)

    @pl.when(kd % 2 == 1)
    def _():
        scores(kd, s1_ref)
        tile(kd - 1, even, False)
        tile(kd, odd, True)

    acc = acc_ref[...]
    o_slc = (acc[:, :, :DH] / acc[:, :, DH:]).reshape(R * tq, DH)

    gate = gate_ref[...]
    for r in range(R):
        y = y_ref[r * tq:(r + 1) * tq, :] + gate[:, R + r:R + r + 1] * head(o_slc, r)
        out_ref[:, r * DH:(r + 1) * DH] = y.astype(BF16)


def _nsa(o_r, o_g, kcmp, vcmp, b, s, tq, tk):
    ncp = s // CMP_STRIDE
    n_s = s // SEL_LEN
    assert n_s % 8 == 0 and n_s <= DH
    r_view = o_r.reshape(b, s, N_R)
    g_view = o_g.reshape(b, s, N_G)
    c = np.arange(ncp)[:, None]
    j = np.arange(DH)[None, :]
    n_c = (s - CMP_LEN) // CMP_STRIDE + 1
    ov = ((c * CMP_STRIDE <= j * SEL_LEN + SEL_LEN - 1) &
          (c * CMP_STRIDE + CMP_LEN - 1 >= j * SEL_LEN) & (c < n_c) & (j < n_s))
    ov = jnp.asarray(ov.astype(np.float32), BF16)
    et = jnp.asarray((np.arange(s)[None, :] // SEL_LEN == np.arange(DH)[:, None])
                     .astype(np.float32), BF16)
    wq = NSA_R * DH
    ks_col = W_QB // DH
    vs_col = ks_col + NSA_G
    kw_col = vs_col + NSA_G
    vw_col = kw_col + NSA_G
    gate_col = (2 * D_MODEL) // DH
    wlen = (-(-(WIN_LEN - 1) // BLOCK)) * BLOCK + tq
    kv = (None, s, DH)
    out = pl.pallas_call(
        functools.partial(_nsa_kernel, tq=tq, tk=tk, s_len=s, ncp=ncp, n_s=n_s),
        grid=(b, NSA_G, s // tq),
        in_specs=[pl.BlockSpec((None, tq, wq), lambda bi, g, qi: (bi, qi, g)),
                  pl.BlockSpec(kv, lambda bi, g, qi: (bi, 0, ks_col + g)),
                  pl.BlockSpec(kv, lambda bi, g, qi: (bi, 0, vs_col + g)),
                  pl.BlockSpec(kv, lambda bi, g, qi: (bi, 0, kw_col + g)),
                  pl.BlockSpec(kv, lambda bi, g, qi: (bi, 0, vw_col + g)),
                  pl.BlockSpec((None, None, ncp, DH), lambda bi, g, qi: (bi, g, 0, 0)),
                  pl.BlockSpec((None, None, ncp, DH), lambda bi, g, qi: (bi, g, 0, 0)),
                  pl.BlockSpec((None, tq, DH), lambda bi, g, qi: (bi, qi, gate_col + g)),
                  pl.BlockSpec((ncp, DH), lambda bi, g, qi: (0, 0)),
                  pl.BlockSpec((DH, s), lambda bi, g, qi: (0, 0))],
        out_specs=pl.BlockSpec((None, tq, wq), lambda bi, g, qi: (bi, qi, g)),
        out_shape=jax.ShapeDtypeStruct((b, s, W_QB), BF16),
        scratch_shapes=[pltpu.VMEM((NSA_R * tq, 2 * DH), BF16),
                        pltpu.VMEM((s // tk, 2 * DH, tk), BF16),
                        pltpu.VMEM((s // BLOCK, DH, BLOCK), BF16),
                        pltpu.VMEM((DH, ncp), BF16),
                        pltpu.VMEM((s, 2 * DH), BF16),
                        pltpu.VMEM((s, 2 * DH), BF16),
                        pltpu.VMEM((NSA_R * tq, ncp), BF16),
                        pltpu.VMEM((NSA_R * tq, wlen), BF16),
                        pltpu.VMEM((NSA_R * tq, DH), F32),
                        pltpu.VMEM((NSA_R * tq, tk), F32),
                        pltpu.VMEM((NSA_R * tq, tk), F32),
                        pltpu.VMEM((NSA_R * tq, tk), BF16),
                        pltpu.VMEM((NSA_R * tq, tk), BF16),
                        pltpu.VMEM((NSA_R, tq, DH), F32),
                        pltpu.VMEM((NSA_R, tq, DH), F32),
                        pltpu.VMEM((NSA_R, tq, DH), F32),
                        pltpu.VMEM((NSA_R, tq, 2 * DH), F32)],
        compiler_params=_cparams(("arbitrary", "arbitrary", "arbitrary")),
        name="nsa_attention",
    )(r_view, r_view, r_view, r_view, r_view, kcmp, vcmp, g_view, ov, et)
    return out.reshape(b * s, W_QB)


def _out_kernel(ya_ref, yb_ref, ga_ref, gb_ref, x_ref, woa_ref, wob_ref, wo_ref, out_ref):
    pa = _dot(ya_ref[...], woa_ref[...])
    pb = _dot(yb_ref[...], wob_ref[...])
    y = (ga_ref[...] * pa + gb_ref[...] * pb).astype(BF16)
    out_ref[...] = x_ref[...] + _dot(y, wo_ref[...])


def _merge_out(ya, yb, o_g, x2, woa, wob, wo, layer, tm):
    m = x2.shape[0]
    row = lambda w: pl.BlockSpec((tm, w), lambda i: (i, 0))
    const = lambda shape: pl.BlockSpec((None,) + shape, lambda i: (layer, 0, 0),
                                       pipeline_mode=pl.Buffered(1))
    return pl.pallas_call(
        _out_kernel,
        grid=(m // tm,),
        in_specs=[row(DIL_OUT), row(W_QB),
                  pl.BlockSpec((tm, D_MODEL), lambda i: (i, 0)),
                  pl.BlockSpec((tm, D_MODEL), lambda i: (i, 1)),
                  row(D_MODEL),
                  const((DIL_OUT, D_MODEL)), const((W_QB, D_MODEL)), const((D_MODEL, D_MODEL))],
        out_specs=row(D_MODEL),
        out_shape=jax.ShapeDtypeStruct((m, D_MODEL), F32),
        compiler_params=_cparams(("arbitrary",)),
        name="merge_out",
    )(ya, yb, o_g, o_g, x2, woa, wob, wo)


def _ffn_kernel(x_ref, g_ref, wg_ref, wu_ref, wd_ref, gf_ref, out_ref, h_ref, *, nf, final):
    f = pl.program_id(1)

    @pl.when(f == 0)
    def _():
        x = x_ref[...]
        h_ref[...] = _rms(x, g_ref[...]).astype(BF16)
        out_ref[...] = x

    h = h_ref[...]
    a = _dot(h, wg_ref[...])
    u = _dot(h, wu_ref[...])
    act = (a * jax.nn.sigmoid(a) * u).astype(BF16)
    out_ref[...] += _dot(act, wd_ref[...])

    if final:
        @pl.when(f == nf - 1)
        def _():
            out_ref[...] = _rms(out_ref[...], gf_ref[...])


def _ffn(x2, g, wg, wu, wd, layer, gf, tm, tf, final):
    m = x2.shape[0]
    nf = D_FF // tf
    return pl.pallas_call(
        functools.partial(_ffn_kernel, nf=nf, final=final),
        grid=(m // tm, nf),
        in_specs=[pl.BlockSpec((tm, D_MODEL), lambda i, f: (i, 0)),
                  pl.BlockSpec((1, D_MODEL), lambda i, f: (0, 0)),
                  pl.BlockSpec((None, D_MODEL, tf), lambda i, f: (layer, 0, f)),
                  pl.BlockSpec((None, D_MODEL, tf), lambda i, f: (layer, 0, f)),
                  pl.BlockSpec((None, tf, D_MODEL), lambda i, f: (layer, f, 0)),
                  pl.BlockSpec((1, D_MODEL), lambda i, f: (0, 0))],
        out_specs=pl.BlockSpec((tm, D_MODEL), lambda i, f: (i, 0)),
        out_shape=jax.ShapeDtypeStruct((m, D_MODEL), F32),
        scratch_shapes=[pltpu.VMEM((tm, D_MODEL), BF16)],
        compiler_params=_cparams(("arbitrary", "arbitrary")),
        name="ffn",
    )(x2, g, wg, wu, wd, gf)


def _prep_w_gates(w):
    depth = w.shape[0]
    n_gb = NSA_Q_HEADS * 3
    gate_b = w[:, :, N_SRC:N_SRC + n_gb]
    gam = w[:, :, N_SRC + n_gb:N_SRC + n_gb + D_MODEL]
    gbm = w[:, :, N_SRC + n_gb + D_MODEL:]
    gate_b = gate_b.reshape(depth, D_MODEL, NSA_G, NSA_R, 3).transpose(0, 1, 2, 4, 3)
    gate_b = gate_b.reshape(depth, D_MODEL, NSA_G, 3 * NSA_R)
    gate_b = jnp.pad(gate_b, ((0, 0), (0, 0), (0, 0), (0, DH - 3 * NSA_R)))
    gate_b = gate_b.reshape(depth, D_MODEL, NSA_G * DH)
    gate_b = jnp.pad(gate_b, ((0, 0), (0, 0), (0, N_G - 2 * D_MODEL - NSA_G * DH)))
    return jnp.concatenate([gam, gbm, gate_b], axis=2).astype(BF16)


def kernel(x, positions, ln_mix, w_in, cmp_pos_k, cmp_pos_v, cmp_w1_k, cmp_w2_k, cmp_w1_v, cmp_w2_v,
           w_out_a, w_out_b, w_out, ln_ffn, w_ffn_gate, w_ffn_up, w_ffn_down, ln_final):
    b, s, d = x.shape
    depth = w_in.shape[0]
    assert d == D_MODEL and s % DIL_UNIT == 0
    m = b * s
    tm_in = min(2048, m)
    tm_out = min(256, m)
    tm_ffn = min(1024, m)
    tq, tk = 256, min(512, s)
    ncp = s // CMP_STRIDE

    pos_f = positions.astype(F32)
    cos, sin = _rope_tables(pos_f.reshape(m, 1), tm_in)
    blk_end = np.minimum(np.arange(ncp) * CMP_STRIDE + CMP_LEN - 1, s - 1)
    cosc, sinc = _rope_tables(pos_f[:, blk_end].reshape(b * ncp, 1), ncp)
    cosc = cosc.reshape(b, ncp, DH)
    sinc = sinc.reshape(b, ncp, DH)

    cs = np.ones((1, N_SRC), np.float32)
    cs[:, :W_QA] = Q_SCALE
    cs[:, 3 * W_QA:3 * W_QA + W_QB] = Q_SCALE
    cs = jnp.asarray(cs)

    w_in_b = w_in.astype(BF16)
    w_gate_b = _prep_w_gates(w_in)
    woa_b, wob_b, wo_b = w_out_a.astype(BF16), w_out_b.astype(BF16), w_out.astype(BF16)
    wg_b, wu_b, wd_b = w_ffn_gate.astype(BF16), w_ffn_up.astype(BF16), w_ffn_down.astype(BF16)

    x2 = x.reshape(m, d)
    for l in range(depth):
        h = _norm(x2, ln_mix[l][None, :], min(1024, m))
        o_a, o_r, o_g = _inproj(h, w_in_b, w_gate_b, l, cs, cos, sin, tm_in)
        ya = _dilated(o_a, b, s)
        kcmp, vcmp = _compress(
            o_a, b, s, cmp_pos_k[l], cmp_pos_v[l],
            cmp_w1_k[l].reshape(CMP_LEN, DH, CMP_HIDDEN).astype(BF16),
            cmp_w1_v[l].reshape(CMP_LEN, DH, CMP_HIDDEN).astype(BF16),
            cmp_w2_k[l].astype(BF16), cmp_w2_v[l].astype(BF16), cosc, sinc)
        yb = _nsa(o_r, o_g, kcmp, vcmp, b, s, tq, tk)
        x2 = _merge_out(ya, yb, o_g, x2, woa_b, wob_b, wo_b, l, tm_out)
        x2 = _ffn(x2, ln_ffn[l][None, :], wg_b, wu_b, wd_b, l, ln_final[None, :], tm_ffn, 512,
                  l == depth - 1)
    return x2.reshape(b, s, d)
```

```python
import functools
import math

import numpy as np
import jax
import jax.numpy as jnp
from jax import lax
from jax.experimental import pallas as pl
from jax.experimental.pallas import tpu as pltpu

F32 = jnp.float32
BF16 = jnp.bfloat16

D_MODEL = 2048
DH = 128
HALF = DH // 2
ROPE_THETA = 10000.0
NORM_EPS = 1e-6
NEG_INF = -1e30
BLOCK = 128

DIL_GROUPS = ((128, 1), (512, 4), (2048, 16))
DIL_HPG = 4
DIL_HEADS = DIL_HPG * len(DIL_GROUPS)
DIL_OUT = DIL_HPG * DH
DIL_UNIT = DIL_GROUPS[-1][1] * BLOCK

NSA_Q_HEADS = 16
NSA_G = 2
NSA_R = NSA_Q_HEADS // NSA_G
CMP_LEN = 32
CMP_STRIDE = 16
CMP_HIDDEN = 256
SEL_LEN = 64
SEL_TOPK = 16
WIN_LEN = 512
FORCE_BONUS = 1e4
D_FF = 5632

W_QA = DIL_HEADS * DH
W_QB = NSA_Q_HEADS * DH
W_KV = NSA_G * DH
TN_IN = 512
IN_SPLIT = 8
N_SRC = 3 * W_QA + W_QB + 6 * W_KV
N_A = 3 * W_QA + 2 * W_KV
N_R = W_QB + 4 * W_KV
N_G = 2 * D_MODEL + TN_IN
COL_KC = 3 * W_QA // DH
VMEM_LIMIT = 56 * 1024 * 1024
Q_SCALE = DH ** -0.5 * math.log2(math.e)


def _cparams(sem):
    return pltpu.CompilerParams(dimension_semantics=sem, vmem_limit_bytes=VMEM_LIMIT)


def _dot(a, b):
    return jnp.dot(a, b, preferred_element_type=F32)


def _rope_tab_kernel(pos_ref, inv_ref, sgn_ref, cos_ref, sin_ref):
    ang = pos_ref[...] * inv_ref[...]
    cos_ref[...] = jnp.cos(ang)
    sin_ref[...] = jnp.sin(ang) * sgn_ref[...]


def _rope_tables(pos_f, tm):
    m = pos_f.shape[0]
    inv = ROPE_THETA ** (-2.0 * jnp.arange(HALF, dtype=F32) / DH)
    inv = jnp.concatenate([inv, inv])[None, :]
    sgn = jnp.concatenate([-jnp.ones((HALF,), F32), jnp.ones((HALF,), F32)])[None, :]
    return pl.pallas_call(
        _rope_tab_kernel,
        grid=(m // tm,),
        in_specs=[pl.BlockSpec((tm, 1), lambda i: (i, 0)),
                  pl.BlockSpec((1, DH), lambda i: (0, 0)),
                  pl.BlockSpec((1, DH), lambda i: (0, 0))],
        out_specs=[pl.BlockSpec((tm, DH), lambda i: (i, 0)),
                   pl.BlockSpec((tm, DH), lambda i: (i, 0))],
        out_shape=[jax.ShapeDtypeStruct((m, DH), F32)] * 2,
        compiler_params=_cparams(("arbitrary",)),
        name="rope_tables",
    )(pos_f, inv, sgn)


def _rope(a, cos, sin_signed):
    return a * cos + pltpu.roll(a, HALF, 1) * sin_signed


def _rms(x, g):
    ms = jnp.mean(x * x, axis=-1, keepdims=True)
    return x * lax.rsqrt(ms + NORM_EPS) * g


T_QK = 2 * W_QA // TN_IN
T_VA = 3 * W_QA // TN_IN
T_QB = T_VA + W_QB // TN_IN
T_KC = T_QB + 2 * W_KV // TN_IN
T_SRC = N_SRC // TN_IN
T_ALL = T_SRC + N_G // TN_IN


def _norm_kernel(x_ref, g_ref, h_ref):
    h_ref[...] = _rms(x_ref[...], g_ref[...]).astype(BF16)


def _norm(x2, g, tm):
    m = x2.shape[0]
    return pl.pallas_call(
        _norm_kernel,
        grid=(m // tm,),
        in_specs=[pl.BlockSpec((tm, D_MODEL), lambda i: (i, 0)),
                  pl.BlockSpec((1, D_MODEL), lambda i: (0, 0))],
        out_specs=pl.BlockSpec((tm, D_MODEL), lambda i: (i, 0)),
        out_shape=jax.ShapeDtypeStruct((m, D_MODEL), BF16),
        compiler_params=_cparams(("arbitrary",)),
        name="rmsnorm",
    )(x2, g)


def _inproj_kernel(h_ref, w_ref, wg_ref, cs_ref, cos_ref, sin_ref, oa_ref, or_ref, og_ref):
    j = pl.program_id(1)

    tm = h_ref.shape[0]
    groups = [slice(k * (tm // IN_SPLIT), (k + 1) * (tm // IN_SPLIT)) for k in range(IN_SPLIT)]
    heads = [slice(hh * DH, (hh + 1) * DH) for hh in range(TN_IN // DH)]

    def region(out_ref, epilogue, weights=w_ref):
        accs = [_dot(h_ref[rows, :], weights[...]) for rows in groups]
        for rows, acc in zip(groups, accs):
            epilogue(out_ref, rows, acc)

    def rope_heads(out_ref, rows, acc, which):
        cos = cos_ref[rows, :]
        sin = sin_ref[rows, :]
        for sl in which:
            r = _rope(acc[:, sl], cos, sin) * cs_ref[:, sl]
            out_ref[rows, sl] = r.astype(out_ref.dtype)

    def roped(out_ref, rows, acc):
        rope_heads(out_ref, rows, acc, heads)

    def plain(out_ref, rows, acc):
        out_ref[rows, :] = acc.astype(out_ref.dtype)

    def key_value(out_ref, rows, acc):
        rope_heads(out_ref, rows, acc, heads[:len(heads) // 2])
        out_ref[rows, TN_IN // 2:] = acc[:, TN_IN // 2:].astype(out_ref.dtype)

    def sigmoid(out_ref, rows, acc):
        out_ref[rows, :] = jax.nn.sigmoid(acc)

    pl.when(j < T_QK)(lambda: region(oa_ref, roped))
    pl.when((j >= T_QK) & (j < T_VA))(lambda: region(oa_ref, plain))
    pl.when((j >= T_VA) & (j < T_QB))(lambda: region(or_ref, roped))
    pl.when((j >= T_QB) & (j < T_KC))(lambda: region(oa_ref, plain))
    pl.when((j >= T_KC) & (j < T_SRC))(lambda: region(or_ref, key_value))
    pl.when(j >= T_SRC)(lambda: region(og_ref, sigmoid, wg_ref))


def _inproj(h, w, wg, layer, cs, cos, sin, tm):
    m = h.shape[0]
    n_kc = T_KC - T_QB

    def a_tile(j):
        return jnp.where(j < T_VA, j, jnp.clip(j - T_QB, -1, n_kc - 1) + T_VA)

    def r_tile(j):
        first = jnp.clip(j - T_VA, 0, T_QB - T_VA - 1)
        return jnp.where(j < T_KC, first, jnp.minimum(j, T_SRC - 1) - T_KC + T_QB - T_VA)

    return pl.pallas_call(
        _inproj_kernel,
        grid=(m // tm, T_ALL),
        in_specs=[pl.BlockSpec((tm, D_MODEL), lambda i, j: (i, 0)),
                  pl.BlockSpec((None, D_MODEL, TN_IN),
                               lambda i, j: (layer, 0, jnp.minimum(j, T_SRC - 1))),
                  pl.BlockSpec((None, D_MODEL, TN_IN),
                               lambda i, j: (layer, 0, jnp.maximum(j - T_SRC, 0))),
                  pl.BlockSpec((1, TN_IN), lambda i, j: (0, jnp.minimum(j, T_SRC - 1))),
                  pl.BlockSpec((tm, DH), lambda i, j: (i, 0)),
                  pl.BlockSpec((tm, DH), lambda i, j: (i, 0))],
        out_specs=[pl.BlockSpec((tm, TN_IN), lambda i, j: (i, a_tile(j))),
                   pl.BlockSpec((tm, TN_IN), lambda i, j: (i, r_tile(j))),
                   pl.BlockSpec((tm, TN_IN), lambda i, j: (i, jnp.maximum(j - T_SRC, 0)))],
        out_shape=[jax.ShapeDtypeStruct((m, N_A), F32),
                   jax.ShapeDtypeStruct((m, N_R), BF16),
                   jax.ShapeDtypeStruct((m, N_G), F32)],
        compiler_params=_cparams(("arbitrary", "arbitrary")),
        name="inproj",
    )(h, w, wg, cs, cos, sin)


def _rows(start, dil):
    return pl.ds(start, BLOCK) if dil == 1 else pl.ds(start, BLOCK, stride=dil)


def _dil_kernel(*refs):
    out_ref, acc_scr, l_scr, m_scr = refs[-4:]
    u = pl.program_id(1)
    row = lax.broadcasted_iota(jnp.int32, (BLOCK, 2 * BLOCK), 0)
    col = lax.broadcasted_iota(jnp.int32, (BLOCK, 2 * BLOCK), 1)
    band = ((col < BLOCK) & (col >= row)) | ((col >= BLOCK) & (col - BLOCK <= row))
    bias = jnp.where(band, 0.0, NEG_INF)
    bias_first = jnp.where(band & ((col >= BLOCK) | (u > 0)), 0.0, NEG_INF)
    ones = jnp.ones((2 * BLOCK, DH), BF16)
    for gi, (_, dil) in enumerate(DIL_GROUPS):
        q_ref, kc_ref, kp_ref, vc_ref, vp_ref = refs[5 * gi:5 * gi + 5]
        span = BLOCK * dil
        for rho in range(dil):
            for ub in range(DIL_UNIT // span):
                cur = _rows(ub * span + rho, dil)
                if ub == 0:
                    kp = kp_ref[_rows(rho, dil), :].astype(BF16)
                    vp = vp_ref[_rows(rho, dil), :].astype(BF16)
                    b_add = bias_first
                else:
                    kp, vp, b_add = kc, vc, bias
                q = q_ref[cur, :].astype(BF16)
                kc = kc_ref[cur, :].astype(BF16)
                vc = vc_ref[cur, :].astype(BF16)
                k = jnp.concatenate([kp, kc], axis=0)
                v = jnp.concatenate([vp, vc], axis=0)
                s = lax.dot_general(q, k, (((1,), (1,)), ((), ())), preferred_element_type=F32) + b_add
                m = jnp.broadcast_to(jnp.max(s, axis=-1, keepdims=True), (BLOCK, DH))
                p = jnp.concatenate([jnp.exp2(s[:, :BLOCK] - m), jnp.exp2(s[:, BLOCK:] - m)], axis=1)
                pv = _dot(p.astype(BF16), jnp.concatenate([v, ones], axis=1))
                acc_scr[gi, cur, :] = pv[:, :DH]
                l_scr[gi, cur, :] = pv[:, DH:]
                m_scr[gi, cur, :] = m
    m0, m1, m2 = m_scr[0], m_scr[1], m_scr[2]
    m = jnp.maximum(jnp.maximum(m0, m1), m2)
    e0, e1, e2 = jnp.exp2(m0 - m), jnp.exp2(m1 - m), jnp.exp2(m2 - m)
    num = e0 * acc_scr[0] + e1 * acc_scr[1] + e2 * acc_scr[2]
    den = e0 * l_scr[0] + e1 * l_scr[1] + e2 * l_scr[2]
    out_ref[...] = (num / den).astype(BF16)


def _dilated(o_a, b, s):
    a_view = o_a.reshape(b, s, N_A)
    in_specs, args = [], []
    for gi, (_, dil) in enumerate(DIL_GROUPS):
        span = BLOCK * dil
        per = DIL_UNIT // span

        def cur(colbase, gi=gi):
            return pl.BlockSpec((None, DIL_UNIT, DH),
                                lambda bi, u, j: (bi, u, colbase + gi * DIL_HPG + j))

        def prev(colbase, gi=gi, span=span, per=per):
            return pl.BlockSpec((None, span, DH),
                                lambda bi, u, j: (bi, jnp.maximum(u * per - 1, 0),
                                                  colbase + gi * DIL_HPG + j))

        kcol, vcol = W_QA // DH, 2 * W_QA // DH
        in_specs += [cur(0), cur(kcol), prev(kcol), cur(vcol), prev(vcol)]
        args += [a_view] * 5
    out = pl.pallas_call(
        _dil_kernel,
        grid=(b, s // DIL_UNIT, DIL_HPG),
        in_specs=in_specs,
        out_specs=pl.BlockSpec((None, DIL_UNIT, DH), lambda bi, u, j: (bi, u, j)),
        out_shape=jax.ShapeDtypeStruct((b, s, DIL_OUT), BF16),
        scratch_shapes=[pltpu.VMEM((len(DIL_GROUPS), DIL_UNIT, DH), F32)] * 3,
        compiler_params=_cparams(("arbitrary", "arbitrary", "arbitrary")),
        name="dilated",
    )(*args)
    return out.reshape(b * s, DIL_OUT)


def _cmp_kernel(xk0_ref, xk1_ref, xv0_ref, xv1_ref, posk_ref, posv_ref, w1k_ref, w1v_ref,
                w2k_ref, w2v_ref, cos_ref, sin_ref, kc_ref, vc_ref, *, ncp):
    half = CMP_LEN // 2
    x_refs = ((xk0_ref, xk1_ref), (xv0_ref, xv1_ref))
    for kind in range(2):
        pos_ref, w1_ref, w2_ref, o_ref = ((posk_ref, w1k_ref, w2k_ref, kc_ref) if kind == 0
                                          else (posv_ref, w1v_ref, w2v_ref, vc_ref))
        for g in range(NSA_G):
            x_ref = x_refs[kind][g]
            a = jnp.zeros((ncp, CMP_HIDDEN), F32)
            bm = jnp.zeros((ncp, CMP_HIDDEN), F32)
            for l in range(half):
                x = x_ref[pl.ds(l, ncp, stride=CMP_STRIDE), :]
                a = a + _dot((x + pos_ref[l:l + 1, :]).astype(BF16), w1_ref[l])
                bm = bm + _dot((x + pos_ref[half + l:half + l + 1, :]).astype(BF16),
                               w1_ref[half + l])
            hid = a + pltpu.roll(bm, ncp - 1, 0)
            hid = jax.nn.gelu(hid, approximate=True).astype(BF16)
            out = _dot(hid, w2_ref[...])
            if kind == 0:
                out = _rope(out, cos_ref[...], sin_ref[...])
            o_ref[g] = out.astype(BF16)


def _compress(o_a, b, s, posk, posv, w1k, w1v, w2k, w2v, cosc, sinc):
    ncp = s // CMP_STRIDE
    x_view = o_a.reshape(b, s, N_A)
    full = lambda shape: pl.BlockSpec(shape, lambda bi: (0,) * len(shape))
    return pl.pallas_call(
        functools.partial(_cmp_kernel, ncp=ncp),
        grid=(b,),
        in_specs=[pl.BlockSpec((None, s, DH), lambda bi, c=c: (bi, 0, COL_KC + c))
                  for c in range(2 * NSA_G)] +
                 [full((CMP_LEN, DH)), full((CMP_LEN, DH)),
                  full((CMP_LEN, DH, CMP_HIDDEN)), full((CMP_LEN, DH, CMP_HIDDEN)),
                  full((CMP_HIDDEN, DH)), full((CMP_HIDDEN, DH)),
                  pl.BlockSpec((None, ncp, DH), lambda bi: (bi, 0, 0)),
                  pl.BlockSpec((None, ncp, DH), lambda bi: (bi, 0, 0))],
        out_specs=[pl.BlockSpec((None, NSA_G, ncp, DH), lambda bi: (bi, 0, 0, 0)),
                   pl.BlockSpec((None, NSA_G, ncp, DH), lambda bi: (bi, 0, 0, 0))],
        out_shape=[jax.ShapeDtypeStruct((b, NSA_G, ncp, DH), BF16)] * 2,
        compiler_params=_cparams(("arbitrary",)),
        name="nsa_compress",
    )(x_view, x_view, x_view, x_view, posk, posv, w1k, w1v, w2k, w2v, cosc, sinc)


def _nsa_kernel(q_ref, ks_ref, vs_ref, kw_ref, vw_ref, kc_ref, vc_ref, gate_ref, ov_ref, et_ref,
                out_ref, qa_ref, kat_ref, kwt_ref, kct_ref, va_ref, wa_ref, p_ref, pw_ref, y_ref,
                s0_ref, s1_ref, p0_ref, p1_ref, m_ref, al0_ref, al1_ref, acc_ref,
                *, tq, tk, s_len, ncp, n_s):
    R = NSA_R
    qi = pl.program_id(2)
    t0 = qi * tq
    transposed = lambda a: a.astype(F32).T.astype(BF16)

    @pl.when(qi == 0)
    def _():
        ones = jnp.ones((s_len, DH), BF16)
        for c in range(s_len // tk):
            kat_ref[c, :DH, :] = transposed(ks_ref[c * tk:(c + 1) * tk, :])
            kat_ref[c, DH:, :] = et_ref[:, c * tk:(c + 1) * tk]
        for c in range(s_len // BLOCK):
            kwt_ref[c] = transposed(kw_ref[c * BLOCK:(c + 1) * BLOCK, :])
        kct_ref[...] = transposed(kc_ref[...])
        va_ref[:, :DH] = vs_ref[...]
        va_ref[:, DH:] = ones
        wa_ref[:, :DH] = vw_ref[...]
        wa_ref[:, DH:] = ones

    for r in range(R):
        qa_ref[r * tq:(r + 1) * tq, :DH] = q_ref[:, r * DH:(r + 1) * DH]
    q = qa_ref[:, :DH]
    trow = t0 + lax.broadcasted_iota(jnp.int32, (tq, 1), 0)
    tlane = t0 + lax.broadcasted_iota(jnp.int32, (1, tq), 1)
    head = lambda a, r: a[r * tq:(r + 1) * tq]

    cend = lax.broadcasted_iota(jnp.int32, (1, ncp), 1) * CMP_STRIDE + (CMP_LEN - 1)
    valid_c = cend <= trow
    bias_c = jnp.where(valid_c, 0.0, NEG_INF)
    keep_c = jnp.where(valid_c, 1.0, 0.0)
    s = _dot(q, kct_ref[...])
    chunks_c = [slice(c * DH, (c + 1) * DH) for c in range(ncp // DH)]
    psum = [jnp.zeros((tq, DH), F32) for _ in chunks_c]
    for r in range(R):
        s_r = head(s, r) + bias_c
        m_c = jnp.broadcast_to(jnp.max(s_r, axis=-1, keepdims=True), (tq, DH))
        e = [jnp.exp2(s_r[:, cols] - m_c) * keep_c[:, cols] for cols in chunks_c]
        l_c = jnp.maximum(jnp.sum(sum(e), axis=-1, keepdims=True), 1e-30)
        l_c = jnp.broadcast_to(l_c, (tq, DH))
        for ci, cols in enumerate(chunks_c):
            p = e[ci] / l_c
            psum[ci] = psum[ci] + p
            p_ref[r * tq:(r + 1) * tq, cols] = p.astype(BF16)
    psum = jnp.concatenate(psum, axis=1)
    o_cmp = _dot(p_ref[:, :ncp], vc_ref[...])

    wlen = (-(-(WIN_LEN - 1) // BLOCK)) * BLOCK + tq
    w0 = pl.multiple_of(jnp.maximum(t0 + tq - wlen, 0), BLOCK)
    kpos = w0 + lax.broadcasted_iota(jnp.int32, (1, wlen), 1)
    bias_w = jnp.where((kpos <= trow) & (trow - kpos <= WIN_LEN - 1), 0.0, NEG_INF)
    wb = w0 // BLOCK
    kwt = jnp.concatenate([kwt_ref[wb + jb] for jb in range(wlen // BLOCK)], axis=1)
    s = _dot(q, kwt)
    for r in range(R):
        s_r = head(s, r) + bias_w
        m_w = jnp.broadcast_to(jnp.max(s_r, axis=-1, keepdims=True), (tq, DH))
        for c in range(wlen // DH):
            cols = slice(c * DH, (c + 1) * DH)
            pw_ref[r * tq:(r + 1) * tq, cols] = jnp.exp2((s_r[:, cols] - m_w).astype(BF16))
    ow = _dot(pw_ref[...], wa_ref[pl.ds(w0, wlen), :])
    o_win = ow[:, :DH] / ow[:, DH:]
    gate = gate_ref[...]
    for r in range(R):
        y_ref[r * tq:(r + 1) * tq, :] = (gate[:, r:r + 1] * head(o_cmp, r) +
                                         gate[:, 2 * R + r:2 * R + r + 1] * head(o_win, r))

    hi = psum.astype(BF16)
    rem = psum - hi.astype(F32)
    mid = rem.astype(BF16)
    lo = (rem - mid.astype(F32)).astype(BF16)
    ov = ov_ref[...]
    p_slc = (_dot(hi, ov) + _dot(mid, ov) + _dot(lo, ov)).T[:n_s]
    jj = lax.broadcasted_iota(jnp.int32, (n_s, 1), 0)
    blk_t = tlane // SEL_LEN
    forced = (jj == 0) | (jj == blk_t) | (jj == blk_t - 1)
    score = jnp.where(jj <= blk_t, p_slc + jnp.where(forced, FORCE_BONUS, 0.0), -1.0)
    nch = n_s // 8
    chunks = [score[c * 8:(c + 1) * 8] for c in range(nch)]
    cnt = [jnp.zeros((8, tq), F32) for _ in range(nch)]
    sub = lax.broadcasted_iota(jnp.int32, (8, 1), 0)
    for i in range(n_s):
        row_i = score[i:i + 1]
        for c in range(nch):
            if c * 8 > i:
                beats = row_i >= chunks[c]
            elif c * 8 + 7 < i:
                beats = row_i > chunks[c]
            else:
                beats = (row_i > chunks[c]) | ((row_i == chunks[c]) & (sub + c * 8 > i))
            cnt[c] = cnt[c] + jnp.where(beats, 1.0, 0.0)
    k_sel = float(min(SEL_TOPK, n_s))
    bias_t = jnp.concatenate([jnp.where(cc < k_sel, 0.0, NEG_INF) for cc in cnt] +
                             [jnp.zeros((DH - n_s, tq), F32)], axis=0)
    bias_q = bias_t.T.astype(BF16)
    for r in range(R):
        qa_ref[r * tq:(r + 1) * tq, DH:] = bias_q

    m_ref[...] = jnp.full(m_ref.shape, NEG_INF, F32)
    acc_ref[...] = jnp.zeros(acc_ref.shape, F32)

    def scores(kt, s_ref):
        s_ref[...] = _dot(qa_ref[...], kat_ref[kt])

    def tile(kt, bufs, causal):
        s_ref, pt_ref, al_ref = bufs
        if causal:
            kpos = kt * tk + lax.broadcasted_iota(jnp.int32, (1, tk), 1)
            bias_d = jnp.where(kpos <= trow, 0.0, NEG_INF)
        for r in range(R):
            rows = slice(r * tq, (r + 1) * tq)
            s_r = s_ref[rows, :]
            if causal:
                s_r = s_r + bias_d
            m_old = m_ref[r]
            m_new = jnp.maximum(m_old, jnp.max(s_r, axis=-1, keepdims=True))
            for c in range(tk // DH):
                cols = slice(c * DH, (c + 1) * DH)
                pt_ref[rows, cols] = jnp.exp2((s_r[:, cols] - m_new).astype(BF16))
            al_ref[r] = jnp.exp2(m_old - m_new)
            m_ref[r] = m_new
        k0 = pl.multiple_of(kt * tk, tk)
        pv = _dot(pt_ref[...], va_ref[pl.ds(k0, tk), :]).reshape(R, tq, 2 * DH)
        al = al_ref[...]
        acc_ref[:, :, :DH] = al * acc_ref[:, :, :DH] + pv[:, :, :DH]
        acc_ref[:, :, DH:] = al * acc_ref[:, :, DH:] + pv[:, :, DH:]

    even = (s0_ref, p0_ref, al0_ref)
    odd = (s1_ref, p1_ref, al1_ref)
    kd = t0 // tk
    scores(0, s0_ref)

    def pair(i, carry):
        scores(2 * i + 1, s1_ref)
        tile(2 * i, even, False)
        scores(2 * i + 2, s0_ref)
        tile(2 * i + 1, odd, False)
        return carry

    lax.fori_loop(0, kd // 2, pair, 0)

    @pl.when(kd % 2 == 0)
    def _():
        tile(kd, even, True)

    @pl.when(kd % 2 == 1)
    def _():
        scores(kd, s1_ref)
        tile(kd - 1, even, False)
        tile(kd, odd, True)

    acc = acc_ref[...]
    o_slc = (acc[:, :, :DH] / acc[:, :, DH:]).reshape(R * tq, DH)

    gate = gate_ref[...]
    for r in range(R):
        y = y_ref[r * tq:(r + 1) * tq, :] + gate[:, R + r:R + r + 1] * head(o_slc, r)
        out_ref[:, r * DH:(r + 1) * DH] = y.astype(BF16)


def _nsa(o_r, o_g, kcmp, vcmp, b, s, tq, tk):
    ncp = s // CMP_STRIDE
    n_s = s // SEL_LEN
    assert n_s % 8 == 0 and n_s <= DH
    r_view = o_r.reshape(b, s, N_R)
    g_view = o_g.reshape(b, s, N_G)
    c = np.arange(ncp)[:, None]
    j = np.arange(DH)[None, :]
    n_c = (s - CMP_LEN) // CMP_STRIDE + 1
    ov = ((c * CMP_STRIDE <= j * SEL_LEN + SEL_LEN - 1) &
          (c * CMP_STRIDE + CMP_LEN - 1 >= j * SEL_LEN) & (c < n_c) & (j < n_s))
    ov = jnp.asarray(ov.astype(np.float32), BF16)
    et = jnp.asarray((np.arange(s)[None, :] // SEL_LEN == np.arange(DH)[:, None])
                     .astype(np.float32), BF16)
    wq = NSA_R * DH
    ks_col = W_QB // DH
    vs_col = ks_col + NSA_G
    kw_col = vs_col + NSA_G
    vw_col = kw_col + NSA_G
    gate_col = (2 * D_MODEL) // DH
    wlen = (-(-(WIN_LEN - 1) // BLOCK)) * BLOCK + tq
    kv = (None, s, DH)
    out = pl.pallas_call(
        functools.partial(_nsa_kernel, tq=tq, tk=tk, s_len=s, ncp=ncp, n_s=n_s),
        grid=(b, NSA_G, s // tq),
        in_specs=[pl.BlockSpec((None, tq, wq), lambda bi, g, qi: (bi, qi, g)),
                  pl.BlockSpec(kv, lambda bi, g, qi: (bi, 0, ks_col + g)),
                  pl.BlockSpec(kv, lambda bi, g, qi: (bi, 0, vs_col + g)),
                  pl.BlockSpec(kv, lambda bi, g, qi: (bi, 0, kw_col + g)),
                  pl.BlockSpec(kv, lambda bi, g, qi: (bi, 0, vw_col + g)),
                  pl.BlockSpec((None, None, ncp, DH), lambda bi, g, qi: (bi, g, 0, 0)),
                  pl.BlockSpec((None, None, ncp, DH), lambda bi, g, qi: (bi, g, 0, 0)),
                  pl.BlockSpec((None, tq, DH), lambda bi, g, qi: (bi, qi, gate_col + g)),
                  pl.BlockSpec((ncp, DH), lambda bi, g, qi: (0, 0)),
                  pl.BlockSpec((DH, s), lambda bi, g, qi: (0, 0))],
        out_specs=pl.BlockSpec((None, tq, wq), lambda bi, g, qi: (bi, qi, g)),
        out_shape=jax.ShapeDtypeStruct((b, s, W_QB), BF16),
        scratch_shapes=[pltpu.VMEM((NSA_R * tq, 2 * DH), BF16),
                        pltpu.VMEM((s // tk, 2 * DH, tk), BF16),
                        pltpu.VMEM((s // BLOCK, DH, BLOCK), BF16),
                        pltpu.VMEM((DH, ncp), BF16),
                        pltpu.VMEM((s, 2 * DH), BF16),
                        pltpu.VMEM((s, 2 * DH), BF16),
                        pltpu.VMEM((NSA_R * tq, ncp), BF16),
                        pltpu.VMEM((NSA_R * tq, wlen), BF16),
                        pltpu.VMEM((NSA_R * tq, DH), F32),
                        pltpu.VMEM((NSA_R * tq, tk), F32),
                        pltpu.VMEM((NSA_R * tq, tk), F32),
                        pltpu.VMEM((NSA_R * tq, tk), BF16),
                        pltpu.VMEM((NSA_R * tq, tk), BF16),
                        pltpu.VMEM((NSA_R, tq, DH), F32),
                        pltpu.VMEM((NSA_R, tq, DH), F32),
                        pltpu.VMEM((NSA_R, tq, DH), F32),
                        pltpu.VMEM((NSA_R, tq, 2 * DH), F32)],
        compiler_params=_cparams(("arbitrary", "arbitrary", "arbitrary")),
        name="nsa_attention",
    )(r_view, r_view, r_view, r_view, r_view, kcmp, vcmp, g_view, ov, et)
    return out.reshape(b * s, W_QB)


def _out_kernel(ya_ref, yb_ref, ga_ref, gb_ref, x_ref, woa_ref, wob_ref, wo_ref, out_ref):
    pa = _dot(ya_ref[...], woa_ref[...])
    pb = _dot(yb_ref[...], wob_ref[...])
    y = (ga_ref[...] * pa + gb_ref[...] * pb).astype(BF16)
    out_ref[...] = x_ref[...] + _dot(y, wo_ref[...])


def _merge_out(ya, yb, o_g, x2, woa, wob, wo, layer, tm):
    m = x2.shape[0]
    row = lambda w: pl.BlockSpec((tm, w), lambda i: (i, 0))
    const = lambda shape: pl.BlockSpec((None,) + shape, lambda i: (layer, 0, 0),
                                       pipeline_mode=pl.Buffered(1))
    return pl.pallas_call(
        _out_kernel,
        grid=(m // tm,),
        in_specs=[row(DIL_OUT), row(W_QB),
                  pl.BlockSpec((tm, D_MODEL), lambda i: (i, 0)),
                  pl.BlockSpec((tm, D_MODEL), lambda i: (i, 1)),
                  row(D_MODEL),
                  const((DIL_OUT, D_MODEL)), const((W_QB, D_MODEL)), const((D_MODEL, D_MODEL))],
        out_specs=row(D_MODEL),
        out_shape=jax.ShapeDtypeStruct((m, D_MODEL), F32),
        compiler_params=_cparams(("arbitrary",)),
        name="merge_out",
    )(ya, yb, o_g, o_g, x2, woa, wob, wo)


def _ffn_kernel(x_ref, g_ref, wg_ref, wu_ref, wd_ref, gf_ref, out_ref, h_ref, *, nf, final):
    f = pl.program_id(1)

    @pl.when(f == 0)
    def _():
        x = x_ref[...]
        h_ref[...] = _rms(x, g_ref[...]).astype(BF16)
        out_ref[...] = x

    h = h_ref[...]
    a = _dot(h, wg_ref[...])
    u = _dot(h, wu_ref[...])
    act = (a * jax.nn.sigmoid(a) * u).astype(BF16)
    out_ref[...] += _dot(act, wd_ref[...])

    if final:
        @pl.when(f == nf - 1)
        def _():
            out_ref[...] = _rms(out_ref[...], gf_ref[...])


def _ffn(x2, g, wg, wu, wd, layer, gf, tm, tf, final):
    m = x2.shape[0]
    nf = D_FF // tf
    return pl.pallas_call(
        functools.partial(_ffn_kernel, nf=nf, final=final),
        grid=(m // tm, nf),
        in_specs=[pl.BlockSpec((tm, D_MODEL), lambda i, f: (i, 0)),
                  pl.BlockSpec((1, D_MODEL), lambda i, f: (0, 0)),
                  pl.BlockSpec((None, D_MODEL, tf), lambda i, f: (layer, 0, f)),
                  pl.BlockSpec((None, D_MODEL, tf), lambda i, f: (layer, 0, f)),
                  pl.BlockSpec((None, tf, D_MODEL), lambda i, f: (layer, f, 0)),
                  pl.BlockSpec((1, D_MODEL), lambda i, f: (0, 0))],
        out_specs=pl.BlockSpec((tm, D_MODEL), lambda i, f: (i, 0)),
        out_shape=jax.ShapeDtypeStruct((m, D_MODEL), F32),
        scratch_shapes=[pltpu.VMEM((tm, D_MODEL), BF16)],
        compiler_params=_cparams(("arbitrary", "arbitrary")),
        name="ffn",
    )(x2, g, wg, wu, wd, gf)


def _prep_w_gates(w):
    depth = w.shape[0]
    n_gb = NSA_Q_HEADS * 3
    gate_b = w[:, :, N_SRC:N_SRC + n_gb]
    gam = w[:, :, N_SRC + n_gb:N_SRC + n_gb + D_MODEL]
    gbm = w[:, :, N_SRC + n_gb + D_MODEL:]
    gate_b = gate_b.reshape(depth, D_MODEL, NSA_G, NSA_R, 3).transpose(0, 1, 2, 4, 3)
    gate_b = gate_b.reshape(depth, D_MODEL, NSA_G, 3 * NSA_R)
    gate_b = jnp.pad(gate_b, ((0, 0), (0, 0), (0, 0), (0, DH - 3 * NSA_R)))
    gate_b = gate_b.reshape(depth, D_MODEL, NSA_G * DH)
    gate_b = jnp.pad(gate_b, ((0, 0), (0, 0), (0, N_G - 2 * D_MODEL - NSA_G * DH)))
    return jnp.concatenate([gam, gbm, gate_b], axis=2).astype(BF16)


def kernel(x, positions, ln_mix, w_in, cmp_pos_k, cmp_pos_v, cmp_w1_k, cmp_w2_k, cmp_w1_v, cmp_w2_v,
           w_out_a, w_out_b, w_out, ln_ffn, w_ffn_gate, w_ffn_up, w_ffn_down, ln_final):
    b, s, d = x.shape
    depth = w_in.shape[0]
    assert d == D_MODEL and s % DIL_UNIT == 0
    m = b * s
    tm_in = min(2048, m)
    tm_out = min(256, m)
    tm_ffn = min(1024, m)
    tq, tk = 256, min(512, s)
    ncp = s // CMP_STRIDE

    pos_f = positions.astype(F32)
    cos, sin = _rope_tables(pos_f.reshape(m, 1), tm_in)
    blk_end = np.minimum(np.arange(ncp) * CMP_STRIDE + CMP_LEN - 1, s - 1)
    cosc, sinc = _rope_tables(pos_f[:, blk_end].reshape(b * ncp, 1), ncp)
    cosc = cosc.reshape(b, ncp, DH)
    sinc = sinc.reshape(b, ncp, DH)

    cs = np.ones((1, N_SRC), np.float32)
    cs[:, :W_QA] = Q_SCALE
    cs[:, 3 * W_QA:3 * W_QA + W_QB] = Q_SCALE
    cs = jnp.asarray(cs)

    w_in_b = w_in.astype(BF16)
    w_gate_b = _prep_w_gates(w_in)
    woa_b, wob_b, wo_b = w_out_a.astype(BF16), w_out_b.astype(BF16), w_out.astype(BF16)
    wg_b, wu_b, wd_b = w_ffn_gate.astype(BF16), w_ffn_up.astype(BF16), w_ffn_down.astype(BF16)

    x2 = x.reshape(m, d)
    for l in range(depth):
        h = _norm(x2, ln_mix[l][None, :], min(1024, m))
        o_a, o_r, o_g = _inproj(h, w_in_b, w_gate_b, l, cs, cos, sin, tm_in)
        ya = _dilated(o_a, b, s)
        kcmp, vcmp = _compress(
            o_a, b, s, cmp_pos_k[l], cmp_pos_v[l],
            cmp_w1_k[l].reshape(CMP_LEN, DH, CMP_HIDDEN).astype(BF16),
            cmp_w1_v[l].reshape(CMP_LEN, DH, CMP_HIDDEN).astype(BF16),
            cmp_w2_k[l].astype(BF16), cmp_w2_v[l].astype(BF16), cosc, sinc)
        yb = _nsa(o_r, o_g, kcmp, vcmp, b, s, tq, tk)
        x2 = _merge_out(ya, yb, o_g, x2, woa_b, wob_b, wo_b, l, tm_out)
        x2 = _ffn(x2, ln_ffn[l][None, :], wg_b, wu_b, wd_b, l, ln_final[None, :], tm_ffn, 512,
                  l == depth - 1)
    return x2.reshape(b, s, d)
```

```python
import functools
import math

import numpy as np
import jax
import jax.numpy as jnp
from jax import lax
from jax.experimental import pallas as pl
from jax.experimental.pallas import tpu as pltpu

F32 = jnp.float32
BF16 = jnp.bfloat16

D_MODEL = 2048
DH = 128
HALF = DH // 2
ROPE_THETA = 10000.0
NORM_EPS = 1e-6
NEG_INF = -1e30
BLOCK = 128

DIL_GROUPS = ((128, 1), (512, 4), (2048, 16))
DIL_HPG = 4
DIL_HEADS = DIL_HPG * len(DIL_GROUPS)
DIL_OUT = DIL_HPG * DH
DIL_UNIT = DIL_GROUPS[-1][1] * BLOCK

NSA_Q_HEADS = 16
NSA_G = 2
NSA_R = NSA_Q_HEADS // NSA_G
CMP_LEN = 32
CMP_STRIDE = 16
CMP_HIDDEN = 256
SEL_LEN = 64
SEL_TOPK = 16
WIN_LEN = 512
FORCE_BONUS = 1e4
D_FF = 5632

W_QA = DIL_HEADS * DH
W_QB = NSA_Q_HEADS * DH
W_KV = NSA_G * DH
TN_IN = 512
IN_SPLIT = 8
N_SRC = 3 * W_QA + W_QB + 6 * W_KV
N_A = 3 * W_QA + 2 * W_KV
N_R = W_QB + 4 * W_KV
N_G = 2 * D_MODEL + TN_IN
COL_KC = 3 * W_QA // DH
VMEM_LIMIT = 60 * 1024 * 1024
Q_SCALE = DH ** -0.5 * math.log2(math.e)


def _cparams(sem):
    return pltpu.CompilerParams(dimension_semantics=sem, vmem_limit_bytes=VMEM_LIMIT)


def _dot(a, b):
    return jnp.dot(a, b, preferred_element_type=F32)


def _rope_tab_kernel(pos_ref, inv_ref, sgn_ref, cos_ref, sin_ref):
    ang = pos_ref[...] * inv_ref[...]
    cos_ref[...] = jnp.cos(ang)
    sin_ref[...] = jnp.sin(ang) * sgn_ref[...]


def _rope_tables(pos_f, tm):
    m = pos_f.shape[0]
    inv = ROPE_THETA ** (-2.0 * jnp.arange(HALF, dtype=F32) / DH)
    inv = jnp.concatenate([inv, inv])[None, :]
    sgn = jnp.concatenate([-jnp.ones((HALF,), F32), jnp.ones((HALF,), F32)])[None, :]
    return pl.pallas_call(
        _rope_tab_kernel,
        grid=(m // tm,),
        in_specs=[pl.BlockSpec((tm, 1), lambda i: (i, 0)),
                  pl.BlockSpec((1, DH), lambda i: (0, 0)),
                  pl.BlockSpec((1, DH), lambda i: (0, 0))],
        out_specs=[pl.BlockSpec((tm, DH), lambda i: (i, 0)),
                   pl.BlockSpec((tm, DH), lambda i: (i, 0))],
        out_shape=[jax.ShapeDtypeStruct((m, DH), F32)] * 2,
        compiler_params=_cparams(("arbitrary",)),
        name="rope_tables",
    )(pos_f, inv, sgn)


def _rope(a, cos, sin_signed):
    return a * cos + pltpu.roll(a, HALF, 1) * sin_signed


def _rms(x, g):
    ms = jnp.mean(x * x, axis=-1, keepdims=True)
    return x * lax.rsqrt(ms + NORM_EPS) * g


T_QK = 2 * W_QA // TN_IN
T_VA = 3 * W_QA // TN_IN
T_QB = T_VA + W_QB // TN_IN
T_KC = T_QB + 2 * W_KV // TN_IN
T_SRC = N_SRC // TN_IN
T_ALL = T_SRC + N_G // TN_IN


def _norm_kernel(x_ref, g_ref, h_ref):
    h_ref[...] = _rms(x_ref[...], g_ref[...]).astype(BF16)


def _norm(x2, g, tm):
    m = x2.shape[0]
    return pl.pallas_call(
        _norm_kernel,
        grid=(m // tm,),
        in_specs=[pl.BlockSpec((tm, D_MODEL), lambda i: (i, 0)),
                  pl.BlockSpec((1, D_MODEL), lambda i: (0, 0))],
        out_specs=pl.BlockSpec((tm, D_MODEL), lambda i: (i, 0)),
        out_shape=jax.ShapeDtypeStruct((m, D_MODEL), BF16),
        compiler_params=_cparams(("arbitrary",)),
        name="rmsnorm",
    )(x2, g)


def _inproj_kernel(h_ref, w_ref, wg_ref, cs_ref, cos_ref, sin_ref, oa_ref, or_ref, og_ref):
    j = pl.program_id(1)

    tm = h_ref.shape[0]
    groups = [slice(k * (tm // IN_SPLIT), (k + 1) * (tm // IN_SPLIT)) for k in range(IN_SPLIT)]
    heads = [slice(hh * DH, (hh + 1) * DH) for hh in range(TN_IN // DH)]

    def region(out_ref, epilogue, weights=w_ref):
        accs = [_dot(h_ref[rows, :], weights[...]) for rows in groups]
        for rows, acc in zip(groups, accs):
            epilogue(out_ref, rows, acc)

    def rope_heads(out_ref, rows, acc, which):
        cos = cos_ref[rows, :]
        sin = sin_ref[rows, :]
        for sl in which:
            r = _rope(acc[:, sl], cos, sin) * cs_ref[:, sl]
            out_ref[rows, sl] = r.astype(out_ref.dtype)

    def roped(out_ref, rows, acc):
        rope_heads(out_ref, rows, acc, heads)

    def plain(out_ref, rows, acc):
        out_ref[rows, :] = acc.astype(out_ref.dtype)

    def key_value(out_ref, rows, acc):
        rope_heads(out_ref, rows, acc, heads[:len(heads) // 2])
        out_ref[rows, TN_IN // 2:] = acc[:, TN_IN // 2:].astype(out_ref.dtype)

    def sigmoid(out_ref, rows, acc):
        out_ref[rows, :] = jax.nn.sigmoid(acc)

    pl.when(j < T_QK)(lambda: region(oa_ref, roped))
    pl.when((j >= T_QK) & (j < T_VA))(lambda: region(oa_ref, plain))
    pl.when((j >= T_VA) & (j < T_QB))(lambda: region(or_ref, roped))
    pl.when((j >= T_QB) & (j < T_KC))(lambda: region(oa_ref, plain))
    pl.when((j >= T_KC) & (j < T_SRC))(lambda: region(or_ref, key_value))
    pl.when(j >= T_SRC)(lambda: region(og_ref, sigmoid, wg_ref))


def _inproj(h, w, wg, layer, cs, cos, sin, tm):
    m = h.shape[0]
    n_kc = T_KC - T_QB

    def a_tile(j):
        return jnp.where(j < T_VA, j, jnp.clip(j - T_QB, -1, n_kc - 1) + T_VA)

    def r_tile(j):
        first = jnp.clip(j - T_VA, 0, T_QB - T_VA - 1)
        return jnp.where(j < T_KC, first, jnp.minimum(j, T_SRC - 1) - T_KC + T_QB - T_VA)

    return pl.pallas_call(
        _inproj_kernel,
        grid=(m // tm, T_ALL),
        in_specs=[pl.BlockSpec((tm, D_MODEL), lambda i, j: (i, 0)),
                  pl.BlockSpec((None, D_MODEL, TN_IN),
                               lambda i, j: (layer, 0, jnp.minimum(j, T_SRC - 1))),
                  pl.BlockSpec((None, D_MODEL, TN_IN),
                               lambda i, j: (layer, 0, jnp.maximum(j - T_SRC, 0))),
                  pl.BlockSpec((1, TN_IN), lambda i, j: (0, jnp.minimum(j, T_SRC - 1))),
                  pl.BlockSpec((tm, DH), lambda i, j: (i, 0)),
                  pl.BlockSpec((tm, DH), lambda i, j: (i, 0))],
        out_specs=[pl.BlockSpec((tm, TN_IN), lambda i, j: (i, a_tile(j))),
                   pl.BlockSpec((tm, TN_IN), lambda i, j: (i, r_tile(j))),
                   pl.BlockSpec((tm, TN_IN), lambda i, j: (i, jnp.maximum(j - T_SRC, 0)))],
        out_shape=[jax.ShapeDtypeStruct((m, N_A), F32),
                   jax.ShapeDtypeStruct((m, N_R), BF16),
                   jax.ShapeDtypeStruct((m, N_G), F32)],
        compiler_params=_cparams(("arbitrary", "arbitrary")),
        name="inproj",
    )(h, w, wg, cs, cos, sin)


def _rows(start, dil):
    return pl.ds(start, BLOCK) if dil == 1 else pl.ds(start, BLOCK, stride=dil)


def _dil_kernel(*refs):
    out_ref, acc_scr, l_scr, m_scr = refs[-4:]
    u = pl.program_id(1)
    row = lax.broadcasted_iota(jnp.int32, (BLOCK, 2 * BLOCK), 0)
    col = lax.broadcasted_iota(jnp.int32, (BLOCK, 2 * BLOCK), 1)
    band = ((col < BLOCK) & (col >= row)) | ((col >= BLOCK) & (col - BLOCK <= row))
    bias = jnp.where(band, 0.0, NEG_INF)
    bias_first = jnp.where(band & ((col >= BLOCK) | (u > 0)), 0.0, NEG_INF)
    ones = jnp.ones((2 * BLOCK, DH), BF16)
    for gi, (_, dil) in enumerate(DIL_GROUPS):
        q_ref, kc_ref, kp_ref, vc_ref, vp_ref = refs[5 * gi:5 * gi + 5]
        span = BLOCK * dil
        for rho in range(dil):
            for ub in range(DIL_UNIT // span):
                cur = _rows(ub * span + rho, dil)
                if ub == 0:
                    kp = kp_ref[_rows(rho, dil), :].astype(BF16)
                    vp = vp_ref[_rows(rho, dil), :].astype(BF16)
                    b_add = bias_first
                else:
                    kp, vp, b_add = kc, vc, bias
                q = q_ref[cur, :].astype(BF16)
                kc = kc_ref[cur, :].astype(BF16)
                vc = vc_ref[cur, :].astype(BF16)
                k = jnp.concatenate([kp, kc], axis=0)
                v = jnp.concatenate([vp, vc], axis=0)
                s = lax.dot_general(q, k, (((1,), (1,)), ((), ())), preferred_element_type=F32) + b_add
                m = jnp.broadcast_to(jnp.max(s, axis=-1, keepdims=True), (BLOCK, DH))
                p = jnp.concatenate([jnp.exp2(s[:, :BLOCK] - m), jnp.exp2(s[:, BLOCK:] - m)], axis=1)
                pv = _dot(p.astype(BF16), jnp.concatenate([v, ones], axis=1))
                acc_scr[gi, cur, :] = pv[:, :DH]
                l_scr[gi, cur, :] = pv[:, DH:]
                m_scr[gi, cur, :] = m
    m0, m1, m2 = m_scr[0], m_scr[1], m_scr[2]
    m = jnp.maximum(jnp.maximum(m0, m1), m2)
    e0, e1, e2 = jnp.exp2(m0 - m), jnp.exp2(m1 - m), jnp.exp2(m2 - m)
    num = e0 * acc_scr[0] + e1 * acc_scr[1] + e2 * acc_scr[2]
    den = e0 * l_scr[0] + e1 * l_scr[1] + e2 * l_scr[2]
    out_ref[...] = (num / den).astype(BF16)


def _dilated(o_a, b, s):
    a_view = o_a.reshape(b, s, N_A)
    in_specs, args = [], []
    for gi, (_, dil) in enumerate(DIL_GROUPS):
        span = BLOCK * dil
        per = DIL_UNIT // span

        def cur(colbase, gi=gi):
            return pl.BlockSpec((None, DIL_UNIT, DH),
                                lambda bi, u, j: (bi, u, colbase + gi * DIL_HPG + j))

        def prev(colbase, gi=gi, span=span, per=per):
            return pl.BlockSpec((None, span, DH),
                                lambda bi, u, j: (bi, jnp.maximum(u * per - 1, 0),
                                                  colbase + gi * DIL_HPG + j))

        kcol, vcol = W_QA // DH, 2 * W_QA // DH
        in_specs += [cur(0), cur(kcol), prev(kcol), cur(vcol), prev(vcol)]
        args += [a_view] * 5
    out = pl.pallas_call(
        _dil_kernel,
        grid=(b, s // DIL_UNIT, DIL_HPG),
        in_specs=in_specs,
        out_specs=pl.BlockSpec((None, DIL_UNIT, DH), lambda bi, u, j: (bi, u, j)),
        out_shape=jax.ShapeDtypeStruct((b, s, DIL_OUT), BF16),
        scratch_shapes=[pltpu.VMEM((len(DIL_GROUPS), DIL_UNIT, DH), F32)] * 3,
        compiler_params=_cparams(("arbitrary", "arbitrary", "arbitrary")),
        name="dilated",
    )(*args)
    return out.reshape(b * s, DIL_OUT)


def _cmp_kernel(xk0_ref, xk1_ref, xv0_ref, xv1_ref, posk_ref, posv_ref, w1k_ref, w1v_ref,
                w2k_ref, w2v_ref, cos_ref, sin_ref, kc_ref, vc_ref, *, ncp):
    half = CMP_LEN // 2
    x_refs = ((xk0_ref, xk1_ref), (xv0_ref, xv1_ref))
    for kind in range(2):
        pos_ref, w1_ref, w2_ref, o_ref = ((posk_ref, w1k_ref, w2k_ref, kc_ref) if kind == 0
                                          else (posv_ref, w1v_ref, w2v_ref, vc_ref))
        for g in range(NSA_G):
            x_ref = x_refs[kind][g]
            a = jnp.zeros((ncp, CMP_HIDDEN), F32)
            bm = jnp.zeros((ncp, CMP_HIDDEN), F32)
            for l in range(half):
                x = x_ref[pl.ds(l, ncp, stride=CMP_STRIDE), :]
                a = a + _dot((x + pos_ref[l:l + 1, :]).astype(BF16), w1_ref[l])
                bm = bm + _dot((x + pos_ref[half + l:half + l + 1, :]).astype(BF16),
                               w1_ref[half + l])
            hid = a + pltpu.roll(bm, ncp - 1, 0)
            hid = jax.nn.gelu(hid, approximate=True).astype(BF16)
            out = _dot(hid, w2_ref[...])
            if kind == 0:
                out = _rope(out, cos_ref[...], sin_ref[...])
            o_ref[g] = out.astype(BF16)


def _compress(o_a, b, s, posk, posv, w1k, w1v, w2k, w2v, cosc, sinc):
    ncp = s // CMP_STRIDE
    x_view = o_a.reshape(b, s, N_A)
    full = lambda shape: pl.BlockSpec(shape, lambda bi: (0,) * len(shape))
    return pl.pallas_call(
        functools.partial(_cmp_kernel, ncp=ncp),
        grid=(b,),
        in_specs=[pl.BlockSpec((None, s, DH), lambda bi, c=c: (bi, 0, COL_KC + c))
                  for c in range(2 * NSA_G)] +
                 [full((CMP_LEN, DH)), full((CMP_LEN, DH)),
                  full((CMP_LEN, DH, CMP_HIDDEN)), full((CMP_LEN, DH, CMP_HIDDEN)),
                  full((CMP_HIDDEN, DH)), full((CMP_HIDDEN, DH)),
                  pl.BlockSpec((None, ncp, DH), lambda bi: (bi, 0, 0)),
                  pl.BlockSpec((None, ncp, DH), lambda bi: (bi, 0, 0))],
        out_specs=[pl.BlockSpec((None, NSA_G, ncp, DH), lambda bi: (bi, 0, 0, 0)),
                   pl.BlockSpec((None, NSA_G, ncp, DH), lambda bi: (bi, 0, 0, 0))],
        out_shape=[jax.ShapeDtypeStruct((b, NSA_G, ncp, DH), BF16)] * 2,
        compiler_params=_cparams(("arbitrary",)),
        name="nsa_compress",
    )(x_view, x_view, x_view, x_view, posk, posv, w1k, w1v, w2k, w2v, cosc, sinc)


def _nsa_kernel(q_ref, ks_ref, vs_ref, kw_ref, vw_ref, kc_ref, vc_ref, gate_ref, ov_ref, et_ref,
                out_ref, qa_ref, kat_ref, kwt_ref, kct_ref, va_ref, wa_ref, p_ref, pw_ref, y_ref,
                s0_ref, s1_ref, p0_ref, p1_ref, m_ref, al0_ref, al1_ref, acc_ref,
                *, tq, tk, s_len, ncp, n_s):
    R = NSA_R
    qi = pl.program_id(2)
    t0 = qi * tq
    transposed = lambda a: a.astype(F32).T.astype(BF16)

    @pl.when(qi == 0)
    def _():
        ones = jnp.ones((s_len, DH), BF16)
        for c in range(s_len // tk):
            kat_ref[c, :DH, :] = transposed(ks_ref[c * tk:(c + 1) * tk, :])
            kat_ref[c, DH:, :] = et_ref[:, c * tk:(c + 1) * tk]
        for c in range(s_len // BLOCK):
            kwt_ref[c] = transposed(kw_ref[c * BLOCK:(c + 1) * BLOCK, :])
        kct_ref[...] = transposed(kc_ref[...])
        va_ref[:, :DH] = vs_ref[...]
        va_ref[:, DH:] = ones
        wa_ref[:, :DH] = vw_ref[...]
        wa_ref[:, DH:] = ones

    for r in range(R):
        qa_ref[r * tq:(r + 1) * tq, :DH] = q_ref[:, r * DH:(r + 1) * DH]
    q = qa_ref[:, :DH]
    trow = t0 + lax.broadcasted_iota(jnp.int32, (tq, 1), 0)
    tlane = t0 + lax.broadcasted_iota(jnp.int32, (1, tq), 1)
    head = lambda a, r: a[r * tq:(r + 1) * tq]

    cend = lax.broadcasted_iota(jnp.int32, (1, ncp), 1) * CMP_STRIDE + (CMP_LEN - 1)
    valid_c = cend <= trow
    bias_c = jnp.where(valid_c, 0.0, NEG_INF)
    keep_c = jnp.where(valid_c, 1.0, 0.0)
    s = _dot(q, kct_ref[...])
    chunks_c = [slice(c * DH, (c + 1) * DH) for c in range(ncp // DH)]
    psum = [jnp.zeros((tq, DH), F32) for _ in chunks_c]
    for r in range(R):
        s_r = head(s, r) + bias_c
        m_c = jnp.broadcast_to(jnp.max(s_r, axis=-1, keepdims=True), (tq, DH))
        e = [jnp.exp2(s_r[:, cols] - m_c) * keep_c[:, cols] for cols in chunks_c]
        l_c = jnp.maximum(jnp.sum(sum(e), axis=-1, keepdims=True), 1e-30)
        l_c = jnp.broadcast_to(l_c, (tq, DH))
        for ci, cols in enumerate(chunks_c):
            p = e[ci] / l_c
            psum[ci] = psum[ci] + p
            p_ref[r * tq:(r + 1) * tq, cols] = p.astype(BF16)
    psum = jnp.concatenate(psum, axis=1)
    o_cmp = _dot(p_ref[:, :ncp], vc_ref[...])

    wlen = (-(-(WIN_LEN - 1) // BLOCK)) * BLOCK + tq
    w0 = pl.multiple_of(jnp.maximum(t0 + tq - wlen, 0), BLOCK)
    kpos = w0 + lax.broadcasted_iota(jnp.int32, (1, wlen), 1)
    bias_w = jnp.where((kpos <= trow) & (trow - kpos <= WIN_LEN - 1), 0.0, NEG_INF)
    wb = w0 // BLOCK
    kwt = jnp.concatenate([kwt_ref[wb + jb] for jb in range(wlen // BLOCK)], axis=1)
    s = _dot(q, kwt)
    for r in range(R):
        s_r = head(s, r) + bias_w
        m_w = jnp.broadcast_to(jnp.max(s_r, axis=-1, keepdims=True), (tq, DH))
        for c in range(wlen // DH):
            cols = slice(c * DH, (c + 1) * DH)
            pw_ref[r * tq:(r + 1) * tq, cols] = jnp.exp2(s_r[:, cols] - m_w).astype(BF16)
    ow = _dot(pw_ref[...], wa_ref[pl.ds(w0, wlen), :])
    o_win = ow[:, :DH] / ow[:, DH:]
    gate = gate_ref[...]
    for r in range(R):
        y_ref[r * tq:(r + 1) * tq, :] = (gate[:, r:r + 1] * head(o_cmp, r) +
                                         gate[:, 2 * R + r:2 * R + r + 1] * head(o_win, r))

    hi = psum.astype(BF16)
    rem = psum - hi.astype(F32)
    mid = rem.astype(BF16)
    lo = (rem - mid.astype(F32)).astype(BF16)
    ov = ov_ref[...]
    p_slc = (_dot(hi, ov) + _dot(mid, ov) + _dot(lo, ov)).T[:n_s]
    jj = lax.broadcasted_iota(jnp.int32, (n_s, 1), 0)
    blk_t = tlane // SEL_LEN
    forced = (jj == 0) | (jj == blk_t) | (jj == blk_t - 1)
    score = jnp.where(jj <= blk_t, p_slc + jnp.where(forced, FORCE_BONUS, 0.0), -1.0)
    nch = n_s // 8
    chunks = [score[c * 8:(c + 1) * 8] for c in range(nch)]
    cnt = [jnp.zeros((8, tq), F32) for _ in range(nch)]
    sub = lax.broadcasted_iota(jnp.int32, (8, 1), 0)
    for i in range(n_s):
        row_i = score[i:i + 1]
        for c in range(nch):
            if c * 8 > i:
                beats = row_i >= chunks[c]
            elif c * 8 + 7 < i:
                beats = row_i > chunks[c]
            else:
                beats = (row_i > chunks[c]) | ((row_i == chunks[c]) & (sub + c * 8 > i))
            cnt[c] = cnt[c] + jnp.where(beats, 1.0, 0.0)
    k_sel = float(min(SEL_TOPK, n_s))
    bias_t = jnp.concatenate([jnp.where(cc < k_sel, 0.0, NEG_INF) for cc in cnt] +
                             [jnp.zeros((DH - n_s, tq), F32)], axis=0)
    bias_q = bias_t.T.astype(BF16)
    for r in range(R):
        qa_ref[r * tq:(r + 1) * tq, DH:] = bias_q

    m_ref[...] = jnp.full(m_ref.shape, NEG_INF, F32)
    acc_ref[...] = jnp.zeros(acc_ref.shape, F32)

    def scores(kt, s_ref):
        s_ref[...] = _dot(qa_ref[...], kat_ref[kt])

    def tile(kt, bufs, causal):
        s_ref, pt_ref, al_ref = bufs
        if causal:
            kpos = kt * tk + lax.broadcasted_iota(jnp.int32, (1, tk), 1)
            bias_d = jnp.where(kpos <= trow, 0.0, NEG_INF)
        for r in range(R):
            rows = slice(r * tq, (r + 1) * tq)
            s_r = s_ref[rows, :]
            if causal:
                s_r = s_r + bias_d
            m_old = m_ref[r]
            m_new = jnp.maximum(m_old, jnp.max(s_r, axis=-1, keepdims=True))
            for c in range(tk // DH):
                cols = slice(c * DH, (c + 1) * DH)
                pt_ref[rows, cols] = jnp.exp2(s_r[:, cols] - m_new).astype(BF16)
            al_ref[r] = jnp.exp2(m_old - m_new)
            m_ref[r] = m_new
        k0 = pl.multiple_of(kt * tk, tk)
        pv = _dot(pt_ref[...], va_ref[pl.ds(k0, tk), :]).reshape(R, tq, 2 * DH)
        al = al_ref[...]
        acc_ref[:, :, :DH] = al * acc_ref[:, :, :DH] + pv[:, :, :DH]
        acc_ref[:, :, DH:] = al * acc_ref[:, :, DH:] + pv[:, :, DH:]

    even = (s0_ref, p0_ref, al0_ref)
    odd = (s1_ref, p1_ref, al1_ref)
    kd = t0 // tk
    scores(0, s0_ref)

    def pair(i, carry):
        scores(2 * i + 1, s1_ref)
        tile(2 * i, even, False)
        scores(2 * i + 2, s0_ref)
        tile(2 * i + 1, odd, False)
        return carry

    lax.fori_loop(0, kd // 2, pair, 0)

    @pl.when(kd % 2 == 0)
    def _():
        tile(kd, even, True)

    @pl.when(kd % 2 == 1)
    def _():
        scores(kd, s1_ref)
        tile(kd - 1, even, False)
        tile(kd, odd, True)

    acc = acc_ref[...]
    o_slc = (acc[:, :, :DH] / acc[:, :, DH:]).reshape(R * tq, DH)

    gate = gate_ref[...]
    for r in range(R):
        y = y_ref[r * tq:(r + 1) * tq, :] + gate[:, R + r:R + r + 1] * head(o_slc, r)
        out_ref[:, r * DH:(r + 1) * DH] = y.astype(BF16)


def _nsa(o_r, o_g, kcmp, vcmp, b, s, tq, tk):
    ncp = s // CMP_STRIDE
    n_s = s // SEL_LEN
    assert n_s % 8 == 0 and n_s <= DH
    r_view = o_r.reshape(b, s, N_R)
    g_view = o_g.reshape(b, s, N_G)
    c = np.arange(ncp)[:, None]
    j = np.arange(DH)[None, :]
    n_c = (s - CMP_LEN) // CMP_STRIDE + 1
    ov = ((c * CMP_STRIDE <= j * SEL_LEN + SEL_LEN - 1) &
          (c * CMP_STRIDE + CMP_LEN - 1 >= j * SEL_LEN) & (c < n_c) & (j < n_s))
    ov = jnp.asarray(ov.astype(np.float32), BF16)
    et = jnp.asarray((np.arange(s)[None, :] // SEL_LEN == np.arange(DH)[:, None])
                     .astype(np.float32), BF16)
    wq = NSA_R * DH
    ks_col = W_QB // DH
    vs_col = ks_col + NSA_G
    kw_col = vs_col + NSA_G
    vw_col = kw_col + NSA_G
    gate_col = (2 * D_MODEL) // DH
    wlen = (-(-(WIN_LEN - 1) // BLOCK)) * BLOCK + tq
    kv = (None, s, DH)
    out = pl.pallas_call(
        functools.partial(_nsa_kernel, tq=tq, tk=tk, s_len=s, ncp=ncp, n_s=n_s),
        grid=(b, NSA_G, s // tq),
        in_specs=[pl.BlockSpec((None, tq, wq), lambda bi, g, qi: (bi, qi, g)),
                  pl.BlockSpec(kv, lambda bi, g, qi: (bi, 0, ks_col + g)),
                  pl.BlockSpec(kv, lambda bi, g, qi: (bi, 0, vs_col + g)),
                  pl.BlockSpec(kv, lambda bi, g, qi: (bi, 0, kw_col + g)),
                  pl.BlockSpec(kv, lambda bi, g, qi: (bi, 0, vw_col + g)),
                  pl.BlockSpec((None, None, ncp, DH), lambda bi, g, qi: (bi, g, 0, 0)),
                  pl.BlockSpec((None, None, ncp, DH), lambda bi, g, qi: (bi, g, 0, 0)),
                  pl.BlockSpec((None, tq, DH), lambda bi, g, qi: (bi, qi, gate_col + g)),
                  pl.BlockSpec((ncp, DH), lambda bi, g, qi: (0, 0)),
                  pl.BlockSpec((DH, s), lambda bi, g, qi: (0, 0))],
        out_specs=pl.BlockSpec((None, tq, wq), lambda bi, g, qi: (bi, qi, g)),
        out_shape=jax.ShapeDtypeStruct((b, s, W_QB), BF16),
        scratch_shapes=[pltpu.VMEM((NSA_R * tq, 2 * DH), BF16),
                        pltpu.VMEM((s // tk, 2 * DH, tk), BF16),
                        pltpu.VMEM((s // BLOCK, DH, BLOCK), BF16),
                        pltpu.VMEM((DH, ncp), BF16),
                        pltpu.VMEM((s, 2 * DH), BF16),
                        pltpu.VMEM((s, 2 * DH), BF16),
                        pltpu.VMEM((NSA_R * tq, ncp), BF16),
                        pltpu.VMEM((NSA_R * tq, wlen), BF16),
                        pltpu.VMEM((NSA_R * tq, DH), F32),
                        pltpu.VMEM((NSA_R * tq, tk), F32),
                        pltpu.VMEM((NSA_R * tq, tk), F32),
                        pltpu.VMEM((NSA_R * tq, tk), BF16),
                        pltpu.VMEM((NSA_R * tq, tk), BF16),
                        pltpu.VMEM((NSA_R, tq, DH), F32),
                        pltpu.VMEM((NSA_R, tq, DH), F32),
                        pltpu.VMEM((NSA_R, tq, DH), F32),
                        pltpu.VMEM((NSA_R, tq, 2 * DH), F32)],
        compiler_params=_cparams(("arbitrary", "arbitrary", "arbitrary")),
        name="nsa_attention",
    )(r_view, r_view, r_view, r_view, r_view, kcmp, vcmp, g_view, ov, et)
    return out.reshape(b * s, W_QB)


def _out_kernel(ya_ref, yb_ref, ga_ref, gb_ref, x_ref, woa_ref, wob_ref, wo_ref, g_ref,
                out_ref, h_ref):
    pa = _dot(ya_ref[...], woa_ref[...])
    pb = _dot(yb_ref[...], wob_ref[...])
    y = (ga_ref[...] * pa + gb_ref[...] * pb).astype(BF16)
    x_new = x_ref[...] + _dot(y, wo_ref[...])
    out_ref[...] = x_new
    h_ref[...] = _rms(x_new, g_ref[...]).astype(BF16)


def _merge_out(ya, yb, o_g, x2, woa, wob, wo, layer, g_ffn, tm):
    m = x2.shape[0]
    row = lambda w: pl.BlockSpec((tm, w), lambda i: (i, 0))
    const = lambda shape: pl.BlockSpec((None,) + shape, lambda i: (layer, 0, 0),
                                       pipeline_mode=pl.Buffered(1))
    return pl.pallas_call(
        _out_kernel,
        grid=(m // tm,),
        in_specs=[row(DIL_OUT), row(W_QB),
                  pl.BlockSpec((tm, D_MODEL), lambda i: (i, 0)),
                  pl.BlockSpec((tm, D_MODEL), lambda i: (i, 1)),
                  row(D_MODEL),
                  const((DIL_OUT, D_MODEL)), const((W_QB, D_MODEL)), const((D_MODEL, D_MODEL)),
                  pl.BlockSpec((1, D_MODEL), lambda i: (0, 0))],
        out_specs=[row(D_MODEL), row(D_MODEL)],
        out_shape=[jax.ShapeDtypeStruct((m, D_MODEL), F32),
                   jax.ShapeDtypeStruct((m, D_MODEL), BF16)],
        compiler_params=_cparams(("arbitrary",)),
        name="merge_out",
    )(ya, yb, o_g, o_g, x2, woa, wob, wo, g_ffn)


def _ffn_kernel(x_ref, h_ref, wg_ref, wu_ref, wd_ref, gf_ref, out_ref, *, nf, final):
    f = pl.program_id(1)

    @pl.when(f == 0)
    def _():
        out_ref[...] = x_ref[...]

    h = h_ref[...]
    a = _dot(h, wg_ref[...])
    u = _dot(h, wu_ref[...])
    act = (a * jax.nn.sigmoid(a) * u).astype(BF16)
    out_ref[...] += _dot(act, wd_ref[...])

    if final:
        @pl.when(f == nf - 1)
        def _():
            out_ref[...] = _rms(out_ref[...], gf_ref[...])


def _ffn(x2, h, wg, wu, wd, layer, gf, tm, tf, final):
    m = x2.shape[0]
    nf = D_FF // tf
    return pl.pallas_call(
        functools.partial(_ffn_kernel, nf=nf, final=final),
        grid=(m // tm, nf),
        in_specs=[pl.BlockSpec((tm, D_MODEL), lambda i, f: (i, 0)),
                  pl.BlockSpec((tm, D_MODEL), lambda i, f: (i, 0)),
                  pl.BlockSpec((None, D_MODEL, tf), lambda i, f: (layer, 0, f)),
                  pl.BlockSpec((None, D_MODEL, tf), lambda i, f: (layer, 0, f)),
                  pl.BlockSpec((None, tf, D_MODEL), lambda i, f: (layer, f, 0)),
                  pl.BlockSpec((1, D_MODEL), lambda i, f: (0, 0))],
        out_specs=pl.BlockSpec((tm, D_MODEL), lambda i, f: (i, 0)),
        out_shape=jax.ShapeDtypeStruct((m, D_MODEL), F32),
        compiler_params=_cparams(("arbitrary", "arbitrary")),
        name="ffn",
    )(x2, h, wg, wu, wd, gf)


def _prep_w_gates(w):
    depth = w.shape[0]
    n_gb = NSA_Q_HEADS * 3
    gate_b = w[:, :, N_SRC:N_SRC + n_gb]
    gam = w[:, :, N_SRC + n_gb:N_SRC + n_gb + D_MODEL]
    gbm = w[:, :, N_SRC + n_gb + D_MODEL:]
    gate_b = gate_b.reshape(depth, D_MODEL, NSA_G, NSA_R, 3).transpose(0, 1, 2, 4, 3)
    gate_b = gate_b.reshape(depth, D_MODEL, NSA_G, 3 * NSA_R)
    gate_b = jnp.pad(gate_b, ((0, 0), (0, 0), (0, 0), (0, DH - 3 * NSA_R)))
    gate_b = gate_b.reshape(depth, D_MODEL, NSA_G * DH)
    gate_b = jnp.pad(gate_b, ((0, 0), (0, 0), (0, N_G - 2 * D_MODEL - NSA_G * DH)))
    return jnp.concatenate([gam, gbm, gate_b], axis=2).astype(BF16)


def kernel(x, positions, ln_mix, w_in, cmp_pos_k, cmp_pos_v, cmp_w1_k, cmp_w2_k, cmp_w1_v, cmp_w2_v,
           w_out_a, w_out_b, w_out, ln_ffn, w_ffn_gate, w_ffn_up, w_ffn_down, ln_final):
    b, s, d = x.shape
    depth = w_in.shape[0]
    assert d == D_MODEL and s % DIL_UNIT == 0
    m = b * s
    tm_in = min(2048, m)
    tm_out = min(256, m)
    tm_ffn = min(1024, m)
    tq, tk = 256, min(512, s)
    ncp = s // CMP_STRIDE

    pos_f = positions.astype(F32)
    cos, sin = _rope_tables(pos_f.reshape(m, 1), tm_in)
    blk_end = np.minimum(np.arange(ncp) * CMP_STRIDE + CMP_LEN - 1, s - 1)
    cosc, sinc = _rope_tables(pos_f[:, blk_end].reshape(b * ncp, 1), ncp)
    cosc = cosc.reshape(b, ncp, DH)
    sinc = sinc.reshape(b, ncp, DH)

    cs = np.ones((1, N_SRC), np.float32)
    cs[:, :W_QA] = Q_SCALE
    cs[:, 3 * W_QA:3 * W_QA + W_QB] = Q_SCALE
    cs = jnp.asarray(cs)

    w_in_b = w_in.astype(BF16)
    w_gate_b = _prep_w_gates(w_in)
    woa_b, wob_b, wo_b = w_out_a.astype(BF16), w_out_b.astype(BF16), w_out.astype(BF16)
    wg_b, wu_b, wd_b = w_ffn_gate.astype(BF16), w_ffn_up.astype(BF16), w_ffn_down.astype(BF16)

    x2 = x.reshape(m, d)
    for l in range(depth):
        h = _norm(x2, ln_mix[l][None, :], min(1024, m))
        o_a, o_r, o_g = _inproj(h, w_in_b, w_gate_b, l, cs, cos, sin, tm_in)
        ya = _dilated(o_a, b, s)
        kcmp, vcmp = _compress(
            o_a, b, s, cmp_pos_k[l], cmp_pos_v[l],
            cmp_w1_k[l].reshape(CMP_LEN, DH, CMP_HIDDEN).astype(BF16),
            cmp_w1_v[l].reshape(CMP_LEN, DH, CMP_HIDDEN).astype(BF16),
            cmp_w2_k[l].astype(BF16), cmp_w2_v[l].astype(BF16), cosc, sinc)
        yb = _nsa(o_r, o_g, kcmp, vcmp, b, s, tq, tk)
        x2, h_ffn = _merge_out(ya, yb, o_g, x2, woa_b, wob_b, wo_b, l, ln_ffn[l][None, :], tm_out)
        x2 = _ffn(x2, h_ffn, wg_b, wu_b, wd_b, l, ln_final[None, :], tm_ffn, 512, l == depth - 1)
    return x2.reshape(b, s, d)
```

```python
import functools
import math

import numpy as np
import jax
import jax.numpy as jnp
from jax import lax
from jax.experimental import pallas as pl
from jax.experimental.pallas import tpu as pltpu

F32 = jnp.float32
BF16 = jnp.bfloat16

D_MODEL = 2048
DH = 128
HALF = DH // 2
ROPE_THETA = 10000.0
NORM_EPS = 1e-6
NEG_INF = -1e30
BLOCK = 128

DIL_GROUPS = ((128, 1), (512, 4), (2048, 16))
DIL_HPG = 4
DIL_HEADS = DIL_HPG * len(DIL_GROUPS)
DIL_OUT = DIL_HPG * DH
DIL_UNIT = DIL_GROUPS[-1][1] * BLOCK

NSA_Q_HEADS = 16
NSA_G = 2
NSA_R = NSA_Q_HEADS // NSA_G
CMP_LEN = 32
CMP_STRIDE = 16
CMP_HIDDEN = 256
SEL_LEN = 64
SEL_TOPK = 16
WIN_LEN = 512
FORCE_BONUS = 1e4
D_FF = 5632

W_QA = DIL_HEADS * DH
W_QB = NSA_Q_HEADS * DH
W_KV = NSA_G * DH
TN_IN = 512
IN_SPLIT = 8
N_SRC = 3 * W_QA + W_QB + 6 * W_KV
N_A = 3 * W_QA + 2 * W_KV
N_R = W_QB + 4 * W_KV
N_G = 2 * D_MODEL + TN_IN
COL_KC = 3 * W_QA // DH
VMEM_LIMIT = 60 * 1024 * 1024
Q_SCALE = DH ** -0.5 * math.log2(math.e)


def _cparams(sem):
    return pltpu.CompilerParams(dimension_semantics=sem, vmem_limit_bytes=VMEM_LIMIT)


def _dot(a, b):
    return jnp.dot(a, b, preferred_element_type=F32)


def _rope_tab_kernel(pos_ref, inv_ref, sgn_ref, cos_ref, sin_ref):
    ang = pos_ref[...] * inv_ref[...]
    cos_ref[...] = jnp.cos(ang)
    sin_ref[...] = jnp.sin(ang) * sgn_ref[...]


def _rope_tables(pos_f, tm):
    m = pos_f.shape[0]
    inv = ROPE_THETA ** (-2.0 * jnp.arange(HALF, dtype=F32) / DH)
    inv = jnp.concatenate([inv, inv])[None, :]
    sgn = jnp.concatenate([-jnp.ones((HALF,), F32), jnp.ones((HALF,), F32)])[None, :]
    return pl.pallas_call(
        _rope_tab_kernel,
        grid=(m // tm,),
        in_specs=[pl.BlockSpec((tm, 1), lambda i: (i, 0)),
                  pl.BlockSpec((1, DH), lambda i: (0, 0)),
                  pl.BlockSpec((1, DH), lambda i: (0, 0))],
        out_specs=[pl.BlockSpec((tm, DH), lambda i: (i, 0)),
                   pl.BlockSpec((tm, DH), lambda i: (i, 0))],
        out_shape=[jax.ShapeDtypeStruct((m, DH), F32)] * 2,
        compiler_params=_cparams(("arbitrary",)),
        name="rope_tables",
    )(pos_f, inv, sgn)


def _rope(a, cos, sin_signed):
    return a * cos + pltpu.roll(a, HALF, 1) * sin_signed


def _rms(x, g):
    ms = jnp.mean(x * x, axis=-1, keepdims=True)
    return x * lax.rsqrt(ms + NORM_EPS) * g


T_QK = 2 * W_QA // TN_IN
T_VA = 3 * W_QA // TN_IN
T_QB = T_VA + W_QB // TN_IN
T_KC = T_QB + 2 * W_KV // TN_IN
T_SRC = N_SRC // TN_IN
T_ALL = T_SRC + N_G // TN_IN


def _inproj_kernel(x_hbm, g_ref, w_ref, wg_ref, cs_ref, cos_ref, sin_ref, oa_ref, or_ref, og_ref,
                   xbuf_ref, h_ref, sem):
    i = pl.program_id(0)
    j = pl.program_id(1)
    tm = h_ref.shape[0]

    def x_copy(tile):
        rows = pl.ds(pl.multiple_of(tile * tm, tm), tm)
        return pltpu.make_async_copy(x_hbm.at[rows, :], xbuf_ref, sem)

    @pl.when((i == 0) & (j == 0))
    def _():
        x_copy(0).start()

    @pl.when(j == 0)
    def _():
        x_copy(i).wait()
        h_ref[...] = _rms(xbuf_ref[...], g_ref[...]).astype(BF16)

    @pl.when((j == 1) & (i + 1 < pl.num_programs(0)))
    def _():
        x_copy(i + 1).start()

    groups = [slice(k * (tm // IN_SPLIT), (k + 1) * (tm // IN_SPLIT)) for k in range(IN_SPLIT)]
    heads = [slice(hh * DH, (hh + 1) * DH) for hh in range(TN_IN // DH)]

    def region(out_ref, epilogue, weights=w_ref):
        accs = [_dot(h_ref[rows, :], weights[...]) for rows in groups]
        for rows, acc in zip(groups, accs):
            epilogue(out_ref, rows, acc)

    def rope_heads(out_ref, rows, acc, which):
        cos = cos_ref[rows, :]
        sin = sin_ref[rows, :]
        for sl in which:
            r = _rope(acc[:, sl], cos, sin) * cs_ref[:, sl]
            out_ref[rows, sl] = r.astype(out_ref.dtype)

    def roped(out_ref, rows, acc):
        rope_heads(out_ref, rows, acc, heads)

    def plain(out_ref, rows, acc):
        out_ref[rows, :] = acc.astype(out_ref.dtype)

    def key_value(out_ref, rows, acc):
        rope_heads(out_ref, rows, acc, heads[:len(heads) // 2])
        out_ref[rows, TN_IN // 2:] = acc[:, TN_IN // 2:].astype(out_ref.dtype)

    def sigmoid(out_ref, rows, acc):
        out_ref[rows, :] = jax.nn.sigmoid(acc)

    pl.when(j < T_QK)(lambda: region(oa_ref, roped))
    pl.when((j >= T_QK) & (j < T_VA))(lambda: region(oa_ref, plain))
    pl.when((j >= T_VA) & (j < T_QB))(lambda: region(or_ref, roped))
    pl.when((j >= T_QB) & (j < T_KC))(lambda: region(oa_ref, plain))
    pl.when((j >= T_KC) & (j < T_SRC))(lambda: region(or_ref, key_value))
    pl.when(j >= T_SRC)(lambda: region(og_ref, sigmoid, wg_ref))


def _inproj(x2, g, w, wg, layer, cs, cos, sin, tm):
    m = x2.shape[0]
    n_kc = T_KC - T_QB

    def a_tile(j):
        return jnp.where(j < T_VA, j, jnp.clip(j - T_QB, -1, n_kc - 1) + T_VA)

    def r_tile(j):
        first = jnp.clip(j - T_VA, 0, T_QB - T_VA - 1)
        return jnp.where(j < T_KC, first, jnp.minimum(j, T_SRC - 1) - T_KC + T_QB - T_VA)

    return pl.pallas_call(
        _inproj_kernel,
        grid=(m // tm, T_ALL),
        in_specs=[pl.BlockSpec(memory_space=pl.ANY),
                  pl.BlockSpec((1, D_MODEL), lambda i, j: (0, 0)),
                  pl.BlockSpec((None, D_MODEL, TN_IN),
                               lambda i, j: (layer, 0, jnp.minimum(j, T_SRC - 1))),
                  pl.BlockSpec((None, D_MODEL, TN_IN),
                               lambda i, j: (layer, 0, jnp.maximum(j - T_SRC, 0))),
                  pl.BlockSpec((1, TN_IN), lambda i, j: (0, jnp.minimum(j, T_SRC - 1))),
                  pl.BlockSpec((tm, DH), lambda i, j: (i, 0)),
                  pl.BlockSpec((tm, DH), lambda i, j: (i, 0))],
        out_specs=[pl.BlockSpec((tm, TN_IN), lambda i, j: (i, a_tile(j))),
                   pl.BlockSpec((tm, TN_IN), lambda i, j: (i, r_tile(j))),
                   pl.BlockSpec((tm, TN_IN), lambda i, j: (i, jnp.maximum(j - T_SRC, 0)))],
        out_shape=[jax.ShapeDtypeStruct((m, N_A), F32),
                   jax.ShapeDtypeStruct((m, N_R), BF16),
                   jax.ShapeDtypeStruct((m, N_G), F32)],
        scratch_shapes=[pltpu.VMEM((tm, D_MODEL), F32),
                        pltpu.VMEM((tm, D_MODEL), BF16),
                        pltpu.SemaphoreType.DMA(())],
        compiler_params=_cparams(("arbitrary", "arbitrary")),
        name="inproj",
    )(x2, g, w, wg, cs, cos, sin)


def _rows(start, dil):
    return pl.ds(start, BLOCK) if dil == 1 else pl.ds(start, BLOCK, stride=dil)


def _dil_kernel(*refs):
    out_ref, acc_scr, l_scr, m_scr = refs[-4:]
    u = pl.program_id(1)
    row = lax.broadcasted_iota(jnp.int32, (BLOCK, 2 * BLOCK), 0)
    col = lax.broadcasted_iota(jnp.int32, (BLOCK, 2 * BLOCK), 1)
    band = ((col < BLOCK) & (col >= row)) | ((col >= BLOCK) & (col - BLOCK <= row))
    bias = jnp.where(band, 0.0, NEG_INF)
    bias_first = jnp.where(band & ((col >= BLOCK) | (u > 0)), 0.0, NEG_INF)
    ones = jnp.ones((2 * BLOCK, DH), BF16)
    for gi, (_, dil) in enumerate(DIL_GROUPS):
        q_ref, kc_ref, kp_ref, vc_ref, vp_ref = refs[5 * gi:5 * gi + 5]
        span = BLOCK * dil
        for rho in range(dil):
            for ub in range(DIL_UNIT // span):
                cur = _rows(ub * span + rho, dil)
                if ub == 0:
                    kp = kp_ref[_rows(rho, dil), :].astype(BF16)
                    vp = vp_ref[_rows(rho, dil), :].astype(BF16)
                    b_add = bias_first
                else:
                    kp, vp, b_add = kc, vc, bias
                q = q_ref[cur, :].astype(BF16)
                kc = kc_ref[cur, :].astype(BF16)
                vc = vc_ref[cur, :].astype(BF16)
                k = jnp.concatenate([kp, kc], axis=0)
                v = jnp.concatenate([vp, vc], axis=0)
                s = lax.dot_general(q, k, (((1,), (1,)), ((), ())), preferred_element_type=F32) + b_add
                m = jnp.broadcast_to(jnp.max(s, axis=-1, keepdims=True), (BLOCK, DH))
                p = jnp.concatenate([jnp.exp2(s[:, :BLOCK] - m), jnp.exp2(s[:, BLOCK:] - m)], axis=1)
                pv = _dot(p.astype(BF16), jnp.concatenate([v, ones], axis=1))
                acc_scr[gi, cur, :] = pv[:, :DH]
                l_scr[gi, cur, :] = pv[:, DH:]
                m_scr[gi, cur, :] = m
    m0, m1, m2 = m_scr[0], m_scr[1], m_scr[2]
    m = jnp.maximum(jnp.maximum(m0, m1), m2)
    e0, e1, e2 = jnp.exp2(m0 - m), jnp.exp2(m1 - m), jnp.exp2(m2 - m)
    num = e0 * acc_scr[0] + e1 * acc_scr[1] + e2 * acc_scr[2]
    den = e0 * l_scr[0] + e1 * l_scr[1] + e2 * l_scr[2]
    out_ref[...] = (num / den).astype(BF16)


def _dilated(o_a, b, s):
    a_view = o_a.reshape(b, s, N_A)
    in_specs, args = [], []
    for gi, (_, dil) in enumerate(DIL_GROUPS):
        span = BLOCK * dil
        per = DIL_UNIT // span

        def cur(colbase, gi=gi):
            return pl.BlockSpec((None, DIL_UNIT, DH),
                                lambda bi, u, j: (bi, u, colbase + gi * DIL_HPG + j))

        def prev(colbase, gi=gi, span=span, per=per):
            return pl.BlockSpec((None, span, DH),
                                lambda bi, u, j: (bi, jnp.maximum(u * per - 1, 0),
                                                  colbase + gi * DIL_HPG + j))

        kcol, vcol = W_QA // DH, 2 * W_QA // DH
        in_specs += [cur(0), cur(kcol), prev(kcol), cur(vcol), prev(vcol)]
        args += [a_view] * 5
    out = pl.pallas_call(
        _dil_kernel,
        grid=(b, s // DIL_UNIT, DIL_HPG),
        in_specs=in_specs,
        out_specs=pl.BlockSpec((None, DIL_UNIT, DH), lambda bi, u, j: (bi, u, j)),
        out_shape=jax.ShapeDtypeStruct((b, s, DIL_OUT), BF16),
        scratch_shapes=[pltpu.VMEM((len(DIL_GROUPS), DIL_UNIT, DH), F32)] * 3,
        compiler_params=_cparams(("arbitrary", "arbitrary", "arbitrary")),
        name="dilated",
    )(*args)
    return out.reshape(b * s, DIL_OUT)


def _cmp_kernel(xk0_ref, xk1_ref, xv0_ref, xv1_ref, posk_ref, posv_ref, w1k_ref, w1v_ref,
                w2k_ref, w2v_ref, cos_ref, sin_ref, kc_ref, vc_ref, *, ncp):
    half = CMP_LEN // 2
    x_refs = ((xk0_ref, xk1_ref), (xv0_ref, xv1_ref))
    for kind in range(2):
        pos_ref, w1_ref, w2_ref, o_ref = ((posk_ref, w1k_ref, w2k_ref, kc_ref) if kind == 0
                                          else (posv_ref, w1v_ref, w2v_ref, vc_ref))
        for g in range(NSA_G):
            x_ref = x_refs[kind][g]
            a = jnp.zeros((ncp, CMP_HIDDEN), F32)
            bm = jnp.zeros((ncp, CMP_HIDDEN), F32)
            for l in range(half):
                x = x_ref[pl.ds(l, ncp, stride=CMP_STRIDE), :]
                a = a + _dot((x + pos_ref[l:l + 1, :]).astype(BF16), w1_ref[l])
                bm = bm + _dot((x + pos_ref[half + l:half + l + 1, :]).astype(BF16),
                               w1_ref[half + l])
            hid = a + pltpu.roll(bm, ncp - 1, 0)
            hid = jax.nn.gelu(hid, approximate=True).astype(BF16)
            out = _dot(hid, w2_ref[...])
            if kind == 0:
                out = _rope(out, cos_ref[...], sin_ref[...])
            o_ref[g] = out.astype(BF16)


def _compress(o_a, b, s, posk, posv, w1k, w1v, w2k, w2v, cosc, sinc):
    ncp = s // CMP_STRIDE
    x_view = o_a.reshape(b, s, N_A)
    full = lambda shape: pl.BlockSpec(shape, lambda bi: (0,) * len(shape))
    return pl.pallas_call(
        functools.partial(_cmp_kernel, ncp=ncp),
        grid=(b,),
        in_specs=[pl.BlockSpec((None, s, DH), lambda bi, c=c: (bi, 0, COL_KC + c))
                  for c in range(2 * NSA_G)] +
                 [full((CMP_LEN, DH)), full((CMP_LEN, DH)),
                  full((CMP_LEN, DH, CMP_HIDDEN)), full((CMP_LEN, DH, CMP_HIDDEN)),
                  full((CMP_HIDDEN, DH)), full((CMP_HIDDEN, DH)),
                  pl.BlockSpec((None, ncp, DH), lambda bi: (bi, 0, 0)),
                  pl.BlockSpec((None, ncp, DH), lambda bi: (bi, 0, 0))],
        out_specs=[pl.BlockSpec((None, NSA_G, ncp, DH), lambda bi: (bi, 0, 0, 0)),
                   pl.BlockSpec((None, NSA_G, ncp, DH), lambda bi: (bi, 0, 0, 0))],
        out_shape=[jax.ShapeDtypeStruct((b, NSA_G, ncp, DH), BF16)] * 2,
        compiler_params=_cparams(("arbitrary",)),
        name="nsa_compress",
    )(x_view, x_view, x_view, x_view, posk, posv, w1k, w1v, w2k, w2v, cosc, sinc)


def _nsa_kernel(q_ref, ks_ref, vs_ref, kw_ref, vw_ref, kc_ref, vc_ref, gate_ref, ov_ref, et_ref,
                out_ref, qa_ref, kat_ref, kwt_ref, kct_ref, va_ref, wa_ref, p_ref, pw_ref, y_ref,
                s0_ref, s1_ref, p0_ref, p1_ref, m_ref, al0_ref, al1_ref, acc_ref,
                *, tq, tk, s_len, ncp, n_s):
    R = NSA_R
    qi = pl.program_id(2)
    t0 = qi * tq
    transposed = lambda a: a.astype(F32).T.astype(BF16)

    @pl.when(qi == 0)
    def _():
        ones = jnp.ones((s_len, DH), BF16)
        for c in range(s_len // tk):
            kat_ref[c, :DH, :] = transposed(ks_ref[c * tk:(c + 1) * tk, :])
            kat_ref[c, DH:, :] = et_ref[:, c * tk:(c + 1) * tk]
        for c in range(s_len // BLOCK):
            kwt_ref[c] = transposed(kw_ref[c * BLOCK:(c + 1) * BLOCK, :])
        kct_ref[...] = transposed(kc_ref[...])
        va_ref[:, :DH] = vs_ref[...]
        va_ref[:, DH:] = ones
        wa_ref[:, :DH] = vw_ref[...]
        wa_ref[:, DH:] = ones

    for r in range(R):
        qa_ref[r * tq:(r + 1) * tq, :DH] = q_ref[:, r * DH:(r + 1) * DH]
    q = qa_ref[:, :DH]
    trow = t0 + lax.broadcasted_iota(jnp.int32, (tq, 1), 0)
    tlane = t0 + lax.broadcasted_iota(jnp.int32, (1, tq), 1)
    head = lambda a, r: a[r * tq:(r + 1) * tq]

    cend = lax.broadcasted_iota(jnp.int32, (1, ncp), 1) * CMP_STRIDE + (CMP_LEN - 1)
    valid_c = cend <= trow
    bias_c = jnp.where(valid_c, 0.0, NEG_INF)
    keep_c = jnp.where(valid_c, 1.0, 0.0)
    s = _dot(q, kct_ref[...])
    chunks_c = [slice(c * DH, (c + 1) * DH) for c in range(ncp // DH)]
    psum = [jnp.zeros((tq, DH), F32) for _ in chunks_c]
    for r in range(R):
        s_r = head(s, r) + bias_c
        m_c = jnp.broadcast_to(jnp.max(s_r, axis=-1, keepdims=True), (tq, DH))
        e = [jnp.exp2(s_r[:, cols] - m_c) * keep_c[:, cols] for cols in chunks_c]
        l_c = jnp.maximum(jnp.sum(sum(e), axis=-1, keepdims=True), 1e-30)
        l_c = jnp.broadcast_to(l_c, (tq, DH))
        for ci, cols in enumerate(chunks_c):
            p = e[ci] / l_c
            psum[ci] = psum[ci] + p
            p_ref[r * tq:(r + 1) * tq, cols] = p.astype(BF16)
    psum = jnp.concatenate(psum, axis=1)
    o_cmp = _dot(p_ref[:, :ncp], vc_ref[...])

    wlen = (-(-(WIN_LEN - 1) // BLOCK)) * BLOCK + tq
    w0 = pl.multiple_of(jnp.maximum(t0 + tq - wlen, 0), BLOCK)
    kpos = w0 + lax.broadcasted_iota(jnp.int32, (1, wlen), 1)
    bias_w = jnp.where((kpos <= trow) & (trow - kpos <= WIN_LEN - 1), 0.0, NEG_INF)
    wb = w0 // BLOCK
    kwt = jnp.concatenate([kwt_ref[wb + jb] for jb in range(wlen // BLOCK)], axis=1)
    s = _dot(q, kwt)
    for r in range(R):
        s_r = head(s, r) + bias_w
        m_w = jnp.broadcast_to(jnp.max(s_r, axis=-1, keepdims=True), (tq, DH))
        for c in range(wlen // DH):
            cols = slice(c * DH, (c + 1) * DH)
            pw_ref[r * tq:(r + 1) * tq, cols] = jnp.exp2(s_r[:, cols] - m_w).astype(BF16)
    ow = _dot(pw_ref[...], wa_ref[pl.ds(w0, wlen), :])
    o_win = ow[:, :DH] / ow[:, DH:]
    gate = gate_ref[...]
    for r in range(R):
        y_ref[r * tq:(r + 1) * tq, :] = (gate[:, r:r + 1] * head(o_cmp, r) +
                                         gate[:, 2 * R + r:2 * R + r + 1] * head(o_win, r))

    hi = psum.astype(BF16)
    rem = psum - hi.astype(F32)
    mid = rem.astype(BF16)
    lo = (rem - mid.astype(F32)).astype(BF16)
    ov = ov_ref[...]
    p_slc = (_dot(hi, ov) + _dot(mid, ov) + _dot(lo, ov)).T[:n_s]
    jj = lax.broadcasted_iota(jnp.int32, (n_s, 1), 0)
    blk_t = tlane // SEL_LEN
    forced = (jj == 0) | (jj == blk_t) | (jj == blk_t - 1)
    score = jnp.where(jj <= blk_t, p_slc + jnp.where(forced, FORCE_BONUS, 0.0), -1.0)
    nch = n_s // 8
    chunks = [score[c * 8:(c + 1) * 8] for c in range(nch)]
    cnt = [jnp.zeros((8, tq), F32) for _ in range(nch)]
    sub = lax.broadcasted_iota(jnp.int32, (8, 1), 0)
    for i in range(n_s):
        row_i = score[i:i + 1]
        for c in range(nch):
            if c * 8 > i:
                beats = row_i >= chunks[c]
            elif c * 8 + 7 < i:
                beats = row_i > chunks[c]
            else:
                beats = (row_i > chunks[c]) | ((row_i == chunks[c]) & (sub + c * 8 > i))
            cnt[c] = cnt[c] + jnp.where(beats, 1.0, 0.0)
    k_sel = float(min(SEL_TOPK, n_s))
    bias_t = jnp.concatenate([jnp.where(cc < k_sel, 0.0, NEG_INF) for cc in cnt] +
                             [jnp.zeros((DH - n_s, tq), F32)], axis=0)
    bias_q = bias_t.T.astype(BF16)
    for r in range(R):
        qa_ref[r * tq:(r + 1) * tq, DH:] = bias_q

    m_ref[...] = jnp.full(m_ref.shape, NEG_INF, F32)
    acc_ref[...] = jnp.zeros(acc_ref.shape, F32)

    def scores(kt, s_ref):
        s_ref[...] = _dot(qa_ref[...], kat_ref[kt])

    def tile(kt, bufs, causal):
        s_ref, pt_ref, al_ref = bufs
        if causal:
            kpos = kt * tk + lax.broadcasted_iota(jnp.int32, (1, tk), 1)
            bias_d = jnp.where(kpos <= trow, 0.0, NEG_INF)
        for r in range(R):
            rows = slice(r * tq, (r + 1) * tq)
            s_r = s_ref[rows, :]
            if causal:
                s_r = s_r + bias_d
            m_old = m_ref[r]
            m_new = jnp.maximum(m_old, jnp.max(s_r, axis=-1, keepdims=True))
            for c in range(tk // DH):
                cols = slice(c * DH, (c + 1) * DH)
                pt_ref[rows, cols] = jnp.exp2(s_r[:, cols] - m_new).astype(BF16)
            al_ref[r] = jnp.exp2(m_old - m_new)
            m_ref[r] = m_new
        k0 = pl.multiple_of(kt * tk, tk)
        pv = _dot(pt_ref[...], va_ref[pl.ds(k0, tk), :]).reshape(R, tq, 2 * DH)
        al = al_ref[...]
        acc_ref[:, :, :DH] = al * acc_ref[:, :, :DH] + pv[:, :, :DH]
        acc_ref[:, :, DH:] = al * acc_ref[:, :, DH:] + pv[:, :, DH:]

    even = (s0_ref, p0_ref, al0_ref)
    odd = (s1_ref, p1_ref, al1_ref)
    kd = t0 // tk
    scores(0, s0_ref)

    def pair(i, carry):
        scores(2 * i + 1, s1_ref)
        tile(2 * i, even, False)
        scores(2 * i + 2, s0_ref)
        tile(2 * i + 1, odd, False)
        return carry

    lax.fori_loop(0, kd // 2, pair, 0)

    @pl.when(kd % 2 == 0)
    def _():
        tile(kd, even, True)

    @pl.when(kd % 2 == 1)
    def _():
        scores(kd, s1_ref)
        tile(kd - 1, even, False)
        tile(kd, odd, True)

    acc = acc_ref[...]
    o_slc = (acc[:, :, :DH] / acc[:, :, DH:]).reshape(R * tq, DH)

    gate = gate_ref[...]
    for r in range(R):
        y = y_ref[r * tq:(r + 1) * tq, :] + gate[:, R + r:R + r + 1] * head(o_slc, r)
        out_ref[:, r * DH:(r + 1) * DH] = y.astype(BF16)


def _nsa(o_r, o_g, kcmp, vcmp, b, s, tq, tk):
    ncp = s // CMP_STRIDE
    n_s = s // SEL_LEN
    assert n_s % 8 == 0 and n_s <= DH
    r_view = o_r.reshape(b, s, N_R)
    g_view = o_g.reshape(b, s, N_G)
    c = np.arange(ncp)[:, None]
    j = np.arange(DH)[None, :]
    n_c = (s - CMP_LEN) // CMP_STRIDE + 1
    ov = ((c * CMP_STRIDE <= j * SEL_LEN + SEL_LEN - 1) &
          (c * CMP_STRIDE + CMP_LEN - 1 >= j * SEL_LEN) & (c < n_c) & (j < n_s))
    ov = jnp.asarray(ov.astype(np.float32), BF16)
    et = jnp.asarray((np.arange(s)[None, :] // SEL_LEN == np.arange(DH)[:, None])
                     .astype(np.float32), BF16)
    wq = NSA_R * DH
    ks_col = W_QB // DH
    vs_col = ks_col + NSA_G
    kw_col = vs_col + NSA_G
    vw_col = kw_col + NSA_G
    gate_col = (2 * D_MODEL) // DH
    wlen = (-(-(WIN_LEN - 1) // BLOCK)) * BLOCK + tq
    kv = (None, s, DH)
    out = pl.pallas_call(
        functools.partial(_nsa_kernel, tq=tq, tk=tk, s_len=s, ncp=ncp, n_s=n_s),
        grid=(b, NSA_G, s // tq),
        in_specs=[pl.BlockSpec((None, tq, wq), lambda bi, g, qi: (bi, qi, g)),
                  pl.BlockSpec(kv, lambda bi, g, qi: (bi, 0, ks_col + g)),
                  pl.BlockSpec(kv, lambda bi, g, qi: (bi, 0, vs_col + g)),
                  pl.BlockSpec(kv, lambda bi, g, qi: (bi, 0, kw_col + g)),
                  pl.BlockSpec(kv, lambda bi, g, qi: (bi, 0, vw_col + g)),
                  pl.BlockSpec((None, None, ncp, DH), lambda bi, g, qi: (bi, g, 0, 0)),
                  pl.BlockSpec((None, None, ncp, DH), lambda bi, g, qi: (bi, g, 0, 0)),
                  pl.BlockSpec((None, tq, DH), lambda bi, g, qi: (bi, qi, gate_col + g)),
                  pl.BlockSpec((ncp, DH), lambda bi, g, qi: (0, 0)),
                  pl.BlockSpec((DH, s), lambda bi, g, qi: (0, 0))],
        out_specs=pl.BlockSpec((None, tq, wq), lambda bi, g, qi: (bi, qi, g)),
        out_shape=jax.ShapeDtypeStruct((b, s, W_QB), BF16),
        scratch_shapes=[pltpu.VMEM((NSA_R * tq, 2 * DH), BF16),
                        pltpu.VMEM((s // tk, 2 * DH, tk), BF16),
                        pltpu.VMEM((s // BLOCK, DH, BLOCK), BF16),
                        pltpu.VMEM((DH, ncp), BF16),
                        pltpu.VMEM((s, 2 * DH), BF16),
                        pltpu.VMEM((s, 2 * DH), BF16),
                        pltpu.VMEM((NSA_R * tq, ncp), BF16),
                        pltpu.VMEM((NSA_R * tq, wlen), BF16),
                        pltpu.VMEM((NSA_R * tq, DH), F32),
                        pltpu.VMEM((NSA_R * tq, tk), F32),
                        pltpu.VMEM((NSA_R * tq, tk), F32),
                        pltpu.VMEM((NSA_R * tq, tk), BF16),
                        pltpu.VMEM((NSA_R * tq, tk), BF16),
                        pltpu.VMEM((NSA_R, tq, DH), F32),
                        pltpu.VMEM((NSA_R, tq, DH), F32),
                        pltpu.VMEM((NSA_R, tq, DH), F32),
                        pltpu.VMEM((NSA_R, tq, 2 * DH), F32)],
        compiler_params=_cparams(("arbitrary", "arbitrary", "arbitrary")),
        name="nsa_attention",
    )(r_view, r_view, r_view, r_view, r_view, kcmp, vcmp, g_view, ov, et)
    return out.reshape(b * s, W_QB)


def _out_kernel(ya_ref, yb_ref, ga_ref, gb_ref, x_ref, woa_ref, wob_ref, wo_ref, g_ref,
                out_ref, h_ref):
    pa = _dot(ya_ref[...], woa_ref[...])
    pb = _dot(yb_ref[...], wob_ref[...])
    y = (ga_ref[...] * pa + gb_ref[...] * pb).astype(BF16)
    x_new = x_ref[...] + _dot(y, wo_ref[...])
    out_ref[...] = x_new
    h_ref[...] = _rms(x_new, g_ref[...]).astype(BF16)


def _merge_out(ya, yb, o_g, x2, woa, wob, wo, layer, g_ffn, tm):
    m = x2.shape[0]
    row = lambda w: pl.BlockSpec((tm, w), lambda i: (i, 0))
    const = lambda shape: pl.BlockSpec((None,) + shape, lambda i: (layer, 0, 0),
                                       pipeline_mode=pl.Buffered(1))
    return pl.pallas_call(
        _out_kernel,
        grid=(m // tm,),
        in_specs=[row(DIL_OUT), row(W_QB),
                  pl.BlockSpec((tm, D_MODEL), lambda i: (i, 0)),
                  pl.BlockSpec((tm, D_MODEL), lambda i: (i, 1)),
                  row(D_MODEL),
                  const((DIL_OUT, D_MODEL)), const((W_QB, D_MODEL)), const((D_MODEL, D_MODEL)),
                  pl.BlockSpec((1, D_MODEL), lambda i: (0, 0))],
        out_specs=[row(D_MODEL), row(D_MODEL)],
        out_shape=[jax.ShapeDtypeStruct((m, D_MODEL), F32),
                   jax.ShapeDtypeStruct((m, D_MODEL), BF16)],
        compiler_params=_cparams(("arbitrary",)),
        name="merge_out",
    )(ya, yb, o_g, o_g, x2, woa, wob, wo, g_ffn)


def _ffn_kernel(x_ref, h_ref, wg_ref, wu_ref, wd_ref, gf_ref, out_ref, *, nf, final):
    f = pl.program_id(1)

    @pl.when(f == 0)
    def _():
        out_ref[...] = x_ref[...]

    h = h_ref[...]
    a = _dot(h, wg_ref[...])
    u = _dot(h, wu_ref[...])
    act = (a * jax.nn.sigmoid(a) * u).astype(BF16)
    out_ref[...] += _dot(act, wd_ref[...])

    if final:
        @pl.when(f == nf - 1)
        def _():
            out_ref[...] = _rms(out_ref[...], gf_ref[...])


def _ffn(x2, h, wg, wu, wd, layer, gf, tm, tf, final):
    m = x2.shape[0]
    nf = D_FF // tf
    return pl.pallas_call(
        functools.partial(_ffn_kernel, nf=nf, final=final),
        grid=(m // tm, nf),
        in_specs=[pl.BlockSpec((tm, D_MODEL), lambda i, f: (i, 0)),
                  pl.BlockSpec((tm, D_MODEL), lambda i, f: (i, 0)),
                  pl.BlockSpec((None, D_MODEL, tf), lambda i, f: (layer, 0, f)),
                  pl.BlockSpec((None, D_MODEL, tf), lambda i, f: (layer, 0, f)),
                  pl.BlockSpec((None, tf, D_MODEL), lambda i, f: (layer, f, 0)),
                  pl.BlockSpec((1, D_MODEL), lambda i, f: (0, 0))],
        out_specs=pl.BlockSpec((tm, D_MODEL), lambda i, f: (i, 0)),
        out_shape=jax.ShapeDtypeStruct((m, D_MODEL), F32),
        compiler_params=_cparams(("arbitrary", "arbitrary")),
        name="ffn",
    )(x2, h, wg, wu, wd, gf)


def _prep_w_gates(w):
    depth = w.shape[0]
    n_gb = NSA_Q_HEADS * 3
    gate_b = w[:, :, N_SRC:N_SRC + n_gb]
    gam = w[:, :, N_SRC + n_gb:N_SRC + n_gb + D_MODEL]
    gbm = w[:, :, N_SRC + n_gb + D_MODEL:]
    gate_b = gate_b.reshape(depth, D_MODEL, NSA_G, NSA_R, 3).transpose(0, 1, 2, 4, 3)
    gate_b = gate_b.reshape(depth, D_MODEL, NSA_G, 3 * NSA_R)
    gate_b = jnp.pad(gate_b, ((0, 0), (0, 0), (0, 0), (0, DH - 3 * NSA_R)))
    gate_b = gate_b.reshape(depth, D_MODEL, NSA_G * DH)
    gate_b = jnp.pad(gate_b, ((0, 0), (0, 0), (0, N_G - 2 * D_MODEL - NSA_G * DH)))
    return jnp.concatenate([gam, gbm, gate_b], axis=2).astype(BF16)


def kernel(x, positions, ln_mix, w_in, cmp_pos_k, cmp_pos_v, cmp_w1_k, cmp_w2_k, cmp_w1_v, cmp_w2_v,
           w_out_a, w_out_b, w_out, ln_ffn, w_ffn_gate, w_ffn_up, w_ffn_down, ln_final):
    b, s, d = x.shape
    depth = w_in.shape[0]
    assert d == D_MODEL and s % DIL_UNIT == 0
    m = b * s
    tm_in = min(2048, m)
    tm_out = min(256, m)
    tm_ffn = min(1024, m)
    tq, tk = 256, min(512, s)
    ncp = s // CMP_STRIDE

    pos_f = positions.astype(F32)
    cos, sin = _rope_tables(pos_f.reshape(m, 1), tm_in)
    blk_end = np.minimum(np.arange(ncp) * CMP_STRIDE + CMP_LEN - 1, s - 1)
    cosc, sinc = _rope_tables(pos_f[:, blk_end].reshape(b * ncp, 1), ncp)
    cosc = cosc.reshape(b, ncp, DH)
    sinc = sinc.reshape(b, ncp, DH)

    cs = np.ones((1, N_SRC), np.float32)
    cs[:, :W_QA] = Q_SCALE
    cs[:, 3 * W_QA:3 * W_QA + W_QB] = Q_SCALE
    cs = jnp.asarray(cs)

    w_in_b = w_in.astype(BF16)
    w_gate_b = _prep_w_gates(w_in)
    woa_b, wob_b, wo_b = w_out_a.astype(BF16), w_out_b.astype(BF16), w_out.astype(BF16)
    wg_b, wu_b, wd_b = w_ffn_gate.astype(BF16), w_ffn_up.astype(BF16), w_ffn_down.astype(BF16)

    x2 = x.reshape(m, d)
    for l in range(depth):
        o_a, o_r, o_g = _inproj(x2, ln_mix[l][None, :], w_in_b, w_gate_b, l, cs, cos, sin, tm_in)
        ya = _dilated(o_a, b, s)
        kcmp, vcmp = _compress(
            o_a, b, s, cmp_pos_k[l], cmp_pos_v[l],
            cmp_w1_k[l].reshape(CMP_LEN, DH, CMP_HIDDEN).astype(BF16),
            cmp_w1_v[l].reshape(CMP_LEN, DH, CMP_HIDDEN).astype(BF16),
            cmp_w2_k[l].astype(BF16), cmp_w2_v[l].astype(BF16), cosc, sinc)
        yb = _nsa(o_r, o_g, kcmp, vcmp, b, s, tq, tk)
        x2, h_ffn = _merge_out(ya, yb, o_g, x2, woa_b, wob_b, wo_b, l, ln_ffn[l][None, :], tm_out)
        x2 = _ffn(x2, h_ffn, wg_b, wu_b, wd_b, l, ln_final[None, :], tm_ffn, 512, l == depth - 1)
    return x2.reshape(b, s, d)
```

```python
import functools
import math

import numpy as np
import jax
import jax.numpy as jnp
from jax import lax
from jax.experimental import pallas as pl
from jax.experimental.pallas import tpu as pltpu

F32 = jnp.float32
BF16 = jnp.bfloat16

D_MODEL = 2048
DH = 128
HALF = DH // 2
ROPE_THETA = 10000.0
NORM_EPS = 1e-6
NEG_INF = -1e30
BLOCK = 128

DIL_GROUPS = ((128, 1), (512, 4), (2048, 16))
DIL_HPG = 4
DIL_HEADS = DIL_HPG * len(DIL_GROUPS)
DIL_OUT = DIL_HPG * DH
DIL_UNIT = DIL_GROUPS[-1][1] * BLOCK

NSA_Q_HEADS = 16
NSA_G = 2
NSA_R = NSA_Q_HEADS // NSA_G
CMP_LEN = 32
CMP_STRIDE = 16
CMP_HIDDEN = 256
SEL_LEN = 64
SEL_TOPK = 16
WIN_LEN = 512
FORCE_BONUS = 1e4
D_FF = 5632

W_QA = DIL_HEADS * DH
W_QB = NSA_Q_HEADS * DH
W_KV = NSA_G * DH
TN_IN = 512
IN_SPLIT = 8
N_SRC = 3 * W_QA + W_QB + 6 * W_KV
N_A = 3 * W_QA + 2 * W_KV
N_R = W_QB + 4 * W_KV
N_G = 2 * D_MODEL + TN_IN
COL_KC = 3 * W_QA // DH
VMEM_LIMIT = 60 * 1024 * 1024
Q_SCALE = DH ** -0.5 * math.log2(math.e)


def _cparams(sem):
    return pltpu.CompilerParams(dimension_semantics=sem, vmem_limit_bytes=VMEM_LIMIT)


def _dot(a, b):
    return jnp.dot(a, b, preferred_element_type=F32)


def _rope_tab_kernel(pos_ref, inv_ref, sgn_ref, cos_ref, sin_ref):
    ang = pos_ref[...] * inv_ref[...]
    cos_ref[...] = jnp.cos(ang)
    sin_ref[...] = jnp.sin(ang) * sgn_ref[...]


def _rope_tables(pos_f, tm):
    m = pos_f.shape[0]
    inv = ROPE_THETA ** (-2.0 * jnp.arange(HALF, dtype=F32) / DH)
    inv = jnp.concatenate([inv, inv])[None, :]
    sgn = jnp.concatenate([-jnp.ones((HALF,), F32), jnp.ones((HALF,), F32)])[None, :]
    return pl.pallas_call(
        _rope_tab_kernel,
        grid=(m // tm,),
        in_specs=[pl.BlockSpec((tm, 1), lambda i: (i, 0)),
                  pl.BlockSpec((1, DH), lambda i: (0, 0)),
                  pl.BlockSpec((1, DH), lambda i: (0, 0))],
        out_specs=[pl.BlockSpec((tm, DH), lambda i: (i, 0)),
                   pl.BlockSpec((tm, DH), lambda i: (i, 0))],
        out_shape=[jax.ShapeDtypeStruct((m, DH), F32)] * 2,
        compiler_params=_cparams(("arbitrary",)),
        name="rope_tables",
    )(pos_f, inv, sgn)


def _rope(a, cos, sin_signed):
    return a * cos + pltpu.roll(a, HALF, 1) * sin_signed


def _rms(x, g):
    ms = jnp.mean(x * x, axis=-1, keepdims=True)
    return x * lax.rsqrt(ms + NORM_EPS) * g


T_QK = 2 * W_QA // TN_IN
T_VA = 3 * W_QA // TN_IN
T_QB = T_VA + W_QB // TN_IN
T_KC = T_QB + 2 * W_KV // TN_IN
T_SRC = N_SRC // TN_IN
T_ALL = T_SRC + N_G // TN_IN


def _inproj_kernel(x_hbm, g_ref, w_ref, wg_ref, cs_ref, cos_ref, sin_ref, oa_ref, or_ref, og_ref,
                   xbuf_ref, h_ref, sem):
    i = pl.program_id(0)
    j = pl.program_id(1)
    tm = h_ref.shape[0]

    def x_copy(tile):
        rows = pl.ds(pl.multiple_of(tile * tm, tm), tm)
        return pltpu.make_async_copy(x_hbm.at[rows, :], xbuf_ref, sem)

    @pl.when((i == 0) & (j == 0))
    def _():
        x_copy(0).start()

    @pl.when(j == 0)
    def _():
        x_copy(i).wait()
        h_ref[...] = _rms(xbuf_ref[...], g_ref[...]).astype(BF16)

    @pl.when((j == 1) & (i + 1 < pl.num_programs(0)))
    def _():
        x_copy(i + 1).start()

    groups = [slice(k * (tm // IN_SPLIT), (k + 1) * (tm // IN_SPLIT)) for k in range(IN_SPLIT)]
    heads = [slice(hh * DH, (hh + 1) * DH) for hh in range(TN_IN // DH)]

    def region(out_ref, epilogue, weights=w_ref):
        accs = [_dot(h_ref[rows, :], weights[...]) for rows in groups]
        for rows, acc in zip(groups, accs):
            epilogue(out_ref, rows, acc)

    def rope_heads(out_ref, rows, acc, which):
        cos = cos_ref[rows, :]
        sin = sin_ref[rows, :]
        for sl in which:
            r = _rope(acc[:, sl], cos, sin) * cs_ref[:, sl]
            out_ref[rows, sl] = r.astype(out_ref.dtype)

    def roped(out_ref, rows, acc):
        rope_heads(out_ref, rows, acc, heads)

    def plain(out_ref, rows, acc):
        out_ref[rows, :] = acc.astype(out_ref.dtype)

    def key_value(out_ref, rows, acc):
        rope_heads(out_ref, rows, acc, heads[:len(heads) // 2])
        out_ref[rows, TN_IN // 2:] = acc[:, TN_IN // 2:].astype(out_ref.dtype)

    def sigmoid(out_ref, rows, acc):
        out_ref[rows, :] = jax.nn.sigmoid(acc)

    pl.when(j < T_QK)(lambda: region(oa_ref, roped))
    pl.when((j >= T_QK) & (j < T_VA))(lambda: region(oa_ref, plain))
    pl.when((j >= T_VA) & (j < T_QB))(lambda: region(or_ref, roped))
    pl.when((j >= T_QB) & (j < T_KC))(lambda: region(oa_ref, plain))
    pl.when((j >= T_KC) & (j < T_SRC))(lambda: region(or_ref, key_value))
    pl.when(j >= T_SRC)(lambda: region(og_ref, sigmoid, wg_ref))


def _inproj(x2, g, w, wg, layer, cs, cos, sin, tm):
    m = x2.shape[0]
    n_kc = T_KC - T_QB

    def a_tile(j):
        return jnp.where(j < T_VA, j, jnp.clip(j - T_QB, -1, n_kc - 1) + T_VA)

    def r_tile(j):
        first = jnp.clip(j - T_VA, 0, T_QB - T_VA - 1)
        return jnp.where(j < T_KC, first, jnp.minimum(j, T_SRC - 1) - T_KC + T_QB - T_VA)

    return pl.pallas_call(
        _inproj_kernel,
        grid=(m // tm, T_ALL),
        in_specs=[pl.BlockSpec(memory_space=pl.ANY),
                  pl.BlockSpec((1, D_MODEL), lambda i, j: (0, 0)),
                  pl.BlockSpec((None, D_MODEL, TN_IN),
                               lambda i, j: (layer, 0, jnp.minimum(j, T_SRC - 1))),
                  pl.BlockSpec((None, D_MODEL, TN_IN),
                               lambda i, j: (layer, 0, jnp.maximum(j - T_SRC, 0))),
                  pl.BlockSpec((1, TN_IN), lambda i, j: (0, jnp.minimum(j, T_SRC - 1))),
                  pl.BlockSpec((tm, DH), lambda i, j: (i, 0)),
                  pl.BlockSpec((tm, DH), lambda i, j: (i, 0))],
        out_specs=[pl.BlockSpec((tm, TN_IN), lambda i, j: (i, a_tile(j))),
                   pl.BlockSpec((tm, TN_IN), lambda i, j: (i, r_tile(j))),
                   pl.BlockSpec((tm, TN_IN), lambda i, j: (i, jnp.maximum(j - T_SRC, 0)))],
        out_shape=[jax.ShapeDtypeStruct((m, N_A), F32),
                   jax.ShapeDtypeStruct((m, N_R), BF16),
                   jax.ShapeDtypeStruct((m, N_G), F32)],
        scratch_shapes=[pltpu.VMEM((tm, D_MODEL), F32),
                        pltpu.VMEM((tm, D_MODEL), BF16),
                        pltpu.SemaphoreType.DMA(())],
        compiler_params=_cparams(("arbitrary", "arbitrary")),
        name="inproj",
    )(x2, g, w, wg, cs, cos, sin)


def _rows(start, dil):
    return pl.ds(start, BLOCK) if dil == 1 else pl.ds(start, BLOCK, stride=dil)


def _dil_kernel(*refs):
    out_ref, acc_scr, l_scr, m_scr = refs[-4:]
    u = pl.program_id(1)
    row = lax.broadcasted_iota(jnp.int32, (BLOCK, 2 * BLOCK), 0)
    col = lax.broadcasted_iota(jnp.int32, (BLOCK, 2 * BLOCK), 1)
    band = ((col < BLOCK) & (col >= row)) | ((col >= BLOCK) & (col - BLOCK <= row))
    bias = jnp.where(band, 0.0, NEG_INF)
    bias_first = jnp.where(band & ((col >= BLOCK) | (u > 0)), 0.0, NEG_INF)
    ones = jnp.ones((2 * BLOCK, DH), BF16)
    for gi, (_, dil) in enumerate(DIL_GROUPS):
        q_ref, kc_ref, kp_ref, vc_ref, vp_ref = refs[5 * gi:5 * gi + 5]
        span = BLOCK * dil
        for rho in range(dil):
            for ub in range(DIL_UNIT // span):
                cur = _rows(ub * span + rho, dil)
                if ub == 0:
                    kp = kp_ref[_rows(rho, dil), :].astype(BF16)
                    vp = vp_ref[_rows(rho, dil), :].astype(BF16)
                    b_add = bias_first
                else:
                    kp, vp, b_add = kc, vc, bias
                q = q_ref[cur, :].astype(BF16)
                kc = kc_ref[cur, :].astype(BF16)
                vc = vc_ref[cur, :].astype(BF16)
                k = jnp.concatenate([kp, kc], axis=0)
                v = jnp.concatenate([vp, vc], axis=0)
                s = lax.dot_general(q, k, (((1,), (1,)), ((), ())), preferred_element_type=F32) + b_add
                m = jnp.broadcast_to(jnp.max(s, axis=-1, keepdims=True), (BLOCK, DH))
                p = jnp.concatenate([jnp.exp2(s[:, :BLOCK] - m), jnp.exp2(s[:, BLOCK:] - m)], axis=1)
                pv = _dot(p.astype(BF16), jnp.concatenate([v, ones], axis=1))
                acc_scr[gi, cur, :] = pv[:, :DH]
                l_scr[gi, cur, :] = pv[:, DH:]
                m_scr[gi, cur, :] = m
    m0, m1, m2 = m_scr[0], m_scr[1], m_scr[2]
    m = jnp.maximum(jnp.maximum(m0, m1), m2)
    e0, e1, e2 = jnp.exp2(m0 - m), jnp.exp2(m1 - m), jnp.exp2(m2 - m)
    num = e0 * acc_scr[0] + e1 * acc_scr[1] + e2 * acc_scr[2]
    den = e0 * l_scr[0] + e1 * l_scr[1] + e2 * l_scr[2]
    out_ref[...] = (num / den).astype(BF16)


def _dilated(o_a, b, s):
    a_view = o_a.reshape(b, s, N_A)
    in_specs, args = [], []
    for gi, (_, dil) in enumerate(DIL_GROUPS):
        span = BLOCK * dil
        per = DIL_UNIT // span

        def cur(colbase, gi=gi):
            return pl.BlockSpec((None, DIL_UNIT, DH),
                                lambda bi, u, j: (bi, u, colbase + gi * DIL_HPG + j))

        def prev(colbase, gi=gi, span=span, per=per):
            return pl.BlockSpec((None, span, DH),
                                lambda bi, u, j: (bi, jnp.maximum(u * per - 1, 0),
                                                  colbase + gi * DIL_HPG + j))

        kcol, vcol = W_QA // DH, 2 * W_QA // DH
        in_specs += [cur(0), cur(kcol), prev(kcol), cur(vcol), prev(vcol)]
        args += [a_view] * 5
    out = pl.pallas_call(
        _dil_kernel,
        grid=(b, s // DIL_UNIT, DIL_HPG),
        in_specs=in_specs,
        out_specs=pl.BlockSpec((None, DIL_UNIT, DH), lambda bi, u, j: (bi, u, j)),
        out_shape=jax.ShapeDtypeStruct((b, s, DIL_OUT), BF16),
        scratch_shapes=[pltpu.VMEM((len(DIL_GROUPS), DIL_UNIT, DH), F32)] * 3,
        compiler_params=_cparams(("parallel", "parallel", "parallel")),
        name="dilated",
    )(*args)
    return out.reshape(b * s, DIL_OUT)


def _cmp_kernel(xk0_ref, xk1_ref, xv0_ref, xv1_ref, posk_ref, posv_ref, w1k_ref, w1v_ref,
                w2k_ref, w2v_ref, cos_ref, sin_ref, kc_ref, vc_ref, *, ncp):
    half = CMP_LEN // 2
    x_refs = ((xk0_ref, xk1_ref), (xv0_ref, xv1_ref))
    for kind in range(2):
        pos_ref, w1_ref, w2_ref, o_ref = ((posk_ref, w1k_ref, w2k_ref, kc_ref) if kind == 0
                                          else (posv_ref, w1v_ref, w2v_ref, vc_ref))
        for g in range(NSA_G):
            x_ref = x_refs[kind][g]
            a = jnp.zeros((ncp, CMP_HIDDEN), F32)
            bm = jnp.zeros((ncp, CMP_HIDDEN), F32)
            for l in range(half):
                x = x_ref[pl.ds(l, ncp, stride=CMP_STRIDE), :]
                a = a + _dot((x + pos_ref[l:l + 1, :]).astype(BF16), w1_ref[l])
                bm = bm + _dot((x + pos_ref[half + l:half + l + 1, :]).astype(BF16),
                               w1_ref[half + l])
            hid = a + pltpu.roll(bm, ncp - 1, 0)
            hid = jax.nn.gelu(hid, approximate=True).astype(BF16)
            out = _dot(hid, w2_ref[...])
            if kind == 0:
                out = _rope(out, cos_ref[...], sin_ref[...])
            o_ref[g] = out.astype(BF16)


def _compress(o_a, b, s, posk, posv, w1k, w1v, w2k, w2v, cosc, sinc):
    ncp = s // CMP_STRIDE
    x_view = o_a.reshape(b, s, N_A)
    full = lambda shape: pl.BlockSpec(shape, lambda bi: (0,) * len(shape))
    return pl.pallas_call(
        functools.partial(_cmp_kernel, ncp=ncp),
        grid=(b,),
        in_specs=[pl.BlockSpec((None, s, DH), lambda bi, c=c: (bi, 0, COL_KC + c))
                  for c in range(2 * NSA_G)] +
                 [full((CMP_LEN, DH)), full((CMP_LEN, DH)),
                  full((CMP_LEN, DH, CMP_HIDDEN)), full((CMP_LEN, DH, CMP_HIDDEN)),
                  full((CMP_HIDDEN, DH)), full((CMP_HIDDEN, DH)),
                  pl.BlockSpec((None, ncp, DH), lambda bi: (bi, 0, 0)),
                  pl.BlockSpec((None, ncp, DH), lambda bi: (bi, 0, 0))],
        out_specs=[pl.BlockSpec((None, NSA_G, ncp, DH), lambda bi: (bi, 0, 0, 0)),
                   pl.BlockSpec((None, NSA_G, ncp, DH), lambda bi: (bi, 0, 0, 0))],
        out_shape=[jax.ShapeDtypeStruct((b, NSA_G, ncp, DH), BF16)] * 2,
        compiler_params=_cparams(("arbitrary",)),
        name="nsa_compress",
    )(x_view, x_view, x_view, x_view, posk, posv, w1k, w1v, w2k, w2v, cosc, sinc)


def _nsa_kernel(q_ref, ks_ref, vs_ref, kw_ref, vw_ref, kc_ref, vc_ref, gate_ref, ov_ref, et_ref,
                out_ref, qa_ref, kat_ref, kwt_ref, kct_ref, va_ref, wa_ref, p_ref, pw_ref, y_ref,
                s0_ref, s1_ref, p0_ref, p1_ref, m_ref, al0_ref, al1_ref, acc_ref,
                *, tq, tk, s_len, ncp, n_s):
    R = NSA_R
    qi = pl.program_id(2)
    t0 = qi * tq
    transposed = lambda a: a.astype(F32).T.astype(BF16)

    @pl.when(qi == 0)
    def _():
        ones = jnp.ones((s_len, DH), BF16)
        for c in range(s_len // tk):
            kat_ref[c, :DH, :] = transposed(ks_ref[c * tk:(c + 1) * tk, :])
            kat_ref[c, DH:, :] = et_ref[:, c * tk:(c + 1) * tk]
        for c in range(s_len // BLOCK):
            kwt_ref[c] = transposed(kw_ref[c * BLOCK:(c + 1) * BLOCK, :])
        kct_ref[...] = transposed(kc_ref[...])
        va_ref[:, :DH] = vs_ref[...]
        va_ref[:, DH:] = ones
        wa_ref[:, :DH] = vw_ref[...]
        wa_ref[:, DH:] = ones

    for r in range(R):
        qa_ref[r * tq:(r + 1) * tq, :DH] = q_ref[:, r * DH:(r + 1) * DH]
    q = qa_ref[:, :DH]
    trow = t0 + lax.broadcasted_iota(jnp.int32, (tq, 1), 0)
    tlane = t0 + lax.broadcasted_iota(jnp.int32, (1, tq), 1)
    head = lambda a, r: a[r * tq:(r + 1) * tq]

    cend = lax.broadcasted_iota(jnp.int32, (1, ncp), 1) * CMP_STRIDE + (CMP_LEN - 1)
    valid_c = cend <= trow
    bias_c = jnp.where(valid_c, 0.0, NEG_INF)
    keep_c = jnp.where(valid_c, 1.0, 0.0)
    s = _dot(q, kct_ref[...])
    chunks_c = [slice(c * DH, (c + 1) * DH) for c in range(ncp // DH)]
    psum = [jnp.zeros((tq, DH), F32) for _ in chunks_c]
    for r in range(R):
        s_r = head(s, r) + bias_c
        m_c = jnp.broadcast_to(jnp.max(s_r, axis=-1, keepdims=True), (tq, DH))
        e = [jnp.exp2(s_r[:, cols] - m_c) * keep_c[:, cols] for cols in chunks_c]
        l_c = jnp.maximum(jnp.sum(sum(e), axis=-1, keepdims=True), 1e-30)
        l_c = jnp.broadcast_to(l_c, (tq, DH))
        for ci, cols in enumerate(chunks_c):
            p = e[ci] / l_c
            psum[ci] = psum[ci] + p
            p_ref[r * tq:(r + 1) * tq, cols] = p.astype(BF16)
    psum = jnp.concatenate(psum, axis=1)
    o_cmp = _dot(p_ref[:, :ncp], vc_ref[...])

    wlen = (-(-(WIN_LEN - 1) // BLOCK)) * BLOCK + tq
    w0 = pl.multiple_of(jnp.maximum(t0 + tq - wlen, 0), BLOCK)
    kpos = w0 + lax.broadcasted_iota(jnp.int32, (1, wlen), 1)
    bias_w = jnp.where((kpos <= trow) & (trow - kpos <= WIN_LEN - 1), 0.0, NEG_INF)
    wb = w0 // BLOCK
    kwt = jnp.concatenate([kwt_ref[wb + jb] for jb in range(wlen // BLOCK)], axis=1)
    s = _dot(q, kwt)
    for r in range(R):
        s_r = head(s, r) + bias_w
        m_w = jnp.broadcast_to(jnp.max(s_r, axis=-1, keepdims=True), (tq, DH))
        for c in range(wlen // DH):
            cols = slice(c * DH, (c + 1) * DH)
            pw_ref[r * tq:(r + 1) * tq, cols] = jnp.exp2(s_r[:, cols] - m_w).astype(BF16)
    ow = _dot(pw_ref[...], wa_ref[pl.ds(w0, wlen), :])
    o_win = ow[:, :DH] / ow[:, DH:]
    gate = gate_ref[...]
    for r in range(R):
        y_ref[r * tq:(r + 1) * tq, :] = (gate[:, r:r + 1] * head(o_cmp, r) +
                                         gate[:, 2 * R + r:2 * R + r + 1] * head(o_win, r))

    hi = psum.astype(BF16)
    rem = psum - hi.astype(F32)
    mid = rem.astype(BF16)
    lo = (rem - mid.astype(F32)).astype(BF16)
    ov = ov_ref[...]
    p_slc = (_dot(hi, ov) + _dot(mid, ov) + _dot(lo, ov)).T[:n_s]
    jj = lax.broadcasted_iota(jnp.int32, (n_s, 1), 0)
    blk_t = tlane // SEL_LEN
    forced = (jj == 0) | (jj == blk_t) | (jj == blk_t - 1)
    score = jnp.where(jj <= blk_t, p_slc + jnp.where(forced, FORCE_BONUS, 0.0), -1.0)
    nch = n_s // 8
    chunks = [score[c * 8:(c + 1) * 8] for c in range(nch)]
    cnt = [jnp.zeros((8, tq), F32) for _ in range(nch)]
    sub = lax.broadcasted_iota(jnp.int32, (8, 1), 0)
    for i in range(n_s):
        row_i = score[i:i + 1]
        for c in range(nch):
            if c * 8 > i:
                beats = row_i >= chunks[c]
            elif c * 8 + 7 < i:
                beats = row_i > chunks[c]
            else:
                beats = (row_i > chunks[c]) | ((row_i == chunks[c]) & (sub + c * 8 > i))
            cnt[c] = cnt[c] + jnp.where(beats, 1.0, 0.0)
    k_sel = float(min(SEL_TOPK, n_s))
    bias_t = jnp.concatenate([jnp.where(cc < k_sel, 0.0, NEG_INF) for cc in cnt] +
                             [jnp.zeros((DH - n_s, tq), F32)], axis=0)
    bias_q = bias_t.T.astype(BF16)
    for r in range(R):
        qa_ref[r * tq:(r + 1) * tq, DH:] = bias_q

    m_ref[...] = jnp.full(m_ref.shape, NEG_INF, F32)
    acc_ref[...] = jnp.zeros(acc_ref.shape, F32)

    def scores(kt, s_ref):
        s_ref[...] = _dot(qa_ref[...], kat_ref[kt])

    def tile(kt, bufs, causal):
        s_ref, pt_ref, al_ref = bufs
        if causal:
            kpos = kt * tk + lax.broadcasted_iota(jnp.int32, (1, tk), 1)
            bias_d = jnp.where(kpos <= trow, 0.0, NEG_INF)
        for r in range(R):
            rows = slice(r * tq, (r + 1) * tq)
            s_r = s_ref[rows, :]
            if causal:
                s_r = s_r + bias_d
            m_old = m_ref[r]
            m_new = jnp.maximum(m_old, jnp.max(s_r, axis=-1, keepdims=True))
            for c in range(tk // DH):
                cols = slice(c * DH, (c + 1) * DH)
                pt_ref[rows, cols] = jnp.exp2(s_r[:, cols] - m_new).astype(BF16)
            al_ref[r] = jnp.exp2(m_old - m_new)
            m_ref[r] = m_new
        k0 = pl.multiple_of(kt * tk, tk)
        pv = _dot(pt_ref[...], va_ref[pl.ds(k0, tk), :]).reshape(R, tq, 2 * DH)
        al = al_ref[...]
        acc_ref[:, :, :DH] = al * acc_ref[:, :, :DH] + pv[:, :, :DH]
        acc_ref[:, :, DH:] = al * acc_ref[:, :, DH:] + pv[:, :, DH:]

    even = (s0_ref, p0_ref, al0_ref)
    odd = (s1_ref, p1_ref, al1_ref)
    kd = t0 // tk
    scores(0, s0_ref)

    def pair(i, carry):
        scores(2 * i + 1, s1_ref)
        tile(2 * i, even, False)
        scores(2 * i + 2, s0_ref)
        tile(2 * i + 1, odd, False)
        return carry

    lax.fori_loop(0, kd // 2, pair, 0)

    @pl.when(kd % 2 == 0)
    def _():
        tile(kd, even, True)

    @pl.when(kd % 2 == 1)
    def _():
        scores(kd, s1_ref)
        tile(kd - 1, even, False)
        tile(kd, odd, True)

    acc = acc_ref[...]
    o_slc = (acc[:, :, :DH] / acc[:, :, DH:]).reshape(R * tq, DH)

    gate = gate_ref[...]
    for r in range(R):
        y = y_ref[r * tq:(r + 1) * tq, :] + gate[:, R + r:R + r + 1] * head(o_slc, r)
        out_ref[:, r * DH:(r + 1) * DH] = y.astype(BF16)


def _nsa(o_r, o_g, kcmp, vcmp, b, s, tq, tk):
    ncp = s // CMP_STRIDE
    n_s = s // SEL_LEN
    assert n_s % 8 == 0 and n_s <= DH
    r_view = o_r.reshape(b, s, N_R)
    g_view = o_g.reshape(b, s, N_G)
    c = np.arange(ncp)[:, None]
    j = np.arange(DH)[None, :]
    n_c = (s - CMP_LEN) // CMP_STRIDE + 1
    ov = ((c * CMP_STRIDE <= j * SEL_LEN + SEL_LEN - 1) &
          (c * CMP_STRIDE + CMP_LEN - 1 >= j * SEL_LEN) & (c < n_c) & (j < n_s))
    ov = jnp.asarray(ov.astype(np.float32), BF16)
    et = jnp.asarray((np.arange(s)[None, :] // SEL_LEN == np.arange(DH)[:, None])
                     .astype(np.float32), BF16)
    wq = NSA_R * DH
    ks_col = W_QB // DH
    vs_col = ks_col + NSA_G
    kw_col = vs_col + NSA_G
    vw_col = kw_col + NSA_G
    gate_col = (2 * D_MODEL) // DH
    wlen = (-(-(WIN_LEN - 1) // BLOCK)) * BLOCK + tq
    kv = (None, s, DH)
    out = pl.pallas_call(
        functools.partial(_nsa_kernel, tq=tq, tk=tk, s_len=s, ncp=ncp, n_s=n_s),
        grid=(b, NSA_G, s // tq),
        in_specs=[pl.BlockSpec((None, tq, wq), lambda bi, g, qi: (bi, qi, g)),
                  pl.BlockSpec(kv, lambda bi, g, qi: (bi, 0, ks_col + g)),
                  pl.BlockSpec(kv, lambda bi, g, qi: (bi, 0, vs_col + g)),
                  pl.BlockSpec(kv, lambda bi, g, qi: (bi, 0, kw_col + g)),
                  pl.BlockSpec(kv, lambda bi, g, qi: (bi, 0, vw_col + g)),
                  pl.BlockSpec((None, None, ncp, DH), lambda bi, g, qi: (bi, g, 0, 0)),
                  pl.BlockSpec((None, None, ncp, DH), lambda bi, g, qi: (bi, g, 0, 0)),
                  pl.BlockSpec((None, tq, DH), lambda bi, g, qi: (bi, qi, gate_col + g)),
                  pl.BlockSpec((ncp, DH), lambda bi, g, qi: (0, 0)),
                  pl.BlockSpec((DH, s), lambda bi, g, qi: (0, 0))],
        out_specs=pl.BlockSpec((None, tq, wq), lambda bi, g, qi: (bi, qi, g)),
        out_shape=jax.ShapeDtypeStruct((b, s, W_QB), BF16),
        scratch_shapes=[pltpu.VMEM((NSA_R * tq, 2 * DH), BF16),
                        pltpu.VMEM((s // tk, 2 * DH, tk), BF16),
                        pltpu.VMEM((s // BLOCK, DH, BLOCK), BF16),
                        pltpu.VMEM((DH, ncp), BF16),
                        pltpu.VMEM((s, 2 * DH), BF16),
                        pltpu.VMEM((s, 2 * DH), BF16),
                        pltpu.VMEM((NSA_R * tq, ncp), BF16),
                        pltpu.VMEM((NSA_R * tq, wlen), BF16),
                        pltpu.VMEM((NSA_R * tq, DH), F32),
                        pltpu.VMEM((NSA_R * tq, tk), F32),
                        pltpu.VMEM((NSA_R * tq, tk), F32),
                        pltpu.VMEM((NSA_R * tq, tk), BF16),
                        pltpu.VMEM((NSA_R * tq, tk), BF16),
                        pltpu.VMEM((NSA_R, tq, DH), F32),
                        pltpu.VMEM((NSA_R, tq, DH), F32),
                        pltpu.VMEM((NSA_R, tq, DH), F32),
                        pltpu.VMEM((NSA_R, tq, 2 * DH), F32)],
        compiler_params=_cparams(("parallel", "parallel", "arbitrary")),
        name="nsa_attention",
    )(r_view, r_view, r_view, r_view, r_view, kcmp, vcmp, g_view, ov, et)
    return out.reshape(b * s, W_QB)


def _out_kernel(ya_ref, yb_ref, ga_ref, gb_ref, x_ref, woa_ref, wob_ref, wo_ref, g_ref,
                out_ref, h_ref):
    pa = _dot(ya_ref[...], woa_ref[...])
    pb = _dot(yb_ref[...], wob_ref[...])
    y = (ga_ref[...] * pa + gb_ref[...] * pb).astype(BF16)
    x_new = x_ref[...] + _dot(y, wo_ref[...])
    out_ref[...] = x_new
    h_ref[...] = _rms(x_new, g_ref[...]).astype(BF16)


def _merge_out(ya, yb, o_g, x2, woa, wob, wo, layer, g_ffn, tm):
    m = x2.shape[0]
    row = lambda w: pl.BlockSpec((tm, w), lambda i: (i, 0))
    const = lambda shape: pl.BlockSpec((None,) + shape, lambda i: (layer, 0, 0),
                                       pipeline_mode=pl.Buffered(1))
    return pl.pallas_call(
        _out_kernel,
        grid=(m // tm,),
        in_specs=[row(DIL_OUT), row(W_QB),
                  pl.BlockSpec((tm, D_MODEL), lambda i: (i, 0)),
                  pl.BlockSpec((tm, D_MODEL), lambda i: (i, 1)),
                  row(D_MODEL),
                  const((DIL_OUT, D_MODEL)), const((W_QB, D_MODEL)), const((D_MODEL, D_MODEL)),
                  pl.BlockSpec((1, D_MODEL), lambda i: (0, 0))],
        out_specs=[row(D_MODEL), row(D_MODEL)],
        out_shape=[jax.ShapeDtypeStruct((m, D_MODEL), F32),
                   jax.ShapeDtypeStruct((m, D_MODEL), BF16)],
        compiler_params=_cparams(("parallel",)),
        name="merge_out",
    )(ya, yb, o_g, o_g, x2, woa, wob, wo, g_ffn)


def _ffn_kernel(x_ref, h_ref, wg_ref, wu_ref, wd_ref, gf_ref, out_ref, *, nf, final):
    f = pl.program_id(1)

    @pl.when(f == 0)
    def _():
        out_ref[...] = x_ref[...]

    h = h_ref[...]
    a = _dot(h, wg_ref[...])
    u = _dot(h, wu_ref[...])
    act = (a * jax.nn.sigmoid(a) * u).astype(BF16)
    out_ref[...] += _dot(act, wd_ref[...])

    if final:
        @pl.when(f == nf - 1)
        def _():
            out_ref[...] = _rms(out_ref[...], gf_ref[...])


def _ffn(x2, h, wg, wu, wd, layer, gf, tm, tf, final):
    m = x2.shape[0]
    nf = D_FF // tf
    return pl.pallas_call(
        functools.partial(_ffn_kernel, nf=nf, final=final),
        grid=(m // tm, nf),
        in_specs=[pl.BlockSpec((tm, D_MODEL), lambda i, f: (i, 0)),
                  pl.BlockSpec((tm, D_MODEL), lambda i, f: (i, 0)),
                  pl.BlockSpec((None, D_MODEL, tf), lambda i, f: (layer, 0, f)),
                  pl.BlockSpec((None, D_MODEL, tf), lambda i, f: (layer, 0, f)),
                  pl.BlockSpec((None, tf, D_MODEL), lambda i, f: (layer, f, 0)),
                  pl.BlockSpec((1, D_MODEL), lambda i, f: (0, 0))],
        out_specs=pl.BlockSpec((tm, D_MODEL), lambda i, f: (i, 0)),
        out_shape=jax.ShapeDtypeStruct((m, D_MODEL), F32),
        compiler_params=_cparams(("parallel", "arbitrary")),
        name="ffn",
    )(x2, h, wg, wu, wd, gf)


def _prep_w_gates(w):
    depth = w.shape[0]
    n_gb = NSA_Q_HEADS * 3
    gate_b = w[:, :, N_SRC:N_SRC + n_gb]
    gam = w[:, :, N_SRC + n_gb:N_SRC + n_gb + D_MODEL]
    gbm = w[:, :, N_SRC + n_gb + D_MODEL:]
    gate_b = gate_b.reshape(depth, D_MODEL, NSA_G, NSA_R, 3).transpose(0, 1, 2, 4, 3)
    gate_b = gate_b.reshape(depth, D_MODEL, NSA_G, 3 * NSA_R)
    gate_b = jnp.pad(gate_b, ((0, 0), (0, 0), (0, 0), (0, DH - 3 * NSA_R)))
    gate_b = gate_b.reshape(depth, D_MODEL, NSA_G * DH)
    gate_b = jnp.pad(gate_b, ((0, 0), (0, 0), (0, N_G - 2 * D_MODEL - NSA_G * DH)))
    return jnp.concatenate([gam, gbm, gate_b], axis=2).astype(BF16)


def kernel(x, positions, ln_mix, w_in, cmp_pos_k, cmp_pos_v, cmp_w1_k, cmp_w2_k, cmp_w1_v, cmp_w2_v,
           w_out_a, w_out_b, w_out, ln_ffn, w_ffn_gate, w_ffn_up, w_ffn_down, ln_final):
    b, s, d = x.shape
    depth = w_in.shape[0]
    assert d == D_MODEL and s % DIL_UNIT == 0
    m = b * s
    tm_in = min(2048, m)
    tm_out = min(256, m)
    tm_ffn = min(1024, m)
    tq, tk = 256, min(512, s)
    ncp = s // CMP_STRIDE

    pos_f = positions.astype(F32)
    cos, sin = _rope_tables(pos_f.reshape(m, 1), tm_in)
    blk_end = np.minimum(np.arange(ncp) * CMP_STRIDE + CMP_LEN - 1, s - 1)
    cosc, sinc = _rope_tables(pos_f[:, blk_end].reshape(b * ncp, 1), ncp)
    cosc = cosc.reshape(b, ncp, DH)
    sinc = sinc.reshape(b, ncp, DH)

    cs = np.ones((1, N_SRC), np.float32)
    cs[:, :W_QA] = Q_SCALE
    cs[:, 3 * W_QA:3 * W_QA + W_QB] = Q_SCALE
    cs = jnp.asarray(cs)

    w_in_b = w_in.astype(BF16)
    w_gate_b = _prep_w_gates(w_in)
    woa_b, wob_b, wo_b = w_out_a.astype(BF16), w_out_b.astype(BF16), w_out.astype(BF16)
    wg_b, wu_b, wd_b = w_ffn_gate.astype(BF16), w_ffn_up.astype(BF16), w_ffn_down.astype(BF16)

    x2 = x.reshape(m, d)
    for l in range(depth):
        o_a, o_r, o_g = _inproj(x2, ln_mix[l][None, :], w_in_b, w_gate_b, l, cs, cos, sin, tm_in)
        ya = _dilated(o_a, b, s)
        kcmp, vcmp = _compress(
            o_a, b, s, cmp_pos_k[l], cmp_pos_v[l],
            cmp_w1_k[l].reshape(CMP_LEN, DH, CMP_HIDDEN).astype(BF16),
            cmp_w1_v[l].reshape(CMP_LEN, DH, CMP_HIDDEN).astype(BF16),
            cmp_w2_k[l].astype(BF16), cmp_w2_v[l].astype(BF16), cosc, sinc)
        yb = _nsa(o_r, o_g, kcmp, vcmp, b, s, tq, tk)
        x2, h_ffn = _merge_out(ya, yb, o_g, x2, woa_b, wob_b, wo_b, l, ln_ffn[l][None, :], tm_out)
        x2 = _ffn(x2, h_ffn, wg_b, wu_b, wd_b, l, ln_final[None, :], tm_ffn, 512, l == depth - 1)
    return x2.reshape(b, s, d)
```
